```python
import math
import jax
import jax.numpy as jnp
from jax import lax
import numpy as np

D_MODEL = 1024
BATCH = 4
SEQ = 8192
DEPTH = 4
DEC_BATCH = 16
DEC_SEQ = 64
PAST_LEN = 4096

CHUNK = 64
N_MIXERS = 4
Q_BLOCK = 128
NORM_EPS = 1e-6
ROPE_THETA = 10000.0
NEG_INF = -1e30
D_FF = 4 * D_MODEL

S5_GROUP = 16
S5_GROUPS = D_MODEL // S5_GROUP
S5_STATE = 64
S5_SCAN_BLOCK = 128
S5_DT_MIN = 1e-3
S5_DT_MAX = 1e-1

DIFF_HEADS = 8
DIFF_DH = D_MODEL // (2 * DIFF_HEADS)
DIFF_LAYER = 1
LAMBDA_INIT = 0.8 - 0.6 * math.exp(-0.3 * DIFF_LAYER)

MLA_HEADS = 16
MLA_Q_RANK = 256
MLA_KV_RANK = 128
MLA_NOPE = 64
MLA_ROPE = 32
MLA_V = 64
MLA_SCALE = (MLA_NOPE + MLA_ROPE) ** -0.5

SGU_CHUNK = 128
SGU_WIDTH = 2 * D_MODEL
SGU_GROUPS = 8
SGU_GDIM = SGU_WIDTH // SGU_GROUPS

kernel_name = 'hybrid_streaming_encoder_step'


def rmsnorm(x, g):
    xf = x.astype(jnp.float32)
    y = xf * lax.rsqrt(jnp.mean(xf * xf, axis=-1, keepdims=True) + NORM_EPS)
    return (y * g.astype(jnp.float32)).astype(x.dtype)


def modulate(x, g, shift, scale):
    return rmsnorm(x, g) * (1.0 + scale[:, None, :]) + shift[:, None, :]


def rope(x, pos):
    half = x.shape[-1] // 2
    inv_freq = ROPE_THETA ** (-jnp.arange(half, dtype=jnp.float32) / half)
    ang = pos.astype(jnp.float32)[:, None] * inv_freq[None, :]
    cos = jnp.cos(ang)[:, None, :]
    sin = jnp.sin(ang)[:, None, :]
    xf = x.astype(jnp.float32)
    x1, x2 = xf[..., :half], xf[..., half:]
    return jnp.concatenate([x1 * cos - x2 * sin, x1 * sin + x2 * cos], axis=-1).astype(x.dtype)


def block_causal_sweep(fn, q_parts, seq_len):
    nb = seq_len // Q_BLOCK
    key_chunk = jnp.arange(seq_len) // CHUNK

    def to_blocks(a):
        return jnp.moveaxis(a.reshape((a.shape[0], nb, Q_BLOCK) + a.shape[2:]), 1, 0)

    def body(args):
        j, qs = args
        q_chunk = (j * Q_BLOCK + jnp.arange(Q_BLOCK)) // CHUNK
        mask = key_chunk[None, :] <= q_chunk[:, None]
        return fn(qs, mask)

    out = lax.map(body, (jnp.arange(nb), tuple(to_blocks(a) for a in q_parts)))
    out = jnp.moveaxis(out, 0, 1)
    return out.reshape((out.shape[0], seq_len) + out.shape[3:])


def _lin_combine(e1, e2):
    a1, b1 = e1
    a2, b2 = e2
    return a1 * a2, a2 * b1 + b2


def s5_mixer(h, h0_re, h0_im, a_re, a_im, b_re, b_im, c_re, c_im, d, log_dt, w_glu_a, w_glu_b):
    B, L, _ = h.shape
    f32 = jnp.float32
    lam = lax.complex(a_re.astype(f32), a_im.astype(f32))
    dt = jnp.exp(log_dt.astype(f32))[:, None]
    a_bar = jnp.exp(lam * dt)
    b_bar = ((a_bar - 1.0) / lam)[..., None] * lax.complex(b_re.astype(f32), b_im.astype(f32))
    c_mat = lax.complex(c_re.astype(f32), c_im.astype(f32))
    u = h.astype(f32).reshape(B, L, S5_GROUPS, S5_GROUP)
    if h0_re is None:
        h0 = jnp.zeros((B, S5_GROUPS, S5_STATE), jnp.complex64)
    else:
        h0 = lax.complex(h0_re.astype(f32), h0_im.astype(f32))
    T = S5_SCAN_BLOCK if L % S5_SCAN_BLOCK == 0 else L
    ub = jnp.moveaxis(u.reshape(B, L // T, T, S5_GROUPS, S5_GROUP), 1, 0)

    def block(hc, u_blk):
        bu = jnp.einsum('gpc,btgc->btgp', b_bar, u_blk.astype(jnp.complex64))
        bu = bu.at[:, 0].add(a_bar * hc)
        a = jnp.broadcast_to(a_bar, bu.shape)
        _, hs = lax.associative_scan(_lin_combine, (a, bu), axis=1)
        y = jnp.real(jnp.einsum('gcp,btgp->btgc', c_mat, hs))
        return hs[:, -1], y

    h_last, ys = lax.scan(block, h0, ub)
    y = jnp.moveaxis(ys, 0, 1).reshape(B, L, D_MODEL) + d.astype(f32) * h.astype(f32)
    z = jax.nn.gelu(y).astype(h.dtype)
    out = (z @ w_glu_a) * jax.nn.sigmoid(z @ w_glu_b)
    return out, jnp.real(h_last).astype(h.dtype), jnp.imag(h_last).astype(h.dtype)


def diff_core(q, k, v, lam, mask):
    s = jnp.einsum('bqhcd,bkhcd->bhcqk', q, k, preferred_element_type=jnp.float32) * (DIFF_DH ** -0.5)
    if mask is not None:
        s = jnp.where(mask, s, NEG_INF)
    p = jax.nn.softmax(s, axis=-1)
    a = p[:, :, 0] - lam * p[:, :, 1]
    return jnp.einsum('bhqk,bkhe->bqhe', a, v).astype(v.dtype)


def diff_attn_mixer(h, pos, cache_k, cache_v, w_qkv, lq1, lk1, lq2, lk2, g_sub, w_o):
    B, L, _ = h.shape
    q, k, v = jnp.split(h @ w_qkv, 3, axis=-1)
    q = rope(q.reshape(B, L, 2 * DIFF_HEADS, DIFF_DH), pos).reshape(B, L, DIFF_HEADS, 2, DIFF_DH)
    k = rope(k.reshape(B, L, 2 * DIFF_HEADS, DIFF_DH), pos).reshape(B, L, DIFF_HEADS, 2, DIFF_DH)
    v = v.reshape(B, L, DIFF_HEADS, 2 * DIFF_DH)
    f32 = jnp.float32
    lam = (jnp.exp(jnp.sum(lq1.astype(f32) * lk1.astype(f32)))
           - jnp.exp(jnp.sum(lq2.astype(f32) * lk2.astype(f32))) + LAMBDA_INIT)
    if cache_k is None:
        o = block_causal_sweep(lambda qs, m: diff_core(qs[0], k, v, lam, m), (q,), L)
    else:
        P = cache_k.shape[1]
        k_all = jnp.concatenate([cache_k.reshape(B, P, DIFF_HEADS, 2, DIFF_DH), k], axis=1)
        v_all = jnp.concatenate([cache_v, v], axis=1)
        o = diff_core(q, k_all, v_all, lam, None)
    o = rmsnorm(o, g_sub) * (1.0 - LAMBDA_INIT)
    out = o.reshape(B, L, D_MODEL) @ w_o
    return out, k.reshape(B, L, DIFF_HEADS, 2 * DIFF_DH), v


def mla_core(q_nope, q_rope, k_nope, k_rope, v, mask):
    s = (jnp.einsum('bqhn,bkhn->bhqk', q_nope, k_nope, preferred_element_type=jnp.float32)
         + jnp.einsum('bqhe,bke->bhqk', q_rope, k_rope, preferred_element_type=jnp.float32)) * MLA_SCALE
    if mask is not None:
        s = jnp.where(mask, s, NEG_INF)
    p = jax.nn.softmax(s, axis=-1)
    return jnp.einsum('bhqk,bkhv->bqhv', p, v).astype(v.dtype)


def mla_mixer(h, pos, cache_ckv, cache_krope, w_dq, g_q, w_uq, w_dkv, g_kv, w_uk, w_uv, w_o):
    B, L, _ = h.shape
    q = (rmsnorm(h @ w_dq, g_q) @ w_uq).reshape(B, L, MLA_HEADS, MLA_NOPE + MLA_ROPE)
    q_nope = q[..., :MLA_NOPE]
    q_rope = rope(q[..., MLA_NOPE:], pos)
    kv = h @ w_dkv
    ckv = rmsnorm(kv[..., :MLA_KV_RANK], g_kv)
    krope = rope(kv[..., None, MLA_KV_RANK:], pos)[:, :, 0]
    if cache_ckv is None:
        ckv_all, krope_all = ckv, krope
    else:
        ckv_all = jnp.concatenate([cache_ckv, ckv], axis=1)
        krope_all = jnp.concatenate([cache_krope, krope], axis=1)
    k_nope = jnp.einsum('bkr,rhn->bkhn', ckv_all, w_uk)
    v = jnp.einsum('bkr,rhv->bkhv', ckv_all, w_uv)
    if cache_ckv is None:
        o = block_causal_sweep(lambda qs, m: mla_core(qs[0], qs[1], k_nope, krope_all, v, m),
                               (q_nope, q_rope), L)
    else:
        o = mla_core(q_nope, q_rope, k_nope, krope_all, v, None)
    out = o.reshape(B, L, MLA_HEADS * MLA_V) @ w_o
    return out, ckv, krope


def sgu_mixer(h, w_in, g_v, w_s, b_s, w_out):
    B, L, _ = h.shape
    u, v = jnp.split(jax.nn.gelu(h @ w_in), 2, axis=-1)
    v = rmsnorm(v, g_v)
    T = min(L, SGU_CHUNK)
    vb = v.reshape(B, L // T, T, SGU_GROUPS, SGU_GDIM)
    w = jnp.tril(w_s[:, :T, :T])
    sv = jnp.einsum('gts,bnsgd->bntgd', w, vb) + b_s[:, :T].T[:, :, None]
    out = (u * sv.reshape(B, L, SGU_WIDTH)) @ w_out
    return out, v


def sq_relu_mlp(h, w_up, w_down):
    return jnp.square(jax.nn.relu(h @ w_up)) @ w_down


def run_trunk(x, c, pos, past, p):
    new = {}
    cond = jax.nn.silu(c)
    for i in range(DEPTH):
        mod = cond @ p['w_ada'][i] + p['b_ada'][i]
        sh1, sc1, gt1, sh2, sc2, gt2 = jnp.split(mod, 6, axis=-1)
        h = modulate(x, p['g_mix'][i], sh1, sc1)
        kind = i % N_MIXERS
        if kind == 0:
            h0_re = None if past is None else past['s5_re']
            h0_im = None if past is None else past['s5_im']
            out, new['s5_re'], new['s5_im'] = s5_mixer(
                h, h0_re, h0_im, p['s5_a_re'], p['s5_a_im'], p['s5_b_re'], p['s5_b_im'],
                p['s5_c_re'], p['s5_c_im'], p['s5_d'], p['s5_log_dt'], p['s5_w_glu_a'], p['s5_w_glu_b'])
        elif kind == 1:
            ck = None if past is None else past['diff_k']
            cv = None if past is None else past['diff_v']
            out, new['diff_k'], new['diff_v'] = diff_attn_mixer(
                h, pos, ck, cv, p['diff_w_qkv'], p['diff_lambda_q1'], p['diff_lambda_k1'],
                p['diff_lambda_q2'], p['diff_lambda_k2'], p['diff_g_sub'], p['diff_w_o'])
        elif kind == 2:
            cc = None if past is None else past['mla_ckv']
            cr = None if past is None else past['mla_krope']
            out, new['mla_ckv'], new['mla_krope'] = mla_mixer(
                h, pos, cc, cr, p['mla_w_dq'], p['mla_g_q'], p['mla_w_uq'], p['mla_w_dkv'],
                p['mla_g_kv'], p['mla_w_uk'], p['mla_w_uv'], p['mla_w_o'])
        else:
            out, new['sgu_v'] = sgu_mixer(h, p['sgu_w_in'], p['sgu_g_v'], p['sgu_w_s'],
                                          p['sgu_b_s'], p['sgu_w_out'])
        x = x + (1.0 + gt1)[:, None, :] * out
        h = modulate(x, p['g_ffn'][i], sh2, sc2)
        x = x + (1.0 + gt2)[:, None, :] * sq_relu_mlp(h, p['w_up'][i], p['w_down'][i])
    return rmsnorm(x, p['g_final']), new


def setup_inputs(seed: int = 0) -> dict:
    key = jax.random.key(seed)
    keys = list(jax.random.split(key, 64))
    f32 = jnp.float32
    counter = [0]

    def nxt():
        k = keys[counter[0]]
        counter[0] += 1
        return k

    def nrm(shape, scale=1.0):
        return jax.random.normal(nxt(), shape, f32) * scale

    D = D_MODEL
    G, P, C = S5_GROUPS, S5_STATE, S5_GROUP
    n_idx = jnp.arange(P, dtype=f32)[None, :]
    out = {}
    out['x_prompt'] = nrm((BATCH, SEQ, D))
    out['x_sample'] = nrm((DEC_BATCH, DEC_SEQ, D))
    out['c_prompt'] = nrm((BATCH, D))
    out['c_sample'] = nrm((DEC_BATCH, D))
    out['state_s5_re'] = nrm((DEC_BATCH, G, P), 0.3)
    out['state_s5_im'] = nrm((DEC_BATCH, G, P), 0.3)
    out['cache_diff_k'] = nrm((DEC_BATCH, PAST_LEN, DIFF_HEADS, 2 * DIFF_DH))
    out['cache_diff_v'] = nrm((DEC_BATCH, PAST_LEN, DIFF_HEADS, 2 * DIFF_DH))
    out['cache_mla_ckv'] = nrm((DEC_BATCH, PAST_LEN, MLA_KV_RANK))
    out['cache_mla_krope'] = nrm((DEC_BATCH, PAST_LEN, MLA_ROPE))
    out['w_ada'] = nrm((DEPTH, D, 6 * D), 0.1 * D ** -0.5)
    out['b_ada'] = nrm((DEPTH, 6 * D), 0.01)
    out['g_mix'] = 1.0 + nrm((DEPTH, D), 0.05)
    out['g_ffn'] = 1.0 + nrm((DEPTH, D), 0.05)
    out['w_up'] = nrm((DEPTH, D, D_FF), D ** -0.5)
    out['w_down'] = nrm((DEPTH, D_FF, D), D_FF ** -0.5)
    out['g_final'] = 1.0 + nrm((D,), 0.05)
    out['s5_a_re'] = -0.5 + nrm((G, P), 0.01)
    out['s5_a_im'] = math.pi * n_idx + nrm((G, P), 0.01)
    out['s5_b_re'] = nrm((G, P, C), (2 * C) ** -0.5)
    out['s5_b_im'] = nrm((G, P, C), (2 * C) ** -0.5)
    out['s5_c_re'] = nrm((G, C, P), (2 * P) ** -0.5)
    out['s5_c_im'] = nrm((G, C, P), (2 * P) ** -0.5)
    out['s5_d'] = nrm((D,))
    out['s5_log_dt'] = jax.random.uniform(nxt(), (G,), f32, math.log(S5_DT_MIN), math.log(S5_DT_MAX))
    out['s5_w_glu_a'] = nrm((D, D), D ** -0.5)
    out['s5_w_glu_b'] = nrm((D, D), D ** -0.5)
    out['diff_w_qkv'] = nrm((D, 3 * D), D ** -0.5)
    out['diff_lambda_q1'] = nrm((DIFF_DH,), 0.1)
    out['diff_lambda_k1'] = nrm((DIFF_DH,), 0.1)
    out['diff_lambda_q2'] = nrm((DIFF_DH,), 0.1)
    out['diff_lambda_k2'] = nrm((DIFF_DH,), 0.1)
    out['diff_g_sub'] = 1.0 + nrm((2 * DIFF_DH,), 0.05)
    out['diff_w_o'] = nrm((D, D), D ** -0.5)
    out['mla_w_dq'] = nrm((D, MLA_Q_RANK), D ** -0.5)
    out['mla_g_q'] = 1.0 + nrm((MLA_Q_RANK,), 0.05)
    out['mla_w_uq'] = nrm((MLA_Q_RANK, MLA_HEADS * (MLA_NOPE + MLA_ROPE)), MLA_Q_RANK ** -0.5)
    out['mla_w_dkv'] = nrm((D, MLA_KV_RANK + MLA_ROPE), D ** -0.5)
    out['mla_g_kv'] = 1.0 + nrm((MLA_KV_RANK,), 0.05)
    out['mla_w_uk'] = nrm((MLA_KV_RANK, MLA_HEADS, MLA_NOPE), MLA_KV_RANK ** -0.5)
    out['mla_w_uv'] = nrm((MLA_KV_RANK, MLA_HEADS, MLA_V), MLA_KV_RANK ** -0.5)
    out['mla_w_o'] = nrm((MLA_HEADS * MLA_V, D), (MLA_HEADS * MLA_V) ** -0.5)
    out['sgu_w_in'] = nrm((D, 2 * SGU_WIDTH), D ** -0.5)
    out['sgu_g_v'] = 1.0 + nrm((SGU_WIDTH,), 0.05)
    out['sgu_w_s'] = nrm((SGU_GROUPS, SGU_CHUNK, SGU_CHUNK), SGU_CHUNK ** -0.5)
    out['sgu_b_s'] = 1.0 + nrm((SGU_GROUPS, SGU_CHUNK), 0.05)
    out['sgu_w_out'] = nrm((SGU_WIDTH, D), SGU_WIDTH ** -0.5)
    return out


def reference(x_prompt, x_sample, c_prompt, c_sample, state_s5_re, state_s5_im,
              cache_diff_k, cache_diff_v, cache_mla_ckv, cache_mla_krope,
              w_ada, b_ada, g_mix, g_ffn, w_up, w_down, g_final,
              s5_a_re, s5_a_im, s5_b_re, s5_b_im, s5_c_re, s5_c_im, s5_d, s5_log_dt,
              s5_w_glu_a, s5_w_glu_b,
              diff_w_qkv, diff_lambda_q1, diff_lambda_k1, diff_lambda_q2, diff_lambda_k2,
              diff_g_sub, diff_w_o,
              mla_w_dq, mla_g_q, mla_w_uq, mla_w_dkv, mla_g_kv, mla_w_uk, mla_w_uv, mla_w_o,
              sgu_w_in, sgu_g_v, sgu_w_s, sgu_b_s, sgu_w_out):
    params = {
        'w_ada': w_ada, 'b_ada': b_ada, 'g_mix': g_mix, 'g_ffn': g_ffn,
        'w_up': w_up, 'w_down': w_down, 'g_final': g_final,
        's5_a_re': s5_a_re, 's5_a_im': s5_a_im, 's5_b_re': s5_b_re, 's5_b_im': s5_b_im,
        's5_c_re': s5_c_re, 's5_c_im': s5_c_im, 's5_d': s5_d, 's5_log_dt': s5_log_dt,
        's5_w_glu_a': s5_w_glu_a, 's5_w_glu_b': s5_w_glu_b,
        'diff_w_qkv': diff_w_qkv, 'diff_lambda_q1': diff_lambda_q1, 'diff_lambda_k1': diff_lambda_k1,
        'diff_lambda_q2': diff_lambda_q2, 'diff_lambda_k2': diff_lambda_k2,
        'diff_g_sub': diff_g_sub, 'diff_w_o': diff_w_o,
        'mla_w_dq': mla_w_dq, 'mla_g_q': mla_g_q, 'mla_w_uq': mla_w_uq, 'mla_w_dkv': mla_w_dkv,
        'mla_g_kv': mla_g_kv, 'mla_w_uk': mla_w_uk, 'mla_w_uv': mla_w_uv, 'mla_w_o': mla_w_o,
        'sgu_w_in': sgu_w_in, 'sgu_g_v': sgu_g_v, 'sgu_w_s': sgu_w_s, 'sgu_b_s': sgu_b_s,
        'sgu_w_out': sgu_w_out,
    }
    past = {
        's5_re': state_s5_re, 's5_im': state_s5_im,
        'diff_k': cache_diff_k, 'diff_v': cache_diff_v,
        'mla_ckv': cache_mla_ckv, 'mla_krope': cache_mla_krope,
    }
    pos_p = jnp.arange(x_prompt.shape[1], dtype=jnp.int32)
    pos_s = cache_diff_k.shape[1] + jnp.arange(x_sample.shape[1], dtype=jnp.int32)
    y_prompt, sp = run_trunk(x_prompt, c_prompt, pos_p, None, params)
    y_sample, ss = run_trunk(x_sample, c_sample, pos_s, past, params)
    return (y_prompt, y_sample,
            sp['s5_re'], sp['s5_im'], ss['s5_re'], ss['s5_im'],
            sp['diff_k'], sp['diff_v'], ss['diff_k'], ss['diff_v'],
            sp['mla_ckv'], sp['mla_krope'], ss['mla_ckv'], ss['mla_krope'],
            ss['sgu_v'])
```

```python
import functools
import math

import jax
import jax.numpy as jnp
from jax import lax
from jax.experimental import pallas as pl
from jax.experimental.pallas import tpu as pltpu

F32 = jnp.float32
BF16 = jnp.bfloat16
HIGHEST = lax.Precision.HIGHEST

NORM_EPS = 1e-6
ROPE_THETA = 10000.0
NEG_INF = -1e30
CHUNK = 64
N_MIXERS = 4

S5_GROUP = 16
S5_SUB = 16

DIFF_HEADS = 8
DIFF_DH = 64
LAMBDA_INIT = 0.8 - 0.6 * math.exp(-0.3 * 1)

MLA_HEADS = 16
MLA_NOPE = 64
MLA_ROPE = 32
MLA_V = 64
MLA_SCALE = (MLA_NOPE + MLA_ROPE) ** -0.5
MLA_HEAD_PAD = 128

SGU_CHUNK = 128
SGU_GROUPS = 8

LANES = 128
ROW_TILE = 512
ATTN_TQ = 256
ATTN_TK = 256
VMEM_LIMIT_BYTES = 48 * 1024 * 1024


def _params(*sem):
    return pltpu.CompilerParams(dimension_semantics=sem, vmem_limit_bytes=VMEM_LIMIT_BYTES)


def _row_tiles(batch, length, rows=ROW_TILE):
    if length >= rows:
        assert length % rows == 0
        return 1, rows
    nb = max(1, min(batch, rows // length))
    while batch % nb:
        nb -= 1
    return nb, length


def _rms(x, g):
    return x * lax.rsqrt(jnp.mean(x * x, axis=-1, keepdims=True) + NORM_EPS) * g


def _modulated(x_ref, g_ref, sh_ref, sc_ref):
    return _rms(x_ref[...], g_ref[...]) * (1.0 + sc_ref[...]) + sh_ref[...]


def _rotate_pairs(x, half, period, lo):
    width = x.shape[-1]
    lane = lax.broadcasted_iota(jnp.int32, x.shape, x.ndim - 1) % period
    fwd = pltpu.roll(x, width - half, axis=x.ndim - 1)
    bwd = pltpu.roll(x, half, axis=x.ndim - 1)
    return jnp.where((lane >= lo) & (lane < lo + half), fwd, bwd)


def _adaln_kernel(c_ref, w_ref, b_ref, o_ref):
    c = c_ref[...]
    s = c * jax.nn.sigmoid(c)
    o_ref[0] = jnp.dot(s, w_ref[0], precision=HIGHEST, preferred_element_type=F32) + b_ref[0]


def _adaln(c_all, w_ada, b_ada):
    depth, d, n = w_ada.shape
    rows = c_all.shape[0]
    tn = n // 4
    return pl.pallas_call(
        _adaln_kernel,
        grid=(depth, n // tn),
        in_specs=[pl.BlockSpec((rows, d), lambda i, j: (0, 0)),
                  pl.BlockSpec((1, d, tn), lambda i, j: (i, 0, j)),
                  pl.BlockSpec((1, 1, tn), lambda i, j: (i, 0, j))],
        out_specs=pl.BlockSpec((1, rows, tn), lambda i, j: (i, 0, j)),
        out_shape=jax.ShapeDtypeStruct((depth, rows, n), F32),
        compiler_params=_params("parallel", "parallel"),
        name="adaln",
    )(c_all, w_ada, b_ada.reshape(depth, 1, n))


def _tile_specs(batch, length, d, nb, lt, extra_axes=0):
    if extra_axes == 0:
        act = lambda width: pl.BlockSpec((nb, lt, width), lambda i, j: (i, j, 0))
        vec = lambda width: pl.BlockSpec((nb, 1, width), lambda i, j: (i, 0, 0))
        par = lambda shape: pl.BlockSpec(shape, lambda i, j: (0,) * len(shape))
        pos = lambda width: pl.BlockSpec((lt, width), lambda i, j: (j, 0))
    else:
        act = lambda width: pl.BlockSpec((nb, lt, width), lambda i, j, f: (i, j, 0))
        vec = lambda width: pl.BlockSpec((nb, 1, width), lambda i, j, f: (i, 0, 0))
        par = lambda shape: pl.BlockSpec(shape, lambda i, j, f: (0,) * len(shape))
        pos = lambda width: pl.BlockSpec((lt, width), lambda i, j, f: (j, 0))
    return act, vec, par, pos


def _modulate_kernel(x_ref, g_ref, sh_ref, sc_ref, o_ref):
    o_ref[...] = _modulated(x_ref, g_ref, sh_ref, sc_ref)


def _modulate(x, g, sh, sc):
    b, l, d = x.shape
    nb, lt = _row_tiles(b, l)
    act, vec, par, _ = _tile_specs(b, l, d, nb, lt)
    return pl.pallas_call(
        _modulate_kernel,
        grid=(b // nb, l // lt),
        in_specs=[act(d), par((1, d)), vec(d), vec(d)],
        out_specs=act(d),
        out_shape=jax.ShapeDtypeStruct((b, l, d), F32),
        compiler_params=_params("parallel", "parallel"),
        name="modulate",
    )(x, g, sh, sc)


def _mlp_kernel(x_ref, g_ref, sh_ref, sc_ref, gt_ref, wu_ref, wd_ref, gf_ref, o_ref, h_s, acc_s,
                *, final_norm):
    f = pl.program_id(2)
    nb, lt, d = x_ref.shape

    @pl.when(f == 0)
    def _():
        h_s[...] = _modulated(x_ref, g_ref, sh_ref, sc_ref).reshape(nb * lt, d).astype(BF16)
        acc_s[...] = jnp.zeros_like(acc_s)

    a = jnp.dot(h_s[...], wu_ref[...], preferred_element_type=F32)
    a = jnp.square(jnp.maximum(a, 0.0)).astype(BF16)
    acc_s[...] += jnp.dot(a, wd_ref[...], preferred_element_type=F32)

    @pl.when(f == pl.num_programs(2) - 1)
    def _():
        y = x_ref[...] + (1.0 + gt_ref[...]) * acc_s[...].reshape(nb, lt, d)
        if final_norm:
            y = _rms(y, gf_ref[...])
        o_ref[...] = y


def _mlp(x, g, sh, sc, gt, w_up, w_down, g_final, final_norm, tf=1024):
    b, l, d = x.shape
    dff = w_up.shape[1]
    nb, lt = _row_tiles(b, l)
    act, vec, par, _ = _tile_specs(b, l, d, nb, lt, extra_axes=1)
    return pl.pallas_call(
        functools.partial(_mlp_kernel, final_norm=final_norm),
        grid=(b // nb, l // lt, dff // tf),
        in_specs=[act(d), par((1, d)), vec(d), vec(d), vec(d),
                  pl.BlockSpec((d, tf), lambda i, j, f: (0, f)),
                  pl.BlockSpec((tf, d), lambda i, j, f: (f, 0)),
                  par((1, d))],
        out_specs=act(d),
        out_shape=jax.ShapeDtypeStruct((b, l, d), F32),
        scratch_shapes=[pltpu.VMEM((nb * lt, d), BF16), pltpu.VMEM((nb * lt, d), F32)],
        compiler_params=_params("parallel", "parallel", "arbitrary"),
        name="mlp",
    )(x, g, sh, sc, gt, w_up, w_down, g_final)


def _proj_res_kernel(a_ref, w_ref, x_ref, gt_ref, o_ref):
    nb, lt, k = a_ref.shape
    d = w_ref.shape[1]
    r = jnp.dot(a_ref[...].reshape(nb * lt, k), w_ref[...], preferred_element_type=F32)
    o_ref[...] = x_ref[...] + (1.0 + gt_ref[...]) * r.reshape(nb, lt, d)


def _proj_residual(a, w, x, gt):
    b, l, d = x.shape
    k = a.shape[-1]
    nb, lt = _row_tiles(b, l)
    act, vec, par, _ = _tile_specs(b, l, d, nb, lt)
    return pl.pallas_call(
        _proj_res_kernel,
        grid=(b // nb, l // lt),
        in_specs=[act(k), par((k, d)), act(d), vec(d)],
        out_specs=act(d),
        out_shape=jax.ShapeDtypeStruct((b, l, d), F32),
        compiler_params=_params("parallel", "parallel"),
        name="proj_residual",
    )(a, w, x, gt)


def _s5_kernel(*refs, nch, batch, nsteps, has_h0):
    if has_h0:
        u_ref, bc_ref, m_ref, cc_ref, a1_ref, a2_ref, h0_ref, y_ref, hl_ref = refs
    else:
        u_ref, bc_ref, m_ref, cc_ref, a1_ref, a2_ref, y_ref, hl_ref = refs
    u = u_ref[0]
    rows = u.shape[0]
    x = jnp.dot(u, bc_ref[0], precision=HIGHEST, preferred_element_type=F32)
    half = x.shape[1] // 2
    if has_h0:
        h0 = h0_ref[0]
        x = x + a1_ref[0, 0:1, :] * h0 + a2_ref[0, 0:1, :] * pltpu.roll(h0, half, axis=1)
    kidx = lax.broadcasted_iota(jnp.int32, x.shape, 0) & (nch - 1)
    for j in range(nsteps):
        s = 1 << j
        xs = jnp.where(kidx >= s, pltpu.roll(x, s, axis=0), 0.0)
        x = x + a1_ref[0, j:j + 1, :] * xs + a2_ref[0, j:j + 1, :] * pltpu.roll(xs, half, axis=1)
    for b in range(batch):
        r = (b + 1) * nch - 1
        hl_ref[0, b:b + 1, :] = x[r:r + 1, :]
    hstart = jnp.where(kidx >= 1, pltpu.roll(x, 1, axis=0), 0.0)
    if has_h0:
        hstart = hstart + h0
    y = jnp.dot(u.astype(BF16), m_ref[0], preferred_element_type=F32)
    y = y + jnp.dot(hstart.astype(BF16), cc_ref[0], preferred_element_type=F32)
    y_ref[0] = y


def _s5_tables(a_re, a_im, b_re, b_im, c_re, c_im, log_dt, max_steps):
    g, p = a_re.shape
    sub = S5_SUB
    dt = jnp.exp(log_dt.astype(F32))[:, None]
    zr, zi = a_re.astype(F32) * dt, a_im.astype(F32) * dt

    def zpow(n):
        mag = jnp.exp(zr * n)
        return mag * jnp.cos(zi * n), mag * jnp.sin(zi * n)

    er = jnp.expm1(zr) * jnp.cos(zi) - 2.0 * jnp.square(jnp.sin(0.5 * zi))
    ei = jnp.exp(zr) * jnp.sin(zi)
    den = a_re * a_re + a_im * a_im
    fr = (er * a_re + ei * a_im) / den
    fi = (ei * a_re - er * a_im) / den
    bb_re = fr[..., None] * b_re - fi[..., None] * b_im
    bb_im = fr[..., None] * b_im + fi[..., None] * b_re

    lags = jnp.arange(sub + 1, dtype=F32)[:, None, None]
    pr, pi = zpow(lags)

    qr, qi = pr[:sub][::-1], pi[:sub][::-1]
    bcr = qr[..., None] * bb_re[None] - qi[..., None] * bb_im[None]
    bci = qr[..., None] * bb_im[None] + qi[..., None] * bb_re[None]
    bc = jnp.concatenate([bcr, bci], axis=2)
    bc = bc.transpose(1, 0, 3, 2).reshape(g, sub * S5_GROUP, 2 * p)

    ar, ai = pr[1:], pi[1:]
    ccr = c_re[None] * ar[:, :, None, :] - c_im[None] * ai[:, :, None, :]
    cci = c_re[None] * ai[:, :, None, :] + c_im[None] * ar[:, :, None, :]
    cc = jnp.concatenate([ccr, -cci], axis=3)
    cc = cc.transpose(1, 3, 0, 2).reshape(g, 2 * p, sub * S5_GROUP)

    kr = (jnp.einsum('gcp,lgp,gpd->lgcd', c_re, pr[:sub], bb_re, precision=HIGHEST)
          - jnp.einsum('gcp,lgp,gpd->lgcd', c_re, pi[:sub], bb_im, precision=HIGHEST)
          - jnp.einsum('gcp,lgp,gpd->lgcd', c_im, pr[:sub], bb_im, precision=HIGHEST)
          - jnp.einsum('gcp,lgp,gpd->lgcd', c_im, pi[:sub], bb_re, precision=HIGHEST))
    tt = jnp.arange(sub)
    lag = tt[None, :] - tt[:, None]
    toep = jnp.where((lag >= 0)[:, :, None, None, None],
                     kr[jnp.clip(lag, 0, sub - 1)], 0.0)
    m = toep.transpose(2, 0, 4, 1, 3).reshape(g, sub * S5_GROUP, sub * S5_GROUP)

    steps = (sub * (2 ** jnp.arange(max_steps))).astype(F32)[:, None, None]
    sr, si = zpow(steps)
    a1 = jnp.concatenate([sr, sr], axis=-1).transpose(1, 0, 2)
    a2 = jnp.concatenate([-si, si], axis=-1).transpose(1, 0, 2)
    pad = (-max_steps) % 8
    a1 = jnp.pad(a1, ((0, 0), (0, pad), (0, 0)))
    a2 = jnp.pad(a2, ((0, 0), (0, pad), (0, 0)))
    return bc, m.astype(BF16), cc.astype(BF16), a1, a2


def _s5_scan(h, tables, h0_re, h0_im):
    bc, m, cc, a1, a2 = tables
    b, l, d = h.shape
    g = d // S5_GROUP
    p2 = bc.shape[-1]
    nch = l // S5_SUB
    assert nch & (nch - 1) == 0
    nsteps = nch.bit_length() - 1
    rows = b * nch
    k = S5_SUB * S5_GROUP
    u = h.reshape(b, nch, S5_SUB, g, S5_GROUP).transpose(3, 0, 1, 2, 4).reshape(g, rows, k)
    has_h0 = h0_re is not None
    grp = lambda shape: pl.BlockSpec((1,) + shape, lambda i: (i, 0, 0))
    args = [u, bc, m, cc, a1, a2]
    in_specs = [grp((rows, k)), grp((k, p2)), grp((k, k)), grp((p2, k)),
                grp(a1.shape[1:]), grp(a2.shape[1:])]
    if has_h0:
        h0 = jnp.concatenate([h0_re, h0_im], axis=-1).astype(F32).transpose(1, 0, 2)
        h0 = jnp.pad(h0[:, :, None, :], ((0, 0), (0, 0), (0, nch - 1), (0, 0))).reshape(g, rows, p2)
        args.append(h0)
        in_specs.append(grp((rows, p2)))
    y, hl = pl.pallas_call(
        functools.partial(_s5_kernel, nch=nch, batch=b, nsteps=nsteps, has_h0=has_h0),
        grid=(g,),
        in_specs=in_specs,
        out_specs=[grp((rows, k)), grp((b, p2))],
        out_shape=[jax.ShapeDtypeStruct((g, rows, k), F32), jax.ShapeDtypeStruct((g, b, p2), F32)],
        compiler_params=_params("parallel"),
        name="s5_scan",
    )(*args)
    y = y.reshape(g, b, nch, S5_SUB, S5_GROUP).transpose(1, 2, 3, 0, 4).reshape(b, l, d)
    hl = hl.transpose(1, 0, 2)
    return y, hl[..., :p2 // 2], hl[..., p2 // 2:]


def _glu_kernel(y_ref, h_ref, x_ref, d_ref, gt_ref, wa_ref, wb_ref, o_ref):
    nb, lt, d = x_ref.shape
    z = jax.nn.gelu(y_ref[...] + d_ref[...] * h_ref[...]).reshape(nb * lt, d).astype(BF16)
    a = jnp.dot(z, wa_ref[...], preferred_element_type=F32)
    b = jnp.dot(z, wb_ref[...], preferred_element_type=F32)
    out = (a * jax.nn.sigmoid(b)).reshape(nb, lt, d)
    o_ref[...] = x_ref[...] + (1.0 + gt_ref[...]) * out


def _s5_glu(y, h, x, d_skip, gt, wa, wb):
    b, l, d = x.shape
    nb, lt = _row_tiles(b, l)
    act, vec, par, _ = _tile_specs(b, l, d, nb, lt)
    return pl.pallas_call(
        _glu_kernel,
        grid=(b // nb, l // lt),
        in_specs=[act(d), act(d), act(d), par((1, d)), vec(d), par((d, d)), par((d, d))],
        out_specs=act(d),
        out_shape=jax.ShapeDtypeStruct((b, l, d), F32),
        compiler_params=_params("parallel", "parallel"),
        name="s5_glu",
    )(y, h, x, d_skip, gt, wa, wb)


def _diff_qkv_kernel(x_ref, g_ref, sh_ref, sc_ref, w_ref, cos_ref, sin_ref,
                     k_ref, v_ref, qb_ref, kb_ref, vb_ref):
    nb, lt, d = x_ref.shape
    h = _modulated(x_ref, g_ref, sh_ref, sc_ref).reshape(nb * lt, d).astype(BF16)
    reps = d // cos_ref.shape[1]
    cos = jnp.tile(cos_ref[...], (1, reps))[None]
    sin = jnp.tile(sin_ref[...], (1, reps))[None]

    def roped(cols):
        t = jnp.dot(h, w_ref[:, cols * d:(cols + 1) * d], preferred_element_type=F32)
        r = _rotate_pairs(t, DIFF_DH // 2, DIFF_DH, 0)
        return t.reshape(nb, lt, d) * cos + r.reshape(nb, lt, d) * sin

    q = roped(0)
    qb_ref[...] = (q * (DIFF_DH ** -0.5)).astype(BF16)
    k = roped(1)
    k_ref[...] = k
    kb_ref[...] = k.astype(BF16)
    v = jnp.dot(h, w_ref[:, 2 * d:], preferred_element_type=F32).reshape(nb, lt, d)
    v_ref[...] = v
    vb_ref[...] = v.astype(BF16)


def _diff_qkv(x, g, sh, sc, w_qkv, cos, sin):
    b, l, d = x.shape
    nb, lt = _row_tiles(b, l)
    act, vec, par, pos = _tile_specs(b, l, d, nb, lt)
    return pl.pallas_call(
        _diff_qkv_kernel,
        grid=(b // nb, l // lt),
        in_specs=[act(d), par((1, d)), vec(d), vec(d), par((d, 3 * d)),
                  pos(cos.shape[1]), pos(sin.shape[1])],
        out_specs=[act(d)] * 5,
        out_shape=[jax.ShapeDtypeStruct((b, l, d), F32)] * 2
                  + [jax.ShapeDtypeStruct((b, l, d), BF16)] * 3,
        compiler_params=_params("parallel", "parallel"),
        name="diff_qkv",
    )(x, g, sh, sc, w_qkv, cos, sin)


def _softmax_step(s, v, m_ref, l_ref, acc_ref):
    m_prev = m_ref[...]
    m_new = jnp.maximum(m_prev, jnp.max(s, axis=-1, keepdims=True))
    alpha = jnp.exp(m_prev - m_new)
    p = jnp.exp(s - m_new)
    l_ref[...] = alpha * l_ref[...] + jnp.sum(p, axis=-1, keepdims=True)
    acc_ref[...] = alpha * acc_ref[...] + jnp.dot(p.astype(BF16), v.astype(BF16),
                                                  preferred_element_type=F32)
    m_ref[...] = m_new


def _chunk_mask(rows, keys, row0, key0):
    rq = (lax.broadcasted_iota(jnp.int32, (rows, keys), 0) + row0) // CHUNK
    ck = (lax.broadcasted_iota(jnp.int32, (rows, keys), 1) + key0) // CHUNK
    return ck <= rq


def _kv_sweep(step, n_keys, tk, dynamic_full=None):
    if dynamic_full is not None:
        lax.fori_loop(0, dynamic_full, lambda j, c: (step(pl.multiple_of(j * tk, tk), tk), c)[1], 0)
        return
    n_full = n_keys // tk
    if n_full == 1:
        step(0, tk)
    elif n_full > 1:
        lax.fori_loop(0, n_full, lambda j, c: (step(pl.multiple_of(j * tk, tk), tk), c)[1], 0)
    if n_keys % tk:
        step(n_full * tk, n_keys % tk)


def _diff_attn_kernel(*refs, tq, tk, causal, n_src):
    q_ref = refs[0]
    kv_refs = refs[1:1 + 2 * n_src]
    lam_ref, gsub_ref, o_ref, m_s, l_s, acc_s = refs[1 + 2 * n_src:]
    qi = pl.program_id(2)
    q = q_ref[0]
    lane = lax.broadcasted_iota(jnp.int32, q.shape, 1)
    zero = jnp.zeros_like(q)
    q2 = jnp.concatenate([jnp.where(lane < DIFF_DH, q, zero), jnp.where(lane >= DIFF_DH, q, zero)], axis=0)
    m_s[...] = jnp.full_like(m_s, NEG_INF)
    l_s[...] = jnp.zeros_like(l_s)
    acc_s[...] = jnp.zeros_like(acc_s)

    def make_step(k_ref, v_ref, masked):
        def step(start, size):
            kb = k_ref[0, pl.ds(start, size), :].astype(BF16)
            s = lax.dot_general(q2, kb, (((1,), (1,)), ((), ())), preferred_element_type=F32)
            if masked:
                vis = _chunk_mask(tq, size, qi * tq, start)
                s = jnp.where(jnp.concatenate([vis, vis], axis=0), s, NEG_INF)
            _softmax_step(s, v_ref[0, pl.ds(start, size), :], m_s, l_s, acc_s)
        return step

    if causal:
        k_ref, v_ref = kv_refs
        _kv_sweep(make_step(k_ref, v_ref, False), None, tk, dynamic_full=qi * (tq // tk))
        for j in range(tq // tk):
            make_step(k_ref, v_ref, True)(pl.multiple_of(qi * tq + j * tk, tk), tk)
    else:
        for i in range(n_src):
            k_ref, v_ref = kv_refs[2 * i], kv_refs[2 * i + 1]
            _kv_sweep(make_step(k_ref, v_ref, False), k_ref.shape[1], tk)

    lp = lam_ref[...]
    lam = (jnp.exp(jnp.sum(lp[0:1] * lp[1:2], axis=-1, keepdims=True))
           - jnp.exp(jnp.sum(lp[2:3] * lp[3:4], axis=-1, keepdims=True)) + LAMBDA_INIT)
    o2 = acc_s[...] / l_s[...]
    o = o2[:tq] - lam * o2[tq:]
    o = _rms(o, gsub_ref[...]) * (1.0 - LAMBDA_INIT)
    o_ref[0] = o.astype(o_ref.dtype)


def _diff_attention(qb, kv_sources, lam_rows, g_sub, causal):
    b, l, d = qb.shape
    hw = 2 * DIFF_DH
    tq = min(ATTN_TQ, l)
    tk = min(ATTN_TK, tq) if causal else ATTN_TK
    in_specs = [pl.BlockSpec((1, tq, hw), lambda bi, h, qi: (bi, qi, h))]
    args = [qb]
    for k, v in kv_sources:
        lk = k.shape[1]
        in_specs += [pl.BlockSpec((1, lk, hw), lambda bi, h, qi: (bi, 0, h))] * 2
        args += [k, v]
    in_specs += [pl.BlockSpec(lam_rows.shape, lambda bi, h, qi: (0, 0)),
                 pl.BlockSpec((1, hw), lambda bi, h, qi: (0, 0))]
    args += [lam_rows, g_sub]
    return pl.pallas_call(
        functools.partial(_diff_attn_kernel, tq=tq, tk=tk, causal=causal, n_src=len(kv_sources)),
        grid=(b, d // hw, l // tq),
        in_specs=in_specs,
        out_specs=pl.BlockSpec((1, tq, hw), lambda bi, h, qi: (bi, qi, h)),
        out_shape=jax.ShapeDtypeStruct((b, l, d), BF16),
        scratch_shapes=[pltpu.VMEM((2 * tq, 1), F32), pltpu.VMEM((2 * tq, 1), F32),
                        pltpu.VMEM((2 * tq, hw), F32)],
        compiler_params=_params("parallel", "parallel", "arbitrary"),
        name="diff_attention",
    )(*args)


def _mla_down_kernel(x_ref, g_ref, sh_ref, sc_ref, wq_ref, wc_ref, wr_ref, wrr_ref,
                     gq_ref, gkv_ref, cos_ref, sin_ref, cq_ref, ckv_ref, kr_ref):
    nb, lt, d = x_ref.shape
    h = _modulated(x_ref, g_ref, sh_ref, sc_ref).reshape(nb * lt, d).astype(BF16)
    dot = lambda w_ref: jnp.dot(h, w_ref[...], preferred_element_type=F32)
    cq_ref[...] = _rms(dot(wq_ref), gq_ref[...]).reshape(cq_ref.shape).astype(cq_ref.dtype)
    ckv_ref[...] = _rms(dot(wc_ref), gkv_ref[...]).reshape(ckv_ref.shape)
    r = dot(wr_ref).reshape(kr_ref.shape)
    rr = dot(wrr_ref).reshape(kr_ref.shape)
    kr_ref[...] = r * cos_ref[...][None] + rr * sin_ref[...][None]


def _mla_down(x, g, sh, sc, wq, wc, wr, wrr, g_q, g_kv, cos, sin):
    b, l, d = x.shape
    nb, lt = _row_tiles(b, l)
    act, vec, par, pos = _tile_specs(b, l, d, nb, lt)
    qr, kvr, rope = wq.shape[1], wc.shape[1], wr.shape[1]
    return pl.pallas_call(
        _mla_down_kernel,
        grid=(b // nb, l // lt),
        in_specs=[act(d), par((1, d)), vec(d), vec(d),
                  par((d, qr)), par((d, kvr)), par((d, rope)), par((d, rope)),
                  par((1, qr)), par((1, kvr)), pos(rope), pos(rope)],
        out_specs=[act(qr), act(kvr), act(rope)],
        out_shape=[jax.ShapeDtypeStruct((b, l, qr), BF16),
                   jax.ShapeDtypeStruct((b, l, kvr), F32),
                   jax.ShapeDtypeStruct((b, l, rope), F32)],
        compiler_params=_params("parallel", "parallel"),
        name="mla_down",
    )(x, g, sh, sc, wq, wc, wr, wrr, g_q, g_kv, cos, sin)


def _mla_qup_kernel(cq_ref, w_ref, cos_ref, sin_ref, q_ref):
    nb, lt, r = cq_ref.shape
    n = w_ref.shape[1]
    t = jnp.dot(cq_ref[...].reshape(nb * lt, r), w_ref[...], preferred_element_type=F32)
    rot = _rotate_pairs(t, MLA_ROPE // 2, MLA_HEAD_PAD, MLA_NOPE)
    reps = n // cos_ref.shape[1]
    cos = jnp.tile(cos_ref[...], (1, reps))[None]
    sin = jnp.tile(sin_ref[...], (1, reps))[None]
    q = t.reshape(nb, lt, n) * cos + rot.reshape(nb, lt, n) * sin
    q_ref[...] = (q * MLA_SCALE).astype(q_ref.dtype)


def _mla_qup(cq, w_uq_pad, cos, sin):
    b, l, r = cq.shape
    n = w_uq_pad.shape[1]
    nb, lt = _row_tiles(b, l)
    act, vec, par, pos = _tile_specs(b, l, r, nb, lt)
    return pl.pallas_call(
        _mla_qup_kernel,
        grid=(b // nb, l // lt),
        in_specs=[act(r), par((r, n)), pos(cos.shape[1]), pos(sin.shape[1])],
        out_specs=act(n),
        out_shape=jax.ShapeDtypeStruct((b, l, n), BF16),
        compiler_params=_params("parallel", "parallel"),
        name="mla_q_up",
    )(cq, w_uq_pad, cos, sin)


def _mla_kvup_kernel(ckv_ref, kr_ref, wk_ref, we_ref, wv_ref, k_ref, v_ref):
    nb, lt, r = ckv_ref.shape
    c = ckv_ref[...].reshape(nb * lt, r).astype(BF16)
    kr = kr_ref[...].reshape(nb * lt, kr_ref.shape[2]).astype(BF16)
    k = (jnp.dot(c, wk_ref[...], preferred_element_type=F32)
         + jnp.dot(kr, we_ref[...], preferred_element_type=F32))
    k_ref[...] = k.reshape(k_ref.shape).astype(k_ref.dtype)
    v_ref[...] = jnp.dot(c, wv_ref[...], preferred_element_type=F32).reshape(v_ref.shape).astype(v_ref.dtype)


def _mla_kvup(ckv, krope, w_uk_pad, w_place, w_uv):
    b, l, r = ckv.shape
    rows = ROW_TILE if l % ROW_TILE == 0 else l
    nb, lt = _row_tiles(b, l, rows)
    act, vec, par, _ = _tile_specs(b, l, r, nb, lt)
    nk, nv = w_uk_pad.shape[1], w_uv.shape[1]
    rope = krope.shape[2]
    return pl.pallas_call(
        _mla_kvup_kernel,
        grid=(b // nb, l // lt),
        in_specs=[act(r), act(rope), par((r, nk)), par((rope, nk)), par((r, nv))],
        out_specs=[act(nk), act(nv)],
        out_shape=[jax.ShapeDtypeStruct((b, l, nk), BF16), jax.ShapeDtypeStruct((b, l, nv), BF16)],
        compiler_params=_params("parallel", "parallel"),
        name="mla_kv_up",
    )(ckv, krope, w_uk_pad, w_place, w_uv)


def _mla_attn_kernel(*refs, tq, tk, causal, n_src):
    q_ref = refs[0]
    kv_refs = refs[1:1 + 2 * n_src]
    o_ref, m_s, l_s, acc_s = refs[1 + 2 * n_src:]
    qi = pl.program_id(2)
    hp = MLA_HEAD_PAD
    m_s[...] = jnp.full_like(m_s, NEG_INF)
    l_s[...] = jnp.zeros_like(l_s)
    acc_s[...] = jnp.zeros_like(acc_s)

    def make_step(k_ref, v_ref, masked):
        def step(start, size):
            vb = v_ref[0, pl.ds(start, size), :]
            vis = _chunk_mask(tq, size, qi * tq, start) if masked else None
            for hh in range(2):
                q = q_ref[0, :, hh * hp:(hh + 1) * hp]
                kb = k_ref[0, pl.ds(start, size), hh * hp:(hh + 1) * hp]
                s = lax.dot_general(q, kb, (((1,), (1,)), ((), ())), preferred_element_type=F32)
                if masked:
                    s = jnp.where(vis, s, NEG_INF)
                _softmax_step(s, vb, m_s.at[hh], l_s.at[hh], acc_s.at[hh])
        return step

    if causal:
        k_ref, v_ref = kv_refs
        _kv_sweep(make_step(k_ref, v_ref, False), None, tk, dynamic_full=qi * (tq // tk))
        for j in range(tq // tk):
            make_step(k_ref, v_ref, True)(pl.multiple_of(qi * tq + j * tk, tk), tk)
    else:
        for i in range(n_src):
            k_ref, v_ref = kv_refs[2 * i], kv_refs[2 * i + 1]
            _kv_sweep(make_step(k_ref, v_ref, False), k_ref.shape[1], tk)

    o0 = acc_s[0] / l_s[0]
    o1 = acc_s[1] / l_s[1]
    lane = lax.broadcasted_iota(jnp.int32, o0.shape, 1)
    o_ref[0] = jnp.where(lane < MLA_V, o0, o1).astype(o_ref.dtype)


def _mla_attention(qb, kv_sources, causal):
    b, l, _ = qb.shape
    tq = min(ATTN_TQ, l)
    tk = min(ATTN_TK, tq) if causal else ATTN_TK
    qw, vw = 2 * MLA_HEAD_PAD, 2 * MLA_V
    in_specs = [pl.BlockSpec((1, tq, qw), lambda bi, h, qi: (bi, qi, h))]
    args = [qb]
    for k, v in kv_sources:
        lk = k.shape[1]
        in_specs += [pl.BlockSpec((1, lk, qw), lambda bi, h, qi: (bi, 0, h)),
                     pl.BlockSpec((1, lk, vw), lambda bi, h, qi: (bi, 0, h))]
        args += [k, v]
    return pl.pallas_call(
        functools.partial(_mla_attn_kernel, tq=tq, tk=tk, causal=causal, n_src=len(kv_sources)),
        grid=(b, MLA_HEADS // 2, l // tq),
        in_specs=in_specs,
        out_specs=pl.BlockSpec((1, tq, vw), lambda bi, h, qi: (bi, qi, h)),
        out_shape=jax.ShapeDtypeStruct((b, l, MLA_HEADS * MLA_V), BF16),
        scratch_shapes=[pltpu.VMEM((2, tq, 1), F32), pltpu.VMEM((2, tq, 1), F32),
                        pltpu.VMEM((2, tq, vw), F32)],
        compiler_params=_params("parallel", "parallel", "arbitrary"),
        name="mla_attention",
    )(*args)


def _sgu_kernel(x_ref, g_ref, sh_ref, sc_ref, gt_ref, win_ref, gv_ref, ws_ref, bs_ref, wout_ref,
                *out_refs, t, emit_v):
    o_ref = out_refs[0]
    nb, lt, d = x_ref.shape
    rows = nb * lt
    width = gv_ref.shape[1]
    gd = width // SGU_GROUPS
    h = _modulated(x_ref, g_ref, sh_ref, sc_ref).reshape(rows, d).astype(BF16)
    u = jax.nn.gelu(jnp.dot(h, win_ref[:, :width], preferred_element_type=F32))
    v = jax.nn.gelu(jnp.dot(h, win_ref[:, width:], preferred_element_type=F32))
    v = _rms(v, gv_ref[...])
    if emit_v:
        out_refs[1][...] = v.reshape(nb, lt, width)
    vb = v.astype(BF16)
    bias = bs_ref[...]
    gated = []
    for c in range(rows // t):
        sv = [jnp.dot(ws_ref[gi], vb[c * t:(c + 1) * t, gi * gd:(gi + 1) * gd],
                      preferred_element_type=F32) for gi in range(SGU_GROUPS)]
        sv = jnp.concatenate(sv, axis=1) + bias
        gated.append((u[c * t:(c + 1) * t] * sv).astype(BF16))
    gated = jnp.concatenate(gated, axis=0) if len(gated) > 1 else gated[0]
    r = jnp.dot(gated, wout_ref[...], preferred_element_type=F32).reshape(nb, lt, d)
    o_ref[...] = x_ref[...] + (1.0 + gt_ref[...]) * r


def _sgu(x, g, sh, sc, gt, w_in, g_v, w_s, b_full, w_out, emit_v):
    b, l, d = x.shape
    t = w_s.shape[1]
    width = g_v.shape[1]
    nb, lt = _row_tiles(b, l, 256)
    assert lt % t == 0
    act, vec, par, _ = _tile_specs(b, l, d, nb, lt)
    out_specs = [act(d)]
    out_shape = [jax.ShapeDtypeStruct((b, l, d), F32)]
    if emit_v:
        out_specs.append(act(width))
        out_shape.append(jax.ShapeDtypeStruct((b, l, width), F32))
    return pl.pallas_call(
        functools.partial(_sgu_kernel, t=t, emit_v=emit_v),
        grid=(b // nb, l // lt),
        in_specs=[act(d), par((1, d)), vec(d), vec(d), vec(d), par(w_in.shape), par((1, width)),
                  par(w_s.shape), par(b_full.shape), par(w_out.shape)],
        out_specs=out_specs,
        out_shape=out_shape,
        compiler_params=_params("parallel", "parallel"),
        name="sgu",
    )(x, g, sh, sc, gt, w_in, g_v, w_s, b_full, w_out)


def _rope_angles(pos, half):
    inv_freq = ROPE_THETA ** (-jnp.arange(half, dtype=F32) / half)
    ang = pos.astype(F32)[:, None] * inv_freq[None, :]
    return jnp.cos(ang), jnp.sin(ang)


def _diff_rope_tables(pos):
    cos, sin = _rope_angles(pos, DIFF_DH // 2)
    reps = LANES // DIFF_DH
    return (jnp.tile(jnp.concatenate([cos, cos], axis=1), (1, reps)),
            jnp.tile(jnp.concatenate([-sin, sin], axis=1), (1, reps)))


def _mla_rope_tables(pos):
    cos, sin = _rope_angles(pos, MLA_ROPE // 2)
    n = pos.shape[0]
    ones = jnp.ones((n, MLA_NOPE), F32)
    zq = jnp.zeros((n, MLA_NOPE), F32)
    zp = jnp.zeros((n, MLA_HEAD_PAD - MLA_NOPE - MLA_ROPE), F32)
    q_cos = jnp.concatenate([ones, cos, cos, zp], axis=1)
    q_sin = jnp.concatenate([zq, -sin, sin, zp], axis=1)
    k_cos = jnp.concatenate([cos, cos], axis=1)
    k_sin = jnp.concatenate([sin, sin], axis=1)
    return q_cos, q_sin, k_cos, k_sin


def _prepare_weights(p):
    d = p['w_up'].shape[1]
    w = {}
    w['w_up'] = [p['w_up'][i].astype(BF16) for i in range(p['w_up'].shape[0])]
    w['w_down'] = [p['w_down'][i].astype(BF16) for i in range(p['w_down'].shape[0])]
    w['glu_a'] = p['s5_w_glu_a'].astype(BF16)
    w['glu_b'] = p['s5_w_glu_b'].astype(BF16)
    w['diff_qkv'] = p['diff_w_qkv'].astype(BF16)
    w['diff_o'] = p['diff_w_o'].astype(BF16)
    lam = jnp.stack([p['diff_lambda_q1'], p['diff_lambda_k1'], p['diff_lambda_q2'], p['diff_lambda_k2']])
    w['diff_lam'] = jnp.pad(lam.astype(F32), ((0, 4), (0, LANES - lam.shape[1])))
    w['diff_g_sub'] = p['diff_g_sub'].reshape(1, -1)

    w['mla_dq'] = p['mla_w_dq'].astype(BF16)
    kvr = p['mla_g_kv'].shape[0]
    w['mla_dc'] = p['mla_w_dkv'][:, :kvr].astype(BF16)
    wr = p['mla_w_dkv'][:, kvr:]
    hr = MLA_ROPE // 2
    w['mla_dr'] = wr.astype(BF16)
    w['mla_drr'] = jnp.concatenate([-wr[:, hr:], wr[:, :hr]], axis=1).astype(BF16)
    w['mla_g_q'] = p['mla_g_q'].reshape(1, -1)
    w['mla_g_kv'] = p['mla_g_kv'].reshape(1, -1)
    pad = MLA_HEAD_PAD - MLA_NOPE - MLA_ROPE
    qr = p['mla_w_uq'].shape[0]
    uq = p['mla_w_uq'].reshape(qr, MLA_HEADS, MLA_NOPE + MLA_ROPE)
    w['mla_uq'] = jnp.pad(uq, ((0, 0), (0, 0), (0, pad))).reshape(qr, -1).astype(BF16)
    uk = jnp.pad(p['mla_w_uk'], ((0, 0), (0, 0), (0, MLA_HEAD_PAD - MLA_NOPE)))
    w['mla_uk'] = uk.reshape(kvr, -1).astype(BF16)
    place = jnp.pad(jnp.eye(MLA_ROPE, dtype=F32), ((0, 0), (MLA_NOPE, pad)))
    w['mla_place'] = jnp.tile(place, (1, MLA_HEADS)).astype(BF16)
    w['mla_uv'] = p['mla_w_uv'].reshape(kvr, -1).astype(BF16)
    w['mla_o'] = p['mla_w_o'].astype(BF16)

    w['sgu_in'] = p['sgu_w_in'].astype(BF16)
    w['sgu_g_v'] = p['sgu_g_v'].reshape(1, -1)
    w['sgu_out'] = p['sgu_w_out'].astype(BF16)
    return w


def _sgu_spatial(w_s, b_s, t, width):
    ws = jnp.tril(w_s[:, :t, :t]).astype(BF16)
    gd = width // SGU_GROUPS
    b_full = jnp.repeat(b_s[:, :t].T.astype(F32), gd, axis=1)
    return ws, b_full


def _run_trunk(x, mods, pos, past, p, w, s5_tables):
    b, l, d = x.shape
    new = {}
    depth = p['g_mix'].shape[0]
    for i in range(depth):
        sh1, sc1, gt1, sh2, sc2, gt2 = [mods[i][:, k][:, None, :] for k in range(6)]
        g_mix = p['g_mix'][i].reshape(1, d)
        kind = i % N_MIXERS
        if kind == 0:
            h = _modulate(x, g_mix, sh1, sc1)
            h0_re = None if past is None else past['s5_re']
            h0_im = None if past is None else past['s5_im']
            y, new['s5_re'], new['s5_im'] = _s5_scan(h, s5_tables, h0_re, h0_im)
            x = _s5_glu(y, h, x, p['s5_d'].reshape(1, d), gt1, w['glu_a'], w['glu_b'])
        elif kind == 1:
            cos, sin = _diff_rope_tables(pos)
            k, v, qb, kb, vb = _diff_qkv(x, g_mix, sh1, sc1, w['diff_qkv'], cos, sin)
            if past is None:
                sources = [(kb, vb)]
            else:
                pk = past['diff_k'].reshape(b, -1, d)
                pv = past['diff_v'].reshape(b, -1, d)
                sources = [(pk, pv), (kb, vb)]
            o = _diff_attention(qb, sources, w['diff_lam'], w['diff_g_sub'], causal=past is None)
            x = _proj_residual(o, w['diff_o'], x, gt1)
            new['diff_k'] = k.reshape(b, l, DIFF_HEADS, 2 * DIFF_DH)
            new['diff_v'] = v.reshape(b, l, DIFF_HEADS, 2 * DIFF_DH)
        elif kind == 2:
            q_cos, q_sin, k_cos, k_sin = _mla_rope_tables(pos)
            cq, ckv, krope = _mla_down(x, g_mix, sh1, sc1, w['mla_dq'], w['mla_dc'], w['mla_dr'],
                                       w['mla_drr'], w['mla_g_q'], w['mla_g_kv'], k_cos, k_sin)
            qb = _mla_qup(cq, w['mla_uq'], q_cos, q_sin)
            sources = []
            if past is not None:
                sources.append(_mla_kvup(past['mla_ckv'], past['mla_krope'],
                                         w['mla_uk'], w['mla_place'], w['mla_uv']))
            sources.append(_mla_kvup(ckv, krope, w['mla_uk'], w['mla_place'], w['mla_uv']))
            o = _mla_attention(qb, sources, causal=past is None)
            x = _proj_residual(o, w['mla_o'], x, gt1)
            new['mla_ckv'], new['mla_krope'] = ckv, krope
        else:
            t = min(l, SGU_CHUNK)
            ws, b_full = _sgu_spatial(p['sgu_w_s'], p['sgu_b_s'], t, w['sgu_g_v'].shape[1])
            outs = _sgu(x, g_mix, sh1, sc1, gt1, w['sgu_in'], w['sgu_g_v'], ws, b_full, w['sgu_out'],
                        emit_v=past is not None)
            x = outs[0]
            if past is not None:
                new['sgu_v'] = outs[1]
        x = _mlp(x, p['g_ffn'][i].reshape(1, d), sh2, sc2, gt2, w['w_up'][i], w['w_down'][i],
                 p['g_final'].reshape(1, d), final_norm=(i == depth - 1))
    return x, new


def kernel(x_prompt, x_sample, c_prompt, c_sample, state_s5_re, state_s5_im, cache_diff_k, cache_diff_v, cache_mla_ckv, cache_mla_krope, w_ada, b_ada, g_mix, g_ffn, w_up, w_down, g_final, s5_a_re, s5_a_im, s5_b_re, s5_b_im, s5_c_re, s5_c_im, s5_d, s5_log_dt, s5_w_glu_a, s5_w_glu_b, diff_w_qkv, diff_lambda_q1, diff_lambda_k1, diff_lambda_q2, diff_lambda_k2, diff_g_sub, diff_w_o, mla_w_dq, mla_g_q, mla_w_uq, mla_w_dkv, mla_g_kv, mla_w_uk, mla_w_uv, mla_w_o, sgu_w_in, sgu_g_v, sgu_w_s, sgu_b_s, sgu_w_out):
    p = {
        'w_ada': w_ada, 'b_ada': b_ada, 'g_mix': g_mix, 'g_ffn': g_ffn,
        'w_up': w_up, 'w_down': w_down, 'g_final': g_final,
        's5_d': s5_d, 's5_w_glu_a': s5_w_glu_a, 's5_w_glu_b': s5_w_glu_b,
        'diff_w_qkv': diff_w_qkv, 'diff_lambda_q1': diff_lambda_q1, 'diff_lambda_k1': diff_lambda_k1,
        'diff_lambda_q2': diff_lambda_q2, 'diff_lambda_k2': diff_lambda_k2,
        'diff_g_sub': diff_g_sub, 'diff_w_o': diff_w_o,
        'mla_w_dq': mla_w_dq, 'mla_g_q': mla_g_q, 'mla_w_uq': mla_w_uq, 'mla_w_dkv': mla_w_dkv,
        'mla_g_kv': mla_g_kv, 'mla_w_uk': mla_w_uk, 'mla_w_uv': mla_w_uv, 'mla_w_o': mla_w_o,
        'sgu_w_in': sgu_w_in, 'sgu_g_v': sgu_g_v, 'sgu_w_s': sgu_w_s, 'sgu_b_s': sgu_b_s,
        'sgu_w_out': sgu_w_out,
    }
    past = {
        's5_re': state_s5_re, 's5_im': state_s5_im,
        'diff_k': cache_diff_k, 'diff_v': cache_diff_v,
        'mla_ckv': cache_mla_ckv, 'mla_krope': cache_mla_krope,
    }
    bp, lp, d = x_prompt.shape
    bs, ls, _ = x_sample.shape
    depth = w_ada.shape[0]

    c_all = jnp.concatenate([c_prompt, c_sample], axis=0)
    c_all = jnp.pad(c_all, ((0, (-c_all.shape[0]) % 8), (0, 0)))
    mod = _adaln(c_all, w_ada, b_ada).reshape(depth, c_all.shape[0], 6, d)
    mods_p = [mod[i, :bp] for i in range(depth)]
    mods_s = [mod[i, bp:bp + bs] for i in range(depth)]

    w = _prepare_weights(p)
    max_steps = max((lp // S5_SUB).bit_length() - 1, (ls // S5_SUB).bit_length() - 1, 1)
    s5_tables = _s5_tables(s5_a_re, s5_a_im, s5_b_re, s5_b_im, s5_c_re, s5_c_im, s5_log_dt, max_steps)

    pos_p = jnp.arange(lp, dtype=jnp.int32)
    pos_s = cache_diff_k.shape[1] + jnp.arange(ls, dtype=jnp.int32)
    y_prompt, sp = _run_trunk(x_prompt, mods_p, pos_p, None, p, w, s5_tables)
    y_sample, ss = _run_trunk(x_sample, mods_s, pos_s, past, p, w, s5_tables)
    return (y_prompt, y_sample,
            sp['s5_re'], sp['s5_im'], ss['s5_re'], ss['s5_im'],
            sp['diff_k'], sp['diff_v'], ss['diff_k'], ss['diff_v'],
            sp['mla_ckv'], sp['mla_krope'], ss['mla_ckv'], ss['mla_krope'],
            ss['sgu_v'])
```

```python
import functools
import math

import jax
import jax.numpy as jnp
from jax import lax
from jax.experimental import pallas as pl
from jax.experimental.pallas import tpu as pltpu

F32 = jnp.float32
BF16 = jnp.bfloat16
HIGHEST = lax.Precision.HIGHEST

NORM_EPS = 1e-6
ROPE_THETA = 10000.0
NEG_INF = -1e30
LOG2_E = math.log2(math.e)
CHUNK = 64
N_MIXERS = 4

S5_GROUP = 16
S5_SUB = 16

DIFF_HEADS = 8
DIFF_DH = 64
LAMBDA_INIT = 0.8 - 0.6 * math.exp(-0.3 * 1)

MLA_HEADS = 16
MLA_NOPE = 64
MLA_ROPE = 32
MLA_V = 64
MLA_SCALE = (MLA_NOPE + MLA_ROPE) ** -0.5
MLA_HEAD_PAD = 128

SGU_CHUNK = 128
SGU_GROUPS = 8

LANES = 128
ROW_TILE = 512
ATTN_TQ = 256
ATTN_STREAMS = 2
ATTN_TK = 256
VMEM_LIMIT_BYTES = 48 * 1024 * 1024


def _params(*sem):
    return pltpu.CompilerParams(dimension_semantics=sem, vmem_limit_bytes=VMEM_LIMIT_BYTES)


def _row_tiles(batch, length, rows=ROW_TILE):
    if length >= rows:
        assert length % rows == 0
        return 1, rows
    nb = max(1, min(batch, rows // length))
    while batch % nb:
        nb -= 1
    return nb, length


def _rms(x, g):
    return x * lax.rsqrt(jnp.mean(x * x, axis=-1, keepdims=True) + NORM_EPS) * g


def _modulated(x_ref, g_ref, sh_ref, sc_ref):
    return _rms(x_ref[...], g_ref[...]) * (1.0 + sc_ref[...]) + sh_ref[...]


def _rotate_pairs(x, half, period, lo):
    width = x.shape[-1]
    lane = lax.broadcasted_iota(jnp.int32, x.shape, x.ndim - 1) % period
    fwd = pltpu.roll(x, width - half, axis=x.ndim - 1)
    bwd = pltpu.roll(x, half, axis=x.ndim - 1)
    return jnp.where((lane >= lo) & (lane < lo + half), fwd, bwd)


def _adaln_kernel(c_ref, w_ref, b_ref, o_ref):
    c = c_ref[...]
    s = c * jax.nn.sigmoid(c)
    o_ref[0] = jnp.dot(s, w_ref[0], precision=HIGHEST, preferred_element_type=F32) + b_ref[0]


def _adaln(c_all, w_ada, b_ada):
    depth, d, n = w_ada.shape
    rows = c_all.shape[0]
    tn = n // 4
    return pl.pallas_call(
        _adaln_kernel,
        grid=(depth, n // tn),
        in_specs=[pl.BlockSpec((rows, d), lambda i, j: (0, 0)),
                  pl.BlockSpec((1, d, tn), lambda i, j: (i, 0, j)),
                  pl.BlockSpec((1, 1, tn), lambda i, j: (i, 0, j))],
        out_specs=pl.BlockSpec((1, rows, tn), lambda i, j: (i, 0, j)),
        out_shape=jax.ShapeDtypeStruct((depth, rows, n), F32),
        compiler_params=_params("parallel", "parallel"),
        name="adaln",
    )(c_all, w_ada, b_ada.reshape(depth, 1, n))


def _tile_specs(batch, length, d, nb, lt, extra_axes=0):
    if extra_axes == 0:
        act = lambda width: pl.BlockSpec((nb, lt, width), lambda i, j: (i, j, 0))
        vec = lambda width: pl.BlockSpec((nb, 1, width), lambda i, j: (i, 0, 0))
        par = lambda shape: pl.BlockSpec(shape, lambda i, j: (0,) * len(shape))
        pos = lambda width: pl.BlockSpec((lt, width), lambda i, j: (j, 0))
    else:
        act = lambda width: pl.BlockSpec((nb, lt, width), lambda i, j, f: (i, j, 0))
        vec = lambda width: pl.BlockSpec((nb, 1, width), lambda i, j, f: (i, 0, 0))
        par = lambda shape: pl.BlockSpec(shape, lambda i, j, f: (0,) * len(shape))
        pos = lambda width: pl.BlockSpec((lt, width), lambda i, j, f: (j, 0))
    return act, vec, par, pos


def _modulate_kernel(x_ref, g_ref, sh_ref, sc_ref, o_ref):
    o_ref[...] = _modulated(x_ref, g_ref, sh_ref, sc_ref)


def _modulate(x, g, sh, sc):
    b, l, d = x.shape
    nb, lt = _row_tiles(b, l)
    act, vec, par, _ = _tile_specs(b, l, d, nb, lt)
    return pl.pallas_call(
        _modulate_kernel,
        grid=(b // nb, l // lt),
        in_specs=[act(d), par((1, d)), vec(d), vec(d)],
        out_specs=act(d),
        out_shape=jax.ShapeDtypeStruct((b, l, d), F32),
        compiler_params=_params("parallel", "parallel"),
        name="modulate",
    )(x, g, sh, sc)


def _mlp_kernel(x_ref, g_ref, sh_ref, sc_ref, gt_ref, wu_ref, wd_ref, gf_ref, o_ref, h_s, acc_s,
                *, final_norm):
    f = pl.program_id(2)
    nb, lt, d = x_ref.shape

    @pl.when(f == 0)
    def _():
        h_s[...] = _modulated(x_ref, g_ref, sh_ref, sc_ref).reshape(nb * lt, d).astype(BF16)
        acc_s[...] = jnp.zeros_like(acc_s)

    a = jnp.dot(h_s[...], wu_ref[...], preferred_element_type=F32)
    a = jnp.square(jnp.maximum(a, 0.0)).astype(BF16)
    acc_s[...] += jnp.dot(a, wd_ref[...], preferred_element_type=F32)

    @pl.when(f == pl.num_programs(2) - 1)
    def _():
        y = x_ref[...] + (1.0 + gt_ref[...]) * acc_s[...].reshape(nb, lt, d)
        if final_norm:
            y = _rms(y, gf_ref[...])
        o_ref[...] = y


def _mlp(x, g, sh, sc, gt, w_up, w_down, g_final, final_norm, tf=1024):
    b, l, d = x.shape
    dff = w_up.shape[1]
    nb, lt = _row_tiles(b, l)
    act, vec, par, _ = _tile_specs(b, l, d, nb, lt, extra_axes=1)
    return pl.pallas_call(
        functools.partial(_mlp_kernel, final_norm=final_norm),
        grid=(b // nb, l // lt, dff // tf),
        in_specs=[act(d), par((1, d)), vec(d), vec(d), vec(d),
                  pl.BlockSpec((d, tf), lambda i, j, f: (0, f)),
                  pl.BlockSpec((tf, d), lambda i, j, f: (f, 0)),
                  par((1, d))],
        out_specs=act(d),
        out_shape=jax.ShapeDtypeStruct((b, l, d), F32),
        scratch_shapes=[pltpu.VMEM((nb * lt, d), BF16), pltpu.VMEM((nb * lt, d), F32)],
        compiler_params=_params("parallel", "parallel", "arbitrary"),
        name="mlp",
    )(x, g, sh, sc, gt, w_up, w_down, g_final)


def _proj_res_kernel(a_ref, w_ref, x_ref, gt_ref, o_ref):
    nb, lt, k = a_ref.shape
    d = w_ref.shape[1]
    r = jnp.dot(a_ref[...].reshape(nb * lt, k), w_ref[...], preferred_element_type=F32)
    o_ref[...] = x_ref[...] + (1.0 + gt_ref[...]) * r.reshape(nb, lt, d)


def _proj_residual(a, w, x, gt):
    b, l, d = x.shape
    k = a.shape[-1]
    nb, lt = _row_tiles(b, l)
    act, vec, par, _ = _tile_specs(b, l, d, nb, lt)
    return pl.pallas_call(
        _proj_res_kernel,
        grid=(b // nb, l // lt),
        in_specs=[act(k), par((k, d)), act(d), vec(d)],
        out_specs=act(d),
        out_shape=jax.ShapeDtypeStruct((b, l, d), F32),
        compiler_params=_params("parallel", "parallel"),
        name="proj_residual",
    )(a, w, x, gt)


def _s5_kernel(*refs, nch, batch, nsteps, has_h0):
    if has_h0:
        u_ref, bc_ref, m_ref, cc_ref, a1_ref, a2_ref, h0_ref, y_ref, hl_ref = refs
    else:
        u_ref, bc_ref, m_ref, cc_ref, a1_ref, a2_ref, y_ref, hl_ref = refs
    u = u_ref[0]
    rows = u.shape[0]
    x = jnp.dot(u, bc_ref[0], precision=HIGHEST, preferred_element_type=F32)
    half = x.shape[1] // 2
    if has_h0:
        h0 = h0_ref[0]
        x = x + a1_ref[0, 0:1, :] * h0 + a2_ref[0, 0:1, :] * pltpu.roll(h0, half, axis=1)
    kidx = lax.broadcasted_iota(jnp.int32, x.shape, 0) & (nch - 1)
    for j in range(nsteps):
        s = 1 << j
        xs = jnp.where(kidx >= s, pltpu.roll(x, s, axis=0), 0.0)
        x = x + a1_ref[0, j:j + 1, :] * xs + a2_ref[0, j:j + 1, :] * pltpu.roll(xs, half, axis=1)
    for b in range(batch):
        r = (b + 1) * nch - 1
        hl_ref[0, b:b + 1, :] = x[r:r + 1, :]
    hstart = jnp.where(kidx >= 1, pltpu.roll(x, 1, axis=0), 0.0)
    if has_h0:
        hstart = hstart + h0
    y = jnp.dot(u.astype(BF16), m_ref[0], preferred_element_type=F32)
    y = y + jnp.dot(hstart.astype(BF16), cc_ref[0], preferred_element_type=F32)
    y_ref[0] = y


def _s5_tables(a_re, a_im, b_re, b_im, c_re, c_im, log_dt, max_steps):
    g, p = a_re.shape
    sub = S5_SUB
    dt = jnp.exp(log_dt.astype(F32))[:, None]
    zr, zi = a_re.astype(F32) * dt, a_im.astype(F32) * dt

    def zpow(n):
        mag = jnp.exp(zr * n)
        return mag * jnp.cos(zi * n), mag * jnp.sin(zi * n)

    er = jnp.expm1(zr) * jnp.cos(zi) - 2.0 * jnp.square(jnp.sin(0.5 * zi))
    ei = jnp.exp(zr) * jnp.sin(zi)
    den = a_re * a_re + a_im * a_im
    fr = (er * a_re + ei * a_im) / den
    fi = (ei * a_re - er * a_im) / den
    bb_re = fr[..., None] * b_re - fi[..., None] * b_im
    bb_im = fr[..., None] * b_im + fi[..., None] * b_re

    lags = jnp.arange(sub + 1, dtype=F32)[:, None, None]
    pr, pi = zpow(lags)

    qr, qi = pr[:sub][::-1], pi[:sub][::-1]
    bcr = qr[..., None] * bb_re[None] - qi[..., None] * bb_im[None]
    bci = qr[..., None] * bb_im[None] + qi[..., None] * bb_re[None]
    bc = jnp.concatenate([bcr, bci], axis=2)
    bc = bc.transpose(1, 0, 3, 2).reshape(g, sub * S5_GROUP, 2 * p)

    ar, ai = pr[1:], pi[1:]
    ccr = c_re[None] * ar[:, :, None, :] - c_im[None] * ai[:, :, None, :]
    cci = c_re[None] * ai[:, :, None, :] + c_im[None] * ar[:, :, None, :]
    cc = jnp.concatenate([ccr, -cci], axis=3)
    cc = cc.transpose(1, 3, 0, 2).reshape(g, 2 * p, sub * S5_GROUP)

    kr = (jnp.einsum('gcp,lgp,gpd->lgcd', c_re, pr[:sub], bb_re, precision=HIGHEST)
          - jnp.einsum('gcp,lgp,gpd->lgcd', c_re, pi[:sub], bb_im, precision=HIGHEST)
          - jnp.einsum('gcp,lgp,gpd->lgcd', c_im, pr[:sub], bb_im, precision=HIGHEST)
          - jnp.einsum('gcp,lgp,gpd->lgcd', c_im, pi[:sub], bb_re, precision=HIGHEST))
    tt = jnp.arange(sub)
    lag = tt[None, :] - tt[:, None]
    toep = jnp.where((lag >= 0)[:, :, None, None, None],
                     kr[jnp.clip(lag, 0, sub - 1)], 0.0)
    m = toep.transpose(2, 0, 4, 1, 3).reshape(g, sub * S5_GROUP, sub * S5_GROUP)

    steps = (sub * (2 ** jnp.arange(max_steps))).astype(F32)[:, None, None]
    sr, si = zpow(steps)
    a1 = jnp.concatenate([sr, sr], axis=-1).transpose(1, 0, 2)
    a2 = jnp.concatenate([-si, si], axis=-1).transpose(1, 0, 2)
    pad = (-max_steps) % 8
    a1 = jnp.pad(a1, ((0, 0), (0, pad), (0, 0)))
    a2 = jnp.pad(a2, ((0, 0), (0, pad), (0, 0)))
    return bc, m.astype(BF16), cc.astype(BF16), a1, a2


def _s5_scan(h, tables, h0_re, h0_im):
    bc, m, cc, a1, a2 = tables
    b, l, d = h.shape
    g = d // S5_GROUP
    p2 = bc.shape[-1]
    nch = l // S5_SUB
    assert nch & (nch - 1) == 0
    nsteps = nch.bit_length() - 1
    rows = b * nch
    k = S5_SUB * S5_GROUP
    u = h.reshape(b, nch, S5_SUB, g, S5_GROUP).transpose(3, 0, 1, 2, 4).reshape(g, rows, k)
    has_h0 = h0_re is not None
    grp = lambda shape: pl.BlockSpec((1,) + shape, lambda i: (i, 0, 0))
    args = [u, bc, m, cc, a1, a2]
    in_specs = [grp((rows, k)), grp((k, p2)), grp((k, k)), grp((p2, k)),
                grp(a1.shape[1:]), grp(a2.shape[1:])]
    if has_h0:
        h0 = jnp.concatenate([h0_re, h0_im], axis=-1).astype(F32).transpose(1, 0, 2)
        h0 = jnp.pad(h0[:, :, None, :], ((0, 0), (0, 0), (0, nch - 1), (0, 0))).reshape(g, rows, p2)
        args.append(h0)
        in_specs.append(grp((rows, p2)))
    y, hl = pl.pallas_call(
        functools.partial(_s5_kernel, nch=nch, batch=b, nsteps=nsteps, has_h0=has_h0),
        grid=(g,),
        in_specs=in_specs,
        out_specs=[grp((rows, k)), grp((b, p2))],
        out_shape=[jax.ShapeDtypeStruct((g, rows, k), F32), jax.ShapeDtypeStruct((g, b, p2), F32)],
        compiler_params=_params("parallel"),
        name="s5_scan",
    )(*args)
    y = y.reshape(g, b, nch, S5_SUB, S5_GROUP).transpose(1, 2, 3, 0, 4).reshape(b, l, d)
    hl = hl.transpose(1, 0, 2)
    return y, hl[..., :p2 // 2], hl[..., p2 // 2:]


def _glu_kernel(y_ref, h_ref, x_ref, d_ref, gt_ref, wa_ref, wb_ref, o_ref):
    nb, lt, d = x_ref.shape
    z = jax.nn.gelu(y_ref[...] + d_ref[...] * h_ref[...]).reshape(nb * lt, d).astype(BF16)
    a = jnp.dot(z, wa_ref[...], preferred_element_type=F32)
    b = jnp.dot(z, wb_ref[...], preferred_element_type=F32)
    out = (a * jax.nn.sigmoid(b)).reshape(nb, lt, d)
    o_ref[...] = x_ref[...] + (1.0 + gt_ref[...]) * out


def _s5_glu(y, h, x, d_skip, gt, wa, wb):
    b, l, d = x.shape
    nb, lt = _row_tiles(b, l)
    act, vec, par, _ = _tile_specs(b, l, d, nb, lt)
    return pl.pallas_call(
        _glu_kernel,
        grid=(b // nb, l // lt),
        in_specs=[act(d), act(d), act(d), par((1, d)), vec(d), par((d, d)), par((d, d))],
        out_specs=act(d),
        out_shape=jax.ShapeDtypeStruct((b, l, d), F32),
        compiler_params=_params("parallel", "parallel"),
        name="s5_glu",
    )(y, h, x, d_skip, gt, wa, wb)


def _diff_qkv_kernel(x_ref, g_ref, sh_ref, sc_ref, w_ref, cos_ref, sin_ref,
                     k_ref, v_ref, qb_ref, kb_ref, vb_ref):
    nb, lt, d = x_ref.shape
    h = _modulated(x_ref, g_ref, sh_ref, sc_ref).reshape(nb * lt, d).astype(BF16)
    reps = d // cos_ref.shape[1]
    cos = jnp.tile(cos_ref[...], (1, reps))[None]
    sin = jnp.tile(sin_ref[...], (1, reps))[None]

    def roped(cols):
        t = jnp.dot(h, w_ref[:, cols * d:(cols + 1) * d], preferred_element_type=F32)
        r = _rotate_pairs(t, DIFF_DH // 2, DIFF_DH, 0)
        return t.reshape(nb, lt, d) * cos + r.reshape(nb, lt, d) * sin

    q = roped(0)
    qb_ref[...] = (q * (LOG2_E * DIFF_DH ** -0.5)).astype(BF16)
    k = roped(1)
    k_ref[...] = k
    kb_ref[...] = k.astype(BF16)
    v = jnp.dot(h, w_ref[:, 2 * d:], preferred_element_type=F32).reshape(nb, lt, d)
    v_ref[...] = v
    vb_ref[...] = v.astype(BF16)


def _diff_qkv(x, g, sh, sc, w_qkv, cos, sin):
    b, l, d = x.shape
    nb, lt = _row_tiles(b, l)
    act, vec, par, pos = _tile_specs(b, l, d, nb, lt)
    return pl.pallas_call(
        _diff_qkv_kernel,
        grid=(b // nb, l // lt),
        in_specs=[act(d), par((1, d)), vec(d), vec(d), par((d, 3 * d)),
                  pos(cos.shape[1]), pos(sin.shape[1])],
        out_specs=[act(d)] * 5,
        out_shape=[jax.ShapeDtypeStruct((b, l, d), F32)] * 2
                  + [jax.ShapeDtypeStruct((b, l, d), BF16)] * 3,
        compiler_params=_params("parallel", "parallel"),
        name="diff_qkv",
    )(x, g, sh, sc, w_qkv, cos, sin)


_NT = (((1,), (1,)), ((), ()))
_TN = (((0,), (0,)), ((), ()))


def _kv_sweep(step, n_keys, tk, dynamic_full=None):
    if dynamic_full is not None:
        lax.fori_loop(0, dynamic_full, lambda j, c: (step(pl.multiple_of(j * tk, tk), tk), c)[1], 0)
        return
    n_full = n_keys // tk
    if n_full == 1:
        step(0, tk)
    elif n_full > 1:
        lax.fori_loop(0, n_full, lambda j, c: (step(pl.multiple_of(j * tk, tk), tk), c)[1], 0)
    if n_keys % tk:
        step(n_full * tk, n_keys % tk)


def _attend(scores, kv_refs, m_s, l_s, acc_s, s_bufs, *, tq, tk, causal, qi):
    ns = len(scores)
    m_s[...] = jnp.full_like(m_s, NEG_INF)
    l_s[...] = jnp.zeros_like(l_s)
    acc_s[...] = jnp.zeros_like(acc_s)

    def update(i, s, vb):
        m_prev = m_s[i]
        m_new = jnp.maximum(m_prev, jnp.max(s, axis=0, keepdims=True))
        alpha = jnp.exp2(m_prev - m_new)
        p = jnp.exp2(s - m_new)
        l_s[i] = alpha * l_s[i] + jnp.sum(p, axis=0, keepdims=True)
        acc_s[i] = alpha * acc_s[i] + lax.dot_general(vb, p.astype(BF16), _TN,
                                                      preferred_element_type=F32)
        m_s[i] = m_new

    def make_step(k_ref, v_ref):
        def step(start, size):
            kb = k_ref[0, pl.ds(start, size), :].astype(BF16)
            vb = v_ref[0, pl.ds(start, size), :].astype(BF16)
            for i in range(ns):
                update(i, scores[i](kb), vb)
        return step

    if causal:
        assert ns == 2 and tq == tk and tq % CHUNK == 0
        k_ref, v_ref = kv_refs
        s_a, s_b = s_bufs

        def qk(dst, blk, streams):
            kb = k_ref[0, pl.ds(pl.multiple_of(blk * tk, tk), tk), :].astype(BF16)
            for i in streams:
                dst[i] = scores[i](kb)

        def pv(src, blk, streams, masked=None):
            vb = v_ref[0, pl.ds(pl.multiple_of(blk * tk, tk), tk), :].astype(BF16)
            for i in streams:
                update(i, jnp.where(vis, src[i], NEG_INF) if i == masked else src[i], vb)

        def pair(jj, carry):
            qk(s_b, 2 * jj + 1, (0, 1))
            pv(s_a, 2 * jj, (0, 1))
            qk(s_a, 2 * jj + 2, (0, 1))
            pv(s_b, 2 * jj + 1, (0, 1))
            return carry

        qk(s_a, 0, (0, 1))
        lax.fori_loop(0, qi, pair, 0)
        kc = lax.broadcasted_iota(jnp.int32, (tk, tq), 0) // CHUNK
        qc = lax.broadcasted_iota(jnp.int32, (tk, tq), 1) // CHUNK
        vis = jnp.concatenate([kc <= qc] * 2, axis=1)
        qk(s_b, 2 * qi + 1, (1,))
        pv(s_a, 2 * qi, (0, 1), masked=0)
        pv(s_b, 2 * qi + 1, (1,), masked=1)
    else:
        for i in range(len(kv_refs) // 2):
            k_ref, v_ref = kv_refs[2 * i], kv_refs[2 * i + 1]
            _kv_sweep(make_step(k_ref, v_ref), k_ref.shape[1], tk)
    return [acc_s[i] / l_s[i] for i in range(ns)]


def _diff_attn_kernel(*refs, tq, tk, causal, n_src):
    q_ref = refs[0]
    kv_refs = refs[1:1 + 2 * n_src]
    lam_ref, gsub_ref, o_ref, m_s, l_s, acc_s, *s_bufs = refs[1 + 2 * n_src:]
    ns = q_ref.shape[1] // tq
    lane = lax.broadcasted_iota(jnp.int32, (tq, q_ref.shape[2]), 1)

    def make_score(i):
        q = q_ref[0, i * tq:(i + 1) * tq, :]
        zero = jnp.zeros_like(q)
        q2 = jnp.concatenate([jnp.where(lane < DIFF_DH, q, zero), jnp.where(lane >= DIFF_DH, q, zero)],
                             axis=0)
        return lambda kb: lax.dot_general(kb, q2, _NT, preferred_element_type=F32)

    outs = _attend([make_score(i) for i in range(ns)], kv_refs, m_s, l_s, acc_s, s_bufs,
                   tq=tq, tk=tk, causal=causal, qi=pl.program_id(2))
    lp = lam_ref[...]
    lam = (jnp.exp(jnp.sum(lp[0:1] * lp[1:2], axis=-1, keepdims=True))
           - jnp.exp(jnp.sum(lp[2:3] * lp[3:4], axis=-1, keepdims=True)) + LAMBDA_INIT)
    for i, o2 in enumerate(outs):
        o = (o2[:, :tq] - lam * o2[:, tq:]).T
        o = _rms(o, gsub_ref[...]) * (1.0 - LAMBDA_INIT)
        o_ref[0, i * tq:(i + 1) * tq, :] = o.astype(o_ref.dtype)


def _score_buffers(ns, tq, tk, causal):
    return [pltpu.VMEM((ns, tk, 2 * tq), F32)] * 2 if causal else []


def _attn_tiles(l, causal):
    tq = min(ATTN_TQ, l)
    ns = ATTN_STREAMS if l % (ATTN_STREAMS * tq) == 0 else 1
    return tq, ns, (tq if causal else ATTN_TK)


def _diff_attention(qb, kv_sources, lam_rows, g_sub, causal):
    b, l, d = qb.shape
    hw = 2 * DIFF_DH
    tq, ns, tk = _attn_tiles(l, causal)
    in_specs = [pl.BlockSpec((1, ns * tq, hw), lambda bi, h, qi: (bi, qi, h))]
    args = [qb]
    for k, v in kv_sources:
        lk = k.shape[1]
        in_specs += [pl.BlockSpec((1, lk, hw), lambda bi, h, qi: (bi, 0, h))] * 2
        args += [k, v]
    in_specs += [pl.BlockSpec(lam_rows.shape, lambda bi, h, qi: (0, 0)),
                 pl.BlockSpec((1, hw), lambda bi, h, qi: (0, 0))]
    args += [lam_rows, g_sub]
    return pl.pallas_call(
        functools.partial(_diff_attn_kernel, tq=tq, tk=tk, causal=causal, n_src=len(kv_sources)),
        grid=(b, d // hw, l // (ns * tq)),
        in_specs=in_specs,
        out_specs=pl.BlockSpec((1, ns * tq, hw), lambda bi, h, qi: (bi, qi, h)),
        out_shape=jax.ShapeDtypeStruct((b, l, d), BF16),
        scratch_shapes=[pltpu.VMEM((ns, 1, 2 * tq), F32), pltpu.VMEM((ns, 1, 2 * tq), F32),
                        pltpu.VMEM((ns, hw, 2 * tq), F32)] + _score_buffers(ns, tq, tk, causal),
        compiler_params=_params("parallel", "parallel", "arbitrary"),
        name="diff_attention",
    )(*args)


def _mla_down_kernel(x_ref, g_ref, sh_ref, sc_ref, wq_ref, wc_ref, wr_ref, wrr_ref,
                     gq_ref, gkv_ref, cos_ref, sin_ref, cq_ref, ckv_ref, kr_ref):
    nb, lt, d = x_ref.shape
    h = _modulated(x_ref, g_ref, sh_ref, sc_ref).reshape(nb * lt, d).astype(BF16)
    dot = lambda w_ref: jnp.dot(h, w_ref[...], preferred_element_type=F32)
    cq_ref[...] = _rms(dot(wq_ref), gq_ref[...]).reshape(cq_ref.shape).astype(cq_ref.dtype)
    ckv_ref[...] = _rms(dot(wc_ref), gkv_ref[...]).reshape(ckv_ref.shape)
    r = dot(wr_ref).reshape(kr_ref.shape)
    rr = dot(wrr_ref).reshape(kr_ref.shape)
    kr_ref[...] = r * cos_ref[...][None] + rr * sin_ref[...][None]


def _mla_down(x, g, sh, sc, wq, wc, wr, wrr, g_q, g_kv, cos, sin):
    b, l, d = x.shape
    nb, lt = _row_tiles(b, l)
    act, vec, par, pos = _tile_specs(b, l, d, nb, lt)
    qr, kvr, rope = wq.shape[1], wc.shape[1], wr.shape[1]
    return pl.pallas_call(
        _mla_down_kernel,
        grid=(b // nb, l // lt),
        in_specs=[act(d), par((1, d)), vec(d), vec(d),
                  par((d, qr)), par((d, kvr)), par((d, rope)), par((d, rope)),
                  par((1, qr)), par((1, kvr)), pos(rope), pos(rope)],
        out_specs=[act(qr), act(kvr), act(rope)],
        out_shape=[jax.ShapeDtypeStruct((b, l, qr), BF16),
                   jax.ShapeDtypeStruct((b, l, kvr), F32),
                   jax.ShapeDtypeStruct((b, l, rope), F32)],
        compiler_params=_params("parallel", "parallel"),
        name="mla_down",
    )(x, g, sh, sc, wq, wc, wr, wrr, g_q, g_kv, cos, sin)


def _mla_qup_kernel(cq_ref, w_ref, cos_ref, sin_ref, q_ref):
    nb, lt, r = cq_ref.shape
    n = w_ref.shape[1]
    t = jnp.dot(cq_ref[...].reshape(nb * lt, r), w_ref[...], preferred_element_type=F32)
    rot = _rotate_pairs(t, MLA_ROPE // 2, MLA_HEAD_PAD, MLA_NOPE)
    reps = n // cos_ref.shape[1]
    cos = jnp.tile(cos_ref[...], (1, reps))[None]
    sin = jnp.tile(sin_ref[...], (1, reps))[None]
    q = t.reshape(nb, lt, n) * cos + rot.reshape(nb, lt, n) * sin
    q_ref[...] = (q * (LOG2_E * MLA_SCALE)).astype(q_ref.dtype)


def _mla_qup(cq, w_uq_pad, cos, sin):
    b, l, r = cq.shape
    n = w_uq_pad.shape[1]
    nb, lt = _row_tiles(b, l)
    act, vec, par, pos = _tile_specs(b, l, r, nb, lt)
    return pl.pallas_call(
        _mla_qup_kernel,
        grid=(b // nb, l // lt),
        in_specs=[act(r), par((r, n)), pos(cos.shape[1]), pos(sin.shape[1])],
        out_specs=act(n),
        out_shape=jax.ShapeDtypeStruct((b, l, n), BF16),
        compiler_params=_params("parallel", "parallel"),
        name="mla_q_up",
    )(cq, w_uq_pad, cos, sin)


def _mla_kvup_kernel(ckv_ref, kr_ref, wk_ref, we_ref, wv_ref, k_ref, v_ref):
    nb, lt, r = ckv_ref.shape
    c = ckv_ref[...].reshape(nb * lt, r).astype(BF16)
    kr = kr_ref[...].reshape(nb * lt, kr_ref.shape[2]).astype(BF16)
    k = (jnp.dot(c, wk_ref[...], preferred_element_type=F32)
         + jnp.dot(kr, we_ref[...], preferred_element_type=F32))
    k_ref[...] = k.reshape(k_ref.shape).astype(k_ref.dtype)
    v_ref[...] = jnp.dot(c, wv_ref[...], preferred_element_type=F32).reshape(v_ref.shape).astype(v_ref.dtype)


def _mla_kvup(ckv, krope, w_uk_pad, w_place, w_uv):
    b, l, r = ckv.shape
    rows = ROW_TILE if l % ROW_TILE == 0 else l
    nb, lt = _row_tiles(b, l, rows)
    act, vec, par, _ = _tile_specs(b, l, r, nb, lt)
    nk, nv = w_uk_pad.shape[1], w_uv.shape[1]
    rope = krope.shape[2]
    return pl.pallas_call(
        _mla_kvup_kernel,
        grid=(b // nb, l // lt),
        in_specs=[act(r), act(rope), par((r, nk)), par((rope, nk)), par((r, nv))],
        out_specs=[act(nk), act(nv)],
        out_shape=[jax.ShapeDtypeStruct((b, l, nk), BF16), jax.ShapeDtypeStruct((b, l, nv), BF16)],
        compiler_params=_params("parallel", "parallel"),
        name="mla_kv_up",
    )(ckv, krope, w_uk_pad, w_place, w_uv)


def _mla_attn_kernel(*refs, tq, tk, causal, n_src):
    q_ref = refs[0]
    kv_refs = refs[1:1 + 2 * n_src]
    o_ref, m_s, l_s, acc_s, *s_bufs = refs[1 + 2 * n_src:]
    hp = MLA_HEAD_PAD
    ns = q_ref.shape[1] // tq

    def make_score(i):
        qa = q_ref[0, i * tq:(i + 1) * tq, :hp]
        qb = q_ref[0, i * tq:(i + 1) * tq, hp:]
        return lambda kb: jnp.concatenate(
            [lax.dot_general(kb[:, :hp], qa, _NT, preferred_element_type=F32),
             lax.dot_general(kb[:, hp:], qb, _NT, preferred_element_type=F32)], axis=1)

    outs = _attend([make_score(i) for i in range(ns)], kv_refs, m_s, l_s, acc_s, s_bufs,
                   tq=tq, tk=tk, causal=causal, qi=pl.program_id(2))
    row = lax.broadcasted_iota(jnp.int32, (acc_s.shape[1], tq), 0)
    for i, o2 in enumerate(outs):
        o = jnp.where(row < MLA_V, o2[:, :tq], o2[:, tq:]).T
        o_ref[0, i * tq:(i + 1) * tq, :] = o.astype(o_ref.dtype)


def _mla_attention(qb, kv_sources, causal):
    b, l, _ = qb.shape
    tq, ns, tk = _attn_tiles(l, causal)
    qw, vw = 2 * MLA_HEAD_PAD, 2 * MLA_V
    in_specs = [pl.BlockSpec((1, ns * tq, qw), lambda bi, h, qi: (bi, qi, h))]
    args = [qb]
    for k, v in kv_sources:
        lk = k.shape[1]
        in_specs += [pl.BlockSpec((1, lk, qw), lambda bi, h, qi: (bi, 0, h)),
                     pl.BlockSpec((1, lk, vw), lambda bi, h, qi: (bi, 0, h))]
        args += [k, v]
    return pl.pallas_call(
        functools.partial(_mla_attn_kernel, tq=tq, tk=tk, causal=causal, n_src=len(kv_sources)),
        grid=(b, MLA_HEADS // 2, l // (ns * tq)),
        in_specs=in_specs,
        out_specs=pl.BlockSpec((1, ns * tq, vw), lambda bi, h, qi: (bi, qi, h)),
        out_shape=jax.ShapeDtypeStruct((b, l, MLA_HEADS * MLA_V), BF16),
        scratch_shapes=[pltpu.VMEM((ns, 1, 2 * tq), F32), pltpu.VMEM((ns, 1, 2 * tq), F32),
                        pltpu.VMEM((ns, vw, 2 * tq), F32)] + _score_buffers(ns, tq, tk, causal),
        compiler_params=_params("parallel", "parallel", "arbitrary"),
        name="mla_attention",
    )(*args)


def _sgu_kernel(x_ref, g_ref, sh_ref, sc_ref, gt_ref, win_ref, gv_ref, ws_ref, bs_ref, wout_ref,
                *out_refs, t, emit_v):
    o_ref = out_refs[0]
    nb, lt, d = x_ref.shape
    rows = nb * lt
    width = gv_ref.shape[1]
    gd = width // SGU_GROUPS
    h = _modulated(x_ref, g_ref, sh_ref, sc_ref).reshape(rows, d).astype(BF16)
    u = jax.nn.gelu(jnp.dot(h, win_ref[:, :width], preferred_element_type=F32))
    v = jax.nn.gelu(jnp.dot(h, win_ref[:, width:], preferred_element_type=F32))
    v = _rms(v, gv_ref[...])
    if emit_v:
        out_refs[1][...] = v.reshape(nb, lt, width)
    vb = v.astype(BF16)
    bias = bs_ref[...]
    gated = []
    for c in range(rows // t):
        sv = [jnp.dot(ws_ref[gi], vb[c * t:(c + 1) * t, gi * gd:(gi + 1) * gd],
                      preferred_element_type=F32) for gi in range(SGU_GROUPS)]
        sv = jnp.concatenate(sv, axis=1) + bias
        gated.append((u[c * t:(c + 1) * t] * sv).astype(BF16))
    gated = jnp.concatenate(gated, axis=0) if len(gated) > 1 else gated[0]
    r = jnp.dot(gated, wout_ref[...], preferred_element_type=F32).reshape(nb, lt, d)
    o_ref[...] = x_ref[...] + (1.0 + gt_ref[...]) * r


def _sgu(x, g, sh, sc, gt, w_in, g_v, w_s, b_full, w_out, emit_v):
    b, l, d = x.shape
    t = w_s.shape[1]
    width = g_v.shape[1]
    nb, lt = _row_tiles(b, l, 256)
    assert lt % t == 0
    act, vec, par, _ = _tile_specs(b, l, d, nb, lt)
    out_specs = [act(d)]
    out_shape = [jax.ShapeDtypeStruct((b, l, d), F32)]
    if emit_v:
        out_specs.append(act(width))
        out_shape.append(jax.ShapeDtypeStruct((b, l, width), F32))
    return pl.pallas_call(
        functools.partial(_sgu_kernel, t=t, emit_v=emit_v),
        grid=(b // nb, l // lt),
        in_specs=[act(d), par((1, d)), vec(d), vec(d), vec(d), par(w_in.shape), par((1, width)),
                  par(w_s.shape), par(b_full.shape), par(w_out.shape)],
        out_specs=out_specs,
        out_shape=out_shape,
        compiler_params=_params("parallel", "parallel"),
        name="sgu",
    )(x, g, sh, sc, gt, w_in, g_v, w_s, b_full, w_out)


def _rope_angles(pos, half):
    inv_freq = ROPE_THETA ** (-jnp.arange(half, dtype=F32) / half)
    ang = pos.astype(F32)[:, None] * inv_freq[None, :]
    return jnp.cos(ang), jnp.sin(ang)


def _diff_rope_tables(pos):
    cos, sin = _rope_angles(pos, DIFF_DH // 2)
    reps = LANES // DIFF_DH
    return (jnp.tile(jnp.concatenate([cos, cos], axis=1), (1, reps)),
            jnp.tile(jnp.concatenate([-sin, sin], axis=1), (1, reps)))


def _mla_rope_tables(pos):
    cos, sin = _rope_angles(pos, MLA_ROPE // 2)
    n = pos.shape[0]
    ones = jnp.ones((n, MLA_NOPE), F32)
    zq = jnp.zeros((n, MLA_NOPE), F32)
    zp = jnp.zeros((n, MLA_HEAD_PAD - MLA_NOPE - MLA_ROPE), F32)
    q_cos = jnp.concatenate([ones, cos, cos, zp], axis=1)
    q_sin = jnp.concatenate([zq, -sin, sin, zp], axis=1)
    k_cos = jnp.concatenate([cos, cos], axis=1)
    k_sin = jnp.concatenate([sin, sin], axis=1)
    return q_cos, q_sin, k_cos, k_sin


def _prepare_weights(p):
    d = p['w_up'].shape[1]
    w = {}
    w['w_up'] = [p['w_up'][i].astype(BF16) for i in range(p['w_up'].shape[0])]
    w['w_down'] = [p['w_down'][i].astype(BF16) for i in range(p['w_down'].shape[0])]
    w['glu_a'] = p['s5_w_glu_a'].astype(BF16)
    w['glu_b'] = p['s5_w_glu_b'].astype(BF16)
    w['diff_qkv'] = p['diff_w_qkv'].astype(BF16)
    w['diff_o'] = p['diff_w_o'].astype(BF16)
    lam = jnp.stack([p['diff_lambda_q1'], p['diff_lambda_k1'], p['diff_lambda_q2'], p['diff_lambda_k2']])
    w['diff_lam'] = jnp.pad(lam.astype(F32), ((0, 4), (0, LANES - lam.shape[1])))
    w['diff_g_sub'] = p['diff_g_sub'].reshape(1, -1)

    w['mla_dq'] = p['mla_w_dq'].astype(BF16)
    kvr = p['mla_g_kv'].shape[0]
    w['mla_dc'] = p['mla_w_dkv'][:, :kvr].astype(BF16)
    wr = p['mla_w_dkv'][:, kvr:]
    hr = MLA_ROPE // 2
    w['mla_dr'] = wr.astype(BF16)
    w['mla_drr'] = jnp.concatenate([-wr[:, hr:], wr[:, :hr]], axis=1).astype(BF16)
    w['mla_g_q'] = p['mla_g_q'].reshape(1, -1)
    w['mla_g_kv'] = p['mla_g_kv'].reshape(1, -1)
    pad = MLA_HEAD_PAD - MLA_NOPE - MLA_ROPE
    qr = p['mla_w_uq'].shape[0]
    uq = p['mla_w_uq'].reshape(qr, MLA_HEADS, MLA_NOPE + MLA_ROPE)
    w['mla_uq'] = jnp.pad(uq, ((0, 0), (0, 0), (0, pad))).reshape(qr, -1).astype(BF16)
    uk = jnp.pad(p['mla_w_uk'], ((0, 0), (0, 0), (0, MLA_HEAD_PAD - MLA_NOPE)))
    w['mla_uk'] = uk.reshape(kvr, -1).astype(BF16)
    place = jnp.pad(jnp.eye(MLA_ROPE, dtype=F32), ((0, 0), (MLA_NOPE, pad)))
    w['mla_place'] = jnp.tile(place, (1, MLA_HEADS)).astype(BF16)
    w['mla_uv'] = p['mla_w_uv'].reshape(kvr, -1).astype(BF16)
    w['mla_o'] = p['mla_w_o'].astype(BF16)

    w['sgu_in'] = p['sgu_w_in'].astype(BF16)
    w['sgu_g_v'] = p['sgu_g_v'].reshape(1, -1)
    w['sgu_out'] = p['sgu_w_out'].astype(BF16)
    return w


def _sgu_spatial(w_s, b_s, t, width):
    ws = jnp.tril(w_s[:, :t, :t]).astype(BF16)
    gd = width // SGU_GROUPS
    b_full = jnp.repeat(b_s[:, :t].T.astype(F32), gd, axis=1)
    return ws, b_full


def _run_trunk(x, mods, pos, past, p, w, s5_tables):
    b, l, d = x.shape
    new = {}
    depth = p['g_mix'].shape[0]
    for i in range(depth):
        sh1, sc1, gt1, sh2, sc2, gt2 = [mods[i][:, k][:, None, :] for k in range(6)]
        g_mix = p['g_mix'][i].reshape(1, d)
        kind = i % N_MIXERS
        if kind == 0:
            h = _modulate(x, g_mix, sh1, sc1)
            h0_re = None if past is None else past['s5_re']
            h0_im = None if past is None else past['s5_im']
            y, new['s5_re'], new['s5_im'] = _s5_scan(h, s5_tables, h0_re, h0_im)
            x = _s5_glu(y, h, x, p['s5_d'].reshape(1, d), gt1, w['glu_a'], w['glu_b'])
        elif kind == 1:
            cos, sin = _diff_rope_tables(pos)
            k, v, qb, kb, vb = _diff_qkv(x, g_mix, sh1, sc1, w['diff_qkv'], cos, sin)
            if past is None:
                sources = [(kb, vb)]
            else:
                pk = past['diff_k'].reshape(b, -1, d)
                pv = past['diff_v'].reshape(b, -1, d)
                sources = [(pk, pv), (kb, vb)]
            o = _diff_attention(qb, sources, w['diff_lam'], w['diff_g_sub'], causal=past is None)
            x = _proj_residual(o, w['diff_o'], x, gt1)
            new['diff_k'] = k.reshape(b, l, DIFF_HEADS, 2 * DIFF_DH)
            new['diff_v'] = v.reshape(b, l, DIFF_HEADS, 2 * DIFF_DH)
        elif kind == 2:
            q_cos, q_sin, k_cos, k_sin = _mla_rope_tables(pos)
            cq, ckv, krope = _mla_down(x, g_mix, sh1, sc1, w['mla_dq'], w['mla_dc'], w['mla_dr'],
                                       w['mla_drr'], w['mla_g_q'], w['mla_g_kv'], k_cos, k_sin)
            qb = _mla_qup(cq, w['mla_uq'], q_cos, q_sin)
            sources = []
            if past is not None:
                sources.append(_mla_kvup(past['mla_ckv'], past['mla_krope'],
                                         w['mla_uk'], w['mla_place'], w['mla_uv']))
            sources.append(_mla_kvup(ckv, krope, w['mla_uk'], w['mla_place'], w['mla_uv']))
            o = _mla_attention(qb, sources, causal=past is None)
            x = _proj_residual(o, w['mla_o'], x, gt1)
            new['mla_ckv'], new['mla_krope'] = ckv, krope
        else:
            t = min(l, SGU_CHUNK)
            ws, b_full = _sgu_spatial(p['sgu_w_s'], p['sgu_b_s'], t, w['sgu_g_v'].shape[1])
            outs = _sgu(x, g_mix, sh1, sc1, gt1, w['sgu_in'], w['sgu_g_v'], ws, b_full, w['sgu_out'],
                        emit_v=past is not None)
            x = outs[0]
            if past is not None:
                new['sgu_v'] = outs[1]
        x = _mlp(x, p['g_ffn'][i].reshape(1, d), sh2, sc2, gt2, w['w_up'][i], w['w_down'][i],
                 p['g_final'].reshape(1, d), final_norm=(i == depth - 1))
    return x, new


def kernel(x_prompt, x_sample, c_prompt, c_sample, state_s5_re, state_s5_im, cache_diff_k, cache_diff_v, cache_mla_ckv, cache_mla_krope, w_ada, b_ada, g_mix, g_ffn, w_up, w_down, g_final, s5_a_re, s5_a_im, s5_b_re, s5_b_im, s5_c_re, s5_c_im, s5_d, s5_log_dt, s5_w_glu_a, s5_w_glu_b, diff_w_qkv, diff_lambda_q1, diff_lambda_k1, diff_lambda_q2, diff_lambda_k2, diff_g_sub, diff_w_o, mla_w_dq, mla_g_q, mla_w_uq, mla_w_dkv, mla_g_kv, mla_w_uk, mla_w_uv, mla_w_o, sgu_w_in, sgu_g_v, sgu_w_s, sgu_b_s, sgu_w_out):
    p = {
        'w_ada': w_ada, 'b_ada': b_ada, 'g_mix': g_mix, 'g_ffn': g_ffn,
        'w_up': w_up, 'w_down': w_down, 'g_final': g_final,
        's5_d': s5_d, 's5_w_glu_a': s5_w_glu_a, 's5_w_glu_b': s5_w_glu_b,
        'diff_w_qkv': diff_w_qkv, 'diff_lambda_q1': diff_lambda_q1, 'diff_lambda_k1': diff_lambda_k1,
        'diff_lambda_q2': diff_lambda_q2, 'diff_lambda_k2': diff_lambda_k2,
        'diff_g_sub': diff_g_sub, 'diff_w_o': diff_w_o,
        'mla_w_dq': mla_w_dq, 'mla_g_q': mla_g_q, 'mla_w_uq': mla_w_uq, 'mla_w_dkv': mla_w_dkv,
        'mla_g_kv': mla_g_kv, 'mla_w_uk': mla_w_uk, 'mla_w_uv': mla_w_uv, 'mla_w_o': mla_w_o,
        'sgu_w_in': sgu_w_in, 'sgu_g_v': sgu_g_v, 'sgu_w_s': sgu_w_s, 'sgu_b_s': sgu_b_s,
        'sgu_w_out': sgu_w_out,
    }
    past = {
        's5_re': state_s5_re, 's5_im': state_s5_im,
        'diff_k': cache_diff_k, 'diff_v': cache_diff_v,
        'mla_ckv': cache_mla_ckv, 'mla_krope': cache_mla_krope,
    }
    bp, lp, d = x_prompt.shape
    bs, ls, _ = x_sample.shape
    depth = w_ada.shape[0]

    c_all = jnp.concatenate([c_prompt, c_sample], axis=0)
    c_all = jnp.pad(c_all, ((0, (-c_all.shape[0]) % 8), (0, 0)))
    mod = _adaln(c_all, w_ada, b_ada).reshape(depth, c_all.shape[0], 6, d)
    mods_p = [mod[i, :bp] for i in range(depth)]
    mods_s = [mod[i, bp:bp + bs] for i in range(depth)]

    w = _prepare_weights(p)
    max_steps = max((lp // S5_SUB).bit_length() - 1, (ls // S5_SUB).bit_length() - 1, 1)
    s5_tables = _s5_tables(s5_a_re, s5_a_im, s5_b_re, s5_b_im, s5_c_re, s5_c_im, s5_log_dt, max_steps)

    pos_p = jnp.arange(lp, dtype=jnp.int32)
    pos_s = cache_diff_k.shape[1] + jnp.arange(ls, dtype=jnp.int32)
    y_prompt, sp = _run_trunk(x_prompt, mods_p, pos_p, None, p, w, s5_tables)
    y_sample, ss = _run_trunk(x_sample, mods_s, pos_s, past, p, w, s5_tables)
    return (y_prompt, y_sample,
            sp['s5_re'], sp['s5_im'], ss['s5_re'], ss['s5_im'],
            sp['diff_k'], sp['diff_v'], ss['diff_k'], ss['diff_v'],
            sp['mla_ckv'], sp['mla_krope'], ss['mla_ckv'], ss['mla_krope'],
            ss['sgu_v'])
```

```python
import functools
import math

import jax
import jax.numpy as jnp
from jax import lax
from jax.experimental import pallas as pl
from jax.experimental.pallas import tpu as pltpu

F32 = jnp.float32
BF16 = jnp.bfloat16
HIGHEST = lax.Precision.HIGHEST

NORM_EPS = 1e-6
ROPE_THETA = 10000.0
NEG_INF = -1e30
LOG2_E = math.log2(math.e)
CHUNK = 64
N_MIXERS = 4

S5_GROUP = 16
S5_SUB = 16

DIFF_HEADS = 8
DIFF_DH = 64
LAMBDA_INIT = 0.8 - 0.6 * math.exp(-0.3 * 1)

MLA_HEADS = 16
MLA_NOPE = 64
MLA_ROPE = 32
MLA_V = 64
MLA_SCALE = (MLA_NOPE + MLA_ROPE) ** -0.5
MLA_HEAD_PAD = 128

SGU_CHUNK = 128
SGU_GROUPS = 8

LANES = 128
ROW_TILE = 512
ATTN_TQ = 256
ATTN_STREAMS = 2
ATTN_TK = 256
VMEM_LIMIT_BYTES = 48 * 1024 * 1024


def _params(*sem):
    return pltpu.CompilerParams(dimension_semantics=sem, vmem_limit_bytes=VMEM_LIMIT_BYTES)


def _row_tiles(batch, length, rows=ROW_TILE):
    if length >= rows:
        assert length % rows == 0
        return 1, rows
    nb = max(1, min(batch, rows // length))
    while batch % nb:
        nb -= 1
    return nb, length


def _rms(x, g):
    return x * lax.rsqrt(jnp.mean(x * x, axis=-1, keepdims=True) + NORM_EPS) * g


def _modulated(x_ref, g_ref, sh_ref, sc_ref):
    return _rms(x_ref[...], g_ref[...]) * (1.0 + sc_ref[...]) + sh_ref[...]


def _rotate_pairs(x, half, period, lo):
    width = x.shape[-1]
    lane = lax.broadcasted_iota(jnp.int32, x.shape, x.ndim - 1) % period
    fwd = pltpu.roll(x, width - half, axis=x.ndim - 1)
    bwd = pltpu.roll(x, half, axis=x.ndim - 1)
    return jnp.where((lane >= lo) & (lane < lo + half), fwd, bwd)


def _adaln_kernel(c_ref, w_ref, b_ref, o_ref):
    c = c_ref[...]
    s = c * jax.nn.sigmoid(c)
    o_ref[0] = jnp.dot(s, w_ref[0], precision=HIGHEST, preferred_element_type=F32) + b_ref[0]


def _adaln(c_all, w_ada, b_ada):
    depth, d, n = w_ada.shape
    rows = c_all.shape[0]
    tn = n // 4
    return pl.pallas_call(
        _adaln_kernel,
        grid=(depth, n // tn),
        in_specs=[pl.BlockSpec((rows, d), lambda i, j: (0, 0)),
                  pl.BlockSpec((1, d, tn), lambda i, j: (i, 0, j)),
                  pl.BlockSpec((1, 1, tn), lambda i, j: (i, 0, j))],
        out_specs=pl.BlockSpec((1, rows, tn), lambda i, j: (i, 0, j)),
        out_shape=jax.ShapeDtypeStruct((depth, rows, n), F32),
        compiler_params=_params("parallel", "parallel"),
        name="adaln",
    )(c_all, w_ada, b_ada.reshape(depth, 1, n))


def _tile_specs(batch, length, d, nb, lt, extra_axes=0):
    if extra_axes == 0:
        act = lambda width: pl.BlockSpec((nb, lt, width), lambda i, j: (i, j, 0))
        vec = lambda width: pl.BlockSpec((nb, 1, width), lambda i, j: (i, 0, 0))
        par = lambda shape: pl.BlockSpec(shape, lambda i, j: (0,) * len(shape))
        pos = lambda width: pl.BlockSpec((lt, width), lambda i, j: (j, 0))
    else:
        act = lambda width: pl.BlockSpec((nb, lt, width), lambda i, j, f: (i, j, 0))
        vec = lambda width: pl.BlockSpec((nb, 1, width), lambda i, j, f: (i, 0, 0))
        par = lambda shape: pl.BlockSpec(shape, lambda i, j, f: (0,) * len(shape))
        pos = lambda width: pl.BlockSpec((lt, width), lambda i, j, f: (j, 0))
    return act, vec, par, pos


def _modulate_kernel(x_ref, g_ref, sh_ref, sc_ref, o_ref):
    o_ref[...] = _modulated(x_ref, g_ref, sh_ref, sc_ref)


def _modulate(x, g, sh, sc):
    b, l, d = x.shape
    nb, lt = _row_tiles(b, l)
    act, vec, par, _ = _tile_specs(b, l, d, nb, lt)
    return pl.pallas_call(
        _modulate_kernel,
        grid=(b // nb, l // lt),
        in_specs=[act(d), par((1, d)), vec(d), vec(d)],
        out_specs=act(d),
        out_shape=jax.ShapeDtypeStruct((b, l, d), F32),
        compiler_params=_params("parallel", "parallel"),
        name="modulate",
    )(x, g, sh, sc)


def _mlp_kernel(x_ref, g_ref, sh_ref, sc_ref, gt_ref, wu_ref, wd_ref, gf_ref, o_ref, h_s, acc_s,
                *, final_norm):
    f = pl.program_id(2)
    nb, lt, d = x_ref.shape

    @pl.when(f == 0)
    def _():
        h_s[...] = _modulated(x_ref, g_ref, sh_ref, sc_ref).reshape(nb * lt, d).astype(BF16)
        acc_s[...] = jnp.zeros_like(acc_s)

    a = jnp.dot(h_s[...], wu_ref[...], preferred_element_type=F32)
    a = jnp.square(jnp.maximum(a, 0.0)).astype(BF16)
    acc_s[...] += jnp.dot(a, wd_ref[...], preferred_element_type=F32)

    @pl.when(f == pl.num_programs(2) - 1)
    def _():
        y = x_ref[...] + (1.0 + gt_ref[...]) * acc_s[...].reshape(nb, lt, d)
        if final_norm:
            y = _rms(y, gf_ref[...])
        o_ref[...] = y


def _mlp(x, g, sh, sc, gt, w_up, w_down, g_final, final_norm, tf=1024):
    b, l, d = x.shape
    dff = w_up.shape[1]
    nb, lt = _row_tiles(b, l)
    act, vec, par, _ = _tile_specs(b, l, d, nb, lt, extra_axes=1)
    return pl.pallas_call(
        functools.partial(_mlp_kernel, final_norm=final_norm),
        grid=(b // nb, l // lt, dff // tf),
        in_specs=[act(d), par((1, d)), vec(d), vec(d), vec(d),
                  pl.BlockSpec((d, tf), lambda i, j, f: (0, f)),
                  pl.BlockSpec((tf, d), lambda i, j, f: (f, 0)),
                  par((1, d))],
        out_specs=act(d),
        out_shape=jax.ShapeDtypeStruct((b, l, d), F32),
        scratch_shapes=[pltpu.VMEM((nb * lt, d), BF16), pltpu.VMEM((nb * lt, d), F32)],
        compiler_params=_params("parallel", "parallel", "arbitrary"),
        name="mlp",
    )(x, g, sh, sc, gt, w_up, w_down, g_final)


def _proj_res_kernel(a_ref, w_ref, x_ref, gt_ref, o_ref):
    nb, lt, k = a_ref.shape
    d = w_ref.shape[1]
    r = jnp.dot(a_ref[...].reshape(nb * lt, k), w_ref[...], preferred_element_type=F32)
    o_ref[...] = x_ref[...] + (1.0 + gt_ref[...]) * r.reshape(nb, lt, d)


def _proj_residual(a, w, x, gt):
    b, l, d = x.shape
    k = a.shape[-1]
    nb, lt = _row_tiles(b, l)
    act, vec, par, _ = _tile_specs(b, l, d, nb, lt)
    return pl.pallas_call(
        _proj_res_kernel,
        grid=(b // nb, l // lt),
        in_specs=[act(k), par((k, d)), act(d), vec(d)],
        out_specs=act(d),
        out_shape=jax.ShapeDtypeStruct((b, l, d), F32),
        compiler_params=_params("parallel", "parallel"),
        name="proj_residual",
    )(a, w, x, gt)


def _s5_kernel(*refs, nch, batch, nsteps, has_h0):
    if has_h0:
        u_ref, bc_ref, m_ref, cc_ref, a1_ref, a2_ref, h0_ref, y_ref, hl_ref = refs
    else:
        u_ref, bc_ref, m_ref, cc_ref, a1_ref, a2_ref, y_ref, hl_ref = refs
    u = u_ref[0]
    rows = u.shape[0]
    x = jnp.dot(u, bc_ref[0], precision=HIGHEST, preferred_element_type=F32)
    half = x.shape[1] // 2
    if has_h0:
        h0 = h0_ref[0]
        x = x + a1_ref[0, 0:1, :] * h0 + a2_ref[0, 0:1, :] * pltpu.roll(h0, half, axis=1)
    kidx = lax.broadcasted_iota(jnp.int32, x.shape, 0) & (nch - 1)
    for j in range(nsteps):
        s = 1 << j
        xs = jnp.where(kidx >= s, pltpu.roll(x, s, axis=0), 0.0)
        x = x + a1_ref[0, j:j + 1, :] * xs + a2_ref[0, j:j + 1, :] * pltpu.roll(xs, half, axis=1)
    for b in range(batch):
        r = (b + 1) * nch - 1
        hl_ref[0, b:b + 1, :] = x[r:r + 1, :]
    hstart = jnp.where(kidx >= 1, pltpu.roll(x, 1, axis=0), 0.0)
    if has_h0:
        hstart = hstart + h0
    y = jnp.dot(u.astype(BF16), m_ref[0], preferred_element_type=F32)
    y = y + jnp.dot(hstart.astype(BF16), cc_ref[0], preferred_element_type=F32)
    y_ref[0] = y


def _s5_tables(a_re, a_im, b_re, b_im, c_re, c_im, log_dt, max_steps):
    g, p = a_re.shape
    sub = S5_SUB
    dt = jnp.exp(log_dt.astype(F32))[:, None]
    zr, zi = a_re.astype(F32) * dt, a_im.astype(F32) * dt

    def zpow(n):
        mag = jnp.exp(zr * n)
        return mag * jnp.cos(zi * n), mag * jnp.sin(zi * n)

    er = jnp.expm1(zr) * jnp.cos(zi) - 2.0 * jnp.square(jnp.sin(0.5 * zi))
    ei = jnp.exp(zr) * jnp.sin(zi)
    den = a_re * a_re + a_im * a_im
    fr = (er * a_re + ei * a_im) / den
    fi = (ei * a_re - er * a_im) / den
    bb_re = fr[..., None] * b_re - fi[..., None] * b_im
    bb_im = fr[..., None] * b_im + fi[..., None] * b_re

    lags = jnp.arange(sub + 1, dtype=F32)[:, None, None]
    pr, pi = zpow(lags)

    qr, qi = pr[:sub][::-1], pi[:sub][::-1]
    bcr = qr[..., None] * bb_re[None] - qi[..., None] * bb_im[None]
    bci = qr[..., None] * bb_im[None] + qi[..., None] * bb_re[None]
    bc = jnp.concatenate([bcr, bci], axis=2)
    bc = bc.transpose(1, 0, 3, 2).reshape(g, sub * S5_GROUP, 2 * p)

    ar, ai = pr[1:], pi[1:]
    ccr = c_re[None] * ar[:, :, None, :] - c_im[None] * ai[:, :, None, :]
    cci = c_re[None] * ai[:, :, None, :] + c_im[None] * ar[:, :, None, :]
    cc = jnp.concatenate([ccr, -cci], axis=3)
    cc = cc.transpose(1, 3, 0, 2).reshape(g, 2 * p, sub * S5_GROUP)

    kr = (jnp.einsum('gcp,lgp,gpd->lgcd', c_re, pr[:sub], bb_re, precision=HIGHEST)
          - jnp.einsum('gcp,lgp,gpd->lgcd', c_re, pi[:sub], bb_im, precision=HIGHEST)
          - jnp.einsum('gcp,lgp,gpd->lgcd', c_im, pr[:sub], bb_im, precision=HIGHEST)
          - jnp.einsum('gcp,lgp,gpd->lgcd', c_im, pi[:sub], bb_re, precision=HIGHEST))
    tt = jnp.arange(sub)
    lag = tt[None, :] - tt[:, None]
    toep = jnp.where((lag >= 0)[:, :, None, None, None],
                     kr[jnp.clip(lag, 0, sub - 1)], 0.0)
    m = toep.transpose(2, 0, 4, 1, 3).reshape(g, sub * S5_GROUP, sub * S5_GROUP)

    steps = (sub * (2 ** jnp.arange(max_steps))).astype(F32)[:, None, None]
    sr, si = zpow(steps)
    a1 = jnp.concatenate([sr, sr], axis=-1).transpose(1, 0, 2)
    a2 = jnp.concatenate([-si, si], axis=-1).transpose(1, 0, 2)
    pad = (-max_steps) % 8
    a1 = jnp.pad(a1, ((0, 0), (0, pad), (0, 0)))
    a2 = jnp.pad(a2, ((0, 0), (0, pad), (0, 0)))
    return bc, m.astype(BF16), cc.astype(BF16), a1, a2


def _s5_scan(h, tables, h0_re, h0_im):
    bc, m, cc, a1, a2 = tables
    b, l, d = h.shape
    g = d // S5_GROUP
    p2 = bc.shape[-1]
    nch = l // S5_SUB
    assert nch & (nch - 1) == 0
    nsteps = nch.bit_length() - 1
    rows = b * nch
    k = S5_SUB * S5_GROUP
    u = h.reshape(b, nch, S5_SUB, g, S5_GROUP).transpose(3, 0, 1, 2, 4).reshape(g, rows, k)
    has_h0 = h0_re is not None
    grp = lambda shape: pl.BlockSpec((1,) + shape, lambda i: (i, 0, 0))
    args = [u, bc, m, cc, a1, a2]
    in_specs = [grp((rows, k)), grp((k, p2)), grp((k, k)), grp((p2, k)),
                grp(a1.shape[1:]), grp(a2.shape[1:])]
    if has_h0:
        h0 = jnp.concatenate([h0_re, h0_im], axis=-1).astype(F32).transpose(1, 0, 2)
        h0 = jnp.pad(h0[:, :, None, :], ((0, 0), (0, 0), (0, nch - 1), (0, 0))).reshape(g, rows, p2)
        args.append(h0)
        in_specs.append(grp((rows, p2)))
    y, hl = pl.pallas_call(
        functools.partial(_s5_kernel, nch=nch, batch=b, nsteps=nsteps, has_h0=has_h0),
        grid=(g,),
        in_specs=in_specs,
        out_specs=[grp((rows, k)), grp((b, p2))],
        out_shape=[jax.ShapeDtypeStruct((g, rows, k), F32), jax.ShapeDtypeStruct((g, b, p2), F32)],
        compiler_params=_params("parallel"),
        name="s5_scan",
    )(*args)
    y = y.reshape(g, b, nch, S5_SUB, S5_GROUP).transpose(1, 2, 3, 0, 4).reshape(b, l, d)
    hl = hl.transpose(1, 0, 2)
    return y, hl[..., :p2 // 2], hl[..., p2 // 2:]


def _glu_kernel(y_ref, h_ref, x_ref, d_ref, gt_ref, wa_ref, wb_ref, o_ref):
    nb, lt, d = x_ref.shape
    z = jax.nn.gelu(y_ref[...] + d_ref[...] * h_ref[...]).reshape(nb * lt, d).astype(BF16)
    a = jnp.dot(z, wa_ref[...], preferred_element_type=F32)
    b = jnp.dot(z, wb_ref[...], preferred_element_type=F32)
    out = (a * jax.nn.sigmoid(b)).reshape(nb, lt, d)
    o_ref[...] = x_ref[...] + (1.0 + gt_ref[...]) * out


def _s5_glu(y, h, x, d_skip, gt, wa, wb):
    b, l, d = x.shape
    nb, lt = _row_tiles(b, l)
    act, vec, par, _ = _tile_specs(b, l, d, nb, lt)
    return pl.pallas_call(
        _glu_kernel,
        grid=(b // nb, l // lt),
        in_specs=[act(d), act(d), act(d), par((1, d)), vec(d), par((d, d)), par((d, d))],
        out_specs=act(d),
        out_shape=jax.ShapeDtypeStruct((b, l, d), F32),
        compiler_params=_params("parallel", "parallel"),
        name="s5_glu",
    )(y, h, x, d_skip, gt, wa, wb)


def _diff_qkv_kernel(x_ref, g_ref, sh_ref, sc_ref, w_ref, cos_ref, sin_ref,
                     k_ref, v_ref, qb_ref, kb_ref, vb_ref):
    nb, lt, d = x_ref.shape
    h = _modulated(x_ref, g_ref, sh_ref, sc_ref).reshape(nb * lt, d).astype(BF16)
    reps = d // cos_ref.shape[1]
    cos = jnp.tile(cos_ref[...], (1, reps))[None]
    sin = jnp.tile(sin_ref[...], (1, reps))[None]

    def roped(cols):
        t = jnp.dot(h, w_ref[:, cols * d:(cols + 1) * d], preferred_element_type=F32)
        r = _rotate_pairs(t, DIFF_DH // 2, DIFF_DH, 0)
        return t.reshape(nb, lt, d) * cos + r.reshape(nb, lt, d) * sin

    q = roped(0)
    qb_ref[...] = (q * (LOG2_E * DIFF_DH ** -0.5)).astype(BF16)
    k = roped(1)
    kb_ref[...] = k.astype(BF16)
    v = jnp.dot(h, w_ref[:, 2 * d:], preferred_element_type=F32).reshape(nb, lt, d)
    vb_ref[...] = v.astype(BF16)
    hw = k_ref.shape[2]
    heads = d // hw
    for hh in range(heads):
        k_ref[:, pl.ds(hh, lt, stride=heads), :] = k[:, :, hh * hw:(hh + 1) * hw]
        v_ref[:, pl.ds(hh, lt, stride=heads), :] = v[:, :, hh * hw:(hh + 1) * hw]


def _diff_qkv(x, g, sh, sc, w_qkv, cos, sin):
    b, l, d = x.shape
    nb, lt = _row_tiles(b, l)
    act, vec, par, pos = _tile_specs(b, l, d, nb, lt)
    hw = d // DIFF_HEADS
    return pl.pallas_call(
        _diff_qkv_kernel,
        grid=(b // nb, l // lt),
        in_specs=[act(d), par((1, d)), vec(d), vec(d), par((d, 3 * d)),
                  pos(cos.shape[1]), pos(sin.shape[1])],
        out_specs=[pl.BlockSpec((nb, lt * DIFF_HEADS, hw), lambda i, j: (i, j, 0))] * 2 + [act(d)] * 3,
        out_shape=[jax.ShapeDtypeStruct((b, l * DIFF_HEADS, hw), F32)] * 2
                  + [jax.ShapeDtypeStruct((b, l, d), BF16)] * 3,
        compiler_params=_params("parallel", "parallel"),
        name="diff_qkv",
    )(x, g, sh, sc, w_qkv, cos, sin)


_NT = (((1,), (1,)), ((), ()))
_TN = (((0,), (0,)), ((), ()))


def _kv_sweep(step, n_keys, tk, dynamic_full=None):
    if dynamic_full is not None:
        lax.fori_loop(0, dynamic_full, lambda j, c: (step(pl.multiple_of(j * tk, tk), tk), c)[1], 0)
        return
    n_full = n_keys // tk
    if n_full == 1:
        step(0, tk)
    elif n_full > 1:
        lax.fori_loop(0, n_full, lambda j, c: (step(pl.multiple_of(j * tk, tk), tk), c)[1], 0)
    if n_keys % tk:
        step(n_full * tk, n_keys % tk)


def _softmax_update(m_s, l_s, acc_s, i, s, vb):
    m_prev = m_s[i]
    m_new = jnp.maximum(m_prev, jnp.max(s, axis=0, keepdims=True))
    alpha = jnp.exp2(m_prev - m_new)
    p = jnp.exp2(s - m_new)
    l_s[i] = alpha * l_s[i] + jnp.sum(p, axis=0, keepdims=True)
    acc_s[i] = alpha * acc_s[i] + lax.dot_general(vb, p.astype(BF16), _TN, preferred_element_type=F32)
    m_s[i] = m_new


def _softmax_init(m_s, l_s, acc_s):
    m_s[...] = jnp.full_like(m_s, NEG_INF)
    l_s[...] = jnp.zeros_like(l_s)
    acc_s[...] = jnp.zeros_like(acc_s)


def _softmax_result(l_s, acc_s, i):
    return acc_s[i] / l_s[i]


def _softmax_scratch(n, vw, queries):
    return [pltpu.VMEM((n, 1, queries), F32), pltpu.VMEM((n, 1, queries), F32),
            pltpu.VMEM((n, vw, queries), F32)]


def _attend(scores, kv_refs, m_s, l_s, acc_s, s_bufs, *, tq, tk, causal, qi):
    ns = len(scores)
    _softmax_init(m_s, l_s, acc_s)
    update = functools.partial(_softmax_update, m_s, l_s, acc_s)

    def make_step(k_ref, v_ref):
        def step(start, size):
            kb = k_ref[0, pl.ds(start, size), :].astype(BF16)
            vb = v_ref[0, pl.ds(start, size), :].astype(BF16)
            for i in range(ns):
                update(i, scores[i](kb), vb)
        return step

    if causal:
        assert ns == 2 and tq == tk and tq % CHUNK == 0
        k_ref, v_ref = kv_refs
        s_a, s_b = s_bufs

        def qk(dst, blk, streams):
            kb = k_ref[0, pl.ds(pl.multiple_of(blk * tk, tk), tk), :].astype(BF16)
            for i in streams:
                dst[i] = scores[i](kb)

        def pv(src, blk, streams, masked=None):
            vb = v_ref[0, pl.ds(pl.multiple_of(blk * tk, tk), tk), :].astype(BF16)
            for i in streams:
                update(i, jnp.where(vis, src[i], NEG_INF) if i == masked else src[i], vb)

        def pair(jj, carry):
            qk(s_b, 2 * jj + 1, (0, 1))
            pv(s_a, 2 * jj, (0, 1))
            qk(s_a, 2 * jj + 2, (0, 1))
            pv(s_b, 2 * jj + 1, (0, 1))
            return carry

        qk(s_a, 0, (0, 1))
        lax.fori_loop(0, qi, pair, 0)
        kc = lax.broadcasted_iota(jnp.int32, (tk, tq), 0) // CHUNK
        qc = lax.broadcasted_iota(jnp.int32, (tk, tq), 1) // CHUNK
        vis = jnp.concatenate([kc <= qc] * 2, axis=1)
        qk(s_b, 2 * qi + 1, (1,))
        pv(s_a, 2 * qi, (0, 1), masked=0)
        pv(s_b, 2 * qi + 1, (1,), masked=1)
    else:
        for i in range(len(kv_refs) // 2):
            k_ref, v_ref = kv_refs[2 * i], kv_refs[2 * i + 1]
            _kv_sweep(make_step(k_ref, v_ref), k_ref.shape[1], tk)
    return [_softmax_result(l_s, acc_s, i) for i in range(ns)]


def _diff_attn_kernel(*refs, tq, tk, causal, n_src):
    q_ref = refs[0]
    kv_refs = refs[1:1 + 2 * n_src]
    lam_ref, gsub_ref, o_ref, m_s, l_s, acc_s, *s_bufs = refs[1 + 2 * n_src:]
    ns = q_ref.shape[1] // tq

    def make_score(i):
        q2 = _diff_query_groups(q_ref[0, i * tq:(i + 1) * tq, :])
        return lambda kb: lax.dot_general(kb, q2, _NT, preferred_element_type=F32)

    outs = _attend([make_score(i) for i in range(ns)], kv_refs, m_s, l_s, acc_s, s_bufs,
                   tq=tq, tk=tk, causal=causal, qi=pl.program_id(2))
    lam = _diff_lambda(lam_ref)
    for i, o2 in enumerate(outs):
        o = (o2[:, :tq] - lam * o2[:, tq:]).T
        o = _rms(o, gsub_ref[...]) * (1.0 - LAMBDA_INIT)
        o_ref[0, i * tq:(i + 1) * tq, :] = o.astype(o_ref.dtype)


def _score_buffers(ns, tq, tk, causal):
    return [pltpu.VMEM((ns, tk, 2 * tq), F32)] * 2 if causal else []


def _attn_tiles(l, causal):
    tq = min(ATTN_TQ, l)
    ns = ATTN_STREAMS if l % (ATTN_STREAMS * tq) == 0 else 1
    return tq, ns, (tq if causal else ATTN_TK)


def _diff_attention(qb, kv_sources, lam_rows, g_sub, causal):
    b, l, d = qb.shape
    hw = 2 * DIFF_DH
    tq, ns, tk = _attn_tiles(l, causal)
    in_specs = [pl.BlockSpec((1, ns * tq, hw), lambda bi, h, qi: (bi, qi, h))]
    args = [qb]
    for k, v in kv_sources:
        lk = k.shape[1]
        in_specs += [pl.BlockSpec((1, lk, hw), lambda bi, h, qi: (bi, 0, h))] * 2
        args += [k, v]
    in_specs += [pl.BlockSpec(lam_rows.shape, lambda bi, h, qi: (0, 0)),
                 pl.BlockSpec((1, hw), lambda bi, h, qi: (0, 0))]
    args += [lam_rows, g_sub]
    return pl.pallas_call(
        functools.partial(_diff_attn_kernel, tq=tq, tk=tk, causal=causal, n_src=len(kv_sources)),
        grid=(b, d // hw, l // (ns * tq)),
        in_specs=in_specs,
        out_specs=pl.BlockSpec((1, ns * tq, hw), lambda bi, h, qi: (bi, qi, h)),
        out_shape=jax.ShapeDtypeStruct((b, l, d), BF16),
        scratch_shapes=_softmax_scratch(ns, hw, 2 * tq) + _score_buffers(ns, tq, tk, causal),
        compiler_params=_params("parallel", "parallel", "arbitrary"),
        name="diff_attention",
    )(*args)


def _diff_lambda(lam_ref):
    lp = lam_ref[...]
    return (jnp.exp(jnp.sum(lp[0:1] * lp[1:2], axis=-1, keepdims=True))
            - jnp.exp(jnp.sum(lp[2:3] * lp[3:4], axis=-1, keepdims=True)) + LAMBDA_INIT)


def _diff_query_groups(q):
    lane = lax.broadcasted_iota(jnp.int32, q.shape, 1)
    zero = jnp.zeros_like(q)
    return jnp.concatenate([jnp.where(lane < DIFF_DH, q, zero), jnp.where(lane >= DIFF_DH, q, zero)], axis=0)


def _diff_cached_kernel(q_ref, kc_ref, vc_ref, kn_ref, vn_ref, lam_ref, gsub_ref, o_ref,
                        m_s, l_s, acc_s, *, tk):
    c = pl.program_id(1)
    lq = q_ref.shape[1]
    hw = kc_ref.shape[2]
    heads = q_ref.shape[2] // hw
    update = functools.partial(_softmax_update, m_s, l_s, acc_s)
    q2 = lambda hh: _diff_query_groups(q_ref[0, :, hh * hw:(hh + 1) * hw])

    @pl.when(c == 0)
    def _():
        _softmax_init(m_s, l_s, acc_s)

    def block(j, carry):
        rows = pl.ds(pl.multiple_of(j * tk * heads, tk * heads), tk * heads)
        k_blk, v_blk = kc_ref.at[0, rows, :], vc_ref.at[0, rows, :]
        for hh in range(heads):
            kb = k_blk[pl.ds(hh, tk, stride=heads), :].astype(BF16)
            vb = v_blk[pl.ds(hh, tk, stride=heads), :].astype(BF16)
            update(hh, lax.dot_general(kb, q2(hh), _NT, preferred_element_type=F32), vb)
        return carry

    lax.fori_loop(0, kc_ref.shape[1] // (tk * heads), block, 0)

    @pl.when(c == pl.num_programs(1) - 1)
    def _():
        lam = _diff_lambda(lam_ref)
        for hh in range(heads):
            kb = kn_ref[0, :, hh * hw:(hh + 1) * hw]
            vb = vn_ref[0, :, hh * hw:(hh + 1) * hw]
            update(hh, lax.dot_general(kb, q2(hh), _NT, preferred_element_type=F32), vb)
            o2 = _softmax_result(l_s, acc_s, hh)
            o = (o2[:, :lq] - lam * o2[:, lq:]).T
            o = _rms(o, gsub_ref[...]) * (1.0 - LAMBDA_INIT)
            o_ref[0, :, hh * hw:(hh + 1) * hw] = o.astype(o_ref.dtype)


def _diff_attention_cached(qb, cache_k, cache_v, kb, vb, lam_rows, g_sub):
    b, l, d = qb.shape
    _, p, heads, hw = cache_k.shape
    pc = p if p <= 1024 else 1024
    tk = ATTN_TK if pc % ATTN_TK == 0 else CHUNK
    assert p % pc == 0 and pc % tk == 0
    new = pl.BlockSpec((1, l, d), lambda bi, c: (bi, 0, 0))
    cache = pl.BlockSpec((1, pc * heads, hw), lambda bi, c: (bi, c, 0))
    return pl.pallas_call(
        functools.partial(_diff_cached_kernel, tk=tk),
        grid=(b, p // pc),
        in_specs=[new, cache, cache, new, new,
                  pl.BlockSpec(lam_rows.shape, lambda bi, c: (0, 0)),
                  pl.BlockSpec((1, hw), lambda bi, c: (0, 0))],
        out_specs=new,
        out_shape=jax.ShapeDtypeStruct((b, l, d), BF16),
        scratch_shapes=_softmax_scratch(heads, hw, 2 * l),
        compiler_params=_params("parallel", "arbitrary"),
        name="diff_attention_cached",
    )(qb, cache_k.reshape(b, p * heads, hw), cache_v.reshape(b, p * heads, hw), kb, vb, lam_rows, g_sub)


def _mla_down_kernel(x_ref, g_ref, sh_ref, sc_ref, wq_ref, wc_ref, wr_ref, wrr_ref,
                     gq_ref, gkv_ref, cos_ref, sin_ref, cq_ref, ckv_ref, kr_ref):
    nb, lt, d = x_ref.shape
    h = _modulated(x_ref, g_ref, sh_ref, sc_ref).reshape(nb * lt, d).astype(BF16)
    dot = lambda w_ref: jnp.dot(h, w_ref[...], preferred_element_type=F32)
    cq_ref[...] = _rms(dot(wq_ref), gq_ref[...]).reshape(cq_ref.shape).astype(cq_ref.dtype)
    ckv_ref[...] = _rms(dot(wc_ref), gkv_ref[...]).reshape(ckv_ref.shape)
    r = dot(wr_ref).reshape(kr_ref.shape)
    rr = dot(wrr_ref).reshape(kr_ref.shape)
    kr_ref[...] = r * cos_ref[...][None] + rr * sin_ref[...][None]


def _mla_down(x, g, sh, sc, wq, wc, wr, wrr, g_q, g_kv, cos, sin):
    b, l, d = x.shape
    nb, lt = _row_tiles(b, l)
    act, vec, par, pos = _tile_specs(b, l, d, nb, lt)
    qr, kvr, rope = wq.shape[1], wc.shape[1], wr.shape[1]
    return pl.pallas_call(
        _mla_down_kernel,
        grid=(b // nb, l // lt),
        in_specs=[act(d), par((1, d)), vec(d), vec(d),
                  par((d, qr)), par((d, kvr)), par((d, rope)), par((d, rope)),
                  par((1, qr)), par((1, kvr)), pos(rope), pos(rope)],
        out_specs=[act(qr), act(kvr), act(rope)],
        out_shape=[jax.ShapeDtypeStruct((b, l, qr), BF16),
                   jax.ShapeDtypeStruct((b, l, kvr), F32),
                   jax.ShapeDtypeStruct((b, l, rope), F32)],
        compiler_params=_params("parallel", "parallel"),
        name="mla_down",
    )(x, g, sh, sc, wq, wc, wr, wrr, g_q, g_kv, cos, sin)


def _mla_qup_kernel(cq_ref, w_ref, cos_ref, sin_ref, q_ref):
    nb, lt, r = cq_ref.shape
    n = w_ref.shape[1]
    t = jnp.dot(cq_ref[...].reshape(nb * lt, r), w_ref[...], preferred_element_type=F32)
    rot = _rotate_pairs(t, MLA_ROPE // 2, MLA_HEAD_PAD, MLA_NOPE)
    reps = n // cos_ref.shape[1]
    cos = jnp.tile(cos_ref[...], (1, reps))[None]
    sin = jnp.tile(sin_ref[...], (1, reps))[None]
    q = t.reshape(nb, lt, n) * cos + rot.reshape(nb, lt, n) * sin
    q_ref[...] = (q * (LOG2_E * MLA_SCALE)).astype(q_ref.dtype)


def _mla_qup(cq, w_uq_pad, cos, sin):
    b, l, r = cq.shape
    n = w_uq_pad.shape[1]
    nb, lt = _row_tiles(b, l)
    act, vec, par, pos = _tile_specs(b, l, r, nb, lt)
    return pl.pallas_call(
        _mla_qup_kernel,
        grid=(b // nb, l // lt),
        in_specs=[act(r), par((r, n)), pos(cos.shape[1]), pos(sin.shape[1])],
        out_specs=act(n),
        out_shape=jax.ShapeDtypeStruct((b, l, n), BF16),
        compiler_params=_params("parallel", "parallel"),
        name="mla_q_up",
    )(cq, w_uq_pad, cos, sin)


def _mla_kvup_kernel(ckv_ref, kr_ref, wk_ref, we_ref, wv_ref, k_ref, v_ref):
    nb, lt, r = ckv_ref.shape
    c = ckv_ref[...].reshape(nb * lt, r).astype(BF16)
    kr = kr_ref[...].reshape(nb * lt, kr_ref.shape[2]).astype(BF16)
    k = (jnp.dot(c, wk_ref[...], preferred_element_type=F32)
         + jnp.dot(kr, we_ref[...], preferred_element_type=F32))
    k_ref[...] = k.reshape(k_ref.shape).astype(k_ref.dtype)
    v_ref[...] = jnp.dot(c, wv_ref[...], preferred_element_type=F32).reshape(v_ref.shape).astype(v_ref.dtype)


def _mla_kvup(ckv, krope, w_uk_pad, w_place, w_uv):
    b, l, r = ckv.shape
    rows = ROW_TILE if l % ROW_TILE == 0 else l
    nb, lt = _row_tiles(b, l, rows)
    act, vec, par, _ = _tile_specs(b, l, r, nb, lt)
    nk, nv = w_uk_pad.shape[1], w_uv.shape[1]
    rope = krope.shape[2]
    return pl.pallas_call(
        _mla_kvup_kernel,
        grid=(b // nb, l // lt),
        in_specs=[act(r), act(rope), par((r, nk)), par((rope, nk)), par((r, nv))],
        out_specs=[act(nk), act(nv)],
        out_shape=[jax.ShapeDtypeStruct((b, l, nk), BF16), jax.ShapeDtypeStruct((b, l, nv), BF16)],
        compiler_params=_params("parallel", "parallel"),
        name="mla_kv_up",
    )(ckv, krope, w_uk_pad, w_place, w_uv)


def _mla_attn_kernel(*refs, tq, tk, causal, n_src):
    q_ref = refs[0]
    kv_refs = refs[1:1 + 2 * n_src]
    o_ref, m_s, l_s, acc_s, *s_bufs = refs[1 + 2 * n_src:]
    hp = MLA_HEAD_PAD
    ns = q_ref.shape[1] // tq

    def make_score(i):
        qa = q_ref[0, i * tq:(i + 1) * tq, :hp]
        qb = q_ref[0, i * tq:(i + 1) * tq, hp:]
        return lambda kb: jnp.concatenate(
            [lax.dot_general(kb[:, :hp], qa, _NT, preferred_element_type=F32),
             lax.dot_general(kb[:, hp:], qb, _NT, preferred_element_type=F32)], axis=1)

    outs = _attend([make_score(i) for i in range(ns)], kv_refs, m_s, l_s, acc_s, s_bufs,
                   tq=tq, tk=tk, causal=causal, qi=pl.program_id(2))
    row = lax.broadcasted_iota(jnp.int32, (acc_s.shape[1], tq), 0)
    for i, o2 in enumerate(outs):
        o = jnp.where(row < MLA_V, o2[:, :tq], o2[:, tq:]).T
        o_ref[0, i * tq:(i + 1) * tq, :] = o.astype(o_ref.dtype)


def _mla_attention(qb, kv_sources, causal):
    b, l, _ = qb.shape
    tq, ns, tk = _attn_tiles(l, causal)
    qw, vw = 2 * MLA_HEAD_PAD, 2 * MLA_V
    in_specs = [pl.BlockSpec((1, ns * tq, qw), lambda bi, h, qi: (bi, qi, h))]
    args = [qb]
    for k, v in kv_sources:
        lk = k.shape[1]
        in_specs += [pl.BlockSpec((1, lk, qw), lambda bi, h, qi: (bi, 0, h)),
                     pl.BlockSpec((1, lk, vw), lambda bi, h, qi: (bi, 0, h))]
        args += [k, v]
    return pl.pallas_call(
        functools.partial(_mla_attn_kernel, tq=tq, tk=tk, causal=causal, n_src=len(kv_sources)),
        grid=(b, MLA_HEADS // 2, l // (ns * tq)),
        in_specs=in_specs,
        out_specs=pl.BlockSpec((1, ns * tq, vw), lambda bi, h, qi: (bi, qi, h)),
        out_shape=jax.ShapeDtypeStruct((b, l, MLA_HEADS * MLA_V), BF16),
        scratch_shapes=_softmax_scratch(ns, vw, 2 * tq) + _score_buffers(ns, tq, tk, causal),
        compiler_params=_params("parallel", "parallel", "arbitrary"),
        name="mla_attention",
    )(*args)


def _sgu_kernel(x_ref, g_ref, sh_ref, sc_ref, gt_ref, win_ref, gv_ref, ws_ref, bs_ref, wout_ref,
                *out_refs, t, emit_v):
    o_ref = out_refs[0]
    nb, lt, d = x_ref.shape
    rows = nb * lt
    width = gv_ref.shape[1]
    gd = width // SGU_GROUPS
    h = _modulated(x_ref, g_ref, sh_ref, sc_ref).reshape(rows, d).astype(BF16)
    u = jax.nn.gelu(jnp.dot(h, win_ref[:, :width], preferred_element_type=F32))
    v = jax.nn.gelu(jnp.dot(h, win_ref[:, width:], preferred_element_type=F32))
    v = _rms(v, gv_ref[...])
    if emit_v:
        out_refs[1][...] = v.reshape(nb, lt, width)
    vb = v.astype(BF16)
    bias = bs_ref[...]
    gated = []
    for c in range(rows // t):
        sv = [jnp.dot(ws_ref[gi], vb[c * t:(c + 1) * t, gi * gd:(gi + 1) * gd],
                      preferred_element_type=F32) for gi in range(SGU_GROUPS)]
        sv = jnp.concatenate(sv, axis=1) + bias
        gated.append((u[c * t:(c + 1) * t] * sv).astype(BF16))
    gated = jnp.concatenate(gated, axis=0) if len(gated) > 1 else gated[0]
    r = jnp.dot(gated, wout_ref[...], preferred_element_type=F32).reshape(nb, lt, d)
    o_ref[...] = x_ref[...] + (1.0 + gt_ref[...]) * r


def _sgu(x, g, sh, sc, gt, w_in, g_v, w_s, b_full, w_out, emit_v):
    b, l, d = x.shape
    t = w_s.shape[1]
    width = g_v.shape[1]
    nb, lt = _row_tiles(b, l, 256)
    assert lt % t == 0
    act, vec, par, _ = _tile_specs(b, l, d, nb, lt)
    out_specs = [act(d)]
    out_shape = [jax.ShapeDtypeStruct((b, l, d), F32)]
    if emit_v:
        out_specs.append(act(width))
        out_shape.append(jax.ShapeDtypeStruct((b, l, width), F32))
    return pl.pallas_call(
        functools.partial(_sgu_kernel, t=t, emit_v=emit_v),
        grid=(b // nb, l // lt),
        in_specs=[act(d), par((1, d)), vec(d), vec(d), vec(d), par(w_in.shape), par((1, width)),
                  par(w_s.shape), par(b_full.shape), par(w_out.shape)],
        out_specs=out_specs,
        out_shape=out_shape,
        compiler_params=_params("parallel", "parallel"),
        name="sgu",
    )(x, g, sh, sc, gt, w_in, g_v, w_s, b_full, w_out)


def _rope_angles(pos, half):
    inv_freq = ROPE_THETA ** (-jnp.arange(half, dtype=F32) / half)
    ang = pos.astype(F32)[:, None] * inv_freq[None, :]
    return jnp.cos(ang), jnp.sin(ang)


def _diff_rope_tables(pos):
    cos, sin = _rope_angles(pos, DIFF_DH // 2)
    reps = LANES // DIFF_DH
    return (jnp.tile(jnp.concatenate([cos, cos], axis=1), (1, reps)),
            jnp.tile(jnp.concatenate([-sin, sin], axis=1), (1, reps)))


def _mla_rope_tables(pos):
    cos, sin = _rope_angles(pos, MLA_ROPE // 2)
    n = pos.shape[0]
    ones = jnp.ones((n, MLA_NOPE), F32)
    zq = jnp.zeros((n, MLA_NOPE), F32)
    zp = jnp.zeros((n, MLA_HEAD_PAD - MLA_NOPE - MLA_ROPE), F32)
    q_cos = jnp.concatenate([ones, cos, cos, zp], axis=1)
    q_sin = jnp.concatenate([zq, -sin, sin, zp], axis=1)
    k_cos = jnp.concatenate([cos, cos], axis=1)
    k_sin = jnp.concatenate([sin, sin], axis=1)
    return q_cos, q_sin, k_cos, k_sin


def _prepare_weights(p):
    d = p['w_up'].shape[1]
    w = {}
    w['w_up'] = [p['w_up'][i].astype(BF16) for i in range(p['w_up'].shape[0])]
    w['w_down'] = [p['w_down'][i].astype(BF16) for i in range(p['w_down'].shape[0])]
    w['glu_a'] = p['s5_w_glu_a'].astype(BF16)
    w['glu_b'] = p['s5_w_glu_b'].astype(BF16)
    w['diff_qkv'] = p['diff_w_qkv'].astype(BF16)
    w['diff_o'] = p['diff_w_o'].astype(BF16)
    lam = jnp.stack([p['diff_lambda_q1'], p['diff_lambda_k1'], p['diff_lambda_q2'], p['diff_lambda_k2']])
    w['diff_lam'] = jnp.pad(lam.astype(F32), ((0, 4), (0, LANES - lam.shape[1])))
    w['diff_g_sub'] = p['diff_g_sub'].reshape(1, -1)

    w['mla_dq'] = p['mla_w_dq'].astype(BF16)
    kvr = p['mla_g_kv'].shape[0]
    w['mla_dc'] = p['mla_w_dkv'][:, :kvr].astype(BF16)
    wr = p['mla_w_dkv'][:, kvr:]
    hr = MLA_ROPE // 2
    w['mla_dr'] = wr.astype(BF16)
    w['mla_drr'] = jnp.concatenate([-wr[:, hr:], wr[:, :hr]], axis=1).astype(BF16)
    w['mla_g_q'] = p['mla_g_q'].reshape(1, -1)
    w['mla_g_kv'] = p['mla_g_kv'].reshape(1, -1)
    pad = MLA_HEAD_PAD - MLA_NOPE - MLA_ROPE
    qr = p['mla_w_uq'].shape[0]
    uq = p['mla_w_uq'].reshape(qr, MLA_HEADS, MLA_NOPE + MLA_ROPE)
    w['mla_uq'] = jnp.pad(uq, ((0, 0), (0, 0), (0, pad))).reshape(qr, -1).astype(BF16)
    uk = jnp.pad(p['mla_w_uk'], ((0, 0), (0, 0), (0, MLA_HEAD_PAD - MLA_NOPE)))
    w['mla_uk'] = uk.reshape(kvr, -1).astype(BF16)
    place = jnp.pad(jnp.eye(MLA_ROPE, dtype=F32), ((0, 0), (MLA_NOPE, pad)))
    w['mla_place'] = jnp.tile(place, (1, MLA_HEADS)).astype(BF16)
    w['mla_uv'] = p['mla_w_uv'].reshape(kvr, -1).astype(BF16)
    w['mla_o'] = p['mla_w_o'].astype(BF16)

    w['sgu_in'] = p['sgu_w_in'].astype(BF16)
    w['sgu_g_v'] = p['sgu_g_v'].reshape(1, -1)
    w['sgu_out'] = p['sgu_w_out'].astype(BF16)
    return w


def _sgu_spatial(w_s, b_s, t, width):
    ws = jnp.tril(w_s[:, :t, :t]).astype(BF16)
    gd = width // SGU_GROUPS
    b_full = jnp.repeat(b_s[:, :t].T.astype(F32), gd, axis=1)
    return ws, b_full


def _run_trunk(x, mods, pos, past, p, w, s5_tables):
    b, l, d = x.shape
    new = {}
    depth = p['g_mix'].shape[0]
    for i in range(depth):
        sh1, sc1, gt1, sh2, sc2, gt2 = [mods[i][:, k][:, None, :] for k in range(6)]
        g_mix = p['g_mix'][i].reshape(1, d)
        kind = i % N_MIXERS
        if kind == 0:
            h = _modulate(x, g_mix, sh1, sc1)
            h0_re = None if past is None else past['s5_re']
            h0_im = None if past is None else past['s5_im']
            y, new['s5_re'], new['s5_im'] = _s5_scan(h, s5_tables, h0_re, h0_im)
            x = _s5_glu(y, h, x, p['s5_d'].reshape(1, d), gt1, w['glu_a'], w['glu_b'])
        elif kind == 1:
            cos, sin = _diff_rope_tables(pos)
            k, v, qb, kb, vb = _diff_qkv(x, g_mix, sh1, sc1, w['diff_qkv'], cos, sin)
            if past is None:
                o = _diff_attention(qb, [(kb, vb)], w['diff_lam'], w['diff_g_sub'], causal=True)
            else:
                o = _diff_attention_cached(qb, past['diff_k'], past['diff_v'], kb, vb,
                                           w['diff_lam'], w['diff_g_sub'])
            x = _proj_residual(o, w['diff_o'], x, gt1)
            new['diff_k'] = k.reshape(b, l, DIFF_HEADS, 2 * DIFF_DH)
            new['diff_v'] = v.reshape(b, l, DIFF_HEADS, 2 * DIFF_DH)
        elif kind == 2:
            q_cos, q_sin, k_cos, k_sin = _mla_rope_tables(pos)
            cq, ckv, krope = _mla_down(x, g_mix, sh1, sc1, w['mla_dq'], w['mla_dc'], w['mla_dr'],
                                       w['mla_drr'], w['mla_g_q'], w['mla_g_kv'], k_cos, k_sin)
            qb = _mla_qup(cq, w['mla_uq'], q_cos, q_sin)
            sources = []
            if past is not None:
                sources.append(_mla_kvup(past['mla_ckv'], past['mla_krope'],
                                         w['mla_uk'], w['mla_place'], w['mla_uv']))
            sources.append(_mla_kvup(ckv, krope, w['mla_uk'], w['mla_place'], w['mla_uv']))
            o = _mla_attention(qb, sources, causal=past is None)
            x = _proj_residual(o, w['mla_o'], x, gt1)
            new['mla_ckv'], new['mla_krope'] = ckv, krope
        else:
            t = min(l, SGU_CHUNK)
            ws, b_full = _sgu_spatial(p['sgu_w_s'], p['sgu_b_s'], t, w['sgu_g_v'].shape[1])
            outs = _sgu(x, g_mix, sh1, sc1, gt1, w['sgu_in'], w['sgu_g_v'], ws, b_full, w['sgu_out'],
                        emit_v=past is not None)
            x = outs[0]
            if past is not None:
                new['sgu_v'] = outs[1]
        x = _mlp(x, p['g_ffn'][i].reshape(1, d), sh2, sc2, gt2, w['w_up'][i], w['w_down'][i],
                 p['g_final'].reshape(1, d), final_norm=(i == depth - 1))
    return x, new


def kernel(x_prompt, x_sample, c_prompt, c_sample, state_s5_re, state_s5_im, cache_diff_k, cache_diff_v, cache_mla_ckv, cache_mla_krope, w_ada, b_ada, g_mix, g_ffn, w_up, w_down, g_final, s5_a_re, s5_a_im, s5_b_re, s5_b_im, s5_c_re, s5_c_im, s5_d, s5_log_dt, s5_w_glu_a, s5_w_glu_b, diff_w_qkv, diff_lambda_q1, diff_lambda_k1, diff_lambda_q2, diff_lambda_k2, diff_g_sub, diff_w_o, mla_w_dq, mla_g_q, mla_w_uq, mla_w_dkv, mla_g_kv, mla_w_uk, mla_w_uv, mla_w_o, sgu_w_in, sgu_g_v, sgu_w_s, sgu_b_s, sgu_w_out):
    p = {
        'w_ada': w_ada, 'b_ada': b_ada, 'g_mix': g_mix, 'g_ffn': g_ffn,
        'w_up': w_up, 'w_down': w_down, 'g_final': g_final,
        's5_d': s5_d, 's5_w_glu_a': s5_w_glu_a, 's5_w_glu_b': s5_w_glu_b,
        'diff_w_qkv': diff_w_qkv, 'diff_lambda_q1': diff_lambda_q1, 'diff_lambda_k1': diff_lambda_k1,
        'diff_lambda_q2': diff_lambda_q2, 'diff_lambda_k2': diff_lambda_k2,
        'diff_g_sub': diff_g_sub, 'diff_w_o': diff_w_o,
        'mla_w_dq': mla_w_dq, 'mla_g_q': mla_g_q, 'mla_w_uq': mla_w_uq, 'mla_w_dkv': mla_w_dkv,
        'mla_g_kv': mla_g_kv, 'mla_w_uk': mla_w_uk, 'mla_w_uv': mla_w_uv, 'mla_w_o': mla_w_o,
        'sgu_w_in': sgu_w_in, 'sgu_g_v': sgu_g_v, 'sgu_w_s': sgu_w_s, 'sgu_b_s': sgu_b_s,
        'sgu_w_out': sgu_w_out,
    }
    past = {
        's5_re': state_s5_re, 's5_im': state_s5_im,
        'diff_k': cache_diff_k, 'diff_v': cache_diff_v,
        'mla_ckv': cache_mla_ckv, 'mla_krope': cache_mla_krope,
    }
    bp, lp, d = x_prompt.shape
    bs, ls, _ = x_sample.shape
    depth = w_ada.shape[0]

    c_all = jnp.concatenate([c_prompt, c_sample], axis=0)
    c_all = jnp.pad(c_all, ((0, (-c_all.shape[0]) % 8), (0, 0)))
    mod = _adaln(c_all, w_ada, b_ada).reshape(depth, c_all.shape[0], 6, d)
    mods_p = [mod[i, :bp] for i in range(depth)]
    mods_s = [mod[i, bp:bp + bs] for i in range(depth)]

    w = _prepare_weights(p)
    max_steps = max((lp // S5_SUB).bit_length() - 1, (ls // S5_SUB).bit_length() - 1, 1)
    s5_tables = _s5_tables(s5_a_re, s5_a_im, s5_b_re, s5_b_im, s5_c_re, s5_c_im, s5_log_dt, max_steps)

    pos_p = jnp.arange(lp, dtype=jnp.int32)
    pos_s = cache_diff_k.shape[1] + jnp.arange(ls, dtype=jnp.int32)
    y_prompt, sp = _run_trunk(x_prompt, mods_p, pos_p, None, p, w, s5_tables)
    y_sample, ss = _run_trunk(x_sample, mods_s, pos_s, past, p, w, s5_tables)
    return (y_prompt, y_sample,
            sp['s5_re'], sp['s5_im'], ss['s5_re'], ss['s5_im'],
            sp['diff_k'], sp['diff_v'], ss['diff_k'], ss['diff_v'],
            sp['mla_ckv'], sp['mla_krope'], ss['mla_ckv'], ss['mla_krope'],
            ss['sgu_v'])
```

```python
import functools
import math

import jax
import jax.numpy as jnp
from jax import lax
from jax.experimental import pallas as pl
from jax.experimental.pallas import tpu as pltpu

F32 = jnp.float32
BF16 = jnp.bfloat16
HIGHEST = lax.Precision.HIGHEST

NORM_EPS = 1e-6
ROPE_THETA = 10000.0
NEG_INF = -1e30
LOG2_E = math.log2(math.e)
CHUNK = 64
N_MIXERS = 4

S5_GROUP = 16
S5_SUB = 16

DIFF_HEADS = 8
DIFF_DH = 64
LAMBDA_INIT = 0.8 - 0.6 * math.exp(-0.3 * 1)

MLA_HEADS = 16
MLA_NOPE = 64
MLA_ROPE = 32
MLA_V = 64
MLA_SCALE = (MLA_NOPE + MLA_ROPE) ** -0.5
MLA_HEAD_PAD = 128

SGU_CHUNK = 128
SGU_GROUPS = 8

LANES = 128
ROW_TILE = 512
ATTN_TQ = 256
ATTN_STREAMS = 2
ATTN_TK = 256
VMEM_LIMIT_BYTES = 48 * 1024 * 1024


def _params(*sem):
    return pltpu.CompilerParams(dimension_semantics=sem, vmem_limit_bytes=VMEM_LIMIT_BYTES)


def _row_tiles(batch, length, rows=ROW_TILE):
    if length >= rows:
        assert length % rows == 0
        return 1, rows
    nb = max(1, min(batch, rows // length))
    while batch % nb:
        nb -= 1
    return nb, length


def _rms(x, g):
    return x * lax.rsqrt(jnp.mean(x * x, axis=-1, keepdims=True) + NORM_EPS) * g


def _modulated(x_ref, g_ref, sh_ref, sc_ref):
    return _rms(x_ref[...], g_ref[...]) * (1.0 + sc_ref[...]) + sh_ref[...]


def _rotate_pairs(x, half, period, lo):
    width = x.shape[-1]
    lane = lax.broadcasted_iota(jnp.int32, x.shape, x.ndim - 1) % period
    fwd = pltpu.roll(x, width - half, axis=x.ndim - 1)
    bwd = pltpu.roll(x, half, axis=x.ndim - 1)
    return jnp.where((lane >= lo) & (lane < lo + half), fwd, bwd)


def _adaln_kernel(c_ref, w_ref, b_ref, o_ref):
    c = c_ref[...]
    s = c * jax.nn.sigmoid(c)
    o_ref[0] = jnp.dot(s, w_ref[0], precision=HIGHEST, preferred_element_type=F32) + b_ref[0]


def _adaln(c_all, w_ada, b_ada):
    depth, d, n = w_ada.shape
    rows = c_all.shape[0]
    tn = n // 4
    return pl.pallas_call(
        _adaln_kernel,
        grid=(depth, n // tn),
        in_specs=[pl.BlockSpec((rows, d), lambda i, j: (0, 0)),
                  pl.BlockSpec((1, d, tn), lambda i, j: (i, 0, j)),
                  pl.BlockSpec((1, 1, tn), lambda i, j: (i, 0, j))],
        out_specs=pl.BlockSpec((1, rows, tn), lambda i, j: (i, 0, j)),
        out_shape=jax.ShapeDtypeStruct((depth, rows, n), F32),
        compiler_params=_params("parallel", "parallel"),
        name="adaln",
    )(c_all, w_ada, b_ada.reshape(depth, 1, n))


def _tile_specs(batch, length, d, nb, lt, extra_axes=0):
    if extra_axes == 0:
        act = lambda width: pl.BlockSpec((nb, lt, width), lambda i, j: (i, j, 0))
        vec = lambda width: pl.BlockSpec((nb, 1, width), lambda i, j: (i, 0, 0))
        par = lambda shape: pl.BlockSpec(shape, lambda i, j: (0,) * len(shape))
        pos = lambda width: pl.BlockSpec((lt, width), lambda i, j: (j, 0))
    else:
        act = lambda width: pl.BlockSpec((nb, lt, width), lambda i, j, f: (i, j, 0))
        vec = lambda width: pl.BlockSpec((nb, 1, width), lambda i, j, f: (i, 0, 0))
        par = lambda shape: pl.BlockSpec(shape, lambda i, j, f: (0,) * len(shape))
        pos = lambda width: pl.BlockSpec((lt, width), lambda i, j, f: (j, 0))
    return act, vec, par, pos


def _modulate_kernel(x_ref, g_ref, sh_ref, sc_ref, o_ref):
    o_ref[...] = _modulated(x_ref, g_ref, sh_ref, sc_ref)


def _modulate(x, g, sh, sc):
    b, l, d = x.shape
    nb, lt = _row_tiles(b, l)
    act, vec, par, _ = _tile_specs(b, l, d, nb, lt)
    return pl.pallas_call(
        _modulate_kernel,
        grid=(b // nb, l // lt),
        in_specs=[act(d), par((1, d)), vec(d), vec(d)],
        out_specs=act(d),
        out_shape=jax.ShapeDtypeStruct((b, l, d), F32),
        compiler_params=_params("parallel", "parallel"),
        name="modulate",
    )(x, g, sh, sc)


def _mlp_kernel(x_ref, g_ref, sh_ref, sc_ref, gt_ref, wu_ref, wd_ref, gf_ref, o_ref, h_s, acc_s,
                *, final_norm):
    f = pl.program_id(2)
    nb, lt, d = x_ref.shape

    @pl.when(f == 0)
    def _():
        h_s[...] = _modulated(x_ref, g_ref, sh_ref, sc_ref).reshape(nb * lt, d).astype(BF16)
        acc_s[...] = jnp.zeros_like(acc_s)

    a = jnp.dot(h_s[...], wu_ref[...], preferred_element_type=F32)
    a = jnp.square(jnp.maximum(a, 0.0)).astype(BF16)
    acc_s[...] += jnp.dot(a, wd_ref[...], preferred_element_type=F32)

    @pl.when(f == pl.num_programs(2) - 1)
    def _():
        y = x_ref[...] + (1.0 + gt_ref[...]) * acc_s[...].reshape(nb, lt, d)
        if final_norm:
            y = _rms(y, gf_ref[...])
        o_ref[...] = y


def _mlp(x, g, sh, sc, gt, w_up, w_down, g_final, final_norm, tf=1024):
    b, l, d = x.shape
    dff = w_up.shape[1]
    nb, lt = _row_tiles(b, l)
    act, vec, par, _ = _tile_specs(b, l, d, nb, lt, extra_axes=1)
    return pl.pallas_call(
        functools.partial(_mlp_kernel, final_norm=final_norm),
        grid=(b // nb, l // lt, dff // tf),
        in_specs=[act(d), par((1, d)), vec(d), vec(d), vec(d),
                  pl.BlockSpec((d, tf), lambda i, j, f: (0, f)),
                  pl.BlockSpec((tf, d), lambda i, j, f: (f, 0)),
                  par((1, d))],
        out_specs=act(d),
        out_shape=jax.ShapeDtypeStruct((b, l, d), F32),
        scratch_shapes=[pltpu.VMEM((nb * lt, d), BF16), pltpu.VMEM((nb * lt, d), F32)],
        compiler_params=_params("parallel", "parallel", "arbitrary"),
        name="mlp",
    )(x, g, sh, sc, gt, w_up, w_down, g_final)


def _proj_res_kernel(a_ref, w_ref, x_ref, gt_ref, o_ref):
    nb, lt, k = a_ref.shape
    d = w_ref.shape[1]
    r = jnp.dot(a_ref[...].reshape(nb * lt, k), w_ref[...], preferred_element_type=F32)
    o_ref[...] = x_ref[...] + (1.0 + gt_ref[...]) * r.reshape(nb, lt, d)


def _proj_residual(a, w, x, gt):
    b, l, d = x.shape
    k = a.shape[-1]
    nb, lt = _row_tiles(b, l)
    act, vec, par, _ = _tile_specs(b, l, d, nb, lt)
    return pl.pallas_call(
        _proj_res_kernel,
        grid=(b // nb, l // lt),
        in_specs=[act(k), par((k, d)), act(d), vec(d)],
        out_specs=act(d),
        out_shape=jax.ShapeDtypeStruct((b, l, d), F32),
        compiler_params=_params("parallel", "parallel"),
        name="proj_residual",
    )(a, w, x, gt)


def _s5_kernel(*refs, nch, batch, nsteps, has_h0):
    if has_h0:
        u_ref, bc_ref, m_ref, cc_ref, a1_ref, a2_ref, h0_ref, y_ref, hl_ref = refs
    else:
        u_ref, bc_ref, m_ref, cc_ref, a1_ref, a2_ref, y_ref, hl_ref = refs
    u = u_ref[0]
    rows = u.shape[0]
    x = jnp.dot(u, bc_ref[0], precision=HIGHEST, preferred_element_type=F32)
    half = x.shape[1] // 2
    if has_h0:
        h0 = h0_ref[0]
        x = x + a1_ref[0, 0:1, :] * h0 + a2_ref[0, 0:1, :] * pltpu.roll(h0, half, axis=1)
    kidx = lax.broadcasted_iota(jnp.int32, x.shape, 0) & (nch - 1)
    for j in range(nsteps):
        s = 1 << j
        xs = jnp.where(kidx >= s, pltpu.roll(x, s, axis=0), 0.0)
        x = x + a1_ref[0, j:j + 1, :] * xs + a2_ref[0, j:j + 1, :] * pltpu.roll(xs, half, axis=1)
    for b in range(batch):
        r = (b + 1) * nch - 1
        hl_ref[0, b:b + 1, :] = x[r:r + 1, :]
    hstart = jnp.where(kidx >= 1, pltpu.roll(x, 1, axis=0), 0.0)
    if has_h0:
        hstart = hstart + h0
    y = jnp.dot(u.astype(BF16), m_ref[0], preferred_element_type=F32)
    y = y + jnp.dot(hstart.astype(BF16), cc_ref[0], preferred_element_type=F32)
    y_ref[0] = y


def _s5_tables(a_re, a_im, b_re, b_im, c_re, c_im, log_dt, max_steps):
    g, p = a_re.shape
    sub = S5_SUB
    dt = jnp.exp(log_dt.astype(F32))[:, None]
    zr, zi = a_re.astype(F32) * dt, a_im.astype(F32) * dt

    def zpow(n):
        mag = jnp.exp(zr * n)
        return mag * jnp.cos(zi * n), mag * jnp.sin(zi * n)

    er = jnp.expm1(zr) * jnp.cos(zi) - 2.0 * jnp.square(jnp.sin(0.5 * zi))
    ei = jnp.exp(zr) * jnp.sin(zi)
    den = a_re * a_re + a_im * a_im
    fr = (er * a_re + ei * a_im) / den
    fi = (ei * a_re - er * a_im) / den
    bb_re = fr[..., None] * b_re - fi[..., None] * b_im
    bb_im = fr[..., None] * b_im + fi[..., None] * b_re

    lags = jnp.arange(sub + 1, dtype=F32)[:, None, None]
    pr, pi = zpow(lags)

    qr, qi = pr[:sub][::-1], pi[:sub][::-1]
    bcr = qr[..., None] * bb_re[None] - qi[..., None] * bb_im[None]
    bci = qr[..., None] * bb_im[None] + qi[..., None] * bb_re[None]
    bc = jnp.concatenate([bcr, bci], axis=2)
    bc = bc.transpose(1, 0, 3, 2).reshape(g, sub * S5_GROUP, 2 * p)

    ar, ai = pr[1:], pi[1:]
    ccr = c_re[None] * ar[:, :, None, :] - c_im[None] * ai[:, :, None, :]
    cci = c_re[None] * ai[:, :, None, :] + c_im[None] * ar[:, :, None, :]
    cc = jnp.concatenate([ccr, -cci], axis=3)
    cc = cc.transpose(1, 3, 0, 2).reshape(g, 2 * p, sub * S5_GROUP)

    kr = (jnp.einsum('gcp,lgp,gpd->lgcd', c_re, pr[:sub], bb_re, precision=HIGHEST)
          - jnp.einsum('gcp,lgp,gpd->lgcd', c_re, pi[:sub], bb_im, precision=HIGHEST)
          - jnp.einsum('gcp,lgp,gpd->lgcd', c_im, pr[:sub], bb_im, precision=HIGHEST)
          - jnp.einsum('gcp,lgp,gpd->lgcd', c_im, pi[:sub], bb_re, precision=HIGHEST))
    tt = jnp.arange(sub)
    lag = tt[None, :] - tt[:, None]
    toep = jnp.where((lag >= 0)[:, :, None, None, None],
                     kr[jnp.clip(lag, 0, sub - 1)], 0.0)
    m = toep.transpose(2, 0, 4, 1, 3).reshape(g, sub * S5_GROUP, sub * S5_GROUP)

    steps = (sub * (2 ** jnp.arange(max_steps))).astype(F32)[:, None, None]
    sr, si = zpow(steps)
    a1 = jnp.concatenate([sr, sr], axis=-1).transpose(1, 0, 2)
    a2 = jnp.concatenate([-si, si], axis=-1).transpose(1, 0, 2)
    pad = (-max_steps) % 8
    a1 = jnp.pad(a1, ((0, 0), (0, pad), (0, 0)))
    a2 = jnp.pad(a2, ((0, 0), (0, pad), (0, 0)))
    return bc, m.astype(BF16), cc.astype(BF16), a1, a2


def _s5_lane_kernel(h_ref, bct_ref, mt_ref, cct_ref, a1_ref, a2_ref, y_ref, hl_ref, ut_s, yt_s,
                    *, nch, nsteps):
    sub, c = S5_SUB, S5_GROUP
    groups = ut_s.shape[0]
    for t in range(sub):
        att = h_ref[0, pl.ds(t, nch, stride=sub), :].T
        for g in range(groups):
            ut_s[g, t * c:(t + 1) * c, :] = att[g * c:(g + 1) * c, :]
    half = bct_ref.shape[1] // 2
    lane = lax.broadcasted_iota(jnp.int32, (2 * half, nch), 1)
    for g in range(groups):
        ut = ut_s[g]
        x = jnp.dot(bct_ref[g], ut, precision=HIGHEST, preferred_element_type=F32)
        for j in range(nsteps):
            s = 1 << j
            xs = jnp.where(lane >= s, pltpu.roll(x, s, axis=1), 0.0)
            x = x + a1_ref[g, :, j:j + 1] * xs + a2_ref[g, :, j:j + 1] * pltpu.roll(xs, half, axis=0)
        hl_ref[0, g:g + 1, :] = x[:, nch - LANES:].T[LANES - 1:LANES, :]
        hstart = jnp.where(lane >= 1, pltpu.roll(x, 1, axis=1), 0.0)
        yt = jnp.dot(mt_ref[g], ut.astype(BF16), preferred_element_type=F32)
        yt = yt + jnp.dot(cct_ref[g], hstart.astype(BF16), preferred_element_type=F32)
        for t in range(sub):
            yt_s[t, g * c:(g + 1) * c, :] = yt[t * c:(t + 1) * c, :]
    for t in range(sub):
        y_ref[0, pl.ds(t, nch, stride=sub), :] = yt_s[t].T


def _s5_scan_lanes(h, tables):
    bc, m, cc, a1, a2 = tables
    b, l, d = h.shape
    g, k, p2 = bc.shape
    nch = l // S5_SUB
    gpb = LANES // S5_GROUP
    nsteps = nch.bit_length() - 1
    wspec = lambda shape: pl.BlockSpec((gpb,) + shape, lambda j, bi: (j, 0, 0))
    act = pl.BlockSpec((1, l, LANES), lambda j, bi: (bi, 0, j))
    return pl.pallas_call(
        functools.partial(_s5_lane_kernel, nch=nch, nsteps=nsteps),
        grid=(g // gpb, b),
        in_specs=[act, wspec((p2, k)), wspec((k, k)), wspec((k, p2)),
                  wspec((p2, a1.shape[1])), wspec((p2, a2.shape[1]))],
        out_specs=[act, pl.BlockSpec((1, gpb, p2), lambda j, bi: (bi, j, 0))],
        out_shape=[jax.ShapeDtypeStruct((b, l, d), F32), jax.ShapeDtypeStruct((b, g, p2), F32)],
        scratch_shapes=[pltpu.VMEM((gpb, k, nch), F32), pltpu.VMEM((S5_SUB, LANES, nch), F32)],
        compiler_params=_params("parallel", "parallel"),
        name="s5_scan_lanes",
    )(h, bc.transpose(0, 2, 1), m.transpose(0, 2, 1), cc.transpose(0, 2, 1),
      a1.transpose(0, 2, 1), a2.transpose(0, 2, 1))


def _s5_scan(h, tables, h0_re, h0_im):
    bc, m, cc, a1, a2 = tables
    b, l, d = h.shape
    if h0_re is None and (l // S5_SUB) % LANES == 0:
        y, hl = _s5_scan_lanes(h, tables)
        return y, hl[..., :hl.shape[-1] // 2], hl[..., hl.shape[-1] // 2:]
    g = d // S5_GROUP
    p2 = bc.shape[-1]
    nch = l // S5_SUB
    assert nch & (nch - 1) == 0
    nsteps = nch.bit_length() - 1
    rows = b * nch
    k = S5_SUB * S5_GROUP
    u = h.reshape(b, nch, S5_SUB, g, S5_GROUP).transpose(3, 0, 1, 2, 4).reshape(g, rows, k)
    has_h0 = h0_re is not None
    grp = lambda shape: pl.BlockSpec((1,) + shape, lambda i: (i, 0, 0))
    args = [u, bc, m, cc, a1, a2]
    in_specs = [grp((rows, k)), grp((k, p2)), grp((k, k)), grp((p2, k)),
                grp(a1.shape[1:]), grp(a2.shape[1:])]
    if has_h0:
        h0 = jnp.concatenate([h0_re, h0_im], axis=-1).astype(F32).transpose(1, 0, 2)
        h0 = jnp.pad(h0[:, :, None, :], ((0, 0), (0, 0), (0, nch - 1), (0, 0))).reshape(g, rows, p2)
        args.append(h0)
        in_specs.append(grp((rows, p2)))
    y, hl = pl.pallas_call(
        functools.partial(_s5_kernel, nch=nch, batch=b, nsteps=nsteps, has_h0=has_h0),
        grid=(g,),
        in_specs=in_specs,
        out_specs=[grp((rows, k)), grp((b, p2))],
        out_shape=[jax.ShapeDtypeStruct((g, rows, k), F32), jax.ShapeDtypeStruct((g, b, p2), F32)],
        compiler_params=_params("parallel"),
        name="s5_scan",
    )(*args)
    y = y.reshape(g, b, nch, S5_SUB, S5_GROUP).transpose(1, 2, 3, 0, 4).reshape(b, l, d)
    hl = hl.transpose(1, 0, 2)
    return y, hl[..., :p2 // 2], hl[..., p2 // 2:]


def _glu_kernel(y_ref, h_ref, x_ref, d_ref, gt_ref, wa_ref, wb_ref, o_ref):
    nb, lt, d = x_ref.shape
    z = jax.nn.gelu(y_ref[...] + d_ref[...] * h_ref[...]).reshape(nb * lt, d).astype(BF16)
    a = jnp.dot(z, wa_ref[...], preferred_element_type=F32)
    b = jnp.dot(z, wb_ref[...], preferred_element_type=F32)
    out = (a * jax.nn.sigmoid(b)).reshape(nb, lt, d)
    o_ref[...] = x_ref[...] + (1.0 + gt_ref[...]) * out


def _s5_glu(y, h, x, d_skip, gt, wa, wb):
    b, l, d = x.shape
    nb, lt = _row_tiles(b, l)
    act, vec, par, _ = _tile_specs(b, l, d, nb, lt)
    return pl.pallas_call(
        _glu_kernel,
        grid=(b // nb, l // lt),
        in_specs=[act(d), act(d), act(d), par((1, d)), vec(d), par((d, d)), par((d, d))],
        out_specs=act(d),
        out_shape=jax.ShapeDtypeStruct((b, l, d), F32),
        compiler_params=_params("parallel", "parallel"),
        name="s5_glu",
    )(y, h, x, d_skip, gt, wa, wb)


def _diff_qkv_kernel(x_ref, g_ref, sh_ref, sc_ref, w_ref, cos_ref, sin_ref,
                     k_ref, v_ref, qb_ref, kb_ref, vb_ref):
    nb, lt, d = x_ref.shape
    h = _modulated(x_ref, g_ref, sh_ref, sc_ref).reshape(nb * lt, d).astype(BF16)
    reps = d // cos_ref.shape[1]
    cos = jnp.tile(cos_ref[...], (1, reps))[None]
    sin = jnp.tile(sin_ref[...], (1, reps))[None]

    def roped(cols):
        t = jnp.dot(h, w_ref[:, cols * d:(cols + 1) * d], preferred_element_type=F32)
        r = _rotate_pairs(t, DIFF_DH // 2, DIFF_DH, 0)
        return t.reshape(nb, lt, d) * cos + r.reshape(nb, lt, d) * sin

    q = roped(0)
    qb_ref[...] = (q * (LOG2_E * DIFF_DH ** -0.5)).astype(BF16)
    k = roped(1)
    kb_ref[...] = k.astype(BF16)
    v = jnp.dot(h, w_ref[:, 2 * d:], preferred_element_type=F32).reshape(nb, lt, d)
    vb_ref[...] = v.astype(BF16)
    hw = k_ref.shape[2]
    heads = d // hw
    for hh in range(heads):
        k_ref[:, pl.ds(hh, lt, stride=heads), :] = k[:, :, hh * hw:(hh + 1) * hw]
        v_ref[:, pl.ds(hh, lt, stride=heads), :] = v[:, :, hh * hw:(hh + 1) * hw]


def _diff_qkv(x, g, sh, sc, w_qkv, cos, sin):
    b, l, d = x.shape
    nb, lt = _row_tiles(b, l)
    act, vec, par, pos = _tile_specs(b, l, d, nb, lt)
    hw = d // DIFF_HEADS
    return pl.pallas_call(
        _diff_qkv_kernel,
        grid=(b // nb, l // lt),
        in_specs=[act(d), par((1, d)), vec(d), vec(d), par((d, 3 * d)),
                  pos(cos.shape[1]), pos(sin.shape[1])],
        out_specs=[pl.BlockSpec((nb, lt * DIFF_HEADS, hw), lambda i, j: (i, j, 0))] * 2 + [act(d)] * 3,
        out_shape=[jax.ShapeDtypeStruct((b, l * DIFF_HEADS, hw), F32)] * 2
                  + [jax.ShapeDtypeStruct((b, l, d), BF16)] * 3,
        compiler_params=_params("parallel", "parallel"),
        name="diff_qkv",
    )(x, g, sh, sc, w_qkv, cos, sin)


_NT = (((1,), (1,)), ((), ()))
_TN = (((0,), (0,)), ((), ()))


def _kv_sweep(step, n_keys, tk, dynamic_full=None):
    if dynamic_full is not None:
        lax.fori_loop(0, dynamic_full, lambda j, c: (step(pl.multiple_of(j * tk, tk), tk), c)[1], 0)
        return
    n_full = n_keys // tk
    if n_full == 1:
        step(0, tk)
    elif n_full > 1:
        lax.fori_loop(0, n_full, lambda j, c: (step(pl.multiple_of(j * tk, tk), tk), c)[1], 0)
    if n_keys % tk:
        step(n_full * tk, n_keys % tk)


def _softmax_update(m_s, l_s, acc_s, i, s, vb):
    m_prev = m_s[i]
    m_new = jnp.maximum(m_prev, jnp.max(s, axis=0, keepdims=True))
    alpha = jnp.exp2(m_prev - m_new)
    p = jnp.exp2(s - m_new)
    l_s[i] = alpha * l_s[i] + jnp.sum(p, axis=0, keepdims=True)
    acc_s[i] = alpha * acc_s[i] + lax.dot_general(vb, p.astype(BF16), _TN, preferred_element_type=F32)
    m_s[i] = m_new


def _softmax_init(m_s, l_s, acc_s):
    m_s[...] = jnp.full_like(m_s, NEG_INF)
    l_s[...] = jnp.zeros_like(l_s)
    acc_s[...] = jnp.zeros_like(acc_s)


def _softmax_result(l_s, acc_s, i):
    return acc_s[i] / l_s[i]


def _softmax_scratch(n, vw, queries):
    return [pltpu.VMEM((n, 1, queries), F32), pltpu.VMEM((n, 1, queries), F32),
            pltpu.VMEM((n, vw, queries), F32)]


def _attend(scores, kv_refs, m_s, l_s, acc_s, s_bufs, *, tq, tk, causal, qi):
    ns = len(scores)
    _softmax_init(m_s, l_s, acc_s)
    update = functools.partial(_softmax_update, m_s, l_s, acc_s)

    def make_step(k_ref, v_ref):
        def step(start, size):
            kb = k_ref[0, pl.ds(start, size), :].astype(BF16)
            vb = v_ref[0, pl.ds(start, size), :].astype(BF16)
            for i in range(ns):
                update(i, scores[i](kb), vb)
        return step

    if causal:
        assert ns == 2 and tq == tk and tq % CHUNK == 0
        k_ref, v_ref = kv_refs
        s_a, s_b = s_bufs

        def qk(dst, blk, streams):
            kb = k_ref[0, pl.ds(pl.multiple_of(blk * tk, tk), tk), :].astype(BF16)
            for i in streams:
                dst[i] = scores[i](kb)

        def pv(src, blk, streams, masked=None):
            vb = v_ref[0, pl.ds(pl.multiple_of(blk * tk, tk), tk), :].astype(BF16)
            for i in streams:
                update(i, jnp.where(vis, src[i], NEG_INF) if i == masked else src[i], vb)

        def pair(jj, carry):
            qk(s_b, 2 * jj + 1, (0, 1))
            pv(s_a, 2 * jj, (0, 1))
            qk(s_a, 2 * jj + 2, (0, 1))
            pv(s_b, 2 * jj + 1, (0, 1))
            return carry

        qk(s_a, 0, (0, 1))
        lax.fori_loop(0, qi, pair, 0)
        kc = lax.broadcasted_iota(jnp.int32, (tk, tq), 0) // CHUNK
        qc = lax.broadcasted_iota(jnp.int32, (tk, tq), 1) // CHUNK
        vis = jnp.concatenate([kc <= qc] * 2, axis=1)
        qk(s_b, 2 * qi + 1, (1,))
        pv(s_a, 2 * qi, (0, 1), masked=0)
        pv(s_b, 2 * qi + 1, (1,), masked=1)
    else:
        for i in range(len(kv_refs) // 2):
            k_ref, v_ref = kv_refs[2 * i], kv_refs[2 * i + 1]
            _kv_sweep(make_step(k_ref, v_ref), k_ref.shape[1], tk)
    return [_softmax_result(l_s, acc_s, i) for i in range(ns)]


def _diff_attn_kernel(*refs, tq, tk, causal, n_src):
    q_ref = refs[0]
    kv_refs = refs[1:1 + 2 * n_src]
    lam_ref, gsub_ref, o_ref, m_s, l_s, acc_s, *s_bufs = refs[1 + 2 * n_src:]
    ns = q_ref.shape[1] // tq

    def make_score(i):
        q2 = _diff_query_groups(q_ref[0, i * tq:(i + 1) * tq, :])
        return lambda kb: lax.dot_general(kb, q2, _NT, preferred_element_type=F32)

    outs = _attend([make_score(i) for i in range(ns)], kv_refs, m_s, l_s, acc_s, s_bufs,
                   tq=tq, tk=tk, causal=causal, qi=pl.program_id(2))
    lam = _diff_lambda(lam_ref)
    for i, o2 in enumerate(outs):
        o = (o2[:, :tq] - lam * o2[:, tq:]).T
        o = _rms(o, gsub_ref[...]) * (1.0 - LAMBDA_INIT)
        o_ref[0, i * tq:(i + 1) * tq, :] = o.astype(o_ref.dtype)


def _score_buffers(ns, tq, tk, causal):
    return [pltpu.VMEM((ns, tk, 2 * tq), F32)] * 2 if causal else []


def _attn_tiles(l, causal):
    tq = min(ATTN_TQ, l)
    ns = ATTN_STREAMS if l % (ATTN_STREAMS * tq) == 0 else 1
    return tq, ns, (tq if causal else ATTN_TK)


def _diff_attention(qb, kv_sources, lam_rows, g_sub, causal):
    b, l, d = qb.shape
    hw = 2 * DIFF_DH
    tq, ns, tk = _attn_tiles(l, causal)
    in_specs = [pl.BlockSpec((1, ns * tq, hw), lambda bi, h, qi: (bi, qi, h))]
    args = [qb]
    for k, v in kv_sources:
        lk = k.shape[1]
        in_specs += [pl.BlockSpec((1, lk, hw), lambda bi, h, qi: (bi, 0, h))] * 2
        args += [k, v]
    in_specs += [pl.BlockSpec(lam_rows.shape, lambda bi, h, qi: (0, 0)),
                 pl.BlockSpec((1, hw), lambda bi, h, qi: (0, 0))]
    args += [lam_rows, g_sub]
    return pl.pallas_call(
        functools.partial(_diff_attn_kernel, tq=tq, tk=tk, causal=causal, n_src=len(kv_sources)),
        grid=(b, d // hw, l // (ns * tq)),
        in_specs=in_specs,
        out_specs=pl.BlockSpec((1, ns * tq, hw), lambda bi, h, qi: (bi, qi, h)),
        out_shape=jax.ShapeDtypeStruct((b, l, d), BF16),
        scratch_shapes=_softmax_scratch(ns, hw, 2 * tq) + _score_buffers(ns, tq, tk, causal),
        compiler_params=_params("parallel", "parallel", "arbitrary"),
        name="diff_attention",
    )(*args)


def _diff_lambda(lam_ref):
    lp = lam_ref[...]
    return (jnp.exp(jnp.sum(lp[0:1] * lp[1:2], axis=-1, keepdims=True))
            - jnp.exp(jnp.sum(lp[2:3] * lp[3:4], axis=-1, keepdims=True)) + LAMBDA_INIT)


def _diff_query_groups(q):
    lane = lax.broadcasted_iota(jnp.int32, q.shape, 1)
    zero = jnp.zeros_like(q)
    return jnp.concatenate([jnp.where(lane < DIFF_DH, q, zero), jnp.where(lane >= DIFF_DH, q, zero)], axis=0)


def _diff_cached_kernel(q_ref, kc_ref, vc_ref, kn_ref, vn_ref, lam_ref, gsub_ref, o_ref,
                        m_s, l_s, acc_s, *, tk):
    c = pl.program_id(1)
    lq = q_ref.shape[1]
    hw = kc_ref.shape[2]
    heads = q_ref.shape[2] // hw
    update = functools.partial(_softmax_update, m_s, l_s, acc_s)
    q2 = lambda hh: _diff_query_groups(q_ref[0, :, hh * hw:(hh + 1) * hw])

    @pl.when(c == 0)
    def _():
        _softmax_init(m_s, l_s, acc_s)

    def block(j, carry):
        rows = pl.ds(pl.multiple_of(j * tk * heads, tk * heads), tk * heads)
        k_blk, v_blk = kc_ref.at[0, rows, :], vc_ref.at[0, rows, :]
        for hh in range(heads):
            kb = k_blk[pl.ds(hh, tk, stride=heads), :].astype(BF16)
            vb = v_blk[pl.ds(hh, tk, stride=heads), :].astype(BF16)
            update(hh, lax.dot_general(kb, q2(hh), _NT, preferred_element_type=F32), vb)
        return carry

    lax.fori_loop(0, kc_ref.shape[1] // (tk * heads), block, 0)

    @pl.when(c == pl.num_programs(1) - 1)
    def _():
        lam = _diff_lambda(lam_ref)
        for hh in range(heads):
            kb = kn_ref[0, :, hh * hw:(hh + 1) * hw]
            vb = vn_ref[0, :, hh * hw:(hh + 1) * hw]
            update(hh, lax.dot_general(kb, q2(hh), _NT, preferred_element_type=F32), vb)
            o2 = _softmax_result(l_s, acc_s, hh)
            o = (o2[:, :lq] - lam * o2[:, lq:]).T
            o = _rms(o, gsub_ref[...]) * (1.0 - LAMBDA_INIT)
            o_ref[0, :, hh * hw:(hh + 1) * hw] = o.astype(o_ref.dtype)


def _diff_attention_cached(qb, cache_k, cache_v, kb, vb, lam_rows, g_sub):
    b, l, d = qb.shape
    _, p, heads, hw = cache_k.shape
    pc = p if p <= 1024 else 1024
    tk = ATTN_TK if pc % ATTN_TK == 0 else CHUNK
    assert p % pc == 0 and pc % tk == 0
    new = pl.BlockSpec((1, l, d), lambda bi, c: (bi, 0, 0))
    cache = pl.BlockSpec((1, pc * heads, hw), lambda bi, c: (bi, c, 0))
    return pl.pallas_call(
        functools.partial(_diff_cached_kernel, tk=tk),
        grid=(b, p // pc),
        in_specs=[new, cache, cache, new, new,
                  pl.BlockSpec(lam_rows.shape, lambda bi, c: (0, 0)),
                  pl.BlockSpec((1, hw), lambda bi, c: (0, 0))],
        out_specs=new,
        out_shape=jax.ShapeDtypeStruct((b, l, d), BF16),
        scratch_shapes=_softmax_scratch(heads, hw, 2 * l),
        compiler_params=_params("parallel", "arbitrary"),
        name="diff_attention_cached",
    )(qb, cache_k.reshape(b, p * heads, hw), cache_v.reshape(b, p * heads, hw), kb, vb, lam_rows, g_sub)


def _mla_down_kernel(x_ref, g_ref, sh_ref, sc_ref, wq_ref, wc_ref, wr_ref, wrr_ref,
                     gq_ref, gkv_ref, cos_ref, sin_ref, cq_ref, ckv_ref, kr_ref):
    nb, lt, d = x_ref.shape
    h = _modulated(x_ref, g_ref, sh_ref, sc_ref).reshape(nb * lt, d).astype(BF16)
    dot = lambda w_ref: jnp.dot(h, w_ref[...], preferred_element_type=F32)
    cq_ref[...] = _rms(dot(wq_ref), gq_ref[...]).reshape(cq_ref.shape).astype(cq_ref.dtype)
    ckv_ref[...] = _rms(dot(wc_ref), gkv_ref[...]).reshape(ckv_ref.shape)
    r = dot(wr_ref).reshape(kr_ref.shape)
    rr = dot(wrr_ref).reshape(kr_ref.shape)
    kr_ref[...] = r * cos_ref[...][None] + rr * sin_ref[...][None]


def _mla_down(x, g, sh, sc, wq, wc, wr, wrr, g_q, g_kv, cos, sin):
    b, l, d = x.shape
    nb, lt = _row_tiles(b, l)
    act, vec, par, pos = _tile_specs(b, l, d, nb, lt)
    qr, kvr, rope = wq.shape[1], wc.shape[1], wr.shape[1]
    return pl.pallas_call(
        _mla_down_kernel,
        grid=(b // nb, l // lt),
        in_specs=[act(d), par((1, d)), vec(d), vec(d),
                  par((d, qr)), par((d, kvr)), par((d, rope)), par((d, rope)),
                  par((1, qr)), par((1, kvr)), pos(rope), pos(rope)],
        out_specs=[act(qr), act(kvr), act(rope)],
        out_shape=[jax.ShapeDtypeStruct((b, l, qr), BF16),
                   jax.ShapeDtypeStruct((b, l, kvr), F32),
                   jax.ShapeDtypeStruct((b, l, rope), F32)],
        compiler_params=_params("parallel", "parallel"),
        name="mla_down",
    )(x, g, sh, sc, wq, wc, wr, wrr, g_q, g_kv, cos, sin)


def _mla_qup_kernel(cq_ref, w_ref, cos_ref, sin_ref, q_ref):
    nb, lt, r = cq_ref.shape
    n = w_ref.shape[1]
    t = jnp.dot(cq_ref[...].reshape(nb * lt, r), w_ref[...], preferred_element_type=F32)
    rot = _rotate_pairs(t, MLA_ROPE // 2, MLA_HEAD_PAD, MLA_NOPE)
    reps = n // cos_ref.shape[1]
    cos = jnp.tile(cos_ref[...], (1, reps))[None]
    sin = jnp.tile(sin_ref[...], (1, reps))[None]
    q = t.reshape(nb, lt, n) * cos + rot.reshape(nb, lt, n) * sin
    q_ref[...] = (q * (LOG2_E * MLA_SCALE)).astype(q_ref.dtype)


def _mla_qup(cq, w_uq_pad, cos, sin):
    b, l, r = cq.shape
    n = w_uq_pad.shape[1]
    nb, lt = _row_tiles(b, l)
    act, vec, par, pos = _tile_specs(b, l, r, nb, lt)
    return pl.pallas_call(
        _mla_qup_kernel,
        grid=(b // nb, l // lt),
        in_specs=[act(r), par((r, n)), pos(cos.shape[1]), pos(sin.shape[1])],
        out_specs=act(n),
        out_shape=jax.ShapeDtypeStruct((b, l, n), BF16),
        compiler_params=_params("parallel", "parallel"),
        name="mla_q_up",
    )(cq, w_uq_pad, cos, sin)


def _mla_kvup_kernel(ckv_ref, kr_ref, wk_ref, we_ref, wv_ref, k_ref, v_ref):
    nb, lt, r = ckv_ref.shape
    c = ckv_ref[...].reshape(nb * lt, r).astype(BF16)
    kr = kr_ref[...].reshape(nb * lt, kr_ref.shape[2]).astype(BF16)
    k = (jnp.dot(c, wk_ref[...], preferred_element_type=F32)
         + jnp.dot(kr, we_ref[...], preferred_element_type=F32))
    k_ref[...] = k.reshape(k_ref.shape).astype(k_ref.dtype)
    v_ref[...] = jnp.dot(c, wv_ref[...], preferred_element_type=F32).reshape(v_ref.shape).astype(v_ref.dtype)


def _mla_kvup(ckv, krope, w_uk_pad, w_place, w_uv):
    b, l, r = ckv.shape
    rows = ROW_TILE if l % ROW_TILE == 0 else l
    nb, lt = _row_tiles(b, l, rows)
    act, vec, par, _ = _tile_specs(b, l, r, nb, lt)
    nk, nv = w_uk_pad.shape[1], w_uv.shape[1]
    rope = krope.shape[2]
    return pl.pallas_call(
        _mla_kvup_kernel,
        grid=(b // nb, l // lt),
        in_specs=[act(r), act(rope), par((r, nk)), par((rope, nk)), par((r, nv))],
        out_specs=[act(nk), act(nv)],
        out_shape=[jax.ShapeDtypeStruct((b, l, nk), BF16), jax.ShapeDtypeStruct((b, l, nv), BF16)],
        compiler_params=_params("parallel", "parallel"),
        name="mla_kv_up",
    )(ckv, krope, w_uk_pad, w_place, w_uv)


def _mla_attn_kernel(*refs, tq, tk, causal, n_src):
    q_ref = refs[0]
    kv_refs = refs[1:1 + 2 * n_src]
    o_ref, m_s, l_s, acc_s, *s_bufs = refs[1 + 2 * n_src:]
    hp = MLA_HEAD_PAD
    ns = q_ref.shape[1] // tq

    def make_score(i):
        qa = q_ref[0, i * tq:(i + 1) * tq, :hp]
        qb = q_ref[0, i * tq:(i + 1) * tq, hp:]
        return lambda kb: jnp.concatenate(
            [lax.dot_general(kb[:, :hp], qa, _NT, preferred_element_type=F32),
             lax.dot_general(kb[:, hp:], qb, _NT, preferred_element_type=F32)], axis=1)

    outs = _attend([make_score(i) for i in range(ns)], kv_refs, m_s, l_s, acc_s, s_bufs,
                   tq=tq, tk=tk, causal=causal, qi=pl.program_id(2))
    row = lax.broadcasted_iota(jnp.int32, (acc_s.shape[1], tq), 0)
    for i, o2 in enumerate(outs):
        o = jnp.where(row < MLA_V, o2[:, :tq], o2[:, tq:]).T
        o_ref[0, i * tq:(i + 1) * tq, :] = o.astype(o_ref.dtype)


def _mla_attention(qb, kv_sources, causal):
    b, l, _ = qb.shape
    tq, ns, tk = _attn_tiles(l, causal)
    qw, vw = 2 * MLA_HEAD_PAD, 2 * MLA_V
    in_specs = [pl.BlockSpec((1, ns * tq, qw), lambda bi, h, qi: (bi, qi, h))]
    args = [qb]
    for k, v in kv_sources:
        lk = k.shape[1]
        in_specs += [pl.BlockSpec((1, lk, qw), lambda bi, h, qi: (bi, 0, h)),
                     pl.BlockSpec((1, lk, vw), lambda bi, h, qi: (bi, 0, h))]
        args += [k, v]
    return pl.pallas_call(
        functools.partial(_mla_attn_kernel, tq=tq, tk=tk, causal=causal, n_src=len(kv_sources)),
        grid=(b, MLA_HEADS // 2, l // (ns * tq)),
        in_specs=in_specs,
        out_specs=pl.BlockSpec((1, ns * tq, vw), lambda bi, h, qi: (bi, qi, h)),
        out_shape=jax.ShapeDtypeStruct((b, l, MLA_HEADS * MLA_V), BF16),
        scratch_shapes=_softmax_scratch(ns, vw, 2 * tq) + _score_buffers(ns, tq, tk, causal),
        compiler_params=_params("parallel", "parallel", "arbitrary"),
        name="mla_attention",
    )(*args)


def _sgu_kernel(x_ref, g_ref, sh_ref, sc_ref, gt_ref, win_ref, gv_ref, ws_ref, bs_ref, wout_ref,
                *out_refs, t, emit_v):
    o_ref = out_refs[0]
    nb, lt, d = x_ref.shape
    rows = nb * lt
    width = gv_ref.shape[1]
    gd = width // SGU_GROUPS
    h = _modulated(x_ref, g_ref, sh_ref, sc_ref).reshape(rows, d).astype(BF16)
    u = jax.nn.gelu(jnp.dot(h, win_ref[:, :width], preferred_element_type=F32))
    v = jax.nn.gelu(jnp.dot(h, win_ref[:, width:], preferred_element_type=F32))
    v = _rms(v, gv_ref[...])
    if emit_v:
        out_refs[1][...] = v.reshape(nb, lt, width)
    vb = v.astype(BF16)
    bias = bs_ref[...]
    gated = []
    for c in range(rows // t):
        sv = [jnp.dot(ws_ref[gi], vb[c * t:(c + 1) * t, gi * gd:(gi + 1) * gd],
                      preferred_element_type=F32) for gi in range(SGU_GROUPS)]
        sv = jnp.concatenate(sv, axis=1) + bias
        gated.append((u[c * t:(c + 1) * t] * sv).astype(BF16))
    gated = jnp.concatenate(gated, axis=0) if len(gated) > 1 else gated[0]
    r = jnp.dot(gated, wout_ref[...], preferred_element_type=F32).reshape(nb, lt, d)
    o_ref[...] = x_ref[...] + (1.0 + gt_ref[...]) * r


def _sgu(x, g, sh, sc, gt, w_in, g_v, w_s, b_full, w_out, emit_v):
    b, l, d = x.shape
    t = w_s.shape[1]
    width = g_v.shape[1]
    nb, lt = _row_tiles(b, l, 256)
    assert lt % t == 0
    act, vec, par, _ = _tile_specs(b, l, d, nb, lt)
    out_specs = [act(d)]
    out_shape = [jax.ShapeDtypeStruct((b, l, d), F32)]
    if emit_v:
        out_specs.append(act(width))
        out_shape.append(jax.ShapeDtypeStruct((b, l, width), F32))
    return pl.pallas_call(
        functools.partial(_sgu_kernel, t=t, emit_v=emit_v),
        grid=(b // nb, l // lt),
        in_specs=[act(d), par((1, d)), vec(d), vec(d), vec(d), par(w_in.shape), par((1, width)),
                  par(w_s.shape), par(b_full.shape), par(w_out.shape)],
        out_specs=out_specs,
        out_shape=out_shape,
        compiler_params=_params("parallel", "parallel"),
        name="sgu",
    )(x, g, sh, sc, gt, w_in, g_v, w_s, b_full, w_out)


def _rope_angles(pos, half):
    inv_freq = ROPE_THETA ** (-jnp.arange(half, dtype=F32) / half)
    ang = pos.astype(F32)[:, None] * inv_freq[None, :]
    return jnp.cos(ang), jnp.sin(ang)


def _diff_rope_tables(pos):
    cos, sin = _rope_angles(pos, DIFF_DH // 2)
    reps = LANES // DIFF_DH
    return (jnp.tile(jnp.concatenate([cos, cos], axis=1), (1, reps)),
            jnp.tile(jnp.concatenate([-sin, sin], axis=1), (1, reps)))


def _mla_rope_tables(pos):
    cos, sin = _rope_angles(pos, MLA_ROPE // 2)
    n = pos.shape[0]
    ones = jnp.ones((n, MLA_NOPE), F32)
    zq = jnp.zeros((n, MLA_NOPE), F32)
    zp = jnp.zeros((n, MLA_HEAD_PAD - MLA_NOPE - MLA_ROPE), F32)
    q_cos = jnp.concatenate([ones, cos, cos, zp], axis=1)
    q_sin = jnp.concatenate([zq, -sin, sin, zp], axis=1)
    k_cos = jnp.concatenate([cos, cos], axis=1)
    k_sin = jnp.concatenate([sin, sin], axis=1)
    return q_cos, q_sin, k_cos, k_sin


def _prepare_weights(p):
    d = p['w_up'].shape[1]
    w = {}
    w['w_up'] = [p['w_up'][i].astype(BF16) for i in range(p['w_up'].shape[0])]
    w['w_down'] = [p['w_down'][i].astype(BF16) for i in range(p['w_down'].shape[0])]
    w['glu_a'] = p['s5_w_glu_a'].astype(BF16)
    w['glu_b'] = p['s5_w_glu_b'].astype(BF16)
    w['diff_qkv'] = p['diff_w_qkv'].astype(BF16)
    w['diff_o'] = p['diff_w_o'].astype(BF16)
    lam = jnp.stack([p['diff_lambda_q1'], p['diff_lambda_k1'], p['diff_lambda_q2'], p['diff_lambda_k2']])
    w['diff_lam'] = jnp.pad(lam.astype(F32), ((0, 4), (0, LANES - lam.shape[1])))
    w['diff_g_sub'] = p['diff_g_sub'].reshape(1, -1)

    w['mla_dq'] = p['mla_w_dq'].astype(BF16)
    kvr = p['mla_g_kv'].shape[0]
    w['mla_dc'] = p['mla_w_dkv'][:, :kvr].astype(BF16)
    wr = p['mla_w_dkv'][:, kvr:]
    hr = MLA_ROPE // 2
    w['mla_dr'] = wr.astype(BF16)
    w['mla_drr'] = jnp.concatenate([-wr[:, hr:], wr[:, :hr]], axis=1).astype(BF16)
    w['mla_g_q'] = p['mla_g_q'].reshape(1, -1)
    w['mla_g_kv'] = p['mla_g_kv'].reshape(1, -1)
    pad = MLA_HEAD_PAD - MLA_NOPE - MLA_ROPE
    qr = p['mla_w_uq'].shape[0]
    uq = p['mla_w_uq'].reshape(qr, MLA_HEADS, MLA_NOPE + MLA_ROPE)
    w['mla_uq'] = jnp.pad(uq, ((0, 0), (0, 0), (0, pad))).reshape(qr, -1).astype(BF16)
    uk = jnp.pad(p['mla_w_uk'], ((0, 0), (0, 0), (0, MLA_HEAD_PAD - MLA_NOPE)))
    w['mla_uk'] = uk.reshape(kvr, -1).astype(BF16)
    place = jnp.pad(jnp.eye(MLA_ROPE, dtype=F32), ((0, 0), (MLA_NOPE, pad)))
    w['mla_place'] = jnp.tile(place, (1, MLA_HEADS)).astype(BF16)
    w['mla_uv'] = p['mla_w_uv'].reshape(kvr, -1).astype(BF16)
    w['mla_o'] = p['mla_w_o'].astype(BF16)

    w['sgu_in'] = p['sgu_w_in'].astype(BF16)
    w['sgu_g_v'] = p['sgu_g_v'].reshape(1, -1)
    w['sgu_out'] = p['sgu_w_out'].astype(BF16)
    return w


def _sgu_spatial(w_s, b_s, t, width):
    ws = jnp.tril(w_s[:, :t, :t]).astype(BF16)
    gd = width // SGU_GROUPS
    b_full = jnp.repeat(b_s[:, :t].T.astype(F32), gd, axis=1)
    return ws, b_full


def _run_trunk(x, mods, pos, past, p, w, s5_tables):
    b, l, d = x.shape
    new = {}
    depth = p['g_mix'].shape[0]
    for i in range(depth):
        sh1, sc1, gt1, sh2, sc2, gt2 = [mods[i][:, k][:, None, :] for k in range(6)]
        g_mix = p['g_mix'][i].reshape(1, d)
        kind = i % N_MIXERS
        if kind == 0:
            h = _modulate(x, g_mix, sh1, sc1)
            h0_re = None if past is None else past['s5_re']
            h0_im = None if past is None else past['s5_im']
            y, new['s5_re'], new['s5_im'] = _s5_scan(h, s5_tables, h0_re, h0_im)
            x = _s5_glu(y, h, x, p['s5_d'].reshape(1, d), gt1, w['glu_a'], w['glu_b'])
        elif kind == 1:
            cos, sin = _diff_rope_tables(pos)
            k, v, qb, kb, vb = _diff_qkv(x, g_mix, sh1, sc1, w['diff_qkv'], cos, sin)
            if past is None:
                o = _diff_attention(qb, [(kb, vb)], w['diff_lam'], w['diff_g_sub'], causal=True)
            else:
                o = _diff_attention_cached(qb, past['diff_k'], past['diff_v'], kb, vb,
                                           w['diff_lam'], w['diff_g_sub'])
            x = _proj_residual(o, w['diff_o'], x, gt1)
            new['diff_k'] = k.reshape(b, l, DIFF_HEADS, 2 * DIFF_DH)
            new['diff_v'] = v.reshape(b, l, DIFF_HEADS, 2 * DIFF_DH)
        elif kind == 2:
            q_cos, q_sin, k_cos, k_sin = _mla_rope_tables(pos)
            cq, ckv, krope = _mla_down(x, g_mix, sh1, sc1, w['mla_dq'], w['mla_dc'], w['mla_dr'],
                                       w['mla_drr'], w['mla_g_q'], w['mla_g_kv'], k_cos, k_sin)
            qb = _mla_qup(cq, w['mla_uq'], q_cos, q_sin)
            sources = []
            if past is not None:
                sources.append(_mla_kvup(past['mla_ckv'], past['mla_krope'],
                                         w['mla_uk'], w['mla_place'], w['mla_uv']))
            sources.append(_mla_kvup(ckv, krope, w['mla_uk'], w['mla_place'], w['mla_uv']))
            o = _mla_attention(qb, sources, causal=past is None)
            x = _proj_residual(o, w['mla_o'], x, gt1)
            new['mla_ckv'], new['mla_krope'] = ckv, krope
        else:
            t = min(l, SGU_CHUNK)
            ws, b_full = _sgu_spatial(p['sgu_w_s'], p['sgu_b_s'], t, w['sgu_g_v'].shape[1])
            outs = _sgu(x, g_mix, sh1, sc1, gt1, w['sgu_in'], w['sgu_g_v'], ws, b_full, w['sgu_out'],
                        emit_v=past is not None)
            x = outs[0]
            if past is not None:
                new['sgu_v'] = outs[1]
        x = _mlp(x, p['g_ffn'][i].reshape(1, d), sh2, sc2, gt2, w['w_up'][i], w['w_down'][i],
                 p['g_final'].reshape(1, d), final_norm=(i == depth - 1))
    return x, new


def kernel(x_prompt, x_sample, c_prompt, c_sample, state_s5_re, state_s5_im, cache_diff_k, cache_diff_v, cache_mla_ckv, cache_mla_krope, w_ada, b_ada, g_mix, g_ffn, w_up, w_down, g_final, s5_a_re, s5_a_im, s5_b_re, s5_b_im, s5_c_re, s5_c_im, s5_d, s5_log_dt, s5_w_glu_a, s5_w_glu_b, diff_w_qkv, diff_lambda_q1, diff_lambda_k1, diff_lambda_q2, diff_lambda_k2, diff_g_sub, diff_w_o, mla_w_dq, mla_g_q, mla_w_uq, mla_w_dkv, mla_g_kv, mla_w_uk, mla_w_uv, mla_w_o, sgu_w_in, sgu_g_v, sgu_w_s, sgu_b_s, sgu_w_out):
    p = {
        'w_ada': w_ada, 'b_ada': b_ada, 'g_mix': g_mix, 'g_ffn': g_ffn,
        'w_up': w_up, 'w_down': w_down, 'g_final': g_final,
        's5_d': s5_d, 's5_w_glu_a': s5_w_glu_a, 's5_w_glu_b': s5_w_glu_b,
        'diff_w_qkv': diff_w_qkv, 'diff_lambda_q1': diff_lambda_q1, 'diff_lambda_k1': diff_lambda_k1,
        'diff_lambda_q2': diff_lambda_q2, 'diff_lambda_k2': diff_lambda_k2,
        'diff_g_sub': diff_g_sub, 'diff_w_o': diff_w_o,
        'mla_w_dq': mla_w_dq, 'mla_g_q': mla_g_q, 'mla_w_uq': mla_w_uq, 'mla_w_dkv': mla_w_dkv,
        'mla_g_kv': mla_g_kv, 'mla_w_uk': mla_w_uk, 'mla_w_uv': mla_w_uv, 'mla_w_o': mla_w_o,
        'sgu_w_in': sgu_w_in, 'sgu_g_v': sgu_g_v, 'sgu_w_s': sgu_w_s, 'sgu_b_s': sgu_b_s,
        'sgu_w_out': sgu_w_out,
    }
    past = {
        's5_re': state_s5_re, 's5_im': state_s5_im,
        'diff_k': cache_diff_k, 'diff_v': cache_diff_v,
        'mla_ckv': cache_mla_ckv, 'mla_krope': cache_mla_krope,
    }
    bp, lp, d = x_prompt.shape
    bs, ls, _ = x_sample.shape
    depth = w_ada.shape[0]

    c_all = jnp.concatenate([c_prompt, c_sample], axis=0)
    c_all = jnp.pad(c_all, ((0, (-c_all.shape[0]) % 8), (0, 0)))
    mod = _adaln(c_all, w_ada, b_ada).reshape(depth, c_all.shape[0], 6, d)
    mods_p = [mod[i, :bp] for i in range(depth)]
    mods_s = [mod[i, bp:bp + bs] for i in range(depth)]

    w = _prepare_weights(p)
    max_steps = max((lp // S5_SUB).bit_length() - 1, (ls // S5_SUB).bit_length() - 1, 1)
    s5_tables = _s5_tables(s5_a_re, s5_a_im, s5_b_re, s5_b_im, s5_c_re, s5_c_im, s5_log_dt, max_steps)

    pos_p = jnp.arange(lp, dtype=jnp.int32)
    pos_s = cache_diff_k.shape[1] + jnp.arange(ls, dtype=jnp.int32)
    y_prompt, sp = _run_trunk(x_prompt, mods_p, pos_p, None, p, w, s5_tables)
    y_sample, ss = _run_trunk(x_sample, mods_s, pos_s, past, p, w, s5_tables)
    return (y_prompt, y_sample,
            sp['s5_re'], sp['s5_im'], ss['s5_re'], ss['s5_im'],
            sp['diff_k'], sp['diff_v'], ss['diff_k'], ss['diff_v'],
            sp['mla_ckv'], sp['mla_krope'], ss['mla_ckv'], ss['mla_krope'],
            ss['sgu_v'])
```

```python
import functools
import math

import jax
import jax.numpy as jnp
from jax import lax
from jax.experimental import pallas as pl
from jax.experimental.pallas import tpu as pltpu

F32 = jnp.float32
BF16 = jnp.bfloat16
HIGHEST = lax.Precision.HIGHEST

NORM_EPS = 1e-6
ROPE_THETA = 10000.0
NEG_INF = -1e30
LOG2_E = math.log2(math.e)
CHUNK = 64
N_MIXERS = 4

S5_GROUP = 16
S5_SUB = 16

DIFF_HEADS = 8
DIFF_DH = 64
LAMBDA_INIT = 0.8 - 0.6 * math.exp(-0.3 * 1)

MLA_HEADS = 16
MLA_NOPE = 64
MLA_ROPE = 32
MLA_V = 64
MLA_SCALE = (MLA_NOPE + MLA_ROPE) ** -0.5
MLA_HEAD_PAD = 128

SGU_CHUNK = 128
SGU_GROUPS = 8

LANES = 128
ROW_TILE = 512
ATTN_TQ = 256
ATTN_STREAMS = 2
ATTN_TK = 256
VMEM_LIMIT_BYTES = 48 * 1024 * 1024


def _params(*sem):
    return pltpu.CompilerParams(dimension_semantics=sem, vmem_limit_bytes=VMEM_LIMIT_BYTES)


def _row_tiles(batch, length, rows=ROW_TILE):
    if length >= rows:
        assert length % rows == 0
        return 1, rows
    nb = max(1, min(batch, rows // length))
    while batch % nb:
        nb -= 1
    return nb, length


def _rms(x, g):
    return x * lax.rsqrt(jnp.mean(x * x, axis=-1, keepdims=True) + NORM_EPS) * g


def _modulated(x_ref, g_ref, sh_ref, sc_ref):
    return _rms(x_ref[...], g_ref[...]) * (1.0 + sc_ref[...]) + sh_ref[...]


def _rotate_pairs(x, half, period, lo):
    width = x.shape[-1]
    lane = lax.broadcasted_iota(jnp.int32, x.shape, x.ndim - 1) % period
    fwd = pltpu.roll(x, width - half, axis=x.ndim - 1)
    bwd = pltpu.roll(x, half, axis=x.ndim - 1)
    return jnp.where((lane >= lo) & (lane < lo + half), fwd, bwd)


def _adaln_kernel(c_ref, w_ref, b_ref, o_ref):
    c = c_ref[...]
    s = c * jax.nn.sigmoid(c)
    o_ref[0] = jnp.dot(s, w_ref[0], precision=HIGHEST, preferred_element_type=F32) + b_ref[0]


def _adaln(c_all, w_ada, b_ada):
    depth, d, n = w_ada.shape
    rows = c_all.shape[0]
    tn = n // 4
    return pl.pallas_call(
        _adaln_kernel,
        grid=(depth, n // tn),
        in_specs=[pl.BlockSpec((rows, d), lambda i, j: (0, 0)),
                  pl.BlockSpec((1, d, tn), lambda i, j: (i, 0, j)),
                  pl.BlockSpec((1, 1, tn), lambda i, j: (i, 0, j))],
        out_specs=pl.BlockSpec((1, rows, tn), lambda i, j: (i, 0, j)),
        out_shape=jax.ShapeDtypeStruct((depth, rows, n), F32),
        compiler_params=_params("parallel", "parallel"),
        name="adaln",
    )(c_all, w_ada, b_ada.reshape(depth, 1, n))


def _tile_specs(batch, length, d, nb, lt, extra_axes=0):
    if extra_axes == 0:
        act = lambda width: pl.BlockSpec((nb, lt, width), lambda i, j: (i, j, 0))
        vec = lambda width: pl.BlockSpec((nb, 1, width), lambda i, j: (i, 0, 0))
        par = lambda shape: pl.BlockSpec(shape, lambda i, j: (0,) * len(shape))
        pos = lambda width: pl.BlockSpec((lt, width), lambda i, j: (j, 0))
    else:
        act = lambda width: pl.BlockSpec((nb, lt, width), lambda i, j, f: (i, j, 0))
        vec = lambda width: pl.BlockSpec((nb, 1, width), lambda i, j, f: (i, 0, 0))
        par = lambda shape: pl.BlockSpec(shape, lambda i, j, f: (0,) * len(shape))
        pos = lambda width: pl.BlockSpec((lt, width), lambda i, j, f: (j, 0))
    return act, vec, par, pos


def _modulate_kernel(x_ref, g_ref, sh_ref, sc_ref, o_ref):
    o_ref[...] = _modulated(x_ref, g_ref, sh_ref, sc_ref)


def _modulate(x, g, sh, sc):
    b, l, d = x.shape
    nb, lt = _row_tiles(b, l)
    act, vec, par, _ = _tile_specs(b, l, d, nb, lt)
    return pl.pallas_call(
        _modulate_kernel,
        grid=(b // nb, l // lt),
        in_specs=[act(d), par((1, d)), vec(d), vec(d)],
        out_specs=act(d),
        out_shape=jax.ShapeDtypeStruct((b, l, d), F32),
        compiler_params=_params("parallel", "parallel"),
        name="modulate",
    )(x, g, sh, sc)


def _mlp_kernel(x_ref, g_ref, sh_ref, sc_ref, gt_ref, wu_ref, wd_ref, gf_ref, o_ref, h_s, acc_s,
                *, final_norm):
    f = pl.program_id(2)
    nb, lt, d = x_ref.shape

    @pl.when(f == 0)
    def _():
        h_s[...] = _modulated(x_ref, g_ref, sh_ref, sc_ref).reshape(nb * lt, d).astype(BF16)
        acc_s[...] = jnp.zeros_like(acc_s)

    a = jnp.dot(h_s[...], wu_ref[...], preferred_element_type=F32)
    a = jnp.square(jnp.maximum(a, 0.0)).astype(BF16)
    acc_s[...] += jnp.dot(a, wd_ref[...], preferred_element_type=F32)

    @pl.when(f == pl.num_programs(2) - 1)
    def _():
        y = x_ref[...] + (1.0 + gt_ref[...]) * acc_s[...].reshape(nb, lt, d)
        if final_norm:
            y = _rms(y, gf_ref[...])
        o_ref[...] = y


def _mlp(x, g, sh, sc, gt, w_up, w_down, g_final, final_norm, tf=1024):
    b, l, d = x.shape
    dff = w_up.shape[1]
    nb, lt = _row_tiles(b, l)
    act, vec, par, _ = _tile_specs(b, l, d, nb, lt, extra_axes=1)
    return pl.pallas_call(
        functools.partial(_mlp_kernel, final_norm=final_norm),
        grid=(b // nb, l // lt, dff // tf),
        in_specs=[act(d), par((1, d)), vec(d), vec(d), vec(d),
                  pl.BlockSpec((d, tf), lambda i, j, f: (0, f)),
                  pl.BlockSpec((tf, d), lambda i, j, f: (f, 0)),
                  par((1, d))],
        out_specs=act(d),
        out_shape=jax.ShapeDtypeStruct((b, l, d), F32),
        scratch_shapes=[pltpu.VMEM((nb * lt, d), BF16), pltpu.VMEM((nb * lt, d), F32)],
        compiler_params=_params("parallel", "parallel", "arbitrary"),
        name="mlp",
    )(x, g, sh, sc, gt, w_up, w_down, g_final)


def _proj_res_kernel(a_ref, w_ref, x_ref, gt_ref, o_ref):
    nb, lt, k = a_ref.shape
    d = w_ref.shape[1]
    r = jnp.dot(a_ref[...].reshape(nb * lt, k), w_ref[...], preferred_element_type=F32)
    o_ref[...] = x_ref[...] + (1.0 + gt_ref[...]) * r.reshape(nb, lt, d)


def _proj_residual(a, w, x, gt):
    b, l, d = x.shape
    k = a.shape[-1]
    nb, lt = _row_tiles(b, l)
    act, vec, par, _ = _tile_specs(b, l, d, nb, lt)
    return pl.pallas_call(
        _proj_res_kernel,
        grid=(b // nb, l // lt),
        in_specs=[act(k), par((k, d)), act(d), vec(d)],
        out_specs=act(d),
        out_shape=jax.ShapeDtypeStruct((b, l, d), F32),
        compiler_params=_params("parallel", "parallel"),
        name="proj_residual",
    )(a, w, x, gt)


def _s5_kernel(*refs, nch, batch, nsteps, has_h0):
    if has_h0:
        u_ref, bc_ref, m_ref, cc_ref, a1_ref, a2_ref, h0_ref, y_ref, hl_ref = refs
    else:
        u_ref, bc_ref, m_ref, cc_ref, a1_ref, a2_ref, y_ref, hl_ref = refs
    u = u_ref[0]
    rows = u.shape[0]
    x = jnp.dot(u, bc_ref[0], precision=HIGHEST, preferred_element_type=F32)
    half = x.shape[1] // 2
    if has_h0:
        h0 = h0_ref[0]
        x = x + a1_ref[0, 0:1, :] * h0 + a2_ref[0, 0:1, :] * pltpu.roll(h0, half, axis=1)
    kidx = lax.broadcasted_iota(jnp.int32, x.shape, 0) & (nch - 1)
    for j in range(nsteps):
        s = 1 << j
        xs = jnp.where(kidx >= s, pltpu.roll(x, s, axis=0), 0.0)
        x = x + a1_ref[0, j:j + 1, :] * xs + a2_ref[0, j:j + 1, :] * pltpu.roll(xs, half, axis=1)
    for b in range(batch):
        r = (b + 1) * nch - 1
        hl_ref[0, b:b + 1, :] = x[r:r + 1, :]
    hstart = jnp.where(kidx >= 1, pltpu.roll(x, 1, axis=0), 0.0)
    if has_h0:
        hstart = hstart + h0
    y = jnp.dot(u.astype(BF16), m_ref[0], preferred_element_type=F32)
    y = y + jnp.dot(hstart.astype(BF16), cc_ref[0], preferred_element_type=F32)
    y_ref[0] = y


def _s5_tables(a_re, a_im, b_re, b_im, c_re, c_im, log_dt, max_steps):
    g, p = a_re.shape
    sub = S5_SUB
    dt = jnp.exp(log_dt.astype(F32))[:, None]
    zr, zi = a_re.astype(F32) * dt, a_im.astype(F32) * dt

    def zpow(n):
        mag = jnp.exp(zr * n)
        return mag * jnp.cos(zi * n), mag * jnp.sin(zi * n)

    er = jnp.expm1(zr) * jnp.cos(zi) - 2.0 * jnp.square(jnp.sin(0.5 * zi))
    ei = jnp.exp(zr) * jnp.sin(zi)
    den = a_re * a_re + a_im * a_im
    fr = (er * a_re + ei * a_im) / den
    fi = (ei * a_re - er * a_im) / den
    bb_re = fr[..., None] * b_re - fi[..., None] * b_im
    bb_im = fr[..., None] * b_im + fi[..., None] * b_re

    lags = jnp.arange(sub + 1, dtype=F32)[:, None, None]
    pr, pi = zpow(lags)

    qr, qi = pr[:sub][::-1], pi[:sub][::-1]
    bcr = qr[..., None] * bb_re[None] - qi[..., None] * bb_im[None]
    bci = qr[..., None] * bb_im[None] + qi[..., None] * bb_re[None]
    bc = jnp.concatenate([bcr, bci], axis=2)
    bc = bc.transpose(1, 0, 3, 2).reshape(g, sub * S5_GROUP, 2 * p)

    ar, ai = pr[1:], pi[1:]
    ccr = c_re[None] * ar[:, :, None, :] - c_im[None] * ai[:, :, None, :]
    cci = c_re[None] * ai[:, :, None, :] + c_im[None] * ar[:, :, None, :]
    cc = jnp.concatenate([ccr, -cci], axis=3)
    cc = cc.transpose(1, 3, 0, 2).reshape(g, 2 * p, sub * S5_GROUP)

    kr = (jnp.einsum('gcp,lgp,gpd->lgcd', c_re, pr[:sub], bb_re, precision=HIGHEST)
          - jnp.einsum('gcp,lgp,gpd->lgcd', c_re, pi[:sub], bb_im, precision=HIGHEST)
          - jnp.einsum('gcp,lgp,gpd->lgcd', c_im, pr[:sub], bb_im, precision=HIGHEST)
          - jnp.einsum('gcp,lgp,gpd->lgcd', c_im, pi[:sub], bb_re, precision=HIGHEST))
    tt = jnp.arange(sub)
    lag = tt[None, :] - tt[:, None]
    toep = jnp.where((lag >= 0)[:, :, None, None, None],
                     kr[jnp.clip(lag, 0, sub - 1)], 0.0)
    m = toep.transpose(2, 0, 4, 1, 3).reshape(g, sub * S5_GROUP, sub * S5_GROUP)

    steps = (sub * (2 ** jnp.arange(max_steps))).astype(F32)[:, None, None]
    sr, si = zpow(steps)
    a1 = jnp.concatenate([sr, sr], axis=-1).transpose(1, 0, 2)
    a2 = jnp.concatenate([-si, si], axis=-1).transpose(1, 0, 2)
    pad = (-max_steps) % 8
    a1 = jnp.pad(a1, ((0, 0), (0, pad), (0, 0)))
    a2 = jnp.pad(a2, ((0, 0), (0, pad), (0, 0)))
    return bc, m.astype(BF16), cc.astype(BF16), a1, a2


def _s5_lane_kernel(h_ref, bct_ref, mt_ref, cct_ref, a1_ref, a2_ref, y_ref, hl_ref, ut_s, yt_s,
                    *, nch, nsteps):
    sub, c = S5_SUB, S5_GROUP
    groups = ut_s.shape[0]
    for t in range(sub):
        att = h_ref[0, pl.ds(t, nch, stride=sub), :].T
        for g in range(groups):
            ut_s[g, t * c:(t + 1) * c, :] = att[g * c:(g + 1) * c, :]
    half = bct_ref.shape[1] // 2
    lane = lax.broadcasted_iota(jnp.int32, (2 * half, nch), 1)
    for g in range(groups):
        ut = ut_s[g]
        x = jnp.dot(bct_ref[g], ut, precision=HIGHEST, preferred_element_type=F32)
        for j in range(nsteps):
            s = 1 << j
            xs = jnp.where(lane >= s, pltpu.roll(x, s, axis=1), 0.0)
            x = x + a1_ref[g, :, j:j + 1] * xs + a2_ref[g, :, j:j + 1] * pltpu.roll(xs, half, axis=0)
        hl_ref[0, g:g + 1, :] = x[:, nch - LANES:].T[LANES - 1:LANES, :]
        hstart = jnp.where(lane >= 1, pltpu.roll(x, 1, axis=1), 0.0)
        yt = jnp.dot(mt_ref[g], ut.astype(BF16), preferred_element_type=F32)
        yt = yt + jnp.dot(cct_ref[g], hstart.astype(BF16), preferred_element_type=F32)
        for t in range(sub):
            yt_s[t, g * c:(g + 1) * c, :] = yt[t * c:(t + 1) * c, :]
    for t in range(sub):
        y_ref[0, pl.ds(t, nch, stride=sub), :] = yt_s[t].T


def _s5_scan_lanes(h, tables):
    bc, m, cc, a1, a2 = tables
    b, l, d = h.shape
    g, k, p2 = bc.shape
    nch = l // S5_SUB
    gpb = LANES // S5_GROUP
    nsteps = nch.bit_length() - 1
    wspec = lambda shape: pl.BlockSpec((gpb,) + shape, lambda j, bi: (j, 0, 0))
    act = pl.BlockSpec((1, l, LANES), lambda j, bi: (bi, 0, j))
    return pl.pallas_call(
        functools.partial(_s5_lane_kernel, nch=nch, nsteps=nsteps),
        grid=(g // gpb, b),
        in_specs=[act, wspec((p2, k)), wspec((k, k)), wspec((k, p2)),
                  wspec((p2, a1.shape[1])), wspec((p2, a2.shape[1]))],
        out_specs=[act, pl.BlockSpec((1, gpb, p2), lambda j, bi: (bi, j, 0))],
        out_shape=[jax.ShapeDtypeStruct((b, l, d), F32), jax.ShapeDtypeStruct((b, g, p2), F32)],
        scratch_shapes=[pltpu.VMEM((gpb, k, nch), F32), pltpu.VMEM((S5_SUB, LANES, nch), F32)],
        compiler_params=_params("parallel", "parallel"),
        name="s5_scan_lanes",
    )(h, bc.transpose(0, 2, 1), m.transpose(0, 2, 1), cc.transpose(0, 2, 1),
      a1.transpose(0, 2, 1), a2.transpose(0, 2, 1))


def _s5_scan(h, tables, h0_re, h0_im):
    bc, m, cc, a1, a2 = tables
    b, l, d = h.shape
    if h0_re is None and (l // S5_SUB) % LANES == 0:
        y, hl = _s5_scan_lanes(h, tables)
        return y, hl[..., :hl.shape[-1] // 2], hl[..., hl.shape[-1] // 2:]
    g = d // S5_GROUP
    p2 = bc.shape[-1]
    nch = l // S5_SUB
    assert nch & (nch - 1) == 0
    nsteps = nch.bit_length() - 1
    rows = b * nch
    k = S5_SUB * S5_GROUP
    u = h.reshape(b, nch, S5_SUB, g, S5_GROUP).transpose(3, 0, 1, 2, 4).reshape(g, rows, k)
    has_h0 = h0_re is not None
    grp = lambda shape: pl.BlockSpec((1,) + shape, lambda i: (i, 0, 0))
    args = [u, bc, m, cc, a1, a2]
    in_specs = [grp((rows, k)), grp((k, p2)), grp((k, k)), grp((p2, k)),
                grp(a1.shape[1:]), grp(a2.shape[1:])]
    if has_h0:
        h0 = jnp.concatenate([h0_re, h0_im], axis=-1).astype(F32).transpose(1, 0, 2)
        h0 = jnp.pad(h0[:, :, None, :], ((0, 0), (0, 0), (0, nch - 1), (0, 0))).reshape(g, rows, p2)
        args.append(h0)
        in_specs.append(grp((rows, p2)))
    y, hl = pl.pallas_call(
        functools.partial(_s5_kernel, nch=nch, batch=b, nsteps=nsteps, has_h0=has_h0),
        grid=(g,),
        in_specs=in_specs,
        out_specs=[grp((rows, k)), grp((b, p2))],
        out_shape=[jax.ShapeDtypeStruct((g, rows, k), F32), jax.ShapeDtypeStruct((g, b, p2), F32)],
        compiler_params=_params("parallel"),
        name="s5_scan",
    )(*args)
    y = y.reshape(g, b, nch, S5_SUB, S5_GROUP).transpose(1, 2, 3, 0, 4).reshape(b, l, d)
    hl = hl.transpose(1, 0, 2)
    return y, hl[..., :p2 // 2], hl[..., p2 // 2:]


def _glu_kernel(y_ref, h_ref, x_ref, d_ref, gt_ref, wa_ref, wb_ref, o_ref):
    nb, lt, d = x_ref.shape
    z = jax.nn.gelu(y_ref[...] + d_ref[...] * h_ref[...]).reshape(nb * lt, d).astype(BF16)
    a = jnp.dot(z, wa_ref[...], preferred_element_type=F32)
    b = jnp.dot(z, wb_ref[...], preferred_element_type=F32)
    out = (a * jax.nn.sigmoid(b)).reshape(nb, lt, d)
    o_ref[...] = x_ref[...] + (1.0 + gt_ref[...]) * out


def _s5_glu(y, h, x, d_skip, gt, wa, wb):
    b, l, d = x.shape
    nb, lt = _row_tiles(b, l)
    act, vec, par, _ = _tile_specs(b, l, d, nb, lt)
    return pl.pallas_call(
        _glu_kernel,
        grid=(b // nb, l // lt),
        in_specs=[act(d), act(d), act(d), par((1, d)), vec(d), par((d, d)), par((d, d))],
        out_specs=act(d),
        out_shape=jax.ShapeDtypeStruct((b, l, d), F32),
        compiler_params=_params("parallel", "parallel"),
        name="s5_glu",
    )(y, h, x, d_skip, gt, wa, wb)


def _diff_qkv_kernel(x_ref, g_ref, sh_ref, sc_ref, w_ref, cos_ref, sin_ref,
                     k_ref, v_ref, qb_ref, kb_ref, vb_ref):
    nb, lt, d = x_ref.shape
    h = _modulated(x_ref, g_ref, sh_ref, sc_ref).reshape(nb * lt, d).astype(BF16)
    reps = d // cos_ref.shape[1]
    cos = jnp.tile(cos_ref[...], (1, reps))[None]
    sin = jnp.tile(sin_ref[...], (1, reps))[None]

    def roped(cols):
        t = jnp.dot(h, w_ref[:, cols * d:(cols + 1) * d], preferred_element_type=F32)
        r = _rotate_pairs(t, DIFF_DH // 2, DIFF_DH, 0)
        return t.reshape(nb, lt, d) * cos + r.reshape(nb, lt, d) * sin

    q = roped(0)
    qb_ref[...] = (q * (LOG2_E * DIFF_DH ** -0.5)).astype(BF16)
    k = roped(1)
    kb_ref[...] = k.astype(BF16)
    v = jnp.dot(h, w_ref[:, 2 * d:], preferred_element_type=F32).reshape(nb, lt, d)
    vb_ref[...] = v.astype(BF16)
    hw = k_ref.shape[2]
    heads = d // hw
    for hh in range(heads):
        k_ref[:, pl.ds(hh, lt, stride=heads), :] = k[:, :, hh * hw:(hh + 1) * hw]
        v_ref[:, pl.ds(hh, lt, stride=heads), :] = v[:, :, hh * hw:(hh + 1) * hw]


def _diff_qkv(x, g, sh, sc, w_qkv, cos, sin):
    b, l, d = x.shape
    nb, lt = _row_tiles(b, l)
    act, vec, par, pos = _tile_specs(b, l, d, nb, lt)
    hw = d // DIFF_HEADS
    return pl.pallas_call(
        _diff_qkv_kernel,
        grid=(b // nb, l // lt),
        in_specs=[act(d), par((1, d)), vec(d), vec(d), par((d, 3 * d)),
                  pos(cos.shape[1]), pos(sin.shape[1])],
        out_specs=[pl.BlockSpec((nb, lt * DIFF_HEADS, hw), lambda i, j: (i, j, 0))] * 2 + [act(d)] * 3,
        out_shape=[jax.ShapeDtypeStruct((b, l * DIFF_HEADS, hw), F32)] * 2
                  + [jax.ShapeDtypeStruct((b, l, d), BF16)] * 3,
        compiler_params=_params("parallel", "parallel"),
        name="diff_qkv",
    )(x, g, sh, sc, w_qkv, cos, sin)


_NT = (((1,), (1,)), ((), ()))
_TN = (((0,), (0,)), ((), ()))


def _softmax_update(m_s, l_s, acc_s, i, s, vb):
    m_prev = m_s[i]
    m_new = jnp.maximum(m_prev, jnp.max(s, axis=0, keepdims=True))
    alpha = jnp.exp2(m_prev - m_new)
    p = jnp.exp2(s - m_new)
    l_s[i] = alpha * l_s[i] + jnp.sum(p, axis=0, keepdims=True)
    acc_s[i] = alpha * acc_s[i] + lax.dot_general(vb, p.astype(BF16), _TN, preferred_element_type=F32)
    m_s[i] = m_new


def _softmax_init(m_s, l_s, acc_s):
    m_s[...] = jnp.full_like(m_s, NEG_INF)
    l_s[...] = jnp.zeros_like(l_s)
    acc_s[...] = jnp.zeros_like(acc_s)


def _softmax_result(l_s, acc_s, i):
    return acc_s[i] / l_s[i]


def _softmax_scratch(n, vw, queries):
    return [pltpu.VMEM((n, 1, queries), F32), pltpu.VMEM((n, 1, queries), F32),
            pltpu.VMEM((n, vw, queries), F32)]


def _attend_causal(scores, k_ref, v_ref, m_s, l_s, acc_s, s_a, s_b, *, tq, qi):
    assert len(scores) == 2 and tq % CHUNK == 0
    _softmax_init(m_s, l_s, acc_s)
    update = functools.partial(_softmax_update, m_s, l_s, acc_s)
    block = lambda ref, j: ref[0, pl.ds(pl.multiple_of(j * tq, tq), tq), :].astype(BF16)

    def qk(dst, j, streams):
        kb = block(k_ref, j)
        for i in streams:
            dst[i] = scores[i](kb)

    def pv(src, j, streams, masked=None):
        vb = block(v_ref, j)
        for i in streams:
            update(i, jnp.where(vis, src[i], NEG_INF) if i == masked else src[i], vb)

    def pair(jj, carry):
        qk(s_b, 2 * jj + 1, (0, 1))
        pv(s_a, 2 * jj, (0, 1))
        qk(s_a, 2 * jj + 2, (0, 1))
        pv(s_b, 2 * jj + 1, (0, 1))
        return carry

    qk(s_a, 0, (0, 1))
    lax.fori_loop(0, qi, pair, 0)
    kc = lax.broadcasted_iota(jnp.int32, (tq, tq), 0) // CHUNK
    qc = lax.broadcasted_iota(jnp.int32, (tq, tq), 1) // CHUNK
    vis = jnp.concatenate([kc <= qc] * 2, axis=1)
    qk(s_b, 2 * qi + 1, (1,))
    pv(s_a, 2 * qi, (0, 1), masked=0)
    pv(s_b, 2 * qi + 1, (1,), masked=1)
    return [_softmax_result(l_s, acc_s, i) for i in range(2)]


def _diff_attn_kernel(q_ref, k_ref, v_ref, lam_ref, gsub_ref, o_ref, m_s, l_s, acc_s, s_a, s_b, *, tq):
    def make_score(i):
        q2 = _diff_query_groups(q_ref[0, i * tq:(i + 1) * tq, :])
        return lambda kb: lax.dot_general(kb, q2, _NT, preferred_element_type=F32)

    outs = _attend_causal([make_score(0), make_score(1)], k_ref, v_ref, m_s, l_s, acc_s, s_a, s_b,
                          tq=tq, qi=pl.program_id(2))
    lam = _diff_lambda(lam_ref)
    for i, o2 in enumerate(outs):
        o = (o2[:, :tq] - lam * o2[:, tq:]).T
        o = _rms(o, gsub_ref[...]) * (1.0 - LAMBDA_INIT)
        o_ref[0, i * tq:(i + 1) * tq, :] = o.astype(o_ref.dtype)


def _causal_attention_call(kernel, qb, k, v, extra, qw, vw, name):
    b, l, _ = qb.shape
    tq, ns = ATTN_TQ, ATTN_STREAMS
    assert l % (ns * tq) == 0
    tile = lambda width: pl.BlockSpec((1, ns * tq, width), lambda bi, h, qi: (bi, qi, h))
    seq = lambda width: pl.BlockSpec((1, l, width), lambda bi, h, qi: (bi, 0, h))
    units = qb.shape[2] // qw
    return pl.pallas_call(
        functools.partial(kernel, tq=tq),
        grid=(b, units, l // (ns * tq)),
        in_specs=[tile(qw), seq(qw), seq(vw)]
                 + [pl.BlockSpec(a.shape, lambda bi, h, qi: (0, 0)) for a in extra],
        out_specs=tile(vw),
        out_shape=jax.ShapeDtypeStruct((b, l, units * vw), BF16),
        scratch_shapes=_softmax_scratch(ns, vw, 2 * tq) + [pltpu.VMEM((ns, tq, 2 * tq), F32)] * 2,
        compiler_params=_params("parallel", "parallel", "arbitrary"),
        name=name,
    )(qb, k, v, *extra)


def _diff_attention(qb, kb, vb, lam_rows, g_sub):
    hw = 2 * DIFF_DH
    return _causal_attention_call(_diff_attn_kernel, qb, kb, vb, [lam_rows, g_sub], hw, hw,
                                  "diff_attention")


def _diff_lambda(lam_ref):
    lp = lam_ref[...]
    return (jnp.exp(jnp.sum(lp[0:1] * lp[1:2], axis=-1, keepdims=True))
            - jnp.exp(jnp.sum(lp[2:3] * lp[3:4], axis=-1, keepdims=True)) + LAMBDA_INIT)


def _diff_query_groups(q):
    lane = lax.broadcasted_iota(jnp.int32, q.shape, 1)
    zero = jnp.zeros_like(q)
    return jnp.concatenate([jnp.where(lane < DIFF_DH, q, zero), jnp.where(lane >= DIFF_DH, q, zero)], axis=0)


def _diff_cached_kernel(q_ref, kc_ref, vc_ref, kn_ref, vn_ref, lam_ref, gsub_ref, o_ref,
                        m_s, l_s, acc_s, *, tk):
    c = pl.program_id(1)
    lq = q_ref.shape[1]
    hw = kc_ref.shape[2]
    heads = q_ref.shape[2] // hw
    update = functools.partial(_softmax_update, m_s, l_s, acc_s)
    q2 = lambda hh: _diff_query_groups(q_ref[0, :, hh * hw:(hh + 1) * hw])

    @pl.when(c == 0)
    def _():
        _softmax_init(m_s, l_s, acc_s)

    def block(j, carry):
        rows = pl.ds(pl.multiple_of(j * tk * heads, tk * heads), tk * heads)
        k_blk, v_blk = kc_ref.at[0, rows, :], vc_ref.at[0, rows, :]
        for hh in range(heads):
            kb = k_blk[pl.ds(hh, tk, stride=heads), :].astype(BF16)
            vb = v_blk[pl.ds(hh, tk, stride=heads), :].astype(BF16)
            update(hh, lax.dot_general(kb, q2(hh), _NT, preferred_element_type=F32), vb)
        return carry

    lax.fori_loop(0, kc_ref.shape[1] // (tk * heads), block, 0)

    @pl.when(c == pl.num_programs(1) - 1)
    def _():
        lam = _diff_lambda(lam_ref)
        for hh in range(heads):
            kb = kn_ref[0, :, hh * hw:(hh + 1) * hw]
            vb = vn_ref[0, :, hh * hw:(hh + 1) * hw]
            update(hh, lax.dot_general(kb, q2(hh), _NT, preferred_element_type=F32), vb)
            o2 = _softmax_result(l_s, acc_s, hh)
            o = (o2[:, :lq] - lam * o2[:, lq:]).T
            o = _rms(o, gsub_ref[...]) * (1.0 - LAMBDA_INIT)
            o_ref[0, :, hh * hw:(hh + 1) * hw] = o.astype(o_ref.dtype)


def _diff_attention_cached(qb, cache_k, cache_v, kb, vb, lam_rows, g_sub):
    b, l, d = qb.shape
    _, p, heads, hw = cache_k.shape
    pc = p if p <= 1024 else 1024
    tk = ATTN_TK if pc % ATTN_TK == 0 else CHUNK
    assert p % pc == 0 and pc % tk == 0
    new = pl.BlockSpec((1, l, d), lambda bi, c: (bi, 0, 0))
    cache = pl.BlockSpec((1, pc * heads, hw), lambda bi, c: (bi, c, 0))
    return pl.pallas_call(
        functools.partial(_diff_cached_kernel, tk=tk),
        grid=(b, p // pc),
        in_specs=[new, cache, cache, new, new,
                  pl.BlockSpec(lam_rows.shape, lambda bi, c: (0, 0)),
                  pl.BlockSpec((1, hw), lambda bi, c: (0, 0))],
        out_specs=new,
        out_shape=jax.ShapeDtypeStruct((b, l, d), BF16),
        scratch_shapes=_softmax_scratch(heads, hw, 2 * l),
        compiler_params=_params("parallel", "arbitrary"),
        name="diff_attention_cached",
    )(qb, cache_k.reshape(b, p * heads, hw), cache_v.reshape(b, p * heads, hw), kb, vb, lam_rows, g_sub)


def _mla_down_kernel(x_ref, g_ref, sh_ref, sc_ref, wq_ref, wc_ref, wr_ref, wrr_ref,
                     gq_ref, gkv_ref, cos_ref, sin_ref, cq_ref, ckv_ref, kr_ref):
    nb, lt, d = x_ref.shape
    h = _modulated(x_ref, g_ref, sh_ref, sc_ref).reshape(nb * lt, d).astype(BF16)
    dot = lambda w_ref: jnp.dot(h, w_ref[...], preferred_element_type=F32)
    cq_ref[...] = _rms(dot(wq_ref), gq_ref[...]).reshape(cq_ref.shape).astype(cq_ref.dtype)
    ckv_ref[...] = _rms(dot(wc_ref), gkv_ref[...]).reshape(ckv_ref.shape)
    r = dot(wr_ref).reshape(kr_ref.shape)
    rr = dot(wrr_ref).reshape(kr_ref.shape)
    kr_ref[...] = r * cos_ref[...][None] + rr * sin_ref[...][None]


def _mla_down(x, g, sh, sc, wq, wc, wr, wrr, g_q, g_kv, cos, sin):
    b, l, d = x.shape
    nb, lt = _row_tiles(b, l)
    act, vec, par, pos = _tile_specs(b, l, d, nb, lt)
    qr, kvr, rope = wq.shape[1], wc.shape[1], wr.shape[1]
    return pl.pallas_call(
        _mla_down_kernel,
        grid=(b // nb, l // lt),
        in_specs=[act(d), par((1, d)), vec(d), vec(d),
                  par((d, qr)), par((d, kvr)), par((d, rope)), par((d, rope)),
                  par((1, qr)), par((1, kvr)), pos(rope), pos(rope)],
        out_specs=[act(qr), act(kvr), act(rope)],
        out_shape=[jax.ShapeDtypeStruct((b, l, qr), BF16),
                   jax.ShapeDtypeStruct((b, l, kvr), F32),
                   jax.ShapeDtypeStruct((b, l, rope), F32)],
        compiler_params=_params("parallel", "parallel"),
        name="mla_down",
    )(x, g, sh, sc, wq, wc, wr, wrr, g_q, g_kv, cos, sin)


def _mla_qup_kernel(cq_ref, w_ref, cos_ref, sin_ref, q_ref):
    nb, lt, r = cq_ref.shape
    n = w_ref.shape[1]
    t = jnp.dot(cq_ref[...].reshape(nb * lt, r), w_ref[...], preferred_element_type=F32)
    rot = _rotate_pairs(t, MLA_ROPE // 2, MLA_HEAD_PAD, MLA_NOPE)
    reps = n // cos_ref.shape[1]
    cos = jnp.tile(cos_ref[...], (1, reps))[None]
    sin = jnp.tile(sin_ref[...], (1, reps))[None]
    q = t.reshape(nb, lt, n) * cos + rot.reshape(nb, lt, n) * sin
    q_ref[...] = (q * (LOG2_E * MLA_SCALE)).astype(q_ref.dtype)


def _mla_qup(cq, w_uq_pad, cos, sin):
    b, l, r = cq.shape
    n = w_uq_pad.shape[1]
    nb, lt = _row_tiles(b, l)
    act, vec, par, pos = _tile_specs(b, l, r, nb, lt)
    return pl.pallas_call(
        _mla_qup_kernel,
        grid=(b // nb, l // lt),
        in_specs=[act(r), par((r, n)), pos(cos.shape[1]), pos(sin.shape[1])],
        out_specs=act(n),
        out_shape=jax.ShapeDtypeStruct((b, l, n), BF16),
        compiler_params=_params("parallel", "parallel"),
        name="mla_q_up",
    )(cq, w_uq_pad, cos, sin)


def _mla_kvup_kernel(ckv_ref, kr_ref, wk_ref, we_ref, wv_ref, k_ref, v_ref):
    nb, lt, r = ckv_ref.shape
    c = ckv_ref[...].reshape(nb * lt, r).astype(BF16)
    kr = kr_ref[...].reshape(nb * lt, kr_ref.shape[2]).astype(BF16)
    k = (jnp.dot(c, wk_ref[...], preferred_element_type=F32)
         + jnp.dot(kr, we_ref[...], preferred_element_type=F32))
    k_ref[...] = k.reshape(k_ref.shape).astype(k_ref.dtype)
    v_ref[...] = jnp.dot(c, wv_ref[...], preferred_element_type=F32).reshape(v_ref.shape).astype(v_ref.dtype)


def _mla_kvup(ckv, krope, w_uk_pad, w_place, w_uv):
    b, l, r = ckv.shape
    rows = ROW_TILE if l % ROW_TILE == 0 else l
    nb, lt = _row_tiles(b, l, rows)
    act, vec, par, _ = _tile_specs(b, l, r, nb, lt)
    nk, nv = w_uk_pad.shape[1], w_uv.shape[1]
    rope = krope.shape[2]
    return pl.pallas_call(
        _mla_kvup_kernel,
        grid=(b // nb, l // lt),
        in_specs=[act(r), act(rope), par((r, nk)), par((rope, nk)), par((r, nv))],
        out_specs=[act(nk), act(nv)],
        out_shape=[jax.ShapeDtypeStruct((b, l, nk), BF16), jax.ShapeDtypeStruct((b, l, nv), BF16)],
        compiler_params=_params("parallel", "parallel"),
        name="mla_kv_up",
    )(ckv, krope, w_uk_pad, w_place, w_uv)


def _mla_attn_kernel(q_ref, k_ref, v_ref, o_ref, m_s, l_s, acc_s, s_a, s_b, *, tq):
    hp = MLA_HEAD_PAD

    def make_score(i):
        qa = q_ref[0, i * tq:(i + 1) * tq, :hp]
        qb = q_ref[0, i * tq:(i + 1) * tq, hp:]
        return lambda kb: jnp.concatenate(
            [lax.dot_general(kb[:, :hp], qa, _NT, preferred_element_type=F32),
             lax.dot_general(kb[:, hp:], qb, _NT, preferred_element_type=F32)], axis=1)

    outs = _attend_causal([make_score(0), make_score(1)], k_ref, v_ref, m_s, l_s, acc_s, s_a, s_b,
                          tq=tq, qi=pl.program_id(2))
    row = lax.broadcasted_iota(jnp.int32, (acc_s.shape[1], tq), 0)
    for i, o2 in enumerate(outs):
        o = jnp.where(row < MLA_V, o2[:, :tq], o2[:, tq:]).T
        o_ref[0, i * tq:(i + 1) * tq, :] = o.astype(o_ref.dtype)


def _mla_attention(qb, kb, vb):
    return _causal_attention_call(_mla_attn_kernel, qb, kb, vb, [], 2 * MLA_HEAD_PAD, 2 * MLA_V,
                                  "mla_attention")


def _mla_qabs_kernel(cq_ref, w_ref, cos_ref, sin_ref, wabs_ref, q_ref):
    nb, lt, r = cq_ref.shape
    n = w_ref.shape[1]
    t = jnp.dot(cq_ref[...].reshape(nb * lt, r), w_ref[...], preferred_element_type=F32)
    rot = _rotate_pairs(t, MLA_ROPE // 2, MLA_HEAD_PAD, MLA_NOPE)
    reps = n // cos_ref.shape[1]
    cos = jnp.tile(cos_ref[...], (1, reps))[None]
    sin = jnp.tile(sin_ref[...], (1, reps))[None]
    q = (t.reshape(nb, lt, n) * cos + rot.reshape(nb, lt, n) * sin) * (LOG2_E * MLA_SCALE)
    qb = q.reshape(nb * lt, n).astype(BF16)
    hp = MLA_HEAD_PAD
    for hh in range(n // hp):
        o = jnp.dot(qb[:, hh * hp:(hh + 1) * hp], wabs_ref[hh], preferred_element_type=F32)
        q_ref[:, hh] = o.reshape(nb, lt, o.shape[1]).astype(q_ref.dtype)


def _mla_qabs(cq, w_uq_pad, cos, sin, w_abs):
    b, l, r = cq.shape
    n = w_uq_pad.shape[1]
    heads, _, qw = w_abs.shape
    nb, lt = _row_tiles(b, l)
    act, vec, par, pos = _tile_specs(b, l, r, nb, lt)
    return pl.pallas_call(
        _mla_qabs_kernel,
        grid=(b // nb, l // lt),
        in_specs=[act(r), par((r, n)), pos(cos.shape[1]), pos(sin.shape[1]), par(w_abs.shape)],
        out_specs=pl.BlockSpec((nb, heads, lt, qw), lambda i, j: (i, 0, j, 0)),
        out_shape=jax.ShapeDtypeStruct((b, heads, l, qw), BF16),
        compiler_params=_params("parallel", "parallel"),
        name="mla_q_absorb",
    )(cq, w_uq_pad, cos, sin, w_abs)


def _mla_cached_kernel(q_ref, cc_ref, rc_ref, cn_ref, rn_ref, wuvt_ref, o_ref, m_s, l_s, acc_s, *, tk):
    heads, lq, qw = q_ref.shape[1:]
    q2 = q_ref[0].reshape(heads * lq, qw)

    def step(c, r):
        rpad = jnp.concatenate([r, jnp.zeros((r.shape[0], qw - c.shape[1] - r.shape[1]), r.dtype)], axis=1)
        kb = jnp.concatenate([c, rpad], axis=1).astype(BF16)
        s = lax.dot_general(kb, q2, _NT, preferred_element_type=F32)
        _softmax_update(m_s, l_s, acc_s, 0, s, c.astype(BF16))

    def block(j, carry):
        rows = pl.ds(pl.multiple_of(j * tk, tk), tk)
        step(cc_ref[0, rows, :], rc_ref[0, rows, :])
        return carry

    _softmax_init(m_s, l_s, acc_s)
    lax.fori_loop(0, cc_ref.shape[1] // tk, block, 0)
    step(cn_ref[0], rn_ref[0])
    lat = _softmax_result(l_s, acc_s, 0)
    assert 2 * lq == LANES and 2 * MLA_V == LANES
    lane = lax.broadcasted_iota(jnp.int32, (lq, LANES), 1)
    for pr in range(heads // 2):
        lp = lat[:, pr * LANES:(pr + 1) * LANES].astype(BF16)
        tt = jnp.dot(wuvt_ref[pr], lp, preferred_element_type=F32).T
        o_ref[0, :, pr * LANES:(pr + 1) * LANES] = jnp.where(lane < MLA_V, tt[:lq], tt[lq:]).astype(o_ref.dtype)


def _mla_attention_cached(q_abs, cache_ckv, cache_krope, ckv, krope, w_uvt):
    b, heads, l, qw = q_abs.shape
    p, rank = cache_ckv.shape[1:]
    rope = cache_krope.shape[2]
    tk = ATTN_TK if p % ATTN_TK == 0 else CHUNK
    assert p % tk == 0
    full = lambda shape: pl.BlockSpec((1,) + shape, lambda bi: (bi,) + (0,) * len(shape))
    return pl.pallas_call(
        functools.partial(_mla_cached_kernel, tk=tk),
        grid=(b,),
        in_specs=[full((heads, l, qw)), full((p, rank)), full((p, rope)), full((l, rank)), full((l, rope)),
                  pl.BlockSpec(w_uvt.shape, lambda bi: (0, 0, 0))],
        out_specs=full((l, heads * MLA_V)),
        out_shape=jax.ShapeDtypeStruct((b, l, heads * MLA_V), BF16),
        scratch_shapes=_softmax_scratch(1, rank, heads * l),
        compiler_params=_params("parallel"),
        name="mla_attention_cached",
    )(q_abs, cache_ckv, cache_krope, ckv, krope, w_uvt)


def _sgu_kernel(x_ref, g_ref, sh_ref, sc_ref, gt_ref, win_ref, gv_ref, ws_ref, bs_ref, wout_ref,
                *out_refs, t, emit_v):
    o_ref = out_refs[0]
    nb, lt, d = x_ref.shape
    rows = nb * lt
    width = gv_ref.shape[1]
    gd = width // SGU_GROUPS
    h = _modulated(x_ref, g_ref, sh_ref, sc_ref).reshape(rows, d).astype(BF16)
    u = jax.nn.gelu(jnp.dot(h, win_ref[:, :width], preferred_element_type=F32))
    v = jax.nn.gelu(jnp.dot(h, win_ref[:, width:], preferred_element_type=F32))
    v = _rms(v, gv_ref[...])
    if emit_v:
        out_refs[1][...] = v.reshape(nb, lt, width)
    vb = v.astype(BF16)
    bias = bs_ref[...]
    gated = []
    for c in range(rows // t):
        sv = [jnp.dot(ws_ref[gi], vb[c * t:(c + 1) * t, gi * gd:(gi + 1) * gd],
                      preferred_element_type=F32) for gi in range(SGU_GROUPS)]
        sv = jnp.concatenate(sv, axis=1) + bias
        gated.append((u[c * t:(c + 1) * t] * sv).astype(BF16))
    gated = jnp.concatenate(gated, axis=0) if len(gated) > 1 else gated[0]
    r = jnp.dot(gated, wout_ref[...], preferred_element_type=F32).reshape(nb, lt, d)
    o_ref[...] = x_ref[...] + (1.0 + gt_ref[...]) * r


def _sgu(x, g, sh, sc, gt, w_in, g_v, w_s, b_full, w_out, emit_v):
    b, l, d = x.shape
    t = w_s.shape[1]
    width = g_v.shape[1]
    nb, lt = _row_tiles(b, l, 256)
    assert lt % t == 0
    act, vec, par, _ = _tile_specs(b, l, d, nb, lt)
    out_specs = [act(d)]
    out_shape = [jax.ShapeDtypeStruct((b, l, d), F32)]
    if emit_v:
        out_specs.append(act(width))
        out_shape.append(jax.ShapeDtypeStruct((b, l, width), F32))
    return pl.pallas_call(
        functools.partial(_sgu_kernel, t=t, emit_v=emit_v),
        grid=(b // nb, l // lt),
        in_specs=[act(d), par((1, d)), vec(d), vec(d), vec(d), par(w_in.shape), par((1, width)),
                  par(w_s.shape), par(b_full.shape), par(w_out.shape)],
        out_specs=out_specs,
        out_shape=out_shape,
        compiler_params=_params("parallel", "parallel"),
        name="sgu",
    )(x, g, sh, sc, gt, w_in, g_v, w_s, b_full, w_out)


def _rope_angles(pos, half):
    inv_freq = ROPE_THETA ** (-jnp.arange(half, dtype=F32) / half)
    ang = pos.astype(F32)[:, None] * inv_freq[None, :]
    return jnp.cos(ang), jnp.sin(ang)


def _diff_rope_tables(pos):
    cos, sin = _rope_angles(pos, DIFF_DH // 2)
    reps = LANES // DIFF_DH
    return (jnp.tile(jnp.concatenate([cos, cos], axis=1), (1, reps)),
            jnp.tile(jnp.concatenate([-sin, sin], axis=1), (1, reps)))


def _mla_rope_tables(pos):
    cos, sin = _rope_angles(pos, MLA_ROPE // 2)
    n = pos.shape[0]
    ones = jnp.ones((n, MLA_NOPE), F32)
    zq = jnp.zeros((n, MLA_NOPE), F32)
    zp = jnp.zeros((n, MLA_HEAD_PAD - MLA_NOPE - MLA_ROPE), F32)
    q_cos = jnp.concatenate([ones, cos, cos, zp], axis=1)
    q_sin = jnp.concatenate([zq, -sin, sin, zp], axis=1)
    k_cos = jnp.concatenate([cos, cos], axis=1)
    k_sin = jnp.concatenate([sin, sin], axis=1)
    return q_cos, q_sin, k_cos, k_sin


def _prepare_weights(p):
    d = p['w_up'].shape[1]
    w = {}
    w['w_up'] = [p['w_up'][i].astype(BF16) for i in range(p['w_up'].shape[0])]
    w['w_down'] = [p['w_down'][i].astype(BF16) for i in range(p['w_down'].shape[0])]
    w['glu_a'] = p['s5_w_glu_a'].astype(BF16)
    w['glu_b'] = p['s5_w_glu_b'].astype(BF16)
    w['diff_qkv'] = p['diff_w_qkv'].astype(BF16)
    w['diff_o'] = p['diff_w_o'].astype(BF16)
    lam = jnp.stack([p['diff_lambda_q1'], p['diff_lambda_k1'], p['diff_lambda_q2'], p['diff_lambda_k2']])
    w['diff_lam'] = jnp.pad(lam.astype(F32), ((0, 4), (0, LANES - lam.shape[1])))
    w['diff_g_sub'] = p['diff_g_sub'].reshape(1, -1)

    w['mla_dq'] = p['mla_w_dq'].astype(BF16)
    kvr = p['mla_g_kv'].shape[0]
    w['mla_dc'] = p['mla_w_dkv'][:, :kvr].astype(BF16)
    wr = p['mla_w_dkv'][:, kvr:]
    hr = MLA_ROPE // 2
    w['mla_dr'] = wr.astype(BF16)
    w['mla_drr'] = jnp.concatenate([-wr[:, hr:], wr[:, :hr]], axis=1).astype(BF16)
    w['mla_g_q'] = p['mla_g_q'].reshape(1, -1)
    w['mla_g_kv'] = p['mla_g_kv'].reshape(1, -1)
    pad = MLA_HEAD_PAD - MLA_NOPE - MLA_ROPE
    qr = p['mla_w_uq'].shape[0]
    uq = p['mla_w_uq'].reshape(qr, MLA_HEADS, MLA_NOPE + MLA_ROPE)
    w['mla_uq'] = jnp.pad(uq, ((0, 0), (0, 0), (0, pad))).reshape(qr, -1).astype(BF16)
    uk = jnp.pad(p['mla_w_uk'], ((0, 0), (0, 0), (0, MLA_HEAD_PAD - MLA_NOPE)))
    w['mla_uk'] = uk.reshape(kvr, -1).astype(BF16)
    place = jnp.pad(jnp.eye(MLA_ROPE, dtype=F32), ((0, 0), (MLA_NOPE, pad)))
    w['mla_place'] = jnp.tile(place, (1, MLA_HEADS)).astype(BF16)
    w['mla_uv'] = p['mla_w_uv'].reshape(kvr, -1).astype(BF16)
    absorb = jnp.pad(p['mla_w_uk'].transpose(1, 2, 0), ((0, 0), (0, MLA_HEAD_PAD - MLA_NOPE), (0, 0)))
    select = jnp.pad(jnp.eye(MLA_ROPE, dtype=F32), ((MLA_NOPE, pad), (0, MLA_HEAD_PAD - MLA_ROPE)))
    w['mla_abs'] = jnp.concatenate(
        [absorb, jnp.broadcast_to(select, (MLA_HEADS,) + select.shape)], axis=2).astype(BF16)
    w['mla_uvt'] = p['mla_w_uv'].transpose(1, 2, 0).reshape(MLA_HEADS // 2, 2 * MLA_V, kvr).astype(BF16)
    w['mla_o'] = p['mla_w_o'].astype(BF16)

    w['sgu_in'] = p['sgu_w_in'].astype(BF16)
    w['sgu_g_v'] = p['sgu_g_v'].reshape(1, -1)
    w['sgu_out'] = p['sgu_w_out'].astype(BF16)
    return w


def _sgu_spatial(w_s, b_s, t, width):
    ws = jnp.tril(w_s[:, :t, :t]).astype(BF16)
    gd = width // SGU_GROUPS
    b_full = jnp.repeat(b_s[:, :t].T.astype(F32), gd, axis=1)
    return ws, b_full


def _run_trunk(x, mods, pos, past, p, w, s5_tables):
    b, l, d = x.shape
    new = {}
    depth = p['g_mix'].shape[0]
    for i in range(depth):
        sh1, sc1, gt1, sh2, sc2, gt2 = [mods[i][:, k][:, None, :] for k in range(6)]
        g_mix = p['g_mix'][i].reshape(1, d)
        kind = i % N_MIXERS
        if kind == 0:
            h = _modulate(x, g_mix, sh1, sc1)
            h0_re = None if past is None else past['s5_re']
            h0_im = None if past is None else past['s5_im']
            y, new['s5_re'], new['s5_im'] = _s5_scan(h, s5_tables, h0_re, h0_im)
            x = _s5_glu(y, h, x, p['s5_d'].reshape(1, d), gt1, w['glu_a'], w['glu_b'])
        elif kind == 1:
            cos, sin = _diff_rope_tables(pos)
            k, v, qb, kb, vb = _diff_qkv(x, g_mix, sh1, sc1, w['diff_qkv'], cos, sin)
            if past is None:
                o = _diff_attention(qb, kb, vb, w['diff_lam'], w['diff_g_sub'])
            else:
                o = _diff_attention_cached(qb, past['diff_k'], past['diff_v'], kb, vb,
                                           w['diff_lam'], w['diff_g_sub'])
            x = _proj_residual(o, w['diff_o'], x, gt1)
            new['diff_k'] = k.reshape(b, l, DIFF_HEADS, 2 * DIFF_DH)
            new['diff_v'] = v.reshape(b, l, DIFF_HEADS, 2 * DIFF_DH)
        elif kind == 2:
            q_cos, q_sin, k_cos, k_sin = _mla_rope_tables(pos)
            cq, ckv, krope = _mla_down(x, g_mix, sh1, sc1, w['mla_dq'], w['mla_dc'], w['mla_dr'],
                                       w['mla_drr'], w['mla_g_q'], w['mla_g_kv'], k_cos, k_sin)
            if past is None:
                qb = _mla_qup(cq, w['mla_uq'], q_cos, q_sin)
                kb, vb = _mla_kvup(ckv, krope, w['mla_uk'], w['mla_place'], w['mla_uv'])
                o = _mla_attention(qb, kb, vb)
            else:
                q_abs = _mla_qabs(cq, w['mla_uq'], q_cos, q_sin, w['mla_abs'])
                o = _mla_attention_cached(q_abs, past['mla_ckv'], past['mla_krope'], ckv, krope,
                                          w['mla_uvt'])
            x = _proj_residual(o, w['mla_o'], x, gt1)
            new['mla_ckv'], new['mla_krope'] = ckv, krope
        else:
            t = min(l, SGU_CHUNK)
            ws, b_full = _sgu_spatial(p['sgu_w_s'], p['sgu_b_s'], t, w['sgu_g_v'].shape[1])
            outs = _sgu(x, g_mix, sh1, sc1, gt1, w['sgu_in'], w['sgu_g_v'], ws, b_full, w['sgu_out'],
                        emit_v=past is not None)
            x = outs[0]
            if past is not None:
                new['sgu_v'] = outs[1]
        x = _mlp(x, p['g_ffn'][i].reshape(1, d), sh2, sc2, gt2, w['w_up'][i], w['w_down'][i],
                 p['g_final'].reshape(1, d), final_norm=(i == depth - 1))
    return x, new


def kernel(x_prompt, x_sample, c_prompt, c_sample, state_s5_re, state_s5_im, cache_diff_k, cache_diff_v, cache_mla_ckv, cache_mla_krope, w_ada, b_ada, g_mix, g_ffn, w_up, w_down, g_final, s5_a_re, s5_a_im, s5_b_re, s5_b_im, s5_c_re, s5_c_im, s5_d, s5_log_dt, s5_w_glu_a, s5_w_glu_b, diff_w_qkv, diff_lambda_q1, diff_lambda_k1, diff_lambda_q2, diff_lambda_k2, diff_g_sub, diff_w_o, mla_w_dq, mla_g_q, mla_w_uq, mla_w_dkv, mla_g_kv, mla_w_uk, mla_w_uv, mla_w_o, sgu_w_in, sgu_g_v, sgu_w_s, sgu_b_s, sgu_w_out):
    p = {
        'w_ada': w_ada, 'b_ada': b_ada, 'g_mix': g_mix, 'g_ffn': g_ffn,
        'w_up': w_up, 'w_down': w_down, 'g_final': g_final,
        's5_d': s5_d, 's5_w_glu_a': s5_w_glu_a, 's5_w_glu_b': s5_w_glu_b,
        'diff_w_qkv': diff_w_qkv, 'diff_lambda_q1': diff_lambda_q1, 'diff_lambda_k1': diff_lambda_k1,
        'diff_lambda_q2': diff_lambda_q2, 'diff_lambda_k2': diff_lambda_k2,
        'diff_g_sub': diff_g_sub, 'diff_w_o': diff_w_o,
        'mla_w_dq': mla_w_dq, 'mla_g_q': mla_g_q, 'mla_w_uq': mla_w_uq, 'mla_w_dkv': mla_w_dkv,
        'mla_g_kv': mla_g_kv, 'mla_w_uk': mla_w_uk, 'mla_w_uv': mla_w_uv, 'mla_w_o': mla_w_o,
        'sgu_w_in': sgu_w_in, 'sgu_g_v': sgu_g_v, 'sgu_w_s': sgu_w_s, 'sgu_b_s': sgu_b_s,
        'sgu_w_out': sgu_w_out,
    }
    past = {
        's5_re': state_s5_re, 's5_im': state_s5_im,
        'diff_k': cache_diff_k, 'diff_v': cache_diff_v,
        'mla_ckv': cache_mla_ckv, 'mla_krope': cache_mla_krope,
    }
    bp, lp, d = x_prompt.shape
    bs, ls, _ = x_sample.shape
    depth = w_ada.shape[0]

    c_all = jnp.concatenate([c_prompt, c_sample], axis=0)
    c_all = jnp.pad(c_all, ((0, (-c_all.shape[0]) % 8), (0, 0)))
    mod = _adaln(c_all, w_ada, b_ada).reshape(depth, c_all.shape[0], 6, d)
    mods_p = [mod[i, :bp] for i in range(depth)]
    mods_s = [mod[i, bp:bp + bs] for i in range(depth)]

    w = _prepare_weights(p)
    max_steps = max((lp // S5_SUB).bit_length() - 1, (ls // S5_SUB).bit_length() - 1, 1)
    s5_tables = _s5_tables(s5_a_re, s5_a_im, s5_b_re, s5_b_im, s5_c_re, s5_c_im, s5_log_dt, max_steps)

    pos_p = jnp.arange(lp, dtype=jnp.int32)
    pos_s = cache_diff_k.shape[1] + jnp.arange(ls, dtype=jnp.int32)
    y_prompt, sp = _run_trunk(x_prompt, mods_p, pos_p, None, p, w, s5_tables)
    y_sample, ss = _run_trunk(x_sample, mods_s, pos_s, past, p, w, s5_tables)
    return (y_prompt, y_sample,
            sp['s5_re'], sp['s5_im'], ss['s5_re'], ss['s5_im'],
            sp['diff_k'], sp['diff_v'], ss['diff_k'], ss['diff_v'],
            sp['mla_ckv'], sp['mla_krope'], ss['mla_ckv'], ss['mla_krope'],
            ss['sgu_v'])
```

```python
import functools
import math

import jax
import jax.numpy as jnp
from jax import lax
from jax.experimental import pallas as pl
from jax.experimental.pallas import tpu as pltpu

F32 = jnp.float32
BF16 = jnp.bfloat16
HIGHEST = lax.Precision.HIGHEST

NORM_EPS = 1e-6
ROPE_THETA = 10000.0
NEG_INF = -1e30
LOG2_E = math.log2(math.e)
CHUNK = 64
N_MIXERS = 4

S5_GROUP = 16
S5_SUB = 16

DIFF_HEADS = 8
DIFF_DH = 64
LAMBDA_INIT = 0.8 - 0.6 * math.exp(-0.3 * 1)

MLA_HEADS = 16
MLA_NOPE = 64
MLA_ROPE = 32
MLA_V = 64
MLA_SCALE = (MLA_NOPE + MLA_ROPE) ** -0.5
MLA_HEAD_PAD = 128

SGU_CHUNK = 128
SGU_GROUPS = 8

LANES = 128
ROW_TILE = 512
ATTN_TQ = 512
ATTN_STREAMS = 2
ATTN_TK = 256
VMEM_LIMIT_BYTES = 48 * 1024 * 1024


def _params(*sem):
    return pltpu.CompilerParams(dimension_semantics=sem, vmem_limit_bytes=VMEM_LIMIT_BYTES)


def _row_tiles(batch, length, rows=ROW_TILE):
    if length >= rows:
        assert length % rows == 0
        return 1, rows
    nb = max(1, min(batch, rows // length))
    while batch % nb:
        nb -= 1
    return nb, length


def _rms(x, g):
    return x * lax.rsqrt(jnp.mean(x * x, axis=-1, keepdims=True) + NORM_EPS) * g


def _modulated(x_ref, g_ref, sh_ref, sc_ref):
    return _rms(x_ref[...], g_ref[...]) * (1.0 + sc_ref[...]) + sh_ref[...]


def _rotate_pairs(x, half, period, lo):
    width = x.shape[-1]
    lane = lax.broadcasted_iota(jnp.int32, x.shape, x.ndim - 1) % period
    fwd = pltpu.roll(x, width - half, axis=x.ndim - 1)
    bwd = pltpu.roll(x, half, axis=x.ndim - 1)
    return jnp.where((lane >= lo) & (lane < lo + half), fwd, bwd)


def _adaln_kernel(c_ref, w_ref, b_ref, o_ref):
    c = c_ref[...]
    s = c * jax.nn.sigmoid(c)
    o_ref[0] = jnp.dot(s, w_ref[0], precision=HIGHEST, preferred_element_type=F32) + b_ref[0]


def _adaln(c_all, w_ada, b_ada):
    depth, d, n = w_ada.shape
    rows = c_all.shape[0]
    tn = n // 4
    return pl.pallas_call(
        _adaln_kernel,
        grid=(depth, n // tn),
        in_specs=[pl.BlockSpec((rows, d), lambda i, j: (0, 0)),
                  pl.BlockSpec((1, d, tn), lambda i, j: (i, 0, j)),
                  pl.BlockSpec((1, 1, tn), lambda i, j: (i, 0, j))],
        out_specs=pl.BlockSpec((1, rows, tn), lambda i, j: (i, 0, j)),
        out_shape=jax.ShapeDtypeStruct((depth, rows, n), F32),
        compiler_params=_params("parallel", "parallel"),
        name="adaln",
    )(c_all, w_ada, b_ada.reshape(depth, 1, n))


def _tile_specs(batch, length, d, nb, lt, extra_axes=0):
    if extra_axes == 0:
        act = lambda width: pl.BlockSpec((nb, lt, width), lambda i, j: (i, j, 0))
        vec = lambda width: pl.BlockSpec((nb, 1, width), lambda i, j: (i, 0, 0))
        par = lambda shape: pl.BlockSpec(shape, lambda i, j: (0,) * len(shape))
        pos = lambda width: pl.BlockSpec((lt, width), lambda i, j: (j, 0))
    else:
        act = lambda width: pl.BlockSpec((nb, lt, width), lambda i, j, f: (i, j, 0))
        vec = lambda width: pl.BlockSpec((nb, 1, width), lambda i, j, f: (i, 0, 0))
        par = lambda shape: pl.BlockSpec(shape, lambda i, j, f: (0,) * len(shape))
        pos = lambda width: pl.BlockSpec((lt, width), lambda i, j, f: (j, 0))
    return act, vec, par, pos


def _modulate_kernel(x_ref, g_ref, sh_ref, sc_ref, o_ref):
    o_ref[...] = _modulated(x_ref, g_ref, sh_ref, sc_ref)


def _modulate(x, g, sh, sc):
    b, l, d = x.shape
    nb, lt = _row_tiles(b, l)
    act, vec, par, _ = _tile_specs(b, l, d, nb, lt)
    return pl.pallas_call(
        _modulate_kernel,
        grid=(b // nb, l // lt),
        in_specs=[act(d), par((1, d)), vec(d), vec(d)],
        out_specs=act(d),
        out_shape=jax.ShapeDtypeStruct((b, l, d), F32),
        compiler_params=_params("parallel", "parallel"),
        name="modulate",
    )(x, g, sh, sc)


def _mlp_kernel(x_ref, g_ref, sh_ref, sc_ref, gt_ref, wu_ref, wd_ref, gf_ref, o_ref, h_s, acc_s,
                *, final_norm):
    f = pl.program_id(2)
    nb, lt, d = x_ref.shape

    @pl.when(f == 0)
    def _():
        h_s[...] = _modulated(x_ref, g_ref, sh_ref, sc_ref).reshape(nb * lt, d).astype(BF16)
        acc_s[...] = jnp.zeros_like(acc_s)

    a = jnp.dot(h_s[...], wu_ref[...], preferred_element_type=F32)
    a = jnp.square(jnp.maximum(a, 0.0)).astype(BF16)
    acc_s[...] += jnp.dot(a, wd_ref[...], preferred_element_type=F32)

    @pl.when(f == pl.num_programs(2) - 1)
    def _():
        y = x_ref[...] + (1.0 + gt_ref[...]) * acc_s[...].reshape(nb, lt, d)
        if final_norm:
            y = _rms(y, gf_ref[...])
        o_ref[...] = y


def _mlp(x, g, sh, sc, gt, w_up, w_down, g_final, final_norm, tf=2048):
    b, l, d = x.shape
    dff = w_up.shape[1]
    nb, lt = _row_tiles(b, l)
    act, vec, par, _ = _tile_specs(b, l, d, nb, lt, extra_axes=1)
    return pl.pallas_call(
        functools.partial(_mlp_kernel, final_norm=final_norm),
        grid=(b // nb, l // lt, dff // tf),
        in_specs=[act(d), par((1, d)), vec(d), vec(d), vec(d),
                  pl.BlockSpec((d, tf), lambda i, j, f: (0, f)),
                  pl.BlockSpec((tf, d), lambda i, j, f: (f, 0)),
                  par((1, d))],
        out_specs=act(d),
        out_shape=jax.ShapeDtypeStruct((b, l, d), F32),
        scratch_shapes=[pltpu.VMEM((nb * lt, d), BF16), pltpu.VMEM((nb * lt, d), F32)],
        compiler_params=_params("parallel", "parallel", "arbitrary"),
        name="mlp",
    )(x, g, sh, sc, gt, w_up, w_down, g_final)


def _proj_res_kernel(a_ref, w_ref, x_ref, gt_ref, o_ref):
    nb, lt, k = a_ref.shape
    d = w_ref.shape[1]
    r = jnp.dot(a_ref[...].reshape(nb * lt, k), w_ref[...], preferred_element_type=F32)
    o_ref[...] = x_ref[...] + (1.0 + gt_ref[...]) * r.reshape(nb, lt, d)


def _proj_residual(a, w, x, gt):
    b, l, d = x.shape
    k = a.shape[-1]
    nb, lt = _row_tiles(b, l)
    act, vec, par, _ = _tile_specs(b, l, d, nb, lt)
    return pl.pallas_call(
        _proj_res_kernel,
        grid=(b // nb, l // lt),
        in_specs=[act(k), par((k, d)), act(d), vec(d)],
        out_specs=act(d),
        out_shape=jax.ShapeDtypeStruct((b, l, d), F32),
        compiler_params=_params("parallel", "parallel"),
        name="proj_residual",
    )(a, w, x, gt)


def _s5_kernel(*refs, nch, batch, nsteps, has_h0):
    if has_h0:
        u_ref, bc_ref, m_ref, cc_ref, a1_ref, a2_ref, h0_ref, y_ref, hl_ref = refs
    else:
        u_ref, bc_ref, m_ref, cc_ref, a1_ref, a2_ref, y_ref, hl_ref = refs
    u = u_ref[0]
    rows = u.shape[0]
    x = jnp.dot(u, bc_ref[0], precision=HIGHEST, preferred_element_type=F32)
    half = x.shape[1] // 2
    if has_h0:
        h0 = h0_ref[0]
        x = x + a1_ref[0, 0:1, :] * h0 + a2_ref[0, 0:1, :] * pltpu.roll(h0, half, axis=1)
    kidx = lax.broadcasted_iota(jnp.int32, x.shape, 0) & (nch - 1)
    for j in range(nsteps):
        s = 1 << j
        xs = jnp.where(kidx >= s, pltpu.roll(x, s, axis=0), 0.0)
        x = x + a1_ref[0, j:j + 1, :] * xs + a2_ref[0, j:j + 1, :] * pltpu.roll(xs, half, axis=1)
    for b in range(batch):
        r = (b + 1) * nch - 1
        hl_ref[0, b:b + 1, :] = x[r:r + 1, :]
    hstart = jnp.where(kidx >= 1, pltpu.roll(x, 1, axis=0), 0.0)
    if has_h0:
        hstart = hstart + h0
    y = jnp.dot(u.astype(BF16), m_ref[0], preferred_element_type=F32)
    y = y + jnp.dot(hstart.astype(BF16), cc_ref[0], preferred_element_type=F32)
    y_ref[0] = y


def _s5_tables(a_re, a_im, b_re, b_im, c_re, c_im, log_dt, max_steps):
    g, p = a_re.shape
    sub = S5_SUB
    dt = jnp.exp(log_dt.astype(F32))[:, None]
    zr, zi = a_re.astype(F32) * dt, a_im.astype(F32) * dt

    def zpow(n):
        mag = jnp.exp(zr * n)
        return mag * jnp.cos(zi * n), mag * jnp.sin(zi * n)

    er = jnp.expm1(zr) * jnp.cos(zi) - 2.0 * jnp.square(jnp.sin(0.5 * zi))
    ei = jnp.exp(zr) * jnp.sin(zi)
    den = a_re * a_re + a_im * a_im
    fr = (er * a_re + ei * a_im) / den
    fi = (ei * a_re - er * a_im) / den
    bb_re = fr[..., None] * b_re - fi[..., None] * b_im
    bb_im = fr[..., None] * b_im + fi[..., None] * b_re

    lags = jnp.arange(sub + 1, dtype=F32)[:, None, None]
    pr, pi = zpow(lags)

    qr, qi = pr[:sub][::-1], pi[:sub][::-1]
    bcr = qr[..., None] * bb_re[None] - qi[..., None] * bb_im[None]
    bci = qr[..., None] * bb_im[None] + qi[..., None] * bb_re[None]
    bc = jnp.concatenate([bcr, bci], axis=2)
    bc = bc.transpose(1, 0, 3, 2).reshape(g, sub * S5_GROUP, 2 * p)

    ar, ai = pr[1:], pi[1:]
    ccr = c_re[None] * ar[:, :, None, :] - c_im[None] * ai[:, :, None, :]
    cci = c_re[None] * ai[:, :, None, :] + c_im[None] * ar[:, :, None, :]
    cc = jnp.concatenate([ccr, -cci], axis=3)
    cc = cc.transpose(1, 3, 0, 2).reshape(g, 2 * p, sub * S5_GROUP)

    kr = (jnp.einsum('gcp,lgp,gpd->lgcd', c_re, pr[:sub], bb_re, precision=HIGHEST)
          - jnp.einsum('gcp,lgp,gpd->lgcd', c_re, pi[:sub], bb_im, precision=HIGHEST)
          - jnp.einsum('gcp,lgp,gpd->lgcd', c_im, pr[:sub], bb_im, precision=HIGHEST)
          - jnp.einsum('gcp,lgp,gpd->lgcd', c_im, pi[:sub], bb_re, precision=HIGHEST))
    tt = jnp.arange(sub)
    lag = tt[None, :] - tt[:, None]
    toep = jnp.where((lag >= 0)[:, :, None, None, None],
                     kr[jnp.clip(lag, 0, sub - 1)], 0.0)
    m = toep.transpose(2, 0, 4, 1, 3).reshape(g, sub * S5_GROUP, sub * S5_GROUP)

    steps = (sub * (2 ** jnp.arange(max_steps))).astype(F32)[:, None, None]
    sr, si = zpow(steps)
    a1 = jnp.concatenate([sr, sr], axis=-1).transpose(1, 0, 2)
    a2 = jnp.concatenate([-si, si], axis=-1).transpose(1, 0, 2)
    pad = (-max_steps) % 8
    a1 = jnp.pad(a1, ((0, 0), (0, pad), (0, 0)))
    a2 = jnp.pad(a2, ((0, 0), (0, pad), (0, 0)))
    return bc, m.astype(BF16), cc.astype(BF16), a1, a2


def _s5_lane_kernel(h_ref, bct_ref, mt_ref, cct_ref, a1_ref, a2_ref, y_ref, hl_ref, ut_s, yt_s,
                    *, nch, nsteps):
    sub, c = S5_SUB, S5_GROUP
    groups = ut_s.shape[0]
    for t in range(sub):
        att = h_ref[0, pl.ds(t, nch, stride=sub), :].T
        for g in range(groups):
            ut_s[g, t * c:(t + 1) * c, :] = att[g * c:(g + 1) * c, :]
    half = bct_ref.shape[1] // 2
    lane = lax.broadcasted_iota(jnp.int32, (2 * half, nch), 1)
    for g in range(groups):
        ut = ut_s[g]
        x = jnp.dot(bct_ref[g], ut, precision=HIGHEST, preferred_element_type=F32)
        for j in range(nsteps):
            s = 1 << j
            xs = jnp.where(lane >= s, pltpu.roll(x, s, axis=1), 0.0)
            x = x + a1_ref[g, :, j:j + 1] * xs + a2_ref[g, :, j:j + 1] * pltpu.roll(xs, half, axis=0)
        hl_ref[0, g:g + 1, :] = x[:, nch - LANES:].T[LANES - 1:LANES, :]
        hstart = jnp.where(lane >= 1, pltpu.roll(x, 1, axis=1), 0.0)
        yt = jnp.dot(mt_ref[g], ut.astype(BF16), preferred_element_type=F32)
        yt = yt + jnp.dot(cct_ref[g], hstart.astype(BF16), preferred_element_type=F32)
        for t in range(sub):
            yt_s[t, g * c:(g + 1) * c, :] = yt[t * c:(t + 1) * c, :]
    for t in range(sub):
        y_ref[0, pl.ds(t, nch, stride=sub), :] = yt_s[t].T


def _s5_scan_lanes(h, tables):
    bc, m, cc, a1, a2 = tables
    b, l, d = h.shape
    g, k, p2 = bc.shape
    nch = l // S5_SUB
    gpb = LANES // S5_GROUP
    nsteps = nch.bit_length() - 1
    wspec = lambda shape: pl.BlockSpec((gpb,) + shape, lambda j, bi: (j, 0, 0))
    act = pl.BlockSpec((1, l, LANES), lambda j, bi: (bi, 0, j))
    return pl.pallas_call(
        functools.partial(_s5_lane_kernel, nch=nch, nsteps=nsteps),
        grid=(g // gpb, b),
        in_specs=[act, wspec((p2, k)), wspec((k, k)), wspec((k, p2)),
                  wspec((p2, a1.shape[1])), wspec((p2, a2.shape[1]))],
        out_specs=[act, pl.BlockSpec((1, gpb, p2), lambda j, bi: (bi, j, 0))],
        out_shape=[jax.ShapeDtypeStruct((b, l, d), F32), jax.ShapeDtypeStruct((b, g, p2), F32)],
        scratch_shapes=[pltpu.VMEM((gpb, k, nch), F32), pltpu.VMEM((S5_SUB, LANES, nch), F32)],
        compiler_params=_params("parallel", "parallel"),
        name="s5_scan_lanes",
    )(h, bc.transpose(0, 2, 1), m.transpose(0, 2, 1), cc.transpose(0, 2, 1),
      a1.transpose(0, 2, 1), a2.transpose(0, 2, 1))


def _s5_scan(h, tables, h0_re, h0_im):
    bc, m, cc, a1, a2 = tables
    b, l, d = h.shape
    if h0_re is None and (l // S5_SUB) % LANES == 0:
        y, hl = _s5_scan_lanes(h, tables)
        return y, hl[..., :hl.shape[-1] // 2], hl[..., hl.shape[-1] // 2:]
    g = d // S5_GROUP
    p2 = bc.shape[-1]
    nch = l // S5_SUB
    assert nch & (nch - 1) == 0
    nsteps = nch.bit_length() - 1
    rows = b * nch
    k = S5_SUB * S5_GROUP
    u = h.reshape(b, nch, S5_SUB, g, S5_GROUP).transpose(3, 0, 1, 2, 4).reshape(g, rows, k)
    has_h0 = h0_re is not None
    grp = lambda shape: pl.BlockSpec((1,) + shape, lambda i: (i, 0, 0))
    args = [u, bc, m, cc, a1, a2]
    in_specs = [grp((rows, k)), grp((k, p2)), grp((k, k)), grp((p2, k)),
                grp(a1.shape[1:]), grp(a2.shape[1:])]
    if has_h0:
        h0 = jnp.concatenate([h0_re, h0_im], axis=-1).astype(F32).transpose(1, 0, 2)
        h0 = jnp.pad(h0[:, :, None, :], ((0, 0), (0, 0), (0, nch - 1), (0, 0))).reshape(g, rows, p2)
        args.append(h0)
        in_specs.append(grp((rows, p2)))
    y, hl = pl.pallas_call(
        functools.partial(_s5_kernel, nch=nch, batch=b, nsteps=nsteps, has_h0=has_h0),
        grid=(g,),
        in_specs=in_specs,
        out_specs=[grp((rows, k)), grp((b, p2))],
        out_shape=[jax.ShapeDtypeStruct((g, rows, k), F32), jax.ShapeDtypeStruct((g, b, p2), F32)],
        compiler_params=_params("parallel"),
        name="s5_scan",
    )(*args)
    y = y.reshape(g, b, nch, S5_SUB, S5_GROUP).transpose(1, 2, 3, 0, 4).reshape(b, l, d)
    hl = hl.transpose(1, 0, 2)
    return y, hl[..., :p2 // 2], hl[..., p2 // 2:]


def _glu_kernel(y_ref, h_ref, x_ref, d_ref, gt_ref, wa_ref, wb_ref, o_ref):
    nb, lt, d = x_ref.shape
    z = jax.nn.gelu(y_ref[...] + d_ref[...] * h_ref[...]).reshape(nb * lt, d).astype(BF16)
    a = jnp.dot(z, wa_ref[...], preferred_element_type=F32)
    b = jnp.dot(z, wb_ref[...], preferred_element_type=F32)
    out = (a * jax.nn.sigmoid(b)).reshape(nb, lt, d)
    o_ref[...] = x_ref[...] + (1.0 + gt_ref[...]) * out


def _s5_glu(y, h, x, d_skip, gt, wa, wb):
    b, l, d = x.shape
    nb, lt = _row_tiles(b, l)
    act, vec, par, _ = _tile_specs(b, l, d, nb, lt)
    return pl.pallas_call(
        _glu_kernel,
        grid=(b // nb, l // lt),
        in_specs=[act(d), act(d), act(d), par((1, d)), vec(d), par((d, d)), par((d, d))],
        out_specs=act(d),
        out_shape=jax.ShapeDtypeStruct((b, l, d), F32),
        compiler_params=_params("parallel", "parallel"),
        name="s5_glu",
    )(y, h, x, d_skip, gt, wa, wb)


def _diff_qkv_kernel(x_ref, g_ref, sh_ref, sc_ref, w_ref, cos_ref, sin_ref,
                     k_ref, v_ref, qb_ref, kb_ref, vb_ref):
    nb, lt, d = x_ref.shape
    h = _modulated(x_ref, g_ref, sh_ref, sc_ref).reshape(nb * lt, d).astype(BF16)
    reps = d // cos_ref.shape[1]
    cos = jnp.tile(cos_ref[...], (1, reps))[None]
    sin = jnp.tile(sin_ref[...], (1, reps))[None]

    def roped(cols):
        t = jnp.dot(h, w_ref[:, cols * d:(cols + 1) * d], preferred_element_type=F32)
        r = _rotate_pairs(t, DIFF_DH // 2, DIFF_DH, 0)
        return t.reshape(nb, lt, d) * cos + r.reshape(nb, lt, d) * sin

    q = roped(0)
    qb_ref[...] = (q * (LOG2_E * DIFF_DH ** -0.5)).astype(BF16)
    k = roped(1)
    kb_ref[...] = k.astype(BF16)
    v = jnp.dot(h, w_ref[:, 2 * d:], preferred_element_type=F32).reshape(nb, lt, d)
    vb_ref[...] = v.astype(BF16)
    hw = k_ref.shape[2]
    heads = d // hw
    for hh in range(heads):
        k_ref[:, pl.ds(hh, lt, stride=heads), :] = k[:, :, hh * hw:(hh + 1) * hw]
        v_ref[:, pl.ds(hh, lt, stride=heads), :] = v[:, :, hh * hw:(hh + 1) * hw]


def _diff_qkv(x, g, sh, sc, w_qkv, cos, sin):
    b, l, d = x.shape
    nb, lt = _row_tiles(b, l)
    act, vec, par, pos = _tile_specs(b, l, d, nb, lt)
    hw = d // DIFF_HEADS
    return pl.pallas_call(
        _diff_qkv_kernel,
        grid=(b // nb, l // lt),
        in_specs=[act(d), par((1, d)), vec(d), vec(d), par((d, 3 * d)),
                  pos(cos.shape[1]), pos(sin.shape[1])],
        out_specs=[pl.BlockSpec((nb, lt * DIFF_HEADS, hw), lambda i, j: (i, j, 0))] * 2 + [act(d)] * 3,
        out_shape=[jax.ShapeDtypeStruct((b, l * DIFF_HEADS, hw), F32)] * 2
                  + [jax.ShapeDtypeStruct((b, l, d), BF16)] * 3,
        compiler_params=_params("parallel", "parallel"),
        name="diff_qkv",
    )(x, g, sh, sc, w_qkv, cos, sin)


_NT = (((1,), (1,)), ((), ()))
_TN = (((0,), (0,)), ((), ()))


def _softmax_update(m_s, l_s, acc_s, i, s, vb):
    m_prev = m_s[i]
    m_new = jnp.maximum(m_prev, jnp.max(s, axis=0, keepdims=True))
    alpha = jnp.exp2(m_prev - m_new)
    p = jnp.exp2(s - m_new)
    l_s[i] = alpha * l_s[i] + jnp.sum(p, axis=0, keepdims=True)
    acc_s[i] = alpha * acc_s[i] + lax.dot_general(vb, p.astype(BF16), _TN, preferred_element_type=F32)
    m_s[i] = m_new


def _softmax_init(m_s, l_s, acc_s):
    m_s[...] = jnp.full_like(m_s, NEG_INF)
    l_s[...] = jnp.zeros_like(l_s)
    acc_s[...] = jnp.zeros_like(acc_s)


def _softmax_result(l_s, acc_s, i):
    return acc_s[i] / l_s[i]


def _softmax_scratch(n, vw, queries):
    return [pltpu.VMEM((n, 1, queries), F32), pltpu.VMEM((n, 1, queries), F32),
            pltpu.VMEM((n, vw, queries), F32)]


def _attend_causal(scores, k_ref, v_ref, m_s, l_s, acc_s, s_a, s_b, *, tq, qi):
    assert len(scores) == 2 and tq % CHUNK == 0
    _softmax_init(m_s, l_s, acc_s)
    update = functools.partial(_softmax_update, m_s, l_s, acc_s)
    block = lambda ref, j: ref[0, pl.ds(pl.multiple_of(j * tq, tq), tq), :].astype(BF16)

    def qk(dst, j, streams):
        kb = block(k_ref, j)
        for i in streams:
            dst[i] = scores[i](kb)

    def pv(src, j, streams, masked=None):
        vb = block(v_ref, j)
        for i in streams:
            update(i, jnp.where(vis, src[i], NEG_INF) if i == masked else src[i], vb)

    def pair(jj, carry):
        qk(s_b, 2 * jj + 1, (0, 1))
        pv(s_a, 2 * jj, (0, 1))
        qk(s_a, 2 * jj + 2, (0, 1))
        pv(s_b, 2 * jj + 1, (0, 1))
        return carry

    qk(s_a, 0, (0, 1))
    lax.fori_loop(0, qi, pair, 0)
    kc = lax.broadcasted_iota(jnp.int32, (tq, tq), 0) // CHUNK
    qc = lax.broadcasted_iota(jnp.int32, (tq, tq), 1) // CHUNK
    vis = jnp.concatenate([kc <= qc] * 2, axis=1)
    qk(s_b, 2 * qi + 1, (1,))
    pv(s_a, 2 * qi, (0, 1), masked=0)
    pv(s_b, 2 * qi + 1, (1,), masked=1)
    return [_softmax_result(l_s, acc_s, i) for i in range(2)]


def _diff_attn_kernel(q_ref, k_ref, v_ref, lam_ref, gsub_ref, o_ref, m_s, l_s, acc_s, s_a, s_b, *, tq):
    def make_score(i):
        q2 = _diff_query_groups(q_ref[0, i * tq:(i + 1) * tq, :])
        return lambda kb: lax.dot_general(kb, q2, _NT, preferred_element_type=F32)

    outs = _attend_causal([make_score(0), make_score(1)], k_ref, v_ref, m_s, l_s, acc_s, s_a, s_b,
                          tq=tq, qi=pl.program_id(2))
    lam = _diff_lambda(lam_ref)
    for i, o2 in enumerate(outs):
        o = (o2[:, :tq] - lam * o2[:, tq:]).T
        o = _rms(o, gsub_ref[...]) * (1.0 - LAMBDA_INIT)
        o_ref[0, i * tq:(i + 1) * tq, :] = o.astype(o_ref.dtype)


def _causal_attention_call(kernel, qb, k, v, extra, qw, vw, name):
    b, l, _ = qb.shape
    tq, ns = ATTN_TQ, ATTN_STREAMS
    assert l % (ns * tq) == 0
    tile = lambda width: pl.BlockSpec((1, ns * tq, width), lambda bi, h, qi: (bi, qi, h))
    seq = lambda width: pl.BlockSpec((1, l, width), lambda bi, h, qi: (bi, 0, h))
    units = qb.shape[2] // qw
    return pl.pallas_call(
        functools.partial(kernel, tq=tq),
        grid=(b, units, l // (ns * tq)),
        in_specs=[tile(qw), seq(qw), seq(vw)]
                 + [pl.BlockSpec(a.shape, lambda bi, h, qi: (0, 0)) for a in extra],
        out_specs=tile(vw),
        out_shape=jax.ShapeDtypeStruct((b, l, units * vw), BF16),
        scratch_shapes=_softmax_scratch(ns, vw, 2 * tq) + [pltpu.VMEM((ns, tq, 2 * tq), F32)] * 2,
        compiler_params=_params("parallel", "parallel", "arbitrary"),
        name=name,
    )(qb, k, v, *extra)


def _diff_attention(qb, kb, vb, lam_rows, g_sub):
    hw = 2 * DIFF_DH
    return _causal_attention_call(_diff_attn_kernel, qb, kb, vb, [lam_rows, g_sub], hw, hw,
                                  "diff_attention")


def _diff_lambda(lam_ref):
    lp = lam_ref[...]
    return (jnp.exp(jnp.sum(lp[0:1] * lp[1:2], axis=-1, keepdims=True))
            - jnp.exp(jnp.sum(lp[2:3] * lp[3:4], axis=-1, keepdims=True)) + LAMBDA_INIT)


def _diff_query_groups(q):
    lane = lax.broadcasted_iota(jnp.int32, q.shape, 1)
    zero = jnp.zeros_like(q)
    return jnp.concatenate([jnp.where(lane < DIFF_DH, q, zero), jnp.where(lane >= DIFF_DH, q, zero)], axis=0)


def _diff_cached_kernel(q_ref, kc_ref, vc_ref, kn_ref, vn_ref, lam_ref, gsub_ref, o_ref,
                        m_s, l_s, acc_s, *, tk):
    c = pl.program_id(1)
    lq = q_ref.shape[1]
    hw = kc_ref.shape[2]
    heads = q_ref.shape[2] // hw
    update = functools.partial(_softmax_update, m_s, l_s, acc_s)
    q2 = lambda hh: _diff_query_groups(q_ref[0, :, hh * hw:(hh + 1) * hw])

    @pl.when(c == 0)
    def _():
        _softmax_init(m_s, l_s, acc_s)

    def block(j, carry):
        rows = pl.ds(pl.multiple_of(j * tk * heads, tk * heads), tk * heads)
        k_blk, v_blk = kc_ref.at[0, rows, :], vc_ref.at[0, rows, :]
        for hh in range(heads):
            kb = k_blk[pl.ds(hh, tk, stride=heads), :].astype(BF16)
            vb = v_blk[pl.ds(hh, tk, stride=heads), :].astype(BF16)
            update(hh, lax.dot_general(kb, q2(hh), _NT, preferred_element_type=F32), vb)
        return carry

    lax.fori_loop(0, kc_ref.shape[1] // (tk * heads), block, 0)

    @pl.when(c == pl.num_programs(1) - 1)
    def _():
        lam = _diff_lambda(lam_ref)
        for hh in range(heads):
            kb = kn_ref[0, :, hh * hw:(hh + 1) * hw]
            vb = vn_ref[0, :, hh * hw:(hh + 1) * hw]
            update(hh, lax.dot_general(kb, q2(hh), _NT, preferred_element_type=F32), vb)
            o2 = _softmax_result(l_s, acc_s, hh)
            o = (o2[:, :lq] - lam * o2[:, lq:]).T
            o = _rms(o, gsub_ref[...]) * (1.0 - LAMBDA_INIT)
            o_ref[0, :, hh * hw:(hh + 1) * hw] = o.astype(o_ref.dtype)


def _diff_attention_cached(qb, cache_k, cache_v, kb, vb, lam_rows, g_sub):
    b, l, d = qb.shape
    _, p, heads, hw = cache_k.shape
    pc = p if p <= 1024 else 1024
    tk = ATTN_TK if pc % ATTN_TK == 0 else CHUNK
    assert p % pc == 0 and pc % tk == 0
    new = pl.BlockSpec((1, l, d), lambda bi, c: (bi, 0, 0))
    cache = pl.BlockSpec((1, pc * heads, hw), lambda bi, c: (bi, c, 0))
    return pl.pallas_call(
        functools.partial(_diff_cached_kernel, tk=tk),
        grid=(b, p // pc),
        in_specs=[new, cache, cache, new, new,
                  pl.BlockSpec(lam_rows.shape, lambda bi, c: (0, 0)),
                  pl.BlockSpec((1, hw), lambda bi, c: (0, 0))],
        out_specs=new,
        out_shape=jax.ShapeDtypeStruct((b, l, d), BF16),
        scratch_shapes=_softmax_scratch(heads, hw, 2 * l),
        compiler_params=_params("parallel", "arbitrary"),
        name="diff_attention_cached",
    )(qb, cache_k.reshape(b, p * heads, hw), cache_v.reshape(b, p * heads, hw), kb, vb, lam_rows, g_sub)


def _mla_down_kernel(x_ref, g_ref, sh_ref, sc_ref, wq_ref, wc_ref, wr_ref, wrr_ref,
                     gq_ref, gkv_ref, cos_ref, sin_ref, cq_ref, ckv_ref, kr_ref):
    nb, lt, d = x_ref.shape
    h = _modulated(x_ref, g_ref, sh_ref, sc_ref).reshape(nb * lt, d).astype(BF16)
    dot = lambda w_ref: jnp.dot(h, w_ref[...], preferred_element_type=F32)
    cq_ref[...] = _rms(dot(wq_ref), gq_ref[...]).reshape(cq_ref.shape).astype(cq_ref.dtype)
    ckv_ref[...] = _rms(dot(wc_ref), gkv_ref[...]).reshape(ckv_ref.shape)
    r = dot(wr_ref).reshape(kr_ref.shape)
    rr = dot(wrr_ref).reshape(kr_ref.shape)
    kr_ref[...] = r * cos_ref[...][None] + rr * sin_ref[...][None]


def _mla_down(x, g, sh, sc, wq, wc, wr, wrr, g_q, g_kv, cos, sin):
    b, l, d = x.shape
    nb, lt = _row_tiles(b, l)
    act, vec, par, pos = _tile_specs(b, l, d, nb, lt)
    qr, kvr, rope = wq.shape[1], wc.shape[1], wr.shape[1]
    return pl.pallas_call(
        _mla_down_kernel,
        grid=(b // nb, l // lt),
        in_specs=[act(d), par((1, d)), vec(d), vec(d),
                  par((d, qr)), par((d, kvr)), par((d, rope)), par((d, rope)),
                  par((1, qr)), par((1, kvr)), pos(rope), pos(rope)],
        out_specs=[act(qr), act(kvr), act(rope)],
        out_shape=[jax.ShapeDtypeStruct((b, l, qr), BF16),
                   jax.ShapeDtypeStruct((b, l, kvr), F32),
                   jax.ShapeDtypeStruct((b, l, rope), F32)],
        compiler_params=_params("parallel", "parallel"),
        name="mla_down",
    )(x, g, sh, sc, wq, wc, wr, wrr, g_q, g_kv, cos, sin)


def _mla_qup_kernel(cq_ref, w_ref, cos_ref, sin_ref, q_ref):
    nb, lt, r = cq_ref.shape
    n = w_ref.shape[1]
    t = jnp.dot(cq_ref[...].reshape(nb * lt, r), w_ref[...], preferred_element_type=F32)
    rot = _rotate_pairs(t, MLA_ROPE // 2, MLA_HEAD_PAD, MLA_NOPE)
    reps = n // cos_ref.shape[1]
    cos = jnp.tile(cos_ref[...], (1, reps))[None]
    sin = jnp.tile(sin_ref[...], (1, reps))[None]
    q = t.reshape(nb, lt, n) * cos + rot.reshape(nb, lt, n) * sin
    q_ref[...] = (q * (LOG2_E * MLA_SCALE)).astype(q_ref.dtype)


def _mla_qup(cq, w_uq_pad, cos, sin):
    b, l, r = cq.shape
    n = w_uq_pad.shape[1]
    nb, lt = _row_tiles(b, l)
    act, vec, par, pos = _tile_specs(b, l, r, nb, lt)
    return pl.pallas_call(
        _mla_qup_kernel,
        grid=(b // nb, l // lt),
        in_specs=[act(r), par((r, n)), pos(cos.shape[1]), pos(sin.shape[1])],
        out_specs=act(n),
        out_shape=jax.ShapeDtypeStruct((b, l, n), BF16),
        compiler_params=_params("parallel", "parallel"),
        name="mla_q_up",
    )(cq, w_uq_pad, cos, sin)


def _mla_kvup_kernel(ckv_ref, kr_ref, wk_ref, we_ref, wv_ref, k_ref, v_ref):
    nb, lt, r = ckv_ref.shape
    c = ckv_ref[...].reshape(nb * lt, r).astype(BF16)
    kr = kr_ref[...].reshape(nb * lt, kr_ref.shape[2]).astype(BF16)
    k = (jnp.dot(c, wk_ref[...], preferred_element_type=F32)
         + jnp.dot(kr, we_ref[...], preferred_element_type=F32))
    k_ref[...] = k.reshape(k_ref.shape).astype(k_ref.dtype)
    v_ref[...] = jnp.dot(c, wv_ref[...], preferred_element_type=F32).reshape(v_ref.shape).astype(v_ref.dtype)


def _mla_kvup(ckv, krope, w_uk_pad, w_place, w_uv):
    b, l, r = ckv.shape
    rows = ROW_TILE if l % ROW_TILE == 0 else l
    nb, lt = _row_tiles(b, l, rows)
    act, vec, par, _ = _tile_specs(b, l, r, nb, lt)
    nk, nv = w_uk_pad.shape[1], w_uv.shape[1]
    rope = krope.shape[2]
    return pl.pallas_call(
        _mla_kvup_kernel,
        grid=(b // nb, l // lt),
        in_specs=[act(r), act(rope), par((r, nk)), par((rope, nk)), par((r, nv))],
        out_specs=[act(nk), act(nv)],
        out_shape=[jax.ShapeDtypeStruct((b, l, nk), BF16), jax.ShapeDtypeStruct((b, l, nv), BF16)],
        compiler_params=_params("parallel", "parallel"),
        name="mla_kv_up",
    )(ckv, krope, w_uk_pad, w_place, w_uv)


def _mla_attn_kernel(q_ref, k_ref, v_ref, o_ref, m_s, l_s, acc_s, s_a, s_b, *, tq):
    hp = MLA_HEAD_PAD

    def make_score(i):
        qa = q_ref[0, i * tq:(i + 1) * tq, :hp]
        qb = q_ref[0, i * tq:(i + 1) * tq, hp:]
        return lambda kb: jnp.concatenate(
            [lax.dot_general(kb[:, :hp], qa, _NT, preferred_element_type=F32),
             lax.dot_general(kb[:, hp:], qb, _NT, preferred_element_type=F32)], axis=1)

    outs = _attend_causal([make_score(0), make_score(1)], k_ref, v_ref, m_s, l_s, acc_s, s_a, s_b,
                          tq=tq, qi=pl.program_id(2))
    row = lax.broadcasted_iota(jnp.int32, (acc_s.shape[1], tq), 0)
    for i, o2 in enumerate(outs):
        o = jnp.where(row < MLA_V, o2[:, :tq], o2[:, tq:]).T
        o_ref[0, i * tq:(i + 1) * tq, :] = o.astype(o_ref.dtype)


def _mla_attention(qb, kb, vb):
    return _causal_attention_call(_mla_attn_kernel, qb, kb, vb, [], 2 * MLA_HEAD_PAD, 2 * MLA_V,
                                  "mla_attention")


def _mla_qabs_kernel(cq_ref, w_ref, cos_ref, sin_ref, wabs_ref, q_ref):
    nb, lt, r = cq_ref.shape
    n = w_ref.shape[1]
    t = jnp.dot(cq_ref[...].reshape(nb * lt, r), w_ref[...], preferred_element_type=F32)
    rot = _rotate_pairs(t, MLA_ROPE // 2, MLA_HEAD_PAD, MLA_NOPE)
    reps = n // cos_ref.shape[1]
    cos = jnp.tile(cos_ref[...], (1, reps))[None]
    sin = jnp.tile(sin_ref[...], (1, reps))[None]
    q = (t.reshape(nb, lt, n) * cos + rot.reshape(nb, lt, n) * sin) * (LOG2_E * MLA_SCALE)
    qb = q.reshape(nb * lt, n).astype(BF16)
    hp = MLA_HEAD_PAD
    for hh in range(n // hp):
        o = jnp.dot(qb[:, hh * hp:(hh + 1) * hp], wabs_ref[hh], preferred_element_type=F32)
        q_ref[:, hh] = o.reshape(nb, lt, o.shape[1]).astype(q_ref.dtype)


def _mla_qabs(cq, w_uq_pad, cos, sin, w_abs):
    b, l, r = cq.shape
    n = w_uq_pad.shape[1]
    heads, _, qw = w_abs.shape
    nb, lt = _row_tiles(b, l)
    act, vec, par, pos = _tile_specs(b, l, r, nb, lt)
    return pl.pallas_call(
        _mla_qabs_kernel,
        grid=(b // nb, l // lt),
        in_specs=[act(r), par((r, n)), pos(cos.shape[1]), pos(sin.shape[1]), par(w_abs.shape)],
        out_specs=pl.BlockSpec((nb, heads, lt, qw), lambda i, j: (i, 0, j, 0)),
        out_shape=jax.ShapeDtypeStruct((b, heads, l, qw), BF16),
        compiler_params=_params("parallel", "parallel"),
        name="mla_q_absorb",
    )(cq, w_uq_pad, cos, sin, w_abs)


def _mla_cached_kernel(q_ref, cc_ref, rc_ref, cn_ref, rn_ref, wuvt_ref, o_ref, m_s, l_s, acc_s, *, tk):
    heads, lq, qw = q_ref.shape[1:]
    q2 = q_ref[0].reshape(heads * lq, qw)

    def step(c, r):
        rpad = jnp.concatenate([r, jnp.zeros((r.shape[0], qw - c.shape[1] - r.shape[1]), r.dtype)], axis=1)
        kb = jnp.concatenate([c, rpad], axis=1).astype(BF16)
        s = lax.dot_general(kb, q2, _NT, preferred_element_type=F32)
        _softmax_update(m_s, l_s, acc_s, 0, s, c.astype(BF16))

    def block(j, carry):
        rows = pl.ds(pl.multiple_of(j * tk, tk), tk)
        step(cc_ref[0, rows, :], rc_ref[0, rows, :])
        return carry

    _softmax_init(m_s, l_s, acc_s)
    lax.fori_loop(0, cc_ref.shape[1] // tk, block, 0)
    step(cn_ref[0], rn_ref[0])
    lat = _softmax_result(l_s, acc_s, 0)
    assert 2 * lq == LANES and 2 * MLA_V == LANES
    lane = lax.broadcasted_iota(jnp.int32, (lq, LANES), 1)
    for pr in range(heads // 2):
        lp = lat[:, pr * LANES:(pr + 1) * LANES].astype(BF16)
        tt = jnp.dot(wuvt_ref[pr], lp, preferred_element_type=F32).T
        o_ref[0, :, pr * LANES:(pr + 1) * LANES] = jnp.where(lane < MLA_V, tt[:lq], tt[lq:]).astype(o_ref.dtype)


def _mla_attention_cached(q_abs, cache_ckv, cache_krope, ckv, krope, w_uvt):
    b, heads, l, qw = q_abs.shape
    p, rank = cache_ckv.shape[1:]
    rope = cache_krope.shape[2]
    tk = ATTN_TK if p % ATTN_TK == 0 else CHUNK
    assert p % tk == 0
    full = lambda shape: pl.BlockSpec((1,) + shape, lambda bi: (bi,) + (0,) * len(shape))
    return pl.pallas_call(
        functools.partial(_mla_cached_kernel, tk=tk),
        grid=(b,),
        in_specs=[full((heads, l, qw)), full((p, rank)), full((p, rope)), full((l, rank)), full((l, rope)),
                  pl.BlockSpec(w_uvt.shape, lambda bi: (0, 0, 0))],
        out_specs=full((l, heads * MLA_V)),
        out_shape=jax.ShapeDtypeStruct((b, l, heads * MLA_V), BF16),
        scratch_shapes=_softmax_scratch(1, rank, heads * l),
        compiler_params=_params("parallel"),
        name="mla_attention_cached",
    )(q_abs, cache_ckv, cache_krope, ckv, krope, w_uvt)


def _sgu_kernel(x_ref, g_ref, sh_ref, sc_ref, gt_ref, win_ref, gv_ref, ws_ref, bs_ref, wout_ref,
                *out_refs, t, emit_v):
    o_ref = out_refs[0]
    nb, lt, d = x_ref.shape
    rows = nb * lt
    width = gv_ref.shape[1]
    gd = width // SGU_GROUPS
    h = _modulated(x_ref, g_ref, sh_ref, sc_ref).reshape(rows, d).astype(BF16)
    u = jax.nn.gelu(jnp.dot(h, win_ref[:, :width], preferred_element_type=F32))
    v = jax.nn.gelu(jnp.dot(h, win_ref[:, width:], preferred_element_type=F32))
    v = _rms(v, gv_ref[...])
    if emit_v:
        out_refs[1][...] = v.reshape(nb, lt, width)
    vb = v.astype(BF16)
    bias = bs_ref[...]
    gated = []
    for c in range(rows // t):
        sv = [jnp.dot(ws_ref[gi], vb[c * t:(c + 1) * t, gi * gd:(gi + 1) * gd],
                      preferred_element_type=F32) for gi in range(SGU_GROUPS)]
        sv = jnp.concatenate(sv, axis=1) + bias
        gated.append((u[c * t:(c + 1) * t] * sv).astype(BF16))
    gated = jnp.concatenate(gated, axis=0) if len(gated) > 1 else gated[0]
    r = jnp.dot(gated, wout_ref[...], preferred_element_type=F32).reshape(nb, lt, d)
    o_ref[...] = x_ref[...] + (1.0 + gt_ref[...]) * r


def _sgu(x, g, sh, sc, gt, w_in, g_v, w_s, b_full, w_out, emit_v):
    b, l, d = x.shape
    t = w_s.shape[1]
    width = g_v.shape[1]
    nb, lt = _row_tiles(b, l, 256)
    assert lt % t == 0
    act, vec, par, _ = _tile_specs(b, l, d, nb, lt)
    out_specs = [act(d)]
    out_shape = [jax.ShapeDtypeStruct((b, l, d), F32)]
    if emit_v:
        out_specs.append(act(width))
        out_shape.append(jax.ShapeDtypeStruct((b, l, width), F32))
    return pl.pallas_call(
        functools.partial(_sgu_kernel, t=t, emit_v=emit_v),
        grid=(b // nb, l // lt),
        in_specs=[act(d), par((1, d)), vec(d), vec(d), vec(d), par(w_in.shape), par((1, width)),
                  par(w_s.shape), par(b_full.shape), par(w_out.shape)],
        out_specs=out_specs,
        out_shape=out_shape,
        compiler_params=_params("parallel", "parallel"),
        name="sgu",
    )(x, g, sh, sc, gt, w_in, g_v, w_s, b_full, w_out)


def _rope_angles(pos, half):
    inv_freq = ROPE_THETA ** (-jnp.arange(half, dtype=F32) / half)
    ang = pos.astype(F32)[:, None] * inv_freq[None, :]
    return jnp.cos(ang), jnp.sin(ang)


def _diff_rope_tables(pos):
    cos, sin = _rope_angles(pos, DIFF_DH // 2)
    reps = LANES // DIFF_DH
    return (jnp.tile(jnp.concatenate([cos, cos], axis=1), (1, reps)),
            jnp.tile(jnp.concatenate([-sin, sin], axis=1), (1, reps)))


def _mla_rope_tables(pos):
    cos, sin = _rope_angles(pos, MLA_ROPE // 2)
    n = pos.shape[0]
    ones = jnp.ones((n, MLA_NOPE), F32)
    zq = jnp.zeros((n, MLA_NOPE), F32)
    zp = jnp.zeros((n, MLA_HEAD_PAD - MLA_NOPE - MLA_ROPE), F32)
    q_cos = jnp.concatenate([ones, cos, cos, zp], axis=1)
    q_sin = jnp.concatenate([zq, -sin, sin, zp], axis=1)
    k_cos = jnp.concatenate([cos, cos], axis=1)
    k_sin = jnp.concatenate([sin, sin], axis=1)
    return q_cos, q_sin, k_cos, k_sin


def _prepare_weights(p):
    d = p['w_up'].shape[1]
    w = {}
    w['w_up'] = [p['w_up'][i].astype(BF16) for i in range(p['w_up'].shape[0])]
    w['w_down'] = [p['w_down'][i].astype(BF16) for i in range(p['w_down'].shape[0])]
    w['glu_a'] = p['s5_w_glu_a'].astype(BF16)
    w['glu_b'] = p['s5_w_glu_b'].astype(BF16)
    w['diff_qkv'] = p['diff_w_qkv'].astype(BF16)
    w['diff_o'] = p['diff_w_o'].astype(BF16)
    lam = jnp.stack([p['diff_lambda_q1'], p['diff_lambda_k1'], p['diff_lambda_q2'], p['diff_lambda_k2']])
    w['diff_lam'] = jnp.pad(lam.astype(F32), ((0, 4), (0, LANES - lam.shape[1])))
    w['diff_g_sub'] = p['diff_g_sub'].reshape(1, -1)

    w['mla_dq'] = p['mla_w_dq'].astype(BF16)
    kvr = p['mla_g_kv'].shape[0]
    w['mla_dc'] = p['mla_w_dkv'][:, :kvr].astype(BF16)
    wr = p['mla_w_dkv'][:, kvr:]
    hr = MLA_ROPE // 2
    w['mla_dr'] = wr.astype(BF16)
    w['mla_drr'] = jnp.concatenate([-wr[:, hr:], wr[:, :hr]], axis=1).astype(BF16)
    w['mla_g_q'] = p['mla_g_q'].reshape(1, -1)
    w['mla_g_kv'] = p['mla_g_kv'].reshape(1, -1)
    pad = MLA_HEAD_PAD - MLA_NOPE - MLA_ROPE
    qr = p['mla_w_uq'].shape[0]
    uq = p['mla_w_uq'].reshape(qr, MLA_HEADS, MLA_NOPE + MLA_ROPE)
    w['mla_uq'] = jnp.pad(uq, ((0, 0), (0, 0), (0, pad))).reshape(qr, -1).astype(BF16)
    uk = jnp.pad(p['mla_w_uk'], ((0, 0), (0, 0), (0, MLA_HEAD_PAD - MLA_NOPE)))
    w['mla_uk'] = uk.reshape(kvr, -1).astype(BF16)
    place = jnp.pad(jnp.eye(MLA_ROPE, dtype=F32), ((0, 0), (MLA_NOPE, pad)))
    w['mla_place'] = jnp.tile(place, (1, MLA_HEADS)).astype(BF16)
    w['mla_uv'] = p['mla_w_uv'].reshape(kvr, -1).astype(BF16)
    absorb = jnp.pad(p['mla_w_uk'].transpose(1, 2, 0), ((0, 0), (0, MLA_HEAD_PAD - MLA_NOPE), (0, 0)))
    select = jnp.pad(jnp.eye(MLA_ROPE, dtype=F32), ((MLA_NOPE, pad), (0, MLA_HEAD_PAD - MLA_ROPE)))
    w['mla_abs'] = jnp.concatenate(
        [absorb, jnp.broadcast_to(select, (MLA_HEADS,) + select.shape)], axis=2).astype(BF16)
    w['mla_uvt'] = p['mla_w_uv'].transpose(1, 2, 0).reshape(MLA_HEADS // 2, 2 * MLA_V, kvr).astype(BF16)
    w['mla_o'] = p['mla_w_o'].astype(BF16)

    w['sgu_in'] = p['sgu_w_in'].astype(BF16)
    w['sgu_g_v'] = p['sgu_g_v'].reshape(1, -1)
    w['sgu_out'] = p['sgu_w_out'].astype(BF16)
    return w


def _sgu_spatial(w_s, b_s, t, width):
    ws = jnp.tril(w_s[:, :t, :t]).astype(BF16)
    gd = width // SGU_GROUPS
    b_full = jnp.repeat(b_s[:, :t].T.astype(F32), gd, axis=1)
    return ws, b_full


def _run_trunk(x, mods, pos, past, p, w, s5_tables):
    b, l, d = x.shape
    new = {}
    depth = p['g_mix'].shape[0]
    for i in range(depth):
        sh1, sc1, gt1, sh2, sc2, gt2 = [mods[i][:, k][:, None, :] for k in range(6)]
        g_mix = p['g_mix'][i].reshape(1, d)
        kind = i % N_MIXERS
        if kind == 0:
            h = _modulate(x, g_mix, sh1, sc1)
            h0_re = None if past is None else past['s5_re']
            h0_im = None if past is None else past['s5_im']
            y, new['s5_re'], new['s5_im'] = _s5_scan(h, s5_tables, h0_re, h0_im)
            x = _s5_glu(y, h, x, p['s5_d'].reshape(1, d), gt1, w['glu_a'], w['glu_b'])
        elif kind == 1:
            cos, sin = _diff_rope_tables(pos)
            k, v, qb, kb, vb = _diff_qkv(x, g_mix, sh1, sc1, w['diff_qkv'], cos, sin)
            if past is None:
                o = _diff_attention(qb, kb, vb, w['diff_lam'], w['diff_g_sub'])
            else:
                o = _diff_attention_cached(qb, past['diff_k'], past['diff_v'], kb, vb,
                                           w['diff_lam'], w['diff_g_sub'])
            x = _proj_residual(o, w['diff_o'], x, gt1)
            new['diff_k'] = k.reshape(b, l, DIFF_HEADS, 2 * DIFF_DH)
            new['diff_v'] = v.reshape(b, l, DIFF_HEADS, 2 * DIFF_DH)
        elif kind == 2:
            q_cos, q_sin, k_cos, k_sin = _mla_rope_tables(pos)
            cq, ckv, krope = _mla_down(x, g_mix, sh1, sc1, w['mla_dq'], w['mla_dc'], w['mla_dr'],
                                       w['mla_drr'], w['mla_g_q'], w['mla_g_kv'], k_cos, k_sin)
            if past is None:
                qb = _mla_qup(cq, w['mla_uq'], q_cos, q_sin)
                kb, vb = _mla_kvup(ckv, krope, w['mla_uk'], w['mla_place'], w['mla_uv'])
                o = _mla_attention(qb, kb, vb)
            else:
                q_abs = _mla_qabs(cq, w['mla_uq'], q_cos, q_sin, w['mla_abs'])
                o = _mla_attention_cached(q_abs, past['mla_ckv'], past['mla_krope'], ckv, krope,
                                          w['mla_uvt'])
            x = _proj_residual(o, w['mla_o'], x, gt1)
            new['mla_ckv'], new['mla_krope'] = ckv, krope
        else:
            t = min(l, SGU_CHUNK)
            ws, b_full = _sgu_spatial(p['sgu_w_s'], p['sgu_b_s'], t, w['sgu_g_v'].shape[1])
            outs = _sgu(x, g_mix, sh1, sc1, gt1, w['sgu_in'], w['sgu_g_v'], ws, b_full, w['sgu_out'],
                        emit_v=past is not None)
            x = outs[0]
            if past is not None:
                new['sgu_v'] = outs[1]
        x = _mlp(x, p['g_ffn'][i].reshape(1, d), sh2, sc2, gt2, w['w_up'][i], w['w_down'][i],
                 p['g_final'].reshape(1, d), final_norm=(i == depth - 1))
    return x, new


def kernel(x_prompt, x_sample, c_prompt, c_sample, state_s5_re, state_s5_im, cache_diff_k, cache_diff_v, cache_mla_ckv, cache_mla_krope, w_ada, b_ada, g_mix, g_ffn, w_up, w_down, g_final, s5_a_re, s5_a_im, s5_b_re, s5_b_im, s5_c_re, s5_c_im, s5_d, s5_log_dt, s5_w_glu_a, s5_w_glu_b, diff_w_qkv, diff_lambda_q1, diff_lambda_k1, diff_lambda_q2, diff_lambda_k2, diff_g_sub, diff_w_o, mla_w_dq, mla_g_q, mla_w_uq, mla_w_dkv, mla_g_kv, mla_w_uk, mla_w_uv, mla_w_o, sgu_w_in, sgu_g_v, sgu_w_s, sgu_b_s, sgu_w_out):
    p = {
        'w_ada': w_ada, 'b_ada': b_ada, 'g_mix': g_mix, 'g_ffn': g_ffn,
        'w_up': w_up, 'w_down': w_down, 'g_final': g_final,
        's5_d': s5_d, 's5_w_glu_a': s5_w_glu_a, 's5_w_glu_b': s5_w_glu_b,
        'diff_w_qkv': diff_w_qkv, 'diff_lambda_q1': diff_lambda_q1, 'diff_lambda_k1': diff_lambda_k1,
        'diff_lambda_q2': diff_lambda_q2, 'diff_lambda_k2': diff_lambda_k2,
        'diff_g_sub': diff_g_sub, 'diff_w_o': diff_w_o,
        'mla_w_dq': mla_w_dq, 'mla_g_q': mla_g_q, 'mla_w_uq': mla_w_uq, 'mla_w_dkv': mla_w_dkv,
        'mla_g_kv': mla_g_kv, 'mla_w_uk': mla_w_uk, 'mla_w_uv': mla_w_uv, 'mla_w_o': mla_w_o,
        'sgu_w_in': sgu_w_in, 'sgu_g_v': sgu_g_v, 'sgu_w_s': sgu_w_s, 'sgu_b_s': sgu_b_s,
        'sgu_w_out': sgu_w_out,
    }
    past = {
        's5_re': state_s5_re, 's5_im': state_s5_im,
        'diff_k': cache_diff_k, 'diff_v': cache_diff_v,
        'mla_ckv': cache_mla_ckv, 'mla_krope': cache_mla_krope,
    }
    bp, lp, d = x_prompt.shape
    bs, ls, _ = x_sample.shape
    depth = w_ada.shape[0]

    c_all = jnp.concatenate([c_prompt, c_sample], axis=0)
    c_all = jnp.pad(c_all, ((0, (-c_all.shape[0]) % 8), (0, 0)))
    mod = _adaln(c_all, w_ada, b_ada).reshape(depth, c_all.shape[0], 6, d)
    mods_p = [mod[i, :bp] for i in range(depth)]
    mods_s = [mod[i, bp:bp + bs] for i in range(depth)]

    w = _prepare_weights(p)
    max_steps = max((lp // S5_SUB).bit_length() - 1, (ls // S5_SUB).bit_length() - 1, 1)
    s5_tables = _s5_tables(s5_a_re, s5_a_im, s5_b_re, s5_b_im, s5_c_re, s5_c_im, s5_log_dt, max_steps)

    pos_p = jnp.arange(lp, dtype=jnp.int32)
    pos_s = cache_diff_k.shape[1] + jnp.arange(ls, dtype=jnp.int32)
    y_prompt, sp = _run_trunk(x_prompt, mods_p, pos_p, None, p, w, s5_tables)
    y_sample, ss = _run_trunk(x_sample, mods_s, pos_s, past, p, w, s5_tables)
    return (y_prompt, y_sample,
            sp['s5_re'], sp['s5_im'], ss['s5_re'], ss['s5_im'],
            sp['diff_k'], sp['diff_v'], ss['diff_k'], ss['diff_v'],
            sp['mla_ckv'], sp['mla_krope'], ss['mla_ckv'], ss['mla_krope'],
            ss['sgu_v'])
```

```python
import functools
import math

import jax
import jax.numpy as jnp
from jax import lax
from jax.experimental import pallas as pl
from jax.experimental.pallas import tpu as pltpu

F32 = jnp.float32
BF16 = jnp.bfloat16
HIGHEST = lax.Precision.HIGHEST

NORM_EPS = 1e-6
ROPE_THETA = 10000.0
NEG_INF = -1e30
LOG2_E = math.log2(math.e)
CHUNK = 64
N_MIXERS = 4

S5_GROUP = 16
S5_SUB = 16

DIFF_HEADS = 8
DIFF_DH = 64
LAMBDA_INIT = 0.8 - 0.6 * math.exp(-0.3 * 1)

MLA_HEADS = 16
MLA_NOPE = 64
MLA_ROPE = 32
MLA_V = 64
MLA_SCALE = (MLA_NOPE + MLA_ROPE) ** -0.5
MLA_HEAD_PAD = 128

SGU_CHUNK = 128
SGU_GROUPS = 8

LANES = 128
ROW_TILE = 512
ATTN_TQ = 512
ATTN_STREAMS = 2
ATTN_TK = 256
VMEM_LIMIT_BYTES = 48 * 1024 * 1024


def _params(*sem):
    return pltpu.CompilerParams(dimension_semantics=sem, vmem_limit_bytes=VMEM_LIMIT_BYTES)


def _row_tiles(batch, length, rows=ROW_TILE):
    if length >= rows:
        assert length % rows == 0
        return 1, rows
    nb = max(1, min(batch, rows // length))
    while batch % nb:
        nb -= 1
    return nb, length


def _rms(x, g):
    return x * lax.rsqrt(jnp.mean(x * x, axis=-1, keepdims=True) + NORM_EPS) * g


def _modulated(x_ref, g_ref, sh_ref, sc_ref):
    return _rms(x_ref[...], g_ref[...]) * (1.0 + sc_ref[...]) + sh_ref[...]


def _rotate_pairs(x, half, period, lo):
    width = x.shape[-1]
    lane = lax.broadcasted_iota(jnp.int32, x.shape, x.ndim - 1) % period
    fwd = pltpu.roll(x, width - half, axis=x.ndim - 1)
    bwd = pltpu.roll(x, half, axis=x.ndim - 1)
    return jnp.where((lane >= lo) & (lane < lo + half), fwd, bwd)


def _adaln_kernel(c_ref, w_ref, b_ref, o_ref):
    c = c_ref[...]
    s = c * jax.nn.sigmoid(c)
    o_ref[0] = jnp.dot(s, w_ref[0], precision=HIGHEST, preferred_element_type=F32) + b_ref[0]


def _adaln(c_all, w_ada, b_ada):
    depth, d, n = w_ada.shape
    rows = c_all.shape[0]
    tn = n // 4
    return pl.pallas_call(
        _adaln_kernel,
        grid=(depth, n // tn),
        in_specs=[pl.BlockSpec((rows, d), lambda i, j: (0, 0)),
                  pl.BlockSpec((1, d, tn), lambda i, j: (i, 0, j)),
                  pl.BlockSpec((1, 1, tn), lambda i, j: (i, 0, j))],
        out_specs=pl.BlockSpec((1, rows, tn), lambda i, j: (i, 0, j)),
        out_shape=jax.ShapeDtypeStruct((depth, rows, n), F32),
        compiler_params=_params("parallel", "parallel"),
        name="adaln",
    )(c_all, w_ada, b_ada.reshape(depth, 1, n))


def _tile_specs(batch, length, d, nb, lt, extra_axes=0):
    if extra_axes == 0:
        act = lambda width: pl.BlockSpec((nb, lt, width), lambda i, j: (i, j, 0))
        vec = lambda width: pl.BlockSpec((nb, 1, width), lambda i, j: (i, 0, 0))
        par = lambda shape: pl.BlockSpec(shape, lambda i, j: (0,) * len(shape))
        pos = lambda width: pl.BlockSpec((lt, width), lambda i, j: (j, 0))
    else:
        act = lambda width: pl.BlockSpec((nb, lt, width), lambda i, j, f: (i, j, 0))
        vec = lambda width: pl.BlockSpec((nb, 1, width), lambda i, j, f: (i, 0, 0))
        par = lambda shape: pl.BlockSpec(shape, lambda i, j, f: (0,) * len(shape))
        pos = lambda width: pl.BlockSpec((lt, width), lambda i, j, f: (j, 0))
    return act, vec, par, pos


def _modulate_kernel(x_ref, g_ref, sh_ref, sc_ref, o_ref):
    o_ref[...] = _modulated(x_ref, g_ref, sh_ref, sc_ref)


def _modulate(x, g, sh, sc):
    b, l, d = x.shape
    nb, lt = _row_tiles(b, l)
    act, vec, par, _ = _tile_specs(b, l, d, nb, lt)
    return pl.pallas_call(
        _modulate_kernel,
        grid=(b // nb, l // lt),
        in_specs=[act(d), par((1, d)), vec(d), vec(d)],
        out_specs=act(d),
        out_shape=jax.ShapeDtypeStruct((b, l, d), F32),
        compiler_params=_params("parallel", "parallel"),
        name="modulate",
    )(x, g, sh, sc)


def _mlp_kernel(*refs, final_norm, tf, mixer_proj):
    if mixer_proj:
        o_in_ref, wo_ref, gt1_ref, *refs = refs
    x_ref, g_ref, sh_ref, sc_ref, gt_ref, wu_ref, wd_ref, gf_ref, o_ref = refs
    nb, lt, d = x_ref.shape
    x = x_ref[...]
    if mixer_proj:
        r = jnp.dot(o_in_ref[...].reshape(nb * lt, o_in_ref.shape[2]), wo_ref[...],
                    preferred_element_type=F32)
        x = x + (1.0 + gt1_ref[...]) * r.reshape(nb, lt, d)
    h = (_rms(x, g_ref[...]) * (1.0 + sc_ref[...]) + sh_ref[...]).reshape(nb * lt, d).astype(BF16)
    acc = None
    for f in range(wu_ref.shape[1] // tf):
        a = jnp.dot(h, wu_ref[:, f * tf:(f + 1) * tf], preferred_element_type=F32)
        a = jnp.square(jnp.maximum(a, 0.0)).astype(BF16)
        r = jnp.dot(a, wd_ref[f * tf:(f + 1) * tf, :], preferred_element_type=F32)
        acc = r if acc is None else acc + r
    y = x + (1.0 + gt_ref[...]) * acc.reshape(nb, lt, d)
    if final_norm:
        y = _rms(y, gf_ref[...])
    o_ref[...] = y


def _mlp(x, g, sh, sc, gt, w_up, w_down, g_final, final_norm, mixer=None, tf=2048):
    b, l, d = x.shape
    dff = w_up.shape[1]
    nb, lt = _row_tiles(b, l)
    act, vec, par, _ = _tile_specs(b, l, d, nb, lt)
    resident = lambda shape: pl.BlockSpec(shape, lambda i, j: (0, 0), pipeline_mode=pl.Buffered(1))
    args = [x, g, sh, sc, gt, w_up, w_down, g_final]
    in_specs = [act(d), par((1, d)), vec(d), vec(d), vec(d),
                resident((d, dff)), resident((dff, d)), par((1, d))]
    if mixer is not None:
        o, w_o, gt1 = mixer
        args = [o, w_o, gt1] + args
        in_specs = [act(o.shape[2]), resident(w_o.shape), vec(d)] + in_specs
    return pl.pallas_call(
        functools.partial(_mlp_kernel, final_norm=final_norm, tf=tf, mixer_proj=mixer is not None),
        grid=(b // nb, l // lt),
        in_specs=in_specs,
        out_specs=act(d),
        out_shape=jax.ShapeDtypeStruct((b, l, d), F32),
        compiler_params=_params("parallel", "parallel"),
        name="mlp",
    )(*args)


def _s5_kernel(*refs, nch, batch, nsteps, has_h0):
    if has_h0:
        u_ref, bc_ref, m_ref, cc_ref, a1_ref, a2_ref, h0_ref, y_ref, hl_ref = refs
    else:
        u_ref, bc_ref, m_ref, cc_ref, a1_ref, a2_ref, y_ref, hl_ref = refs
    u = u_ref[0]
    rows = u.shape[0]
    x = jnp.dot(u, bc_ref[0], precision=HIGHEST, preferred_element_type=F32)
    half = x.shape[1] // 2
    if has_h0:
        h0 = h0_ref[0]
        x = x + a1_ref[0, 0:1, :] * h0 + a2_ref[0, 0:1, :] * pltpu.roll(h0, half, axis=1)
    kidx = lax.broadcasted_iota(jnp.int32, x.shape, 0) & (nch - 1)
    for j in range(nsteps):
        s = 1 << j
        xs = jnp.where(kidx >= s, pltpu.roll(x, s, axis=0), 0.0)
        x = x + a1_ref[0, j:j + 1, :] * xs + a2_ref[0, j:j + 1, :] * pltpu.roll(xs, half, axis=1)
    for b in range(batch):
        r = (b + 1) * nch - 1
        hl_ref[0, b:b + 1, :] = x[r:r + 1, :]
    hstart = jnp.where(kidx >= 1, pltpu.roll(x, 1, axis=0), 0.0)
    if has_h0:
        hstart = hstart + h0
    y = jnp.dot(u.astype(BF16), m_ref[0], preferred_element_type=F32)
    y = y + jnp.dot(hstart.astype(BF16), cc_ref[0], preferred_element_type=F32)
    y_ref[0] = y


def _s5_tables(a_re, a_im, b_re, b_im, c_re, c_im, log_dt, max_steps):
    g, p = a_re.shape
    sub = S5_SUB
    dt = jnp.exp(log_dt.astype(F32))[:, None]
    zr, zi = a_re.astype(F32) * dt, a_im.astype(F32) * dt

    def zpow(n):
        mag = jnp.exp(zr * n)
        return mag * jnp.cos(zi * n), mag * jnp.sin(zi * n)

    er = jnp.expm1(zr) * jnp.cos(zi) - 2.0 * jnp.square(jnp.sin(0.5 * zi))
    ei = jnp.exp(zr) * jnp.sin(zi)
    den = a_re * a_re + a_im * a_im
    fr = (er * a_re + ei * a_im) / den
    fi = (ei * a_re - er * a_im) / den
    bb_re = fr[..., None] * b_re - fi[..., None] * b_im
    bb_im = fr[..., None] * b_im + fi[..., None] * b_re

    lags = jnp.arange(sub + 1, dtype=F32)[:, None, None]
    pr, pi = zpow(lags)

    qr, qi = pr[:sub][::-1], pi[:sub][::-1]
    bcr = qr[..., None] * bb_re[None] - qi[..., None] * bb_im[None]
    bci = qr[..., None] * bb_im[None] + qi[..., None] * bb_re[None]
    bc = jnp.concatenate([bcr, bci], axis=2)
    bc = bc.transpose(1, 0, 3, 2).reshape(g, sub * S5_GROUP, 2 * p)

    ar, ai = pr[1:], pi[1:]
    ccr = c_re[None] * ar[:, :, None, :] - c_im[None] * ai[:, :, None, :]
    cci = c_re[None] * ai[:, :, None, :] + c_im[None] * ar[:, :, None, :]
    cc = jnp.concatenate([ccr, -cci], axis=3)
    cc = cc.transpose(1, 3, 0, 2).reshape(g, 2 * p, sub * S5_GROUP)

    kr = (jnp.einsum('gcp,lgp,gpd->lgcd', c_re, pr[:sub], bb_re, precision=HIGHEST)
          - jnp.einsum('gcp,lgp,gpd->lgcd', c_re, pi[:sub], bb_im, precision=HIGHEST)
          - jnp.einsum('gcp,lgp,gpd->lgcd', c_im, pr[:sub], bb_im, precision=HIGHEST)
          - jnp.einsum('gcp,lgp,gpd->lgcd', c_im, pi[:sub], bb_re, precision=HIGHEST))
    tt = jnp.arange(sub)
    lag = tt[None, :] - tt[:, None]
    toep = jnp.where((lag >= 0)[:, :, None, None, None],
                     kr[jnp.clip(lag, 0, sub - 1)], 0.0)
    m = toep.transpose(2, 0, 4, 1, 3).reshape(g, sub * S5_GROUP, sub * S5_GROUP)

    steps = (sub * (2 ** jnp.arange(max_steps))).astype(F32)[:, None, None]
    sr, si = zpow(steps)
    a1 = jnp.concatenate([sr, sr], axis=-1).transpose(1, 0, 2)
    a2 = jnp.concatenate([-si, si], axis=-1).transpose(1, 0, 2)
    pad = (-max_steps) % 8
    a1 = jnp.pad(a1, ((0, 0), (0, pad), (0, 0)))
    a2 = jnp.pad(a2, ((0, 0), (0, pad), (0, 0)))
    return bc, m.astype(BF16), cc.astype(BF16), a1, a2


def _s5_lane_kernel(h_ref, bct_ref, mt_ref, cct_ref, a1_ref, a2_ref, y_ref, hl_ref, ut_s, yt_s,
                    *, nch, nsteps):
    sub, c = S5_SUB, S5_GROUP
    groups = ut_s.shape[0]
    for t in range(sub):
        att = h_ref[0, pl.ds(t, nch, stride=sub), :].T
        for g in range(groups):
            ut_s[g, t * c:(t + 1) * c, :] = att[g * c:(g + 1) * c, :]
    half = bct_ref.shape[1] // 2
    lane = lax.broadcasted_iota(jnp.int32, (2 * half, nch), 1)
    for g in range(groups):
        ut = ut_s[g]
        x = jnp.dot(bct_ref[g], ut, precision=HIGHEST, preferred_element_type=F32)
        for j in range(nsteps):
            s = 1 << j
            xs = jnp.where(lane >= s, pltpu.roll(x, s, axis=1), 0.0)
            x = x + a1_ref[g, :, j:j + 1] * xs + a2_ref[g, :, j:j + 1] * pltpu.roll(xs, half, axis=0)
        hl_ref[0, g:g + 1, :] = x[:, nch - LANES:].T[LANES - 1:LANES, :]
        hstart = jnp.where(lane >= 1, pltpu.roll(x, 1, axis=1), 0.0)
        yt = jnp.dot(mt_ref[g], ut.astype(BF16), preferred_element_type=F32)
        yt = yt + jnp.dot(cct_ref[g], hstart.astype(BF16), preferred_element_type=F32)
        for t in range(sub):
            yt_s[t, g * c:(g + 1) * c, :] = yt[t * c:(t + 1) * c, :]
    for t in range(sub):
        y_ref[0, pl.ds(t, nch, stride=sub), :] = yt_s[t].T


def _s5_scan_lanes(h, tables):
    bc, m, cc, a1, a2 = tables
    b, l, d = h.shape
    g, k, p2 = bc.shape
    nch = l // S5_SUB
    gpb = LANES // S5_GROUP
    nsteps = nch.bit_length() - 1
    wspec = lambda shape: pl.BlockSpec((gpb,) + shape, lambda j, bi: (j, 0, 0))
    act = pl.BlockSpec((1, l, LANES), lambda j, bi: (bi, 0, j))
    return pl.pallas_call(
        functools.partial(_s5_lane_kernel, nch=nch, nsteps=nsteps),
        grid=(g // gpb, b),
        in_specs=[act, wspec((p2, k)), wspec((k, k)), wspec((k, p2)),
                  wspec((p2, a1.shape[1])), wspec((p2, a2.shape[1]))],
        out_specs=[act, pl.BlockSpec((1, gpb, p2), lambda j, bi: (bi, j, 0))],
        out_shape=[jax.ShapeDtypeStruct((b, l, d), F32), jax.ShapeDtypeStruct((b, g, p2), F32)],
        scratch_shapes=[pltpu.VMEM((gpb, k, nch), F32), pltpu.VMEM((S5_SUB, LANES, nch), F32)],
        compiler_params=_params("parallel", "parallel"),
        name="s5_scan_lanes",
    )(h, bc.transpose(0, 2, 1), m.transpose(0, 2, 1), cc.transpose(0, 2, 1),
      a1.transpose(0, 2, 1), a2.transpose(0, 2, 1))


def _s5_scan(h, tables, h0_re, h0_im):
    bc, m, cc, a1, a2 = tables
    b, l, d = h.shape
    if h0_re is None and (l // S5_SUB) % LANES == 0:
        y, hl = _s5_scan_lanes(h, tables)
        return y, hl[..., :hl.shape[-1] // 2], hl[..., hl.shape[-1] // 2:]
    g = d // S5_GROUP
    p2 = bc.shape[-1]
    nch = l // S5_SUB
    assert nch & (nch - 1) == 0
    nsteps = nch.bit_length() - 1
    rows = b * nch
    k = S5_SUB * S5_GROUP
    u = h.reshape(b, nch, S5_SUB, g, S5_GROUP).transpose(3, 0, 1, 2, 4).reshape(g, rows, k)
    has_h0 = h0_re is not None
    grp = lambda shape: pl.BlockSpec((1,) + shape, lambda i: (i, 0, 0))
    args = [u, bc, m, cc, a1, a2]
    in_specs = [grp((rows, k)), grp((k, p2)), grp((k, k)), grp((p2, k)),
                grp(a1.shape[1:]), grp(a2.shape[1:])]
    if has_h0:
        h0 = jnp.concatenate([h0_re, h0_im], axis=-1).astype(F32).transpose(1, 0, 2)
        h0 = jnp.pad(h0[:, :, None, :], ((0, 0), (0, 0), (0, nch - 1), (0, 0))).reshape(g, rows, p2)
        args.append(h0)
        in_specs.append(grp((rows, p2)))
    y, hl = pl.pallas_call(
        functools.partial(_s5_kernel, nch=nch, batch=b, nsteps=nsteps, has_h0=has_h0),
        grid=(g,),
        in_specs=in_specs,
        out_specs=[grp((rows, k)), grp((b, p2))],
        out_shape=[jax.ShapeDtypeStruct((g, rows, k), F32), jax.ShapeDtypeStruct((g, b, p2), F32)],
        compiler_params=_params("parallel"),
        name="s5_scan",
    )(*args)
    y = y.reshape(g, b, nch, S5_SUB, S5_GROUP).transpose(1, 2, 3, 0, 4).reshape(b, l, d)
    hl = hl.transpose(1, 0, 2)
    return y, hl[..., :p2 // 2], hl[..., p2 // 2:]


def _glu_kernel(y_ref, h_ref, x_ref, d_ref, gt_ref, wa_ref, wb_ref, o_ref):
    nb, lt, d = x_ref.shape
    z = jax.nn.gelu(y_ref[...] + d_ref[...] * h_ref[...]).reshape(nb * lt, d).astype(BF16)
    a = jnp.dot(z, wa_ref[...], preferred_element_type=F32)
    b = jnp.dot(z, wb_ref[...], preferred_element_type=F32)
    out = (a * jax.nn.sigmoid(b)).reshape(nb, lt, d)
    o_ref[...] = x_ref[...] + (1.0 + gt_ref[...]) * out


def _s5_glu(y, h, x, d_skip, gt, wa, wb):
    b, l, d = x.shape
    nb, lt = _row_tiles(b, l)
    act, vec, par, _ = _tile_specs(b, l, d, nb, lt)
    return pl.pallas_call(
        _glu_kernel,
        grid=(b // nb, l // lt),
        in_specs=[act(d), act(d), act(d), par((1, d)), vec(d), par((d, d)), par((d, d))],
        out_specs=act(d),
        out_shape=jax.ShapeDtypeStruct((b, l, d), F32),
        compiler_params=_params("parallel", "parallel"),
        name="s5_glu",
    )(y, h, x, d_skip, gt, wa, wb)


def _diff_qkv_kernel(x_ref, g_ref, sh_ref, sc_ref, w_ref, cos_ref, sin_ref,
                     k_ref, v_ref, qb_ref, kb_ref, vb_ref):
    nb, lt, d = x_ref.shape
    h = _modulated(x_ref, g_ref, sh_ref, sc_ref).reshape(nb * lt, d).astype(BF16)
    reps = d // cos_ref.shape[1]
    cos = jnp.tile(cos_ref[...], (1, reps))[None]
    sin = jnp.tile(sin_ref[...], (1, reps))[None]

    def roped(cols):
        t = jnp.dot(h, w_ref[:, cols * d:(cols + 1) * d], preferred_element_type=F32)
        r = _rotate_pairs(t, DIFF_DH // 2, DIFF_DH, 0)
        return t.reshape(nb, lt, d) * cos + r.reshape(nb, lt, d) * sin

    q = roped(0)
    qb_ref[...] = (q * (LOG2_E * DIFF_DH ** -0.5)).astype(BF16)
    k = roped(1)
    kb_ref[...] = k.astype(BF16)
    v = jnp.dot(h, w_ref[:, 2 * d:], preferred_element_type=F32).reshape(nb, lt, d)
    vb_ref[...] = v.astype(BF16)
    hw = k_ref.shape[2]
    heads = d // hw
    for hh in range(heads):
        k_ref[:, pl.ds(hh, lt, stride=heads), :] = k[:, :, hh * hw:(hh + 1) * hw]
        v_ref[:, pl.ds(hh, lt, stride=heads), :] = v[:, :, hh * hw:(hh + 1) * hw]


def _diff_qkv(x, g, sh, sc, w_qkv, cos, sin):
    b, l, d = x.shape
    nb, lt = _row_tiles(b, l)
    act, vec, par, pos = _tile_specs(b, l, d, nb, lt)
    hw = d // DIFF_HEADS
    return pl.pallas_call(
        _diff_qkv_kernel,
        grid=(b // nb, l // lt),
        in_specs=[act(d), par((1, d)), vec(d), vec(d), par((d, 3 * d)),
                  pos(cos.shape[1]), pos(sin.shape[1])],
        out_specs=[pl.BlockSpec((nb, lt * DIFF_HEADS, hw), lambda i, j: (i, j, 0))] * 2 + [act(d)] * 3,
        out_shape=[jax.ShapeDtypeStruct((b, l * DIFF_HEADS, hw), F32)] * 2
                  + [jax.ShapeDtypeStruct((b, l, d), BF16)] * 3,
        compiler_params=_params("parallel", "parallel"),
        name="diff_qkv",
    )(x, g, sh, sc, w_qkv, cos, sin)


_NT = (((1,), (1,)), ((), ()))
_TN = (((0,), (0,)), ((), ()))


def _softmax_update(m_s, l_s, acc_s, i, s, vb):
    m_prev = m_s[i]
    m_new = jnp.maximum(m_prev, jnp.max(s, axis=0, keepdims=True))
    alpha = jnp.exp2(m_prev - m_new)
    p = jnp.exp2(s - m_new)
    l_s[i] = alpha * l_s[i] + jnp.sum(p, axis=0, keepdims=True)
    acc_s[i] = alpha * acc_s[i] + lax.dot_general(vb, p.astype(BF16), _TN, preferred_element_type=F32)
    m_s[i] = m_new


def _softmax_init(m_s, l_s, acc_s):
    m_s[...] = jnp.full_like(m_s, NEG_INF)
    l_s[...] = jnp.zeros_like(l_s)
    acc_s[...] = jnp.zeros_like(acc_s)


def _softmax_result(l_s, acc_s, i):
    return acc_s[i] / l_s[i]


def _softmax_scratch(n, vw, queries):
    return [pltpu.VMEM((n, 1, queries), F32), pltpu.VMEM((n, 1, queries), F32),
            pltpu.VMEM((n, vw, queries), F32)]


def _attend_causal(scores, k_ref, v_ref, m_s, l_s, acc_s, s_a, s_b, *, tq, qi):
    assert len(scores) == 2 and tq % CHUNK == 0
    _softmax_init(m_s, l_s, acc_s)
    update = functools.partial(_softmax_update, m_s, l_s, acc_s)
    block = lambda ref, j: ref[0, pl.ds(pl.multiple_of(j * tq, tq), tq), :].astype(BF16)

    def qk(dst, j, streams):
        kb = block(k_ref, j)
        for i in streams:
            dst[i] = scores[i](kb)

    def pv(src, j, streams, masked=None):
        vb = block(v_ref, j)
        for i in streams:
            update(i, jnp.where(vis, src[i], NEG_INF) if i == masked else src[i], vb)

    def pair(jj, carry):
        qk(s_b, 2 * jj + 1, (0, 1))
        pv(s_a, 2 * jj, (0, 1))
        qk(s_a, 2 * jj + 2, (0, 1))
        pv(s_b, 2 * jj + 1, (0, 1))
        return carry

    qk(s_a, 0, (0, 1))
    lax.fori_loop(0, qi, pair, 0)
    kc = lax.broadcasted_iota(jnp.int32, (tq, tq), 0) // CHUNK
    qc = lax.broadcasted_iota(jnp.int32, (tq, tq), 1) // CHUNK
    vis = jnp.concatenate([kc <= qc] * 2, axis=1)
    qk(s_b, 2 * qi + 1, (1,))
    pv(s_a, 2 * qi, (0, 1), masked=0)
    pv(s_b, 2 * qi + 1, (1,), masked=1)
    return [_softmax_result(l_s, acc_s, i) for i in range(2)]


def _diff_attn_kernel(q_ref, k_ref, v_ref, lam_ref, gsub_ref, o_ref, m_s, l_s, acc_s, s_a, s_b, *, tq):
    def make_score(i):
        q2 = _diff_query_groups(q_ref[0, i * tq:(i + 1) * tq, :])
        return lambda kb: lax.dot_general(kb, q2, _NT, preferred_element_type=F32)

    outs = _attend_causal([make_score(0), make_score(1)], k_ref, v_ref, m_s, l_s, acc_s, s_a, s_b,
                          tq=tq, qi=pl.program_id(2))
    lam = _diff_lambda(lam_ref)
    for i, o2 in enumerate(outs):
        o = (o2[:, :tq] - lam * o2[:, tq:]).T
        o = _rms(o, gsub_ref[...]) * (1.0 - LAMBDA_INIT)
        o_ref[0, i * tq:(i + 1) * tq, :] = o.astype(o_ref.dtype)


def _causal_attention_call(kernel, qb, k, v, extra, qw, vw, name):
    b, l, _ = qb.shape
    tq, ns = ATTN_TQ, ATTN_STREAMS
    assert l % (ns * tq) == 0
    tile = lambda width: pl.BlockSpec((1, ns * tq, width), lambda bi, h, qi: (bi, qi, h))
    seq = lambda width: pl.BlockSpec((1, l, width), lambda bi, h, qi: (bi, 0, h))
    units = qb.shape[2] // qw
    return pl.pallas_call(
        functools.partial(kernel, tq=tq),
        grid=(b, units, l // (ns * tq)),
        in_specs=[tile(qw), seq(qw), seq(vw)]
                 + [pl.BlockSpec(a.shape, lambda bi, h, qi: (0, 0)) for a in extra],
        out_specs=tile(vw),
        out_shape=jax.ShapeDtypeStruct((b, l, units * vw), BF16),
        scratch_shapes=_softmax_scratch(ns, vw, 2 * tq) + [pltpu.VMEM((ns, tq, 2 * tq), F32)] * 2,
        compiler_params=_params("parallel", "parallel", "arbitrary"),
        name=name,
    )(qb, k, v, *extra)


def _diff_attention(qb, kb, vb, lam_rows, g_sub):
    hw = 2 * DIFF_DH
    return _causal_attention_call(_diff_attn_kernel, qb, kb, vb, [lam_rows, g_sub], hw, hw,
                                  "diff_attention")


def _diff_lambda(lam_ref):
    lp = lam_ref[...]
    return (jnp.exp(jnp.sum(lp[0:1] * lp[1:2], axis=-1, keepdims=True))
            - jnp.exp(jnp.sum(lp[2:3] * lp[3:4], axis=-1, keepdims=True)) + LAMBDA_INIT)


def _diff_query_groups(q):
    lane = lax.broadcasted_iota(jnp.int32, q.shape, 1)
    zero = jnp.zeros_like(q)
    return jnp.concatenate([jnp.where(lane < DIFF_DH, q, zero), jnp.where(lane >= DIFF_DH, q, zero)], axis=0)


def _diff_cached_kernel(q_ref, kc_ref, vc_ref, kn_ref, vn_ref, lam_ref, gsub_ref, o_ref,
                        m_s, l_s, acc_s, *, tk):
    c = pl.program_id(1)
    lq = q_ref.shape[1]
    hw = kc_ref.shape[2]
    heads = q_ref.shape[2] // hw
    update = functools.partial(_softmax_update, m_s, l_s, acc_s)
    q2 = lambda hh: _diff_query_groups(q_ref[0, :, hh * hw:(hh + 1) * hw])

    @pl.when(c == 0)
    def _():
        _softmax_init(m_s, l_s, acc_s)

    def block(j, carry):
        rows = pl.ds(pl.multiple_of(j * tk * heads, tk * heads), tk * heads)
        k_blk, v_blk = kc_ref.at[0, rows, :], vc_ref.at[0, rows, :]
        for hh in range(heads):
            kb = k_blk[pl.ds(hh, tk, stride=heads), :].astype(BF16)
            vb = v_blk[pl.ds(hh, tk, stride=heads), :].astype(BF16)
            update(hh, lax.dot_general(kb, q2(hh), _NT, preferred_element_type=F32), vb)
        return carry

    lax.fori_loop(0, kc_ref.shape[1] // (tk * heads), block, 0)

    @pl.when(c == pl.num_programs(1) - 1)
    def _():
        lam = _diff_lambda(lam_ref)
        for hh in range(heads):
            kb = kn_ref[0, :, hh * hw:(hh + 1) * hw]
            vb = vn_ref[0, :, hh * hw:(hh + 1) * hw]
            update(hh, lax.dot_general(kb, q2(hh), _NT, preferred_element_type=F32), vb)
            o2 = _softmax_result(l_s, acc_s, hh)
            o = (o2[:, :lq] - lam * o2[:, lq:]).T
            o = _rms(o, gsub_ref[...]) * (1.0 - LAMBDA_INIT)
            o_ref[0, :, hh * hw:(hh + 1) * hw] = o.astype(o_ref.dtype)


def _diff_attention_cached(qb, cache_k, cache_v, kb, vb, lam_rows, g_sub):
    b, l, d = qb.shape
    _, p, heads, hw = cache_k.shape
    pc = p if p <= 1024 else 1024
    tk = ATTN_TK if pc % ATTN_TK == 0 else CHUNK
    assert p % pc == 0 and pc % tk == 0
    new = pl.BlockSpec((1, l, d), lambda bi, c: (bi, 0, 0))
    cache = pl.BlockSpec((1, pc * heads, hw), lambda bi, c: (bi, c, 0))
    return pl.pallas_call(
        functools.partial(_diff_cached_kernel, tk=tk),
        grid=(b, p // pc),
        in_specs=[new, cache, cache, new, new,
                  pl.BlockSpec(lam_rows.shape, lambda bi, c: (0, 0)),
                  pl.BlockSpec((1, hw), lambda bi, c: (0, 0))],
        out_specs=new,
        out_shape=jax.ShapeDtypeStruct((b, l, d), BF16),
        scratch_shapes=_softmax_scratch(heads, hw, 2 * l),
        compiler_params=_params("parallel", "arbitrary"),
        name="diff_attention_cached",
    )(qb, cache_k.reshape(b, p * heads, hw), cache_v.reshape(b, p * heads, hw), kb, vb, lam_rows, g_sub)


def _mla_down_kernel(x_ref, g_ref, sh_ref, sc_ref, wq_ref, wc_ref, wr_ref, wrr_ref,
                     gq_ref, gkv_ref, cos_ref, sin_ref, cq_ref, ckv_ref, kr_ref):
    nb, lt, d = x_ref.shape
    h = _modulated(x_ref, g_ref, sh_ref, sc_ref).reshape(nb * lt, d).astype(BF16)
    dot = lambda w_ref: jnp.dot(h, w_ref[...], preferred_element_type=F32)
    cq_ref[...] = _rms(dot(wq_ref), gq_ref[...]).reshape(cq_ref.shape).astype(cq_ref.dtype)
    ckv_ref[...] = _rms(dot(wc_ref), gkv_ref[...]).reshape(ckv_ref.shape)
    r = dot(wr_ref).reshape(kr_ref.shape)
    rr = dot(wrr_ref).reshape(kr_ref.shape)
    kr_ref[...] = r * cos_ref[...][None] + rr * sin_ref[...][None]


def _mla_down(x, g, sh, sc, wq, wc, wr, wrr, g_q, g_kv, cos, sin):
    b, l, d = x.shape
    nb, lt = _row_tiles(b, l)
    act, vec, par, pos = _tile_specs(b, l, d, nb, lt)
    qr, kvr, rope = wq.shape[1], wc.shape[1], wr.shape[1]
    return pl.pallas_call(
        _mla_down_kernel,
        grid=(b // nb, l // lt),
        in_specs=[act(d), par((1, d)), vec(d), vec(d),
                  par((d, qr)), par((d, kvr)), par((d, rope)), par((d, rope)),
                  par((1, qr)), par((1, kvr)), pos(rope), pos(rope)],
        out_specs=[act(qr), act(kvr), act(rope)],
        out_shape=[jax.ShapeDtypeStruct((b, l, qr), BF16),
                   jax.ShapeDtypeStruct((b, l, kvr), F32),
                   jax.ShapeDtypeStruct((b, l, rope), F32)],
        compiler_params=_params("parallel", "parallel"),
        name="mla_down",
    )(x, g, sh, sc, wq, wc, wr, wrr, g_q, g_kv, cos, sin)


def _mla_roped_q(cq_ref, w_ref, wrot_ref, cos_ref, sin_ref):
    nb, lt, r = cq_ref.shape
    n = w_ref.shape[1]
    cq = cq_ref[...].reshape(nb * lt, r)
    t = jnp.dot(cq, w_ref[...], preferred_element_type=F32).reshape(nb, lt, n)
    t_rot = jnp.dot(cq, wrot_ref[...], preferred_element_type=F32).reshape(nb, lt, n)
    reps = n // cos_ref.shape[1]
    cos = jnp.tile(cos_ref[...], (1, reps))[None]
    sin = jnp.tile(sin_ref[...], (1, reps))[None]
    return t * cos + t_rot * sin


def _mla_qup_kernel(cq_ref, w_ref, wrot_ref, cos_ref, sin_ref, q_ref):
    q_ref[...] = _mla_roped_q(cq_ref, w_ref, wrot_ref, cos_ref, sin_ref).astype(q_ref.dtype)


def _mla_qup(cq, w_uq_pad, w_uq_rot, cos, sin):
    b, l, r = cq.shape
    n = w_uq_pad.shape[1]
    nb, lt = _row_tiles(b, l)
    act, vec, par, pos = _tile_specs(b, l, r, nb, lt)
    return pl.pallas_call(
        _mla_qup_kernel,
        grid=(b // nb, l // lt),
        in_specs=[act(r), par((r, n)), par((r, n)), pos(cos.shape[1]), pos(sin.shape[1])],
        out_specs=act(n),
        out_shape=jax.ShapeDtypeStruct((b, l, n), BF16),
        compiler_params=_params("parallel", "parallel"),
        name="mla_q_up",
    )(cq, w_uq_pad, w_uq_rot, cos, sin)


def _mla_kvup_kernel(ckv_ref, kr_ref, wk_ref, we_ref, wv_ref, k_ref, v_ref):
    nb, lt, r = ckv_ref.shape
    c = ckv_ref[...].reshape(nb * lt, r).astype(BF16)
    kr = kr_ref[...].reshape(nb * lt, kr_ref.shape[2]).astype(BF16)
    k = (jnp.dot(c, wk_ref[...], preferred_element_type=F32)
         + jnp.dot(kr, we_ref[...], preferred_element_type=F32))
    k_ref[...] = k.reshape(k_ref.shape).astype(k_ref.dtype)
    v_ref[...] = jnp.dot(c, wv_ref[...], preferred_element_type=F32).reshape(v_ref.shape).astype(v_ref.dtype)


def _mla_kvup(ckv, krope, w_uk_pad, w_place, w_uv):
    b, l, r = ckv.shape
    rows = ROW_TILE if l % ROW_TILE == 0 else l
    nb, lt = _row_tiles(b, l, rows)
    act, vec, par, _ = _tile_specs(b, l, r, nb, lt)
    nk, nv = w_uk_pad.shape[1], w_uv.shape[1]
    rope = krope.shape[2]
    return pl.pallas_call(
        _mla_kvup_kernel,
        grid=(b // nb, l // lt),
        in_specs=[act(r), act(rope), par((r, nk)), par((rope, nk)), par((r, nv))],
        out_specs=[act(nk), act(nv)],
        out_shape=[jax.ShapeDtypeStruct((b, l, nk), BF16), jax.ShapeDtypeStruct((b, l, nv), BF16)],
        compiler_params=_params("parallel", "parallel"),
        name="mla_kv_up",
    )(ckv, krope, w_uk_pad, w_place, w_uv)


def _mla_attn_kernel(q_ref, k_ref, v_ref, o_ref, m_s, l_s, acc_s, s_a, s_b, *, tq):
    hp = MLA_HEAD_PAD

    def make_score(i):
        qa = q_ref[0, i * tq:(i + 1) * tq, :hp]
        qb = q_ref[0, i * tq:(i + 1) * tq, hp:]
        return lambda kb: jnp.concatenate(
            [lax.dot_general(kb[:, :hp], qa, _NT, preferred_element_type=F32),
             lax.dot_general(kb[:, hp:], qb, _NT, preferred_element_type=F32)], axis=1)

    outs = _attend_causal([make_score(0), make_score(1)], k_ref, v_ref, m_s, l_s, acc_s, s_a, s_b,
                          tq=tq, qi=pl.program_id(2))
    row = lax.broadcasted_iota(jnp.int32, (acc_s.shape[1], tq), 0)
    for i, o2 in enumerate(outs):
        o = jnp.where(row < MLA_V, o2[:, :tq], o2[:, tq:]).T
        o_ref[0, i * tq:(i + 1) * tq, :] = o.astype(o_ref.dtype)


def _mla_attention(qb, kb, vb):
    return _causal_attention_call(_mla_attn_kernel, qb, kb, vb, [], 2 * MLA_HEAD_PAD, 2 * MLA_V,
                                  "mla_attention")


def _mla_qabs_kernel(cq_ref, w_ref, wrot_ref, cos_ref, sin_ref, wabs_ref, q_ref):
    nb, lt, _ = cq_ref.shape
    n = w_ref.shape[1]
    qb = _mla_roped_q(cq_ref, w_ref, wrot_ref, cos_ref, sin_ref).reshape(nb * lt, n).astype(BF16)
    hp = MLA_HEAD_PAD
    for hh in range(n // hp):
        o = jnp.dot(qb[:, hh * hp:(hh + 1) * hp], wabs_ref[hh], preferred_element_type=F32)
        q_ref[:, hh] = o.reshape(nb, lt, o.shape[1]).astype(q_ref.dtype)


def _mla_qabs(cq, w_uq_pad, w_uq_rot, cos, sin, w_abs):
    b, l, r = cq.shape
    n = w_uq_pad.shape[1]
    heads, _, qw = w_abs.shape
    nb, lt = _row_tiles(b, l)
    act, vec, par, pos = _tile_specs(b, l, r, nb, lt)
    return pl.pallas_call(
        _mla_qabs_kernel,
        grid=(b // nb, l // lt),
        in_specs=[act(r), par((r, n)), par((r, n)), pos(cos.shape[1]), pos(sin.shape[1]),
                  par(w_abs.shape)],
        out_specs=pl.BlockSpec((nb, heads, lt, qw), lambda i, j: (i, 0, j, 0)),
        out_shape=jax.ShapeDtypeStruct((b, heads, l, qw), BF16),
        compiler_params=_params("parallel", "parallel"),
        name="mla_q_absorb",
    )(cq, w_uq_pad, w_uq_rot, cos, sin, w_abs)


def _mla_cached_kernel(q_ref, cc_ref, rc_ref, cn_ref, rn_ref, wuvt_ref, o_ref, m_s, l_s, acc_s, *, tk):
    heads, lq, qw = q_ref.shape[1:]
    q2 = q_ref[0].reshape(heads * lq, qw)

    def step(c, r):
        rpad = jnp.concatenate([r, jnp.zeros((r.shape[0], qw - c.shape[1] - r.shape[1]), r.dtype)], axis=1)
        kb = jnp.concatenate([c, rpad], axis=1).astype(BF16)
        s = lax.dot_general(kb, q2, _NT, preferred_element_type=F32)
        _softmax_update(m_s, l_s, acc_s, 0, s, c.astype(BF16))

    def block(j, carry):
        rows = pl.ds(pl.multiple_of(j * tk, tk), tk)
        step(cc_ref[0, rows, :], rc_ref[0, rows, :])
        return carry

    _softmax_init(m_s, l_s, acc_s)
    lax.fori_loop(0, cc_ref.shape[1] // tk, block, 0)
    step(cn_ref[0], rn_ref[0])
    lat = _softmax_result(l_s, acc_s, 0)
    assert 2 * lq == LANES and 2 * MLA_V == LANES
    lane = lax.broadcasted_iota(jnp.int32, (lq, LANES), 1)
    for pr in range(heads // 2):
        lp = lat[:, pr * LANES:(pr + 1) * LANES].astype(BF16)
        tt = jnp.dot(wuvt_ref[pr], lp, preferred_element_type=F32).T
        o_ref[0, :, pr * LANES:(pr + 1) * LANES] = jnp.where(lane < MLA_V, tt[:lq], tt[lq:]).astype(o_ref.dtype)


def _mla_attention_cached(q_abs, cache_ckv, cache_krope, ckv, krope, w_uvt):
    b, heads, l, qw = q_abs.shape
    p, rank = cache_ckv.shape[1:]
    rope = cache_krope.shape[2]
    tk = ATTN_TK if p % ATTN_TK == 0 else CHUNK
    assert p % tk == 0
    full = lambda shape: pl.BlockSpec((1,) + shape, lambda bi: (bi,) + (0,) * len(shape))
    return pl.pallas_call(
        functools.partial(_mla_cached_kernel, tk=tk),
        grid=(b,),
        in_specs=[full((heads, l, qw)), full((p, rank)), full((p, rope)), full((l, rank)), full((l, rope)),
                  pl.BlockSpec(w_uvt.shape, lambda bi: (0, 0, 0))],
        out_specs=full((l, heads * MLA_V)),
        out_shape=jax.ShapeDtypeStruct((b, l, heads * MLA_V), BF16),
        scratch_shapes=_softmax_scratch(1, rank, heads * l),
        compiler_params=_params("parallel"),
        name="mla_attention_cached",
    )(q_abs, cache_ckv, cache_krope, ckv, krope, w_uvt)


def _sgu_kernel(x_ref, g_ref, sh_ref, sc_ref, gt_ref, win_ref, gv_ref, ws_ref, bs_ref, wout_ref,
                *out_refs, t, emit_v):
    o_ref = out_refs[0]
    nb, lt, d = x_ref.shape
    rows = nb * lt
    width = gv_ref.shape[1]
    gd = width // SGU_GROUPS
    h = _modulated(x_ref, g_ref, sh_ref, sc_ref).reshape(rows, d).astype(BF16)
    u = jax.nn.gelu(jnp.dot(h, win_ref[:, :width], preferred_element_type=F32))
    v = jax.nn.gelu(jnp.dot(h, win_ref[:, width:], preferred_element_type=F32))
    v = _rms(v, gv_ref[...])
    if emit_v:
        out_refs[1][...] = v.reshape(nb, lt, width)
    vb = v.astype(BF16)
    bias = bs_ref[...]
    gated = []
    for c in range(rows // t):
        sv = [jnp.dot(ws_ref[gi], vb[c * t:(c + 1) * t, gi * gd:(gi + 1) * gd],
                      preferred_element_type=F32) for gi in range(SGU_GROUPS)]
        sv = jnp.concatenate(sv, axis=1) + bias
        gated.append((u[c * t:(c + 1) * t] * sv).astype(BF16))
    gated = jnp.concatenate(gated, axis=0) if len(gated) > 1 else gated[0]
    r = jnp.dot(gated, wout_ref[...], preferred_element_type=F32).reshape(nb, lt, d)
    o_ref[...] = x_ref[...] + (1.0 + gt_ref[...]) * r


def _sgu(x, g, sh, sc, gt, w_in, g_v, w_s, b_full, w_out, emit_v):
    b, l, d = x.shape
    t = w_s.shape[1]
    width = g_v.shape[1]
    nb, lt = _row_tiles(b, l, 256)
    assert lt % t == 0
    act, vec, par, _ = _tile_specs(b, l, d, nb, lt)
    out_specs = [act(d)]
    out_shape = [jax.ShapeDtypeStruct((b, l, d), F32)]
    if emit_v:
        out_specs.append(act(width))
        out_shape.append(jax.ShapeDtypeStruct((b, l, width), F32))
    return pl.pallas_call(
        functools.partial(_sgu_kernel, t=t, emit_v=emit_v),
        grid=(b // nb, l // lt),
        in_specs=[act(d), par((1, d)), vec(d), vec(d), vec(d), par(w_in.shape), par((1, width)),
                  par(w_s.shape), par(b_full.shape), par(w_out.shape)],
        out_specs=out_specs,
        out_shape=out_shape,
        compiler_params=_params("parallel", "parallel"),
        name="sgu",
    )(x, g, sh, sc, gt, w_in, g_v, w_s, b_full, w_out)


def _rope_angles(pos, half):
    inv_freq = ROPE_THETA ** (-jnp.arange(half, dtype=F32) / half)
    ang = pos.astype(F32)[:, None] * inv_freq[None, :]
    return jnp.cos(ang), jnp.sin(ang)


def _diff_rope_tables(pos):
    cos, sin = _rope_angles(pos, DIFF_DH // 2)
    reps = LANES // DIFF_DH
    return (jnp.tile(jnp.concatenate([cos, cos], axis=1), (1, reps)),
            jnp.tile(jnp.concatenate([-sin, sin], axis=1), (1, reps)))


def _mla_rope_tables(pos):
    cos, sin = _rope_angles(pos, MLA_ROPE // 2)
    n = pos.shape[0]
    ones = jnp.ones((n, MLA_NOPE), F32)
    zq = jnp.zeros((n, MLA_NOPE), F32)
    zp = jnp.zeros((n, MLA_HEAD_PAD - MLA_NOPE - MLA_ROPE), F32)
    scale = LOG2_E * MLA_SCALE
    q_cos = jnp.concatenate([ones, cos, cos, zp], axis=1) * scale
    q_sin = jnp.concatenate([zq, -sin, sin, zp], axis=1) * scale
    k_cos = jnp.concatenate([cos, cos], axis=1)
    k_sin = jnp.concatenate([sin, sin], axis=1)
    return q_cos, q_sin, k_cos, k_sin


def _prepare_weights(p):
    d = p['w_up'].shape[1]
    w = {}
    w['w_up'] = [p['w_up'][i].astype(BF16) for i in range(p['w_up'].shape[0])]
    w['w_down'] = [p['w_down'][i].astype(BF16) for i in range(p['w_down'].shape[0])]
    w['glu_a'] = p['s5_w_glu_a'].astype(BF16)
    w['glu_b'] = p['s5_w_glu_b'].astype(BF16)
    w['diff_qkv'] = p['diff_w_qkv'].astype(BF16)
    w['diff_o'] = p['diff_w_o'].astype(BF16)
    lam = jnp.stack([p['diff_lambda_q1'], p['diff_lambda_k1'], p['diff_lambda_q2'], p['diff_lambda_k2']])
    w['diff_lam'] = jnp.pad(lam.astype(F32), ((0, 4), (0, LANES - lam.shape[1])))
    w['diff_g_sub'] = p['diff_g_sub'].reshape(1, -1)

    w['mla_dq'] = p['mla_w_dq'].astype(BF16)
    kvr = p['mla_g_kv'].shape[0]
    w['mla_dc'] = p['mla_w_dkv'][:, :kvr].astype(BF16)
    wr = p['mla_w_dkv'][:, kvr:]
    hr = MLA_ROPE // 2
    w['mla_dr'] = wr.astype(BF16)
    w['mla_drr'] = jnp.concatenate([-wr[:, hr:], wr[:, :hr]], axis=1).astype(BF16)
    w['mla_g_q'] = p['mla_g_q'].reshape(1, -1)
    w['mla_g_kv'] = p['mla_g_kv'].reshape(1, -1)
    pad = MLA_HEAD_PAD - MLA_NOPE - MLA_ROPE
    qr = p['mla_w_uq'].shape[0]
    uq = p['mla_w_uq'].reshape(qr, MLA_HEADS, MLA_NOPE + MLA_ROPE)
    w['mla_uq'] = jnp.pad(uq, ((0, 0), (0, 0), (0, pad))).reshape(qr, -1).astype(BF16)
    uq_rot = jnp.concatenate([jnp.zeros_like(uq[..., :MLA_NOPE]), uq[..., MLA_NOPE + hr:],
                              uq[..., MLA_NOPE:MLA_NOPE + hr]], axis=2)
    w['mla_uq_rot'] = jnp.pad(uq_rot, ((0, 0), (0, 0), (0, pad))).reshape(qr, -1).astype(BF16)
    uk = jnp.pad(p['mla_w_uk'], ((0, 0), (0, 0), (0, MLA_HEAD_PAD - MLA_NOPE)))
    w['mla_uk'] = uk.reshape(kvr, -1).astype(BF16)
    place = jnp.pad(jnp.eye(MLA_ROPE, dtype=F32), ((0, 0), (MLA_NOPE, pad)))
    w['mla_place'] = jnp.tile(place, (1, MLA_HEADS)).astype(BF16)
    w['mla_uv'] = p['mla_w_uv'].reshape(kvr, -1).astype(BF16)
    absorb = jnp.pad(p['mla_w_uk'].transpose(1, 2, 0), ((0, 0), (0, MLA_HEAD_PAD - MLA_NOPE), (0, 0)))
    select = jnp.pad(jnp.eye(MLA_ROPE, dtype=F32), ((MLA_NOPE, pad), (0, MLA_HEAD_PAD - MLA_ROPE)))
    w['mla_abs'] = jnp.concatenate(
        [absorb, jnp.broadcast_to(select, (MLA_HEADS,) + select.shape)], axis=2).astype(BF16)
    w['mla_uvt'] = p['mla_w_uv'].transpose(1, 2, 0).reshape(MLA_HEADS // 2, 2 * MLA_V, kvr).astype(BF16)
    w['mla_o'] = p['mla_w_o'].astype(BF16)

    w['sgu_in'] = p['sgu_w_in'].astype(BF16)
    w['sgu_g_v'] = p['sgu_g_v'].reshape(1, -1)
    w['sgu_out'] = p['sgu_w_out'].astype(BF16)
    return w


def _sgu_spatial(w_s, b_s, t, width):
    ws = jnp.tril(w_s[:, :t, :t]).astype(BF16)
    gd = width // SGU_GROUPS
    b_full = jnp.repeat(b_s[:, :t].T.astype(F32), gd, axis=1)
    return ws, b_full


def _run_trunk(x, mods, pos, past, p, w, s5_tables):
    b, l, d = x.shape
    new = {}
    depth = p['g_mix'].shape[0]
    for i in range(depth):
        sh1, sc1, gt1, sh2, sc2, gt2 = [mods[i][:, k][:, None, :] for k in range(6)]
        g_mix = p['g_mix'][i].reshape(1, d)
        kind = i % N_MIXERS
        mixer = None
        if kind == 0:
            h = _modulate(x, g_mix, sh1, sc1)
            h0_re = None if past is None else past['s5_re']
            h0_im = None if past is None else past['s5_im']
            y, new['s5_re'], new['s5_im'] = _s5_scan(h, s5_tables, h0_re, h0_im)
            x = _s5_glu(y, h, x, p['s5_d'].reshape(1, d), gt1, w['glu_a'], w['glu_b'])
        elif kind == 1:
            cos, sin = _diff_rope_tables(pos)
            k, v, qb, kb, vb = _diff_qkv(x, g_mix, sh1, sc1, w['diff_qkv'], cos, sin)
            if past is None:
                o = _diff_attention(qb, kb, vb, w['diff_lam'], w['diff_g_sub'])
            else:
                o = _diff_attention_cached(qb, past['diff_k'], past['diff_v'], kb, vb,
                                           w['diff_lam'], w['diff_g_sub'])
            mixer = (o, w['diff_o'], gt1)
            new['diff_k'] = k.reshape(b, l, DIFF_HEADS, 2 * DIFF_DH)
            new['diff_v'] = v.reshape(b, l, DIFF_HEADS, 2 * DIFF_DH)
        elif kind == 2:
            q_cos, q_sin, k_cos, k_sin = _mla_rope_tables(pos)
            cq, ckv, krope = _mla_down(x, g_mix, sh1, sc1, w['mla_dq'], w['mla_dc'], w['mla_dr'],
                                       w['mla_drr'], w['mla_g_q'], w['mla_g_kv'], k_cos, k_sin)
            if past is None:
                qb = _mla_qup(cq, w['mla_uq'], w['mla_uq_rot'], q_cos, q_sin)
                kb, vb = _mla_kvup(ckv, krope, w['mla_uk'], w['mla_place'], w['mla_uv'])
                o = _mla_attention(qb, kb, vb)
            else:
                q_abs = _mla_qabs(cq, w['mla_uq'], w['mla_uq_rot'], q_cos, q_sin, w['mla_abs'])
                o = _mla_attention_cached(q_abs, past['mla_ckv'], past['mla_krope'], ckv, krope,
                                          w['mla_uvt'])
            mixer = (o, w['mla_o'], gt1)
            new['mla_ckv'], new['mla_krope'] = ckv, krope
        else:
            t = min(l, SGU_CHUNK)
            ws, b_full = _sgu_spatial(p['sgu_w_s'], p['sgu_b_s'], t, w['sgu_g_v'].shape[1])
            outs = _sgu(x, g_mix, sh1, sc1, gt1, w['sgu_in'], w['sgu_g_v'], ws, b_full, w['sgu_out'],
                        emit_v=past is not None)
            x = outs[0]
            if past is not None:
                new['sgu_v'] = outs[1]
        x = _mlp(x, p['g_ffn'][i].reshape(1, d), sh2, sc2, gt2, w['w_up'][i], w['w_down'][i],
                 p['g_final'].reshape(1, d), final_norm=(i == depth - 1), mixer=mixer)
    return x, new


def kernel(x_prompt, x_sample, c_prompt, c_sample, state_s5_re, state_s5_im, cache_diff_k, cache_diff_v, cache_mla_ckv, cache_mla_krope, w_ada, b_ada, g_mix, g_ffn, w_up, w_down, g_final, s5_a_re, s5_a_im, s5_b_re, s5_b_im, s5_c_re, s5_c_im, s5_d, s5_log_dt, s5_w_glu_a, s5_w_glu_b, diff_w_qkv, diff_lambda_q1, diff_lambda_k1, diff_lambda_q2, diff_lambda_k2, diff_g_sub, diff_w_o, mla_w_dq, mla_g_q, mla_w_uq, mla_w_dkv, mla_g_kv, mla_w_uk, mla_w_uv, mla_w_o, sgu_w_in, sgu_g_v, sgu_w_s, sgu_b_s, sgu_w_out):
    p = {
        'w_ada': w_ada, 'b_ada': b_ada, 'g_mix': g_mix, 'g_ffn': g_ffn,
        'w_up': w_up, 'w_down': w_down, 'g_final': g_final,
        's5_d': s5_d, 's5_w_glu_a': s5_w_glu_a, 's5_w_glu_b': s5_w_glu_b,
        'diff_w_qkv': diff_w_qkv, 'diff_lambda_q1': diff_lambda_q1, 'diff_lambda_k1': diff_lambda_k1,
        'diff_lambda_q2': diff_lambda_q2, 'diff_lambda_k2': diff_lambda_k2,
        'diff_g_sub': diff_g_sub, 'diff_w_o': diff_w_o,
        'mla_w_dq': mla_w_dq, 'mla_g_q': mla_g_q, 'mla_w_uq': mla_w_uq, 'mla_w_dkv': mla_w_dkv,
        'mla_g_kv': mla_g_kv, 'mla_w_uk': mla_w_uk, 'mla_w_uv': mla_w_uv, 'mla_w_o': mla_w_o,
        'sgu_w_in': sgu_w_in, 'sgu_g_v': sgu_g_v, 'sgu_w_s': sgu_w_s, 'sgu_b_s': sgu_b_s,
        'sgu_w_out': sgu_w_out,
    }
    past = {
        's5_re': state_s5_re, 's5_im': state_s5_im,
        'diff_k': cache_diff_k, 'diff_v': cache_diff_v,
        'mla_ckv': cache_mla_ckv, 'mla_krope': cache_mla_krope,
    }
    bp, lp, d = x_prompt.shape
    bs, ls, _ = x_sample.shape
    depth = w_ada.shape[0]

    c_all = jnp.concatenate([c_prompt, c_sample], axis=0)
    c_all = jnp.pad(c_all, ((0, (-c_all.shape[0]) % 8), (0, 0)))
    mod = _adaln(c_all, w_ada, b_ada).reshape(depth, c_all.shape[0], 6, d)
    mods_p = [mod[i, :bp] for i in range(depth)]
    mods_s = [mod[i, bp:bp + bs] for i in range(depth)]

    w = _prepare_weights(p)
    max_steps = max((lp // S5_SUB).bit_length() - 1, (ls // S5_SUB).bit_length() - 1, 1)
    s5_tables = _s5_tables(s5_a_re, s5_a_im, s5_b_re, s5_b_im, s5_c_re, s5_c_im, s5_log_dt, max_steps)

    pos_p = jnp.arange(lp, dtype=jnp.int32)
    pos_s = cache_diff_k.shape[1] + jnp.arange(ls, dtype=jnp.int32)
    y_prompt, sp = _run_trunk(x_prompt, mods_p, pos_p, None, p, w, s5_tables)
    y_sample, ss = _run_trunk(x_sample, mods_s, pos_s, past, p, w, s5_tables)
    return (y_prompt, y_sample,
            sp['s5_re'], sp['s5_im'], ss['s5_re'], ss['s5_im'],
            sp['diff_k'], sp['diff_v'], ss['diff_k'], ss['diff_v'],
            sp['mla_ckv'], sp['mla_krope'], ss['mla_ckv'], ss['mla_krope'],
            ss['sgu_v'])
```

```python
import functools
import math

import jax
import jax.numpy as jnp
from jax import lax
from jax.experimental import pallas as pl
from jax.experimental.pallas import tpu as pltpu

F32 = jnp.float32
BF16 = jnp.bfloat16
HIGHEST = lax.Precision.HIGHEST

NORM_EPS = 1e-6
ROPE_THETA = 10000.0
NEG_INF = -1e30
LOG2_E = math.log2(math.e)
CHUNK = 64
N_MIXERS = 4

S5_GROUP = 16
S5_SUB = 16

DIFF_HEADS = 8
DIFF_DH = 64
LAMBDA_INIT = 0.8 - 0.6 * math.exp(-0.3 * 1)

MLA_HEADS = 16
MLA_NOPE = 64
MLA_ROPE = 32
MLA_V = 64
MLA_SCALE = (MLA_NOPE + MLA_ROPE) ** -0.5
MLA_HEAD_PAD = 128

SGU_CHUNK = 128
SGU_GROUPS = 8

LANES = 128
ROW_TILE = 512
ATTN_TQ = 512
ATTN_STREAMS = 2
ATTN_TK = 512
VMEM_LIMIT_BYTES = 48 * 1024 * 1024


def _params(*sem):
    return pltpu.CompilerParams(dimension_semantics=sem, vmem_limit_bytes=VMEM_LIMIT_BYTES)


def _row_tiles(batch, length, rows=ROW_TILE):
    if length >= rows:
        assert length % rows == 0
        return 1, rows
    nb = max(1, min(batch, rows // length))
    while batch % nb:
        nb -= 1
    return nb, length


def _rms(x, g):
    return x * lax.rsqrt(jnp.mean(x * x, axis=-1, keepdims=True) + NORM_EPS) * g


def _modulated(x_ref, g_ref, sh_ref, sc_ref):
    return _rms(x_ref[...], g_ref[...]) * (1.0 + sc_ref[...]) + sh_ref[...]


def _rotate_pairs(x, half, period, lo):
    width = x.shape[-1]
    lane = lax.broadcasted_iota(jnp.int32, x.shape, x.ndim - 1) % period
    fwd = pltpu.roll(x, width - half, axis=x.ndim - 1)
    bwd = pltpu.roll(x, half, axis=x.ndim - 1)
    return jnp.where((lane >= lo) & (lane < lo + half), fwd, bwd)


def _adaln_kernel(c_ref, w_ref, b_ref, o_ref):
    c = c_ref[...]
    s = c * jax.nn.sigmoid(c)
    o_ref[0] = jnp.dot(s, w_ref[0], precision=HIGHEST, preferred_element_type=F32) + b_ref[0]


def _adaln(c_all, w_ada, b_ada):
    depth, d, n = w_ada.shape
    rows = c_all.shape[0]
    tn = n // 4
    return pl.pallas_call(
        _adaln_kernel,
        grid=(depth, n // tn),
        in_specs=[pl.BlockSpec((rows, d), lambda i, j: (0, 0)),
                  pl.BlockSpec((1, d, tn), lambda i, j: (i, 0, j)),
                  pl.BlockSpec((1, 1, tn), lambda i, j: (i, 0, j))],
        out_specs=pl.BlockSpec((1, rows, tn), lambda i, j: (i, 0, j)),
        out_shape=jax.ShapeDtypeStruct((depth, rows, n), F32),
        compiler_params=_params("parallel", "parallel"),
        name="adaln",
    )(c_all, w_ada, b_ada.reshape(depth, 1, n))


def _tile_specs(batch, length, d, nb, lt, extra_axes=0):
    if extra_axes == 0:
        act = lambda width: pl.BlockSpec((nb, lt, width), lambda i, j: (i, j, 0))
        vec = lambda width: pl.BlockSpec((nb, 1, width), lambda i, j: (i, 0, 0))
        par = lambda shape: pl.BlockSpec(shape, lambda i, j: (0,) * len(shape))
        pos = lambda width: pl.BlockSpec((lt, width), lambda i, j: (j, 0))
    else:
        act = lambda width: pl.BlockSpec((nb, lt, width), lambda i, j, f: (i, j, 0))
        vec = lambda width: pl.BlockSpec((nb, 1, width), lambda i, j, f: (i, 0, 0))
        par = lambda shape: pl.BlockSpec(shape, lambda i, j, f: (0,) * len(shape))
        pos = lambda width: pl.BlockSpec((lt, width), lambda i, j, f: (j, 0))
    return act, vec, par, pos


def _modulate_kernel(x_ref, g_ref, sh_ref, sc_ref, o_ref):
    o_ref[...] = _modulated(x_ref, g_ref, sh_ref, sc_ref)


def _modulate(x, g, sh, sc):
    b, l, d = x.shape
    nb, lt = _row_tiles(b, l)
    act, vec, par, _ = _tile_specs(b, l, d, nb, lt)
    return pl.pallas_call(
        _modulate_kernel,
        grid=(b // nb, l // lt),
        in_specs=[act(d), par((1, d)), vec(d), vec(d)],
        out_specs=act(d),
        out_shape=jax.ShapeDtypeStruct((b, l, d), F32),
        compiler_params=_params("parallel", "parallel"),
        name="modulate",
    )(x, g, sh, sc)


def _mlp_kernel(*refs, final_norm, tf, mixer_proj):
    if mixer_proj:
        o_in_ref, wo_ref, gt1_ref, *refs = refs
    x_ref, g_ref, sh_ref, sc_ref, gt_ref, wu_ref, wd_ref, gf_ref, o_ref = refs
    nb, lt, d = x_ref.shape
    x = x_ref[...]
    if mixer_proj:
        r = jnp.dot(o_in_ref[...].reshape(nb * lt, o_in_ref.shape[2]), wo_ref[...],
                    preferred_element_type=F32)
        x = x + (1.0 + gt1_ref[...]) * r.reshape(nb, lt, d)
    h = (_rms(x, g_ref[...]) * (1.0 + sc_ref[...]) + sh_ref[...]).reshape(nb * lt, d).astype(BF16)
    acc = None
    for f in range(wu_ref.shape[1] // tf):
        a = jnp.dot(h, wu_ref[:, f * tf:(f + 1) * tf], preferred_element_type=F32)
        a = jnp.square(jnp.maximum(a, 0.0)).astype(BF16)
        r = jnp.dot(a, wd_ref[f * tf:(f + 1) * tf, :], preferred_element_type=F32)
        acc = r if acc is None else acc + r
    y = x + (1.0 + gt_ref[...]) * acc.reshape(nb, lt, d)
    if final_norm:
        y = _rms(y, gf_ref[...])
    o_ref[...] = y


def _mlp(x, g, sh, sc, gt, w_up, w_down, g_final, final_norm, mixer=None, tf=2048):
    b, l, d = x.shape
    dff = w_up.shape[1]
    nb, lt = _row_tiles(b, l)
    act, vec, par, _ = _tile_specs(b, l, d, nb, lt)
    resident = lambda shape: pl.BlockSpec(shape, lambda i, j: (0, 0), pipeline_mode=pl.Buffered(1))
    args = [x, g, sh, sc, gt, w_up, w_down, g_final]
    in_specs = [act(d), par((1, d)), vec(d), vec(d), vec(d),
                resident((d, dff)), resident((dff, d)), par((1, d))]
    if mixer is not None:
        o, w_o, gt1 = mixer
        args = [o, w_o, gt1] + args
        in_specs = [act(o.shape[2]), resident(w_o.shape), vec(d)] + in_specs
    return pl.pallas_call(
        functools.partial(_mlp_kernel, final_norm=final_norm, tf=tf, mixer_proj=mixer is not None),
        grid=(b // nb, l // lt),
        in_specs=in_specs,
        out_specs=act(d),
        out_shape=jax.ShapeDtypeStruct((b, l, d), F32),
        compiler_params=_params("parallel", "parallel"),
        name="mlp",
    )(*args)


def _s5_kernel(*refs, nch, batch, nsteps, has_h0):
    if has_h0:
        u_ref, bc_ref, m_ref, cc_ref, a1_ref, a2_ref, h0_ref, y_ref, hl_ref = refs
    else:
        u_ref, bc_ref, m_ref, cc_ref, a1_ref, a2_ref, y_ref, hl_ref = refs
    u = u_ref[0]
    rows = u.shape[0]
    x = jnp.dot(u, bc_ref[0], precision=HIGHEST, preferred_element_type=F32)
    half = x.shape[1] // 2
    if has_h0:
        h0 = h0_ref[0]
        x = x + a1_ref[0, 0:1, :] * h0 + a2_ref[0, 0:1, :] * pltpu.roll(h0, half, axis=1)
    kidx = lax.broadcasted_iota(jnp.int32, x.shape, 0) & (nch - 1)
    for j in range(nsteps):
        s = 1 << j
        xs = jnp.where(kidx >= s, pltpu.roll(x, s, axis=0), 0.0)
        x = x + a1_ref[0, j:j + 1, :] * xs + a2_ref[0, j:j + 1, :] * pltpu.roll(xs, half, axis=1)
    for b in range(batch):
        r = (b + 1) * nch - 1
        hl_ref[0, b:b + 1, :] = x[r:r + 1, :]
    hstart = jnp.where(kidx >= 1, pltpu.roll(x, 1, axis=0), 0.0)
    if has_h0:
        hstart = hstart + h0
    y = jnp.dot(u.astype(BF16), m_ref[0], preferred_element_type=F32)
    y = y + jnp.dot(hstart.astype(BF16), cc_ref[0], preferred_element_type=F32)
    y_ref[0] = y


def _s5_tables(a_re, a_im, b_re, b_im, c_re, c_im, log_dt, max_steps):
    g, p = a_re.shape
    sub = S5_SUB
    dt = jnp.exp(log_dt.astype(F32))[:, None]
    zr, zi = a_re.astype(F32) * dt, a_im.astype(F32) * dt

    def zpow(n):
        mag = jnp.exp(zr * n)
        return mag * jnp.cos(zi * n), mag * jnp.sin(zi * n)

    er = jnp.expm1(zr) * jnp.cos(zi) - 2.0 * jnp.square(jnp.sin(0.5 * zi))
    ei = jnp.exp(zr) * jnp.sin(zi)
    den = a_re * a_re + a_im * a_im
    fr = (er * a_re + ei * a_im) / den
    fi = (ei * a_re - er * a_im) / den
    bb_re = fr[..., None] * b_re - fi[..., None] * b_im
    bb_im = fr[..., None] * b_im + fi[..., None] * b_re

    lags = jnp.arange(sub + 1, dtype=F32)[:, None, None]
    pr, pi = zpow(lags)

    qr, qi = pr[:sub][::-1], pi[:sub][::-1]
    bcr = qr[..., None] * bb_re[None] - qi[..., None] * bb_im[None]
    bci = qr[..., None] * bb_im[None] + qi[..., None] * bb_re[None]
    bc = jnp.concatenate([bcr, bci], axis=2)
    bc = bc.transpose(1, 0, 3, 2).reshape(g, sub * S5_GROUP, 2 * p)

    ar, ai = pr[1:], pi[1:]
    ccr = c_re[None] * ar[:, :, None, :] - c_im[None] * ai[:, :, None, :]
    cci = c_re[None] * ai[:, :, None, :] + c_im[None] * ar[:, :, None, :]
    cc = jnp.concatenate([ccr, -cci], axis=3)
    cc = cc.transpose(1, 3, 0, 2).reshape(g, 2 * p, sub * S5_GROUP)

    kr = (jnp.einsum('gcp,lgp,gpd->lgcd', c_re, pr[:sub], bb_re, precision=HIGHEST)
          - jnp.einsum('gcp,lgp,gpd->lgcd', c_re, pi[:sub], bb_im, precision=HIGHEST)
          - jnp.einsum('gcp,lgp,gpd->lgcd', c_im, pr[:sub], bb_im, precision=HIGHEST)
          - jnp.einsum('gcp,lgp,gpd->lgcd', c_im, pi[:sub], bb_re, precision=HIGHEST))
    tt = jnp.arange(sub)
    lag = tt[None, :] - tt[:, None]
    toep = jnp.where((lag >= 0)[:, :, None, None, None],
                     kr[jnp.clip(lag, 0, sub - 1)], 0.0)
    m = toep.transpose(2, 0, 4, 1, 3).reshape(g, sub * S5_GROUP, sub * S5_GROUP)

    steps = (sub * (2 ** jnp.arange(max_steps))).astype(F32)[:, None, None]
    sr, si = zpow(steps)
    a1 = jnp.concatenate([sr, sr], axis=-1).transpose(1, 0, 2)
    a2 = jnp.concatenate([-si, si], axis=-1).transpose(1, 0, 2)
    pad = (-max_steps) % 8
    a1 = jnp.pad(a1, ((0, 0), (0, pad), (0, 0)))
    a2 = jnp.pad(a2, ((0, 0), (0, pad), (0, 0)))
    return bc, m.astype(BF16), cc.astype(BF16), a1, a2


def _s5_lane_kernel(h_ref, bct_ref, mt_ref, cct_ref, a1_ref, a2_ref, y_ref, hl_ref, ut_s, yt_s,
                    *, nch, nsteps):
    sub, c = S5_SUB, S5_GROUP
    groups = ut_s.shape[0]
    for t in range(sub):
        att = h_ref[0, pl.ds(t, nch, stride=sub), :].T
        for g in range(groups):
            ut_s[g, t * c:(t + 1) * c, :] = att[g * c:(g + 1) * c, :]
    half = bct_ref.shape[1] // 2
    lane = lax.broadcasted_iota(jnp.int32, (2 * half, nch), 1)
    for g in range(groups):
        ut = ut_s[g]
        x = jnp.dot(bct_ref[g], ut, precision=HIGHEST, preferred_element_type=F32)
        for j in range(nsteps):
            s = 1 << j
            xs = jnp.where(lane >= s, pltpu.roll(x, s, axis=1), 0.0)
            x = x + a1_ref[g, :, j:j + 1] * xs + a2_ref[g, :, j:j + 1] * pltpu.roll(xs, half, axis=0)
        hl_ref[0, g:g + 1, :] = x[:, nch - LANES:].T[LANES - 1:LANES, :]
        hstart = jnp.where(lane >= 1, pltpu.roll(x, 1, axis=1), 0.0)
        yt = jnp.dot(mt_ref[g], ut.astype(BF16), preferred_element_type=F32)
        yt = yt + jnp.dot(cct_ref[g], hstart.astype(BF16), preferred_element_type=F32)
        for t in range(sub):
            yt_s[t, g * c:(g + 1) * c, :] = yt[t * c:(t + 1) * c, :]
    for t in range(sub):
        y_ref[0, pl.ds(t, nch, stride=sub), :] = yt_s[t].T


def _s5_scan_lanes(h, tables):
    bc, m, cc, a1, a2 = tables
    b, l, d = h.shape
    g, k, p2 = bc.shape
    nch = l // S5_SUB
    gpb = LANES // S5_GROUP
    nsteps = nch.bit_length() - 1
    wspec = lambda shape: pl.BlockSpec((gpb,) + shape, lambda j, bi: (j, 0, 0))
    act = pl.BlockSpec((1, l, LANES), lambda j, bi: (bi, 0, j))
    return pl.pallas_call(
        functools.partial(_s5_lane_kernel, nch=nch, nsteps=nsteps),
        grid=(g // gpb, b),
        in_specs=[act, wspec((p2, k)), wspec((k, k)), wspec((k, p2)),
                  wspec((p2, a1.shape[1])), wspec((p2, a2.shape[1]))],
        out_specs=[act, pl.BlockSpec((1, gpb, p2), lambda j, bi: (bi, j, 0))],
        out_shape=[jax.ShapeDtypeStruct((b, l, d), F32), jax.ShapeDtypeStruct((b, g, p2), F32)],
        scratch_shapes=[pltpu.VMEM((gpb, k, nch), F32), pltpu.VMEM((S5_SUB, LANES, nch), F32)],
        compiler_params=_params("parallel", "parallel"),
        name="s5_scan_lanes",
    )(h, bc.transpose(0, 2, 1), m.transpose(0, 2, 1), cc.transpose(0, 2, 1),
      a1.transpose(0, 2, 1), a2.transpose(0, 2, 1))


def _s5_scan(h, tables, h0_re, h0_im):
    bc, m, cc, a1, a2 = tables
    b, l, d = h.shape
    if h0_re is None and (l // S5_SUB) % LANES == 0:
        y, hl = _s5_scan_lanes(h, tables)
        return y, hl[..., :hl.shape[-1] // 2], hl[..., hl.shape[-1] // 2:]
    g = d // S5_GROUP
    p2 = bc.shape[-1]
    nch = l // S5_SUB
    assert nch & (nch - 1) == 0
    nsteps = nch.bit_length() - 1
    rows = b * nch
    k = S5_SUB * S5_GROUP
    u = h.reshape(b, nch, S5_SUB, g, S5_GROUP).transpose(3, 0, 1, 2, 4).reshape(g, rows, k)
    has_h0 = h0_re is not None
    grp = lambda shape: pl.BlockSpec((1,) + shape, lambda i: (i, 0, 0))
    args = [u, bc, m, cc, a1, a2]
    in_specs = [grp((rows, k)), grp((k, p2)), grp((k, k)), grp((p2, k)),
                grp(a1.shape[1:]), grp(a2.shape[1:])]
    if has_h0:
        h0 = jnp.concatenate([h0_re, h0_im], axis=-1).astype(F32).transpose(1, 0, 2)
        h0 = jnp.pad(h0[:, :, None, :], ((0, 0), (0, 0), (0, nch - 1), (0, 0))).reshape(g, rows, p2)
        args.append(h0)
        in_specs.append(grp((rows, p2)))
    y, hl = pl.pallas_call(
        functools.partial(_s5_kernel, nch=nch, batch=b, nsteps=nsteps, has_h0=has_h0),
        grid=(g,),
        in_specs=in_specs,
        out_specs=[grp((rows, k)), grp((b, p2))],
        out_shape=[jax.ShapeDtypeStruct((g, rows, k), F32), jax.ShapeDtypeStruct((g, b, p2), F32)],
        compiler_params=_params("parallel"),
        name="s5_scan",
    )(*args)
    y = y.reshape(g, b, nch, S5_SUB, S5_GROUP).transpose(1, 2, 3, 0, 4).reshape(b, l, d)
    hl = hl.transpose(1, 0, 2)
    return y, hl[..., :p2 // 2], hl[..., p2 // 2:]


def _glu_kernel(y_ref, h_ref, x_ref, d_ref, gt_ref, wa_ref, wb_ref, o_ref):
    nb, lt, d = x_ref.shape
    z = jax.nn.gelu(y_ref[...] + d_ref[...] * h_ref[...]).reshape(nb * lt, d).astype(BF16)
    a = jnp.dot(z, wa_ref[...], preferred_element_type=F32)
    b = jnp.dot(z, wb_ref[...], preferred_element_type=F32)
    out = (a * jax.nn.sigmoid(b)).reshape(nb, lt, d)
    o_ref[...] = x_ref[...] + (1.0 + gt_ref[...]) * out


def _s5_glu(y, h, x, d_skip, gt, wa, wb):
    b, l, d = x.shape
    nb, lt = _row_tiles(b, l)
    act, vec, par, _ = _tile_specs(b, l, d, nb, lt)
    return pl.pallas_call(
        _glu_kernel,
        grid=(b // nb, l // lt),
        in_specs=[act(d), act(d), act(d), par((1, d)), vec(d), par((d, d)), par((d, d))],
        out_specs=act(d),
        out_shape=jax.ShapeDtypeStruct((b, l, d), F32),
        compiler_params=_params("parallel", "parallel"),
        name="s5_glu",
    )(y, h, x, d_skip, gt, wa, wb)


def _diff_qkv_kernel(x_ref, g_ref, sh_ref, sc_ref, w_ref, cos_ref, sin_ref,
                     k_ref, v_ref, qb_ref, kb_ref, vb_ref):
    nb, lt, d = x_ref.shape
    h = _modulated(x_ref, g_ref, sh_ref, sc_ref).reshape(nb * lt, d).astype(BF16)
    reps = d // cos_ref.shape[1]
    cos = jnp.tile(cos_ref[...], (1, reps))[None]
    sin = jnp.tile(sin_ref[...], (1, reps))[None]

    def roped(cols):
        t = jnp.dot(h, w_ref[:, cols * d:(cols + 1) * d], preferred_element_type=F32)
        r = _rotate_pairs(t, DIFF_DH // 2, DIFF_DH, 0)
        return t.reshape(nb, lt, d) * cos + r.reshape(nb, lt, d) * sin

    q = roped(0)
    qb_ref[...] = (q * (LOG2_E * DIFF_DH ** -0.5)).astype(BF16)
    k = roped(1)
    kb_ref[...] = k.astype(BF16)
    v = jnp.dot(h, w_ref[:, 2 * d:], preferred_element_type=F32).reshape(nb, lt, d)
    vb_ref[...] = v.astype(BF16)
    hw = k_ref.shape[2]
    heads = d // hw
    for hh in range(heads):
        k_ref[:, pl.ds(hh, lt, stride=heads), :] = k[:, :, hh * hw:(hh + 1) * hw]
        v_ref[:, pl.ds(hh, lt, stride=heads), :] = v[:, :, hh * hw:(hh + 1) * hw]


def _diff_qkv(x, g, sh, sc, w_qkv, cos, sin):
    b, l, d = x.shape
    nb, lt = _row_tiles(b, l)
    act, vec, par, pos = _tile_specs(b, l, d, nb, lt)
    hw = d // DIFF_HEADS
    return pl.pallas_call(
        _diff_qkv_kernel,
        grid=(b // nb, l // lt),
        in_specs=[act(d), par((1, d)), vec(d), vec(d), par((d, 3 * d)),
                  pos(cos.shape[1]), pos(sin.shape[1])],
        out_specs=[pl.BlockSpec((nb, lt * DIFF_HEADS, hw), lambda i, j: (i, j, 0))] * 2 + [act(d)] * 3,
        out_shape=[jax.ShapeDtypeStruct((b, l * DIFF_HEADS, hw), F32)] * 2
                  + [jax.ShapeDtypeStruct((b, l, d), BF16)] * 3,
        compiler_params=_params("parallel", "parallel"),
        name="diff_qkv",
    )(x, g, sh, sc, w_qkv, cos, sin)


_NT = (((1,), (1,)), ((), ()))
_TN = (((0,), (0,)), ((), ()))


def _softmax_update(m_s, l_s, acc_s, i, s, vb):
    m_prev = m_s[i]
    m_new = jnp.maximum(m_prev, jnp.max(s, axis=0, keepdims=True))
    alpha = jnp.exp2(m_prev - m_new)
    p = jnp.exp2(s - m_new)
    l_s[i] = alpha * l_s[i] + jnp.sum(p, axis=0, keepdims=True)
    acc_s[i] = alpha * acc_s[i] + lax.dot_general(vb, p.astype(BF16), _TN, preferred_element_type=F32)
    m_s[i] = m_new


def _softmax_init(m_s, l_s, acc_s):
    m_s[...] = jnp.full_like(m_s, NEG_INF)
    l_s[...] = jnp.zeros_like(l_s)
    acc_s[...] = jnp.zeros_like(acc_s)


def _softmax_result(l_s, acc_s, i):
    return acc_s[i] / l_s[i]


def _softmax_scratch(n, vw, queries):
    return [pltpu.VMEM((n, 1, queries), F32), pltpu.VMEM((n, 1, queries), F32),
            pltpu.VMEM((n, vw, queries), F32)]


def _attend_causal(scores, k_ref, v_ref, m_s, l_s, acc_s, s_a, s_b, *, tq, qi):
    assert len(scores) == 2 and tq % CHUNK == 0
    _softmax_init(m_s, l_s, acc_s)
    update = functools.partial(_softmax_update, m_s, l_s, acc_s)
    block = lambda ref, j: ref[0, pl.ds(pl.multiple_of(j * tq, tq), tq), :].astype(BF16)

    def qk(dst, j, streams):
        kb = block(k_ref, j)
        for i in streams:
            dst[i] = scores[i](kb)

    def pv(src, j, streams, masked=None):
        vb = block(v_ref, j)
        for i in streams:
            update(i, jnp.where(vis, src[i], NEG_INF) if i == masked else src[i], vb)

    def pair(jj, carry):
        qk(s_b, 2 * jj + 1, (0, 1))
        pv(s_a, 2 * jj, (0, 1))
        qk(s_a, 2 * jj + 2, (0, 1))
        pv(s_b, 2 * jj + 1, (0, 1))
        return carry

    qk(s_a, 0, (0, 1))
    lax.fori_loop(0, qi, pair, 0)
    kc = lax.broadcasted_iota(jnp.int32, (tq, tq), 0) // CHUNK
    qc = lax.broadcasted_iota(jnp.int32, (tq, tq), 1) // CHUNK
    vis = jnp.concatenate([kc <= qc] * 2, axis=1)
    qk(s_b, 2 * qi + 1, (1,))
    pv(s_a, 2 * qi, (0, 1), masked=0)
    pv(s_b, 2 * qi + 1, (1,), masked=1)
    return [_softmax_result(l_s, acc_s, i) for i in range(2)]


def _diff_attn_kernel(q_ref, k_ref, v_ref, lam_ref, gsub_ref, o_ref, m_s, l_s, acc_s, s_a, s_b, *, tq):
    def make_score(i):
        q2 = _diff_query_groups(q_ref[0, i * tq:(i + 1) * tq, :])
        return lambda kb: lax.dot_general(kb, q2, _NT, preferred_element_type=F32)

    outs = _attend_causal([make_score(0), make_score(1)], k_ref, v_ref, m_s, l_s, acc_s, s_a, s_b,
                          tq=tq, qi=pl.program_id(2))
    lam = _diff_lambda(lam_ref)
    for i, o2 in enumerate(outs):
        o = (o2[:, :tq] - lam * o2[:, tq:]).T
        o = _rms(o, gsub_ref[...]) * (1.0 - LAMBDA_INIT)
        o_ref[0, i * tq:(i + 1) * tq, :] = o.astype(o_ref.dtype)


def _causal_attention_call(kernel, qb, k, v, extra, qw, vw, name):
    b, l, _ = qb.shape
    tq, ns = ATTN_TQ, ATTN_STREAMS
    assert l % (ns * tq) == 0
    tile = lambda width: pl.BlockSpec((1, ns * tq, width), lambda bi, h, qi: (bi, qi, h))
    seq = lambda width: pl.BlockSpec((1, l, width), lambda bi, h, qi: (bi, 0, h))
    units = qb.shape[2] // qw
    return pl.pallas_call(
        functools.partial(kernel, tq=tq),
        grid=(b, units, l // (ns * tq)),
        in_specs=[tile(qw), seq(qw), seq(vw)]
                 + [pl.BlockSpec(a.shape, lambda bi, h, qi: (0, 0)) for a in extra],
        out_specs=tile(vw),
        out_shape=jax.ShapeDtypeStruct((b, l, units * vw), BF16),
        scratch_shapes=_softmax_scratch(ns, vw, 2 * tq) + [pltpu.VMEM((ns, tq, 2 * tq), F32)] * 2,
        compiler_params=_params("parallel", "parallel", "arbitrary"),
        name=name,
    )(qb, k, v, *extra)


def _diff_attention(qb, kb, vb, lam_rows, g_sub):
    hw = 2 * DIFF_DH
    return _causal_attention_call(_diff_attn_kernel, qb, kb, vb, [lam_rows, g_sub], hw, hw,
                                  "diff_attention")


def _diff_lambda(lam_ref):
    lp = lam_ref[...]
    return (jnp.exp(jnp.sum(lp[0:1] * lp[1:2], axis=-1, keepdims=True))
            - jnp.exp(jnp.sum(lp[2:3] * lp[3:4], axis=-1, keepdims=True)) + LAMBDA_INIT)


def _diff_query_groups(q):
    lane = lax.broadcasted_iota(jnp.int32, q.shape, 1)
    zero = jnp.zeros_like(q)
    return jnp.concatenate([jnp.where(lane < DIFF_DH, q, zero), jnp.where(lane >= DIFF_DH, q, zero)], axis=0)


def _diff_cached_kernel(q_ref, kc_ref, vc_ref, kn_ref, vn_ref, lam_ref, gsub_ref, o_ref,
                        m_s, l_s, acc_s, *, tk):
    c = pl.program_id(1)
    lq = q_ref.shape[1]
    hw = kc_ref.shape[2]
    heads = q_ref.shape[2] // hw
    update = functools.partial(_softmax_update, m_s, l_s, acc_s)
    q2 = lambda hh: _diff_query_groups(q_ref[0, :, hh * hw:(hh + 1) * hw])

    @pl.when(c == 0)
    def _():
        _softmax_init(m_s, l_s, acc_s)

    def block(j, carry):
        rows = pl.ds(pl.multiple_of(j * tk * heads, tk * heads), tk * heads)
        k_blk, v_blk = kc_ref.at[0, rows, :], vc_ref.at[0, rows, :]
        for hh in range(heads):
            kb = k_blk[pl.ds(hh, tk, stride=heads), :].astype(BF16)
            vb = v_blk[pl.ds(hh, tk, stride=heads), :].astype(BF16)
            update(hh, lax.dot_general(kb, q2(hh), _NT, preferred_element_type=F32), vb)
        return carry

    lax.fori_loop(0, kc_ref.shape[1] // (tk * heads), block, 0)

    @pl.when(c == pl.num_programs(1) - 1)
    def _():
        lam = _diff_lambda(lam_ref)
        for hh in range(heads):
            kb = kn_ref[0, :, hh * hw:(hh + 1) * hw]
            vb = vn_ref[0, :, hh * hw:(hh + 1) * hw]
            update(hh, lax.dot_general(kb, q2(hh), _NT, preferred_element_type=F32), vb)
            o2 = _softmax_result(l_s, acc_s, hh)
            o = (o2[:, :lq] - lam * o2[:, lq:]).T
            o = _rms(o, gsub_ref[...]) * (1.0 - LAMBDA_INIT)
            o_ref[0, :, hh * hw:(hh + 1) * hw] = o.astype(o_ref.dtype)


def _diff_attention_cached(qb, cache_k, cache_v, kb, vb, lam_rows, g_sub):
    b, l, d = qb.shape
    _, p, heads, hw = cache_k.shape
    pc = p if p <= 1024 else 1024
    tk = ATTN_TK if pc % ATTN_TK == 0 else CHUNK
    assert p % pc == 0 and pc % tk == 0
    new = pl.BlockSpec((1, l, d), lambda bi, c: (bi, 0, 0))
    cache = pl.BlockSpec((1, pc * heads, hw), lambda bi, c: (bi, c, 0))
    return pl.pallas_call(
        functools.partial(_diff_cached_kernel, tk=tk),
        grid=(b, p // pc),
        in_specs=[new, cache, cache, new, new,
                  pl.BlockSpec(lam_rows.shape, lambda bi, c: (0, 0)),
                  pl.BlockSpec((1, hw), lambda bi, c: (0, 0))],
        out_specs=new,
        out_shape=jax.ShapeDtypeStruct((b, l, d), BF16),
        scratch_shapes=_softmax_scratch(heads, hw, 2 * l),
        compiler_params=_params("parallel", "arbitrary"),
        name="diff_attention_cached",
    )(qb, cache_k.reshape(b, p * heads, hw), cache_v.reshape(b, p * heads, hw), kb, vb, lam_rows, g_sub)


def _mla_proj_kernel(*refs, cached):
    (x_ref, g_ref, sh_ref, sc_ref, wq_ref, wc_ref, wr_ref, wrr_ref, gq_ref, gkv_ref, kcos_ref, ksin_ref,
     wuq_ref, wuqr_ref, qcos_ref, qsin_ref, *rest) = refs
    nb, lt, d = x_ref.shape
    rows = nb * lt
    dot = lambda a, w_ref: jnp.dot(a, w_ref[...], preferred_element_type=F32)
    h = _modulated(x_ref, g_ref, sh_ref, sc_ref).reshape(rows, d).astype(BF16)
    cq = _rms(dot(h, wq_ref), gq_ref[...]).astype(BF16)
    ckv = _rms(dot(h, wc_ref), gkv_ref[...])
    rope = wr_ref.shape[1]
    kr = (dot(h, wr_ref).reshape(nb, lt, rope) * kcos_ref[...][None]
          + dot(h, wrr_ref).reshape(nb, lt, rope) * ksin_ref[...][None])
    n = wuq_ref.shape[1]
    reps = n // qcos_ref.shape[1]
    cos = jnp.tile(qcos_ref[...], (1, reps))[None]
    sin = jnp.tile(qsin_ref[...], (1, reps))[None]
    q = (dot(cq, wuq_ref).reshape(nb, lt, n) * cos + dot(cq, wuqr_ref).reshape(nb, lt, n) * sin).astype(BF16)
    if cached:
        wabs_ref, ckv_ref, kr_ref, q_ref = rest
        qf = q.reshape(rows, n)
        hp = MLA_HEAD_PAD
        for hh in range(n // hp):
            o = jnp.dot(qf[:, hh * hp:(hh + 1) * hp], wabs_ref[hh], preferred_element_type=F32)
            q_ref[:, hh] = o.reshape(nb, lt, o.shape[1]).astype(q_ref.dtype)
    else:
        wk_ref, we_ref, wv_ref, ckv_ref, kr_ref, q_ref, k_ref, v_ref = rest
        q_ref[...] = q
        c = ckv.astype(BF16)
        k = dot(c, wk_ref) + dot(kr.reshape(rows, rope).astype(BF16), we_ref)
        k_ref[...] = k.reshape(k_ref.shape).astype(k_ref.dtype)
        v_ref[...] = dot(c, wv_ref).reshape(v_ref.shape).astype(v_ref.dtype)
    ckv_ref[...] = ckv.reshape(ckv_ref.shape)
    kr_ref[...] = kr


def _mla_project(x, g, sh, sc, w, k_cos, k_sin, q_cos, q_sin, cached):
    b, l, d = x.shape
    nb, lt = _row_tiles(b, l)
    act, vec, par, pos = _tile_specs(b, l, d, nb, lt)
    weights = [w['mla_dq'], w['mla_dc'], w['mla_dr'], w['mla_drr'], w['mla_g_q'], w['mla_g_kv']]
    q_weights = [w['mla_uq'], w['mla_uq_rot']]
    kvr, rope, n = w['mla_dc'].shape[1], w['mla_dr'].shape[1], w['mla_uq'].shape[1]
    out_specs = [act(kvr), act(rope)]
    out_shape = [jax.ShapeDtypeStruct((b, l, kvr), F32), jax.ShapeDtypeStruct((b, l, rope), F32)]
    if cached:
        tail = [w['mla_abs']]
        heads, _, qw = w['mla_abs'].shape
        out_specs.append(pl.BlockSpec((nb, heads, lt, qw), lambda i, j: (i, 0, j, 0)))
        out_shape.append(jax.ShapeDtypeStruct((b, heads, l, qw), BF16))
    else:
        tail = [w['mla_uk'], w['mla_place'], w['mla_uv']]
        nv = w['mla_uv'].shape[1]
        out_specs += [act(n), act(n), act(nv)]
        out_shape += [jax.ShapeDtypeStruct((b, l, n), BF16)] * 2 + [jax.ShapeDtypeStruct((b, l, nv), BF16)]
    return pl.pallas_call(
        functools.partial(_mla_proj_kernel, cached=cached),
        grid=(b // nb, l // lt),
        in_specs=[act(d), par((1, d)), vec(d), vec(d)] + [par(a.shape) for a in weights]
                 + [pos(rope), pos(rope)] + [par(a.shape) for a in q_weights]
                 + [pos(q_cos.shape[1]), pos(q_sin.shape[1])] + [par(a.shape) for a in tail],
        out_specs=out_specs,
        out_shape=out_shape,
        compiler_params=_params("parallel", "parallel"),
        name="mla_project",
    )(x, g, sh, sc, *weights, k_cos, k_sin, *q_weights, q_cos, q_sin, *tail)


def _mla_attn_kernel(q_ref, k_ref, v_ref, o_ref, m_s, l_s, acc_s, s_a, s_b, *, tq):
    hp = MLA_HEAD_PAD

    def make_score(i):
        qa = q_ref[0, i * tq:(i + 1) * tq, :hp]
        qb = q_ref[0, i * tq:(i + 1) * tq, hp:]
        return lambda kb: jnp.concatenate(
            [lax.dot_general(kb[:, :hp], qa, _NT, preferred_element_type=F32),
             lax.dot_general(kb[:, hp:], qb, _NT, preferred_element_type=F32)], axis=1)

    outs = _attend_causal([make_score(0), make_score(1)], k_ref, v_ref, m_s, l_s, acc_s, s_a, s_b,
                          tq=tq, qi=pl.program_id(2))
    row = lax.broadcasted_iota(jnp.int32, (acc_s.shape[1], tq), 0)
    for i, o2 in enumerate(outs):
        o = jnp.where(row < MLA_V, o2[:, :tq], o2[:, tq:]).T
        o_ref[0, i * tq:(i + 1) * tq, :] = o.astype(o_ref.dtype)


def _mla_attention(qb, kb, vb):
    return _causal_attention_call(_mla_attn_kernel, qb, kb, vb, [], 2 * MLA_HEAD_PAD, 2 * MLA_V,
                                  "mla_attention")


def _mla_cached_kernel(q_ref, cc_ref, rc_ref, cn_ref, rn_ref, wuvt_ref, o_ref, m_s, l_s, acc_s, *, tk):
    heads, lq, qw = q_ref.shape[1:]
    q2 = q_ref[0].reshape(heads * lq, qw)

    def step(c, r):
        rpad = jnp.concatenate([r, jnp.zeros((r.shape[0], qw - c.shape[1] - r.shape[1]), r.dtype)], axis=1)
        kb = jnp.concatenate([c, rpad], axis=1).astype(BF16)
        s = lax.dot_general(kb, q2, _NT, preferred_element_type=F32)
        _softmax_update(m_s, l_s, acc_s, 0, s, c.astype(BF16))

    def block(j, carry):
        rows = pl.ds(pl.multiple_of(j * tk, tk), tk)
        step(cc_ref[0, rows, :], rc_ref[0, rows, :])
        return carry

    _softmax_init(m_s, l_s, acc_s)
    lax.fori_loop(0, cc_ref.shape[1] // tk, block, 0)
    step(cn_ref[0], rn_ref[0])
    lat = _softmax_result(l_s, acc_s, 0)
    assert 2 * lq == LANES and 2 * MLA_V == LANES
    lane = lax.broadcasted_iota(jnp.int32, (lq, LANES), 1)
    for pr in range(heads // 2):
        lp = lat[:, pr * LANES:(pr + 1) * LANES].astype(BF16)
        tt = jnp.dot(wuvt_ref[pr], lp, preferred_element_type=F32).T
        o_ref[0, :, pr * LANES:(pr + 1) * LANES] = jnp.where(lane < MLA_V, tt[:lq], tt[lq:]).astype(o_ref.dtype)


def _mla_attention_cached(q_abs, cache_ckv, cache_krope, ckv, krope, w_uvt):
    b, heads, l, qw = q_abs.shape
    p, rank = cache_ckv.shape[1:]
    rope = cache_krope.shape[2]
    tk = ATTN_TK if p % ATTN_TK == 0 else CHUNK
    assert p % tk == 0
    full = lambda shape: pl.BlockSpec((1,) + shape, lambda bi: (bi,) + (0,) * len(shape))
    return pl.pallas_call(
        functools.partial(_mla_cached_kernel, tk=tk),
        grid=(b,),
        in_specs=[full((heads, l, qw)), full((p, rank)), full((p, rope)), full((l, rank)), full((l, rope)),
                  pl.BlockSpec(w_uvt.shape, lambda bi: (0, 0, 0))],
        out_specs=full((l, heads * MLA_V)),
        out_shape=jax.ShapeDtypeStruct((b, l, heads * MLA_V), BF16),
        scratch_shapes=_softmax_scratch(1, rank, heads * l),
        compiler_params=_params("parallel"),
        name="mla_attention_cached",
    )(q_abs, cache_ckv, cache_krope, ckv, krope, w_uvt)


def _sgu_kernel(x_ref, g_ref, sh_ref, sc_ref, gt_ref, win_ref, gv_ref, ws_ref, bs_ref, wout_ref,
                *out_refs, t, emit_v):
    o_ref = out_refs[0]
    nb, lt, d = x_ref.shape
    rows = nb * lt
    width = gv_ref.shape[1]
    gd = width // SGU_GROUPS
    h = _modulated(x_ref, g_ref, sh_ref, sc_ref).reshape(rows, d).astype(BF16)
    u = jax.nn.gelu(jnp.dot(h, win_ref[:, :width], preferred_element_type=F32))
    v = jax.nn.gelu(jnp.dot(h, win_ref[:, width:], preferred_element_type=F32))
    v = _rms(v, gv_ref[...])
    if emit_v:
        out_refs[1][...] = v.reshape(nb, lt, width)
    vb = v.astype(BF16)
    bias = bs_ref[...]
    gated = []
    for c in range(rows // t):
        sv = [jnp.dot(ws_ref[gi], vb[c * t:(c + 1) * t, gi * gd:(gi + 1) * gd],
                      preferred_element_type=F32) for gi in range(SGU_GROUPS)]
        sv = jnp.concatenate(sv, axis=1) + bias
        gated.append((u[c * t:(c + 1) * t] * sv).astype(BF16))
    gated = jnp.concatenate(gated, axis=0) if len(gated) > 1 else gated[0]
    r = jnp.dot(gated, wout_ref[...], preferred_element_type=F32).reshape(nb, lt, d)
    o_ref[...] = x_ref[...] + (1.0 + gt_ref[...]) * r


def _sgu(x, g, sh, sc, gt, w_in, g_v, w_s, b_full, w_out, emit_v):
    b, l, d = x.shape
    t = w_s.shape[1]
    width = g_v.shape[1]
    nb, lt = _row_tiles(b, l, 512)
    assert lt % t == 0
    act, vec, par, _ = _tile_specs(b, l, d, nb, lt)
    resident = lambda shape: pl.BlockSpec(shape, lambda i, j: (0, 0), pipeline_mode=pl.Buffered(1))
    out_specs = [act(d)]
    out_shape = [jax.ShapeDtypeStruct((b, l, d), F32)]
    if emit_v:
        out_specs.append(act(width))
        out_shape.append(jax.ShapeDtypeStruct((b, l, width), F32))
    return pl.pallas_call(
        functools.partial(_sgu_kernel, t=t, emit_v=emit_v),
        grid=(b // nb, l // lt),
        in_specs=[act(d), par((1, d)), vec(d), vec(d), vec(d), resident(w_in.shape), par((1, width)),
                  par(w_s.shape), par(b_full.shape), resident(w_out.shape)],
        out_specs=out_specs,
        out_shape=out_shape,
        compiler_params=_params("parallel", "parallel"),
        name="sgu",
    )(x, g, sh, sc, gt, w_in, g_v, w_s, b_full, w_out)


def _rope_angles(pos, half):
    inv_freq = ROPE_THETA ** (-jnp.arange(half, dtype=F32) / half)
    ang = pos.astype(F32)[:, None] * inv_freq[None, :]
    return jnp.cos(ang), jnp.sin(ang)


def _diff_rope_tables(pos):
    cos, sin = _rope_angles(pos, DIFF_DH // 2)
    reps = LANES // DIFF_DH
    return (jnp.tile(jnp.concatenate([cos, cos], axis=1), (1, reps)),
            jnp.tile(jnp.concatenate([-sin, sin], axis=1), (1, reps)))


def _mla_rope_tables(pos):
    cos, sin = _rope_angles(pos, MLA_ROPE // 2)
    n = pos.shape[0]
    ones = jnp.ones((n, MLA_NOPE), F32)
    zq = jnp.zeros((n, MLA_NOPE), F32)
    zp = jnp.zeros((n, MLA_HEAD_PAD - MLA_NOPE - MLA_ROPE), F32)
    scale = LOG2_E * MLA_SCALE
    q_cos = jnp.concatenate([ones, cos, cos, zp], axis=1) * scale
    q_sin = jnp.concatenate([zq, -sin, sin, zp], axis=1) * scale
    k_cos = jnp.concatenate([cos, cos], axis=1)
    k_sin = jnp.concatenate([sin, sin], axis=1)
    return q_cos, q_sin, k_cos, k_sin


def _prepare_weights(p):
    d = p['w_up'].shape[1]
    w = {}
    w['w_up'] = [p['w_up'][i].astype(BF16) for i in range(p['w_up'].shape[0])]
    w['w_down'] = [p['w_down'][i].astype(BF16) for i in range(p['w_down'].shape[0])]
    w['glu_a'] = p['s5_w_glu_a'].astype(BF16)
    w['glu_b'] = p['s5_w_glu_b'].astype(BF16)
    w['diff_qkv'] = p['diff_w_qkv'].astype(BF16)
    w['diff_o'] = p['diff_w_o'].astype(BF16)
    lam = jnp.stack([p['diff_lambda_q1'], p['diff_lambda_k1'], p['diff_lambda_q2'], p['diff_lambda_k2']])
    w['diff_lam'] = jnp.pad(lam.astype(F32), ((0, 4), (0, LANES - lam.shape[1])))
    w['diff_g_sub'] = p['diff_g_sub'].reshape(1, -1)

    w['mla_dq'] = p['mla_w_dq'].astype(BF16)
    kvr = p['mla_g_kv'].shape[0]
    w['mla_dc'] = p['mla_w_dkv'][:, :kvr].astype(BF16)
    wr = p['mla_w_dkv'][:, kvr:]
    hr = MLA_ROPE // 2
    w['mla_dr'] = wr.astype(BF16)
    w['mla_drr'] = jnp.concatenate([-wr[:, hr:], wr[:, :hr]], axis=1).astype(BF16)
    w['mla_g_q'] = p['mla_g_q'].reshape(1, -1)
    w['mla_g_kv'] = p['mla_g_kv'].reshape(1, -1)
    pad = MLA_HEAD_PAD - MLA_NOPE - MLA_ROPE
    qr = p['mla_w_uq'].shape[0]
    uq = p['mla_w_uq'].reshape(qr, MLA_HEADS, MLA_NOPE + MLA_ROPE)
    w['mla_uq'] = jnp.pad(uq, ((0, 0), (0, 0), (0, pad))).reshape(qr, -1).astype(BF16)
    uq_rot = jnp.concatenate([jnp.zeros_like(uq[..., :MLA_NOPE]), uq[..., MLA_NOPE + hr:],
                              uq[..., MLA_NOPE:MLA_NOPE + hr]], axis=2)
    w['mla_uq_rot'] = jnp.pad(uq_rot, ((0, 0), (0, 0), (0, pad))).reshape(qr, -1).astype(BF16)
    uk = jnp.pad(p['mla_w_uk'], ((0, 0), (0, 0), (0, MLA_HEAD_PAD - MLA_NOPE)))
    w['mla_uk'] = uk.reshape(kvr, -1).astype(BF16)
    place = jnp.pad(jnp.eye(MLA_ROPE, dtype=F32), ((0, 0), (MLA_NOPE, pad)))
    w['mla_place'] = jnp.tile(place, (1, MLA_HEADS)).astype(BF16)
    w['mla_uv'] = p['mla_w_uv'].reshape(kvr, -1).astype(BF16)
    absorb = jnp.pad(p['mla_w_uk'].transpose(1, 2, 0), ((0, 0), (0, MLA_HEAD_PAD - MLA_NOPE), (0, 0)))
    select = jnp.pad(jnp.eye(MLA_ROPE, dtype=F32), ((MLA_NOPE, pad), (0, MLA_HEAD_PAD - MLA_ROPE)))
    w['mla_abs'] = jnp.concatenate(
        [absorb, jnp.broadcast_to(select, (MLA_HEADS,) + select.shape)], axis=2).astype(BF16)
    w['mla_uvt'] = p['mla_w_uv'].transpose(1, 2, 0).reshape(MLA_HEADS // 2, 2 * MLA_V, kvr).astype(BF16)
    w['mla_o'] = p['mla_w_o'].astype(BF16)

    w['sgu_in'] = p['sgu_w_in'].astype(BF16)
    w['sgu_g_v'] = p['sgu_g_v'].reshape(1, -1)
    w['sgu_out'] = p['sgu_w_out'].astype(BF16)
    return w


def _sgu_spatial(w_s, b_s, t, width):
    ws = jnp.tril(w_s[:, :t, :t]).astype(BF16)
    gd = width // SGU_GROUPS
    b_full = jnp.repeat(b_s[:, :t].T.astype(F32), gd, axis=1)
    return ws, b_full


def _run_trunk(x, mods, pos, past, p, w, s5_tables):
    b, l, d = x.shape
    new = {}
    depth = p['g_mix'].shape[0]
    for i in range(depth):
        sh1, sc1, gt1, sh2, sc2, gt2 = [mods[i][:, k][:, None, :] for k in range(6)]
        g_mix = p['g_mix'][i].reshape(1, d)
        kind = i % N_MIXERS
        mixer = None
        if kind == 0:
            h = _modulate(x, g_mix, sh1, sc1)
            h0_re = None if past is None else past['s5_re']
            h0_im = None if past is None else past['s5_im']
            y, new['s5_re'], new['s5_im'] = _s5_scan(h, s5_tables, h0_re, h0_im)
            x = _s5_glu(y, h, x, p['s5_d'].reshape(1, d), gt1, w['glu_a'], w['glu_b'])
        elif kind == 1:
            cos, sin = _diff_rope_tables(pos)
            k, v, qb, kb, vb = _diff_qkv(x, g_mix, sh1, sc1, w['diff_qkv'], cos, sin)
            if past is None:
                o = _diff_attention(qb, kb, vb, w['diff_lam'], w['diff_g_sub'])
            else:
                o = _diff_attention_cached(qb, past['diff_k'], past['diff_v'], kb, vb,
                                           w['diff_lam'], w['diff_g_sub'])
            mixer = (o, w['diff_o'], gt1)
            new['diff_k'] = k.reshape(b, l, DIFF_HEADS, 2 * DIFF_DH)
            new['diff_v'] = v.reshape(b, l, DIFF_HEADS, 2 * DIFF_DH)
        elif kind == 2:
            q_cos, q_sin, k_cos, k_sin = _mla_rope_tables(pos)
            proj = _mla_project(x, g_mix, sh1, sc1, w, k_cos, k_sin, q_cos, q_sin, cached=past is not None)
            ckv, krope = proj[:2]
            if past is None:
                o = _mla_attention(*proj[2:])
            else:
                o = _mla_attention_cached(proj[2], past['mla_ckv'], past['mla_krope'], ckv, krope,
                                          w['mla_uvt'])
            mixer = (o, w['mla_o'], gt1)
            new['mla_ckv'], new['mla_krope'] = ckv, krope
        else:
            t = min(l, SGU_CHUNK)
            ws, b_full = _sgu_spatial(p['sgu_w_s'], p['sgu_b_s'], t, w['sgu_g_v'].shape[1])
            outs = _sgu(x, g_mix, sh1, sc1, gt1, w['sgu_in'], w['sgu_g_v'], ws, b_full, w['sgu_out'],
                        emit_v=past is not None)
            x = outs[0]
            if past is not None:
                new['sgu_v'] = outs[1]
        x = _mlp(x, p['g_ffn'][i].reshape(1, d), sh2, sc2, gt2, w['w_up'][i], w['w_down'][i],
                 p['g_final'].reshape(1, d), final_norm=(i == depth - 1), mixer=mixer)
    return x, new


def kernel(x_prompt, x_sample, c_prompt, c_sample, state_s5_re, state_s5_im, cache_diff_k, cache_diff_v, cache_mla_ckv, cache_mla_krope, w_ada, b_ada, g_mix, g_ffn, w_up, w_down, g_final, s5_a_re, s5_a_im, s5_b_re, s5_b_im, s5_c_re, s5_c_im, s5_d, s5_log_dt, s5_w_glu_a, s5_w_glu_b, diff_w_qkv, diff_lambda_q1, diff_lambda_k1, diff_lambda_q2, diff_lambda_k2, diff_g_sub, diff_w_o, mla_w_dq, mla_g_q, mla_w_uq, mla_w_dkv, mla_g_kv, mla_w_uk, mla_w_uv, mla_w_o, sgu_w_in, sgu_g_v, sgu_w_s, sgu_b_s, sgu_w_out):
    p = {
        'w_ada': w_ada, 'b_ada': b_ada, 'g_mix': g_mix, 'g_ffn': g_ffn,
        'w_up': w_up, 'w_down': w_down, 'g_final': g_final,
        's5_d': s5_d, 's5_w_glu_a': s5_w_glu_a, 's5_w_glu_b': s5_w_glu_b,
        'diff_w_qkv': diff_w_qkv, 'diff_lambda_q1': diff_lambda_q1, 'diff_lambda_k1': diff_lambda_k1,
        'diff_lambda_q2': diff_lambda_q2, 'diff_lambda_k2': diff_lambda_k2,
        'diff_g_sub': diff_g_sub, 'diff_w_o': diff_w_o,
        'mla_w_dq': mla_w_dq, 'mla_g_q': mla_g_q, 'mla_w_uq': mla_w_uq, 'mla_w_dkv': mla_w_dkv,
        'mla_g_kv': mla_g_kv, 'mla_w_uk': mla_w_uk, 'mla_w_uv': mla_w_uv, 'mla_w_o': mla_w_o,
        'sgu_w_in': sgu_w_in, 'sgu_g_v': sgu_g_v, 'sgu_w_s': sgu_w_s, 'sgu_b_s': sgu_b_s,
        'sgu_w_out': sgu_w_out,
    }
    past = {
        's5_re': state_s5_re, 's5_im': state_s5_im,
        'diff_k': cache_diff_k, 'diff_v': cache_diff_v,
        'mla_ckv': cache_mla_ckv, 'mla_krope': cache_mla_krope,
    }
    bp, lp, d = x_prompt.shape
    bs, ls, _ = x_sample.shape
    depth = w_ada.shape[0]

    c_all = jnp.concatenate([c_prompt, c_sample], axis=0)
    c_all = jnp.pad(c_all, ((0, (-c_all.shape[0]) % 8), (0, 0)))
    mod = _adaln(c_all, w_ada, b_ada).reshape(depth, c_all.shape[0], 6, d)
    mods_p = [mod[i, :bp] for i in range(depth)]
    mods_s = [mod[i, bp:bp + bs] for i in range(depth)]

    w = _prepare_weights(p)
    max_steps = max((lp // S5_SUB).bit_length() - 1, (ls // S5_SUB).bit_length() - 1, 1)
    s5_tables = _s5_tables(s5_a_re, s5_a_im, s5_b_re, s5_b_im, s5_c_re, s5_c_im, s5_log_dt, max_steps)

    pos_p = jnp.arange(lp, dtype=jnp.int32)
    pos_s = cache_diff_k.shape[1] + jnp.arange(ls, dtype=jnp.int32)
    y_prompt, sp = _run_trunk(x_prompt, mods_p, pos_p, None, p, w, s5_tables)
    y_sample, ss = _run_trunk(x_sample, mods_s, pos_s, past, p, w, s5_tables)
    return (y_prompt, y_sample,
            sp['s5_re'], sp['s5_im'], ss['s5_re'], ss['s5_im'],
            sp['diff_k'], sp['diff_v'], ss['diff_k'], ss['diff_v'],
            sp['mla_ckv'], sp['mla_krope'], ss['mla_ckv'], ss['mla_krope'],
            ss['sgu_v'])
```

```python
import functools
import math

import jax
import jax.numpy as jnp
from jax import lax
from jax.experimental import pallas as pl
from jax.experimental.pallas import tpu as pltpu

F32 = jnp.float32
BF16 = jnp.bfloat16
HIGHEST = lax.Precision.HIGHEST

NORM_EPS = 1e-6
ROPE_THETA = 10000.0
NEG_INF = -1e30
LOG2_E = math.log2(math.e)
CHUNK = 64
N_MIXERS = 4

S5_GROUP = 16
S5_SUB = 16

DIFF_HEADS = 8
DIFF_DH = 64
LAMBDA_INIT = 0.8 - 0.6 * math.exp(-0.3 * 1)

MLA_HEADS = 16
MLA_NOPE = 64
MLA_ROPE = 32
MLA_V = 64
MLA_SCALE = (MLA_NOPE + MLA_ROPE) ** -0.5
MLA_HEAD_PAD = 128

SGU_CHUNK = 128
SGU_GROUPS = 8

LANES = 128
ROW_TILE = 512
ATTN_TQ = 512
ATTN_STREAMS = 2
ATTN_TK = 512
VMEM_LIMIT_BYTES = 48 * 1024 * 1024


def _params(*sem):
    return pltpu.CompilerParams(dimension_semantics=sem, vmem_limit_bytes=VMEM_LIMIT_BYTES)


def _row_tiles(batch, length, rows=ROW_TILE):
    if length >= rows:
        assert length % rows == 0
        return 1, rows
    nb = max(1, min(batch, rows // length))
    while batch % nb:
        nb -= 1
    return nb, length


def _rms(x, g):
    return x * lax.rsqrt(jnp.mean(x * x, axis=-1, keepdims=True) + NORM_EPS) * g


def _modulated(x_ref, g_ref, sh_ref, sc_ref):
    return _rms(x_ref[...], g_ref[...]) * (1.0 + sc_ref[...]) + sh_ref[...]


def _rotate_pairs(x, half, period, lo):
    width = x.shape[-1]
    lane = lax.broadcasted_iota(jnp.int32, x.shape, x.ndim - 1) % period
    fwd = pltpu.roll(x, width - half, axis=x.ndim - 1)
    bwd = pltpu.roll(x, half, axis=x.ndim - 1)
    return jnp.where((lane >= lo) & (lane < lo + half), fwd, bwd)


def _adaln_kernel(c_ref, w_ref, b_ref, o_ref):
    c = c_ref[...]
    s = c * jax.nn.sigmoid(c)
    o_ref[0] = jnp.dot(s, w_ref[0], precision=HIGHEST, preferred_element_type=F32) + b_ref[0]


def _adaln(c_all, w_ada, b_ada):
    depth, d, n = w_ada.shape
    rows = c_all.shape[0]
    tn = n // 4
    return pl.pallas_call(
        _adaln_kernel,
        grid=(depth, n // tn),
        in_specs=[pl.BlockSpec((rows, d), lambda i, j: (0, 0)),
                  pl.BlockSpec((1, d, tn), lambda i, j: (i, 0, j)),
                  pl.BlockSpec((1, 1, tn), lambda i, j: (i, 0, j))],
        out_specs=pl.BlockSpec((1, rows, tn), lambda i, j: (i, 0, j)),
        out_shape=jax.ShapeDtypeStruct((depth, rows, n), F32),
        compiler_params=_params("parallel", "parallel"),
        name="adaln",
    )(c_all, w_ada, b_ada.reshape(depth, 1, n))


def _tile_specs(batch, length, d, nb, lt, extra_axes=0):
    if extra_axes == 0:
        act = lambda width: pl.BlockSpec((nb, lt, width), lambda i, j: (i, j, 0))
        vec = lambda width: pl.BlockSpec((nb, 1, width), lambda i, j: (i, 0, 0))
        par = lambda shape: pl.BlockSpec(shape, lambda i, j: (0,) * len(shape))
        pos = lambda width: pl.BlockSpec((lt, width), lambda i, j: (j, 0))
    else:
        act = lambda width: pl.BlockSpec((nb, lt, width), lambda i, j, f: (i, j, 0))
        vec = lambda width: pl.BlockSpec((nb, 1, width), lambda i, j, f: (i, 0, 0))
        par = lambda shape: pl.BlockSpec(shape, lambda i, j, f: (0,) * len(shape))
        pos = lambda width: pl.BlockSpec((lt, width), lambda i, j, f: (j, 0))
    return act, vec, par, pos


def _modulate_kernel(x_ref, g_ref, sh_ref, sc_ref, o_ref):
    o_ref[...] = _modulated(x_ref, g_ref, sh_ref, sc_ref)


def _modulate(x, g, sh, sc):
    b, l, d = x.shape
    nb, lt = _row_tiles(b, l)
    act, vec, par, _ = _tile_specs(b, l, d, nb, lt)
    return pl.pallas_call(
        _modulate_kernel,
        grid=(b // nb, l // lt),
        in_specs=[act(d), par((1, d)), vec(d), vec(d)],
        out_specs=act(d),
        out_shape=jax.ShapeDtypeStruct((b, l, d), F32),
        compiler_params=_params("parallel", "parallel"),
        name="modulate",
    )(x, g, sh, sc)


def _mlp_kernel(*refs, final_norm, tf, mixer):
    if mixer == 'proj':
        o_in_ref, wo_ref, gt1_ref, *refs = refs
    elif mixer == 'glu':
        y_ref, h_ref, dskip_ref, wa_ref, wb_ref, gt1_ref, *refs = refs
    x_ref, g_ref, sh_ref, sc_ref, gt_ref, wu_ref, wd_ref, gf_ref, o_ref = refs
    nb, lt, d = x_ref.shape
    x = x_ref[...]
    if mixer == 'proj':
        r = jnp.dot(o_in_ref[...].reshape(nb * lt, o_in_ref.shape[2]), wo_ref[...],
                    preferred_element_type=F32)
        x = x + (1.0 + gt1_ref[...]) * r.reshape(nb, lt, d)
    elif mixer == 'glu':
        z = jax.nn.gelu(y_ref[...] + dskip_ref[...] * h_ref[...]).reshape(nb * lt, d).astype(BF16)
        r = (jnp.dot(z, wa_ref[...], preferred_element_type=F32)
             * jax.nn.sigmoid(jnp.dot(z, wb_ref[...], preferred_element_type=F32)))
        x = x + (1.0 + gt1_ref[...]) * r.reshape(nb, lt, d)
    h = (_rms(x, g_ref[...]) * (1.0 + sc_ref[...]) + sh_ref[...]).reshape(nb * lt, d).astype(BF16)
    acc = None
    for f in range(wu_ref.shape[1] // tf):
        a = jnp.dot(h, wu_ref[:, f * tf:(f + 1) * tf], preferred_element_type=F32)
        a = jnp.square(jnp.maximum(a, 0.0)).astype(BF16)
        r = jnp.dot(a, wd_ref[f * tf:(f + 1) * tf, :], preferred_element_type=F32)
        acc = r if acc is None else acc + r
    y = x + (1.0 + gt_ref[...]) * acc.reshape(nb, lt, d)
    if final_norm:
        y = _rms(y, gf_ref[...])
    o_ref[...] = y


def _mlp(x, g, sh, sc, gt, w_up, w_down, g_final, final_norm, mixer=None, tf=2048):
    b, l, d = x.shape
    dff = w_up.shape[1]
    nb, lt = _row_tiles(b, l)
    act, vec, par, _ = _tile_specs(b, l, d, nb, lt)
    resident = lambda shape: pl.BlockSpec(shape, lambda i, j: (0, 0), pipeline_mode=pl.Buffered(1))
    args = [x, g, sh, sc, gt, w_up, w_down, g_final]
    in_specs = [act(d), par((1, d)), vec(d), vec(d), vec(d),
                resident((d, dff)), resident((dff, d)), par((1, d))]
    kind = None
    if mixer is not None:
        kind, *head = mixer
        if kind == 'proj':
            o, w_o, _ = head
            head_specs = [act(o.shape[2]), resident(w_o.shape), vec(d)]
        else:
            head_specs = [act(d), act(d), par((1, d)), resident((d, d)), resident((d, d)), vec(d)]
        args = head + args
        in_specs = head_specs + in_specs
    return pl.pallas_call(
        functools.partial(_mlp_kernel, final_norm=final_norm, tf=tf, mixer=kind),
        grid=(b // nb, l // lt),
        in_specs=in_specs,
        out_specs=act(d),
        out_shape=jax.ShapeDtypeStruct((b, l, d), F32),
        compiler_params=_params("parallel", "parallel"),
        name="mlp",
    )(*args)


def _s5_kernel(*refs, nch, batch, nsteps, has_h0):
    if has_h0:
        u_ref, bc_ref, m_ref, cc_ref, a1_ref, a2_ref, h0_ref, y_ref, hl_ref = refs
    else:
        u_ref, bc_ref, m_ref, cc_ref, a1_ref, a2_ref, y_ref, hl_ref = refs
    u = u_ref[0]
    rows = u.shape[0]
    x = jnp.dot(u, bc_ref[0], precision=HIGHEST, preferred_element_type=F32)
    half = x.shape[1] // 2
    if has_h0:
        h0 = h0_ref[0]
        x = x + a1_ref[0, 0:1, :] * h0 + a2_ref[0, 0:1, :] * pltpu.roll(h0, half, axis=1)
    kidx = lax.broadcasted_iota(jnp.int32, x.shape, 0) & (nch - 1)
    for j in range(nsteps):
        s = 1 << j
        xs = jnp.where(kidx >= s, pltpu.roll(x, s, axis=0), 0.0)
        x = x + a1_ref[0, j:j + 1, :] * xs + a2_ref[0, j:j + 1, :] * pltpu.roll(xs, half, axis=1)
    for b in range(batch):
        r = (b + 1) * nch - 1
        hl_ref[0, b:b + 1, :] = x[r:r + 1, :]
    hstart = jnp.where(kidx >= 1, pltpu.roll(x, 1, axis=0), 0.0)
    if has_h0:
        hstart = hstart + h0
    y = jnp.dot(u.astype(BF16), m_ref[0], preferred_element_type=F32)
    y = y + jnp.dot(hstart.astype(BF16), cc_ref[0], preferred_element_type=F32)
    y_ref[0] = y


def _s5_tables(a_re, a_im, b_re, b_im, c_re, c_im, log_dt, max_steps):
    g, p = a_re.shape
    sub = S5_SUB
    dt = jnp.exp(log_dt.astype(F32))[:, None]
    zr, zi = a_re.astype(F32) * dt, a_im.astype(F32) * dt

    def zpow(n):
        mag = jnp.exp(zr * n)
        return mag * jnp.cos(zi * n), mag * jnp.sin(zi * n)

    er = jnp.expm1(zr) * jnp.cos(zi) - 2.0 * jnp.square(jnp.sin(0.5 * zi))
    ei = jnp.exp(zr) * jnp.sin(zi)
    den = a_re * a_re + a_im * a_im
    fr = (er * a_re + ei * a_im) / den
    fi = (ei * a_re - er * a_im) / den
    bb_re = fr[..., None] * b_re - fi[..., None] * b_im
    bb_im = fr[..., None] * b_im + fi[..., None] * b_re

    lags = jnp.arange(sub + 1, dtype=F32)[:, None, None]
    pr, pi = zpow(lags)

    qr, qi = pr[:sub][::-1], pi[:sub][::-1]
    bcr = qr[..., None] * bb_re[None] - qi[..., None] * bb_im[None]
    bci = qr[..., None] * bb_im[None] + qi[..., None] * bb_re[None]
    bc = jnp.concatenate([bcr, bci], axis=2)
    bc = bc.transpose(1, 0, 3, 2).reshape(g, sub * S5_GROUP, 2 * p)

    ar, ai = pr[1:], pi[1:]
    ccr = c_re[None] * ar[:, :, None, :] - c_im[None] * ai[:, :, None, :]
    cci = c_re[None] * ai[:, :, None, :] + c_im[None] * ar[:, :, None, :]
    cc = jnp.concatenate([ccr, -cci], axis=3)
    cc = cc.transpose(1, 3, 0, 2).reshape(g, 2 * p, sub * S5_GROUP)

    kr = (jnp.einsum('gcp,lgp,gpd->lgcd', c_re, pr[:sub], bb_re, precision=HIGHEST)
          - jnp.einsum('gcp,lgp,gpd->lgcd', c_re, pi[:sub], bb_im, precision=HIGHEST)
          - jnp.einsum('gcp,lgp,gpd->lgcd', c_im, pr[:sub], bb_im, precision=HIGHEST)
          - jnp.einsum('gcp,lgp,gpd->lgcd', c_im, pi[:sub], bb_re, precision=HIGHEST))
    tt = jnp.arange(sub)
    lag = tt[None, :] - tt[:, None]
    toep = jnp.where((lag >= 0)[:, :, None, None, None],
                     kr[jnp.clip(lag, 0, sub - 1)], 0.0)
    m = toep.transpose(2, 0, 4, 1, 3).reshape(g, sub * S5_GROUP, sub * S5_GROUP)

    steps = (sub * (2 ** jnp.arange(max_steps))).astype(F32)[:, None, None]
    sr, si = zpow(steps)
    a1 = jnp.concatenate([sr, sr], axis=-1).transpose(1, 0, 2)
    a2 = jnp.concatenate([-si, si], axis=-1).transpose(1, 0, 2)
    pad = (-max_steps) % 8
    a1 = jnp.pad(a1, ((0, 0), (0, pad), (0, 0)))
    a2 = jnp.pad(a2, ((0, 0), (0, pad), (0, 0)))
    return bc, m.astype(BF16), cc.astype(BF16), a1, a2


def _s5_lane_kernel(h_ref, bct_ref, mt_ref, cct_ref, a1_ref, a2_ref, y_ref, hl_ref, ut_s, yt_s, hl_s,
                    *, nch, nsteps):
    sub, c = S5_SUB, S5_GROUP
    groups = ut_s.shape[0]
    for t in range(sub):
        att = h_ref[0, pl.ds(t, nch, stride=sub), :].T
        for g in range(groups):
            ut_s[g, t * c:(t + 1) * c, :] = att[g * c:(g + 1) * c, :]
    half = bct_ref.shape[1] // 2

    def group(g, carry):
        lane = lax.broadcasted_iota(jnp.int32, (2 * half, nch), 1)
        ut = ut_s[g]
        x = jnp.dot(bct_ref[g], ut, precision=HIGHEST, preferred_element_type=F32)
        a1, a2 = a1_ref[g], a2_ref[g]
        for j in range(nsteps):
            s = 1 << j
            xs = jnp.where(lane >= s, pltpu.roll(x, s, axis=1), 0.0)
            x = x + a1[:, j:j + 1] * xs + a2[:, j:j + 1] * pltpu.roll(xs, half, axis=0)
        hl_s[g] = x[:, nch - LANES:].T[LANES - 8:, :]
        hstart = jnp.where(lane >= 1, pltpu.roll(x, 1, axis=1), 0.0)
        yt = jnp.dot(mt_ref[g], ut.astype(BF16), preferred_element_type=F32)
        yt = yt + jnp.dot(cct_ref[g], hstart.astype(BF16), preferred_element_type=F32)
        rows = pl.ds(pl.multiple_of(g * c, c), c)
        for t in range(sub):
            yt_s[t, rows, :] = yt[t * c:(t + 1) * c, :]
        return carry

    lax.fori_loop(0, groups, group, 0)
    for g in range(groups):
        hl_ref[0, g:g + 1, :] = hl_s[g, 7:8, :]
    for t in range(sub):
        y_ref[0, pl.ds(t, nch, stride=sub), :] = yt_s[t].T


def _s5_scan_lanes(h, tables):
    bc, m, cc, a1, a2 = tables
    b, l, d = h.shape
    g, k, p2 = bc.shape
    nch = l // S5_SUB
    gpb = LANES // S5_GROUP
    nsteps = nch.bit_length() - 1
    wspec = lambda shape: pl.BlockSpec((gpb,) + shape, lambda j, bi: (j, 0, 0))
    act = pl.BlockSpec((1, l, LANES), lambda j, bi: (bi, 0, j))
    return pl.pallas_call(
        functools.partial(_s5_lane_kernel, nch=nch, nsteps=nsteps),
        grid=(g // gpb, b),
        in_specs=[act, wspec((p2, k)), wspec((k, k)), wspec((k, p2)),
                  wspec((p2, a1.shape[1])), wspec((p2, a2.shape[1]))],
        out_specs=[act, pl.BlockSpec((1, gpb, p2), lambda j, bi: (bi, j, 0))],
        out_shape=[jax.ShapeDtypeStruct((b, l, d), F32), jax.ShapeDtypeStruct((b, g, p2), F32)],
        scratch_shapes=[pltpu.VMEM((gpb, k, nch), F32), pltpu.VMEM((S5_SUB, LANES, nch), F32),
                        pltpu.VMEM((gpb, 8, p2), F32)],
        compiler_params=_params("parallel", "parallel"),
        name="s5_scan_lanes",
    )(h, bc.transpose(0, 2, 1), m.transpose(0, 2, 1), cc.transpose(0, 2, 1),
      a1.transpose(0, 2, 1), a2.transpose(0, 2, 1))


def _s5_scan(h, tables, h0_re, h0_im):
    bc, m, cc, a1, a2 = tables
    b, l, d = h.shape
    if h0_re is None and (l // S5_SUB) % LANES == 0:
        y, hl = _s5_scan_lanes(h, tables)
        return y, hl[..., :hl.shape[-1] // 2], hl[..., hl.shape[-1] // 2:]
    g = d // S5_GROUP
    p2 = bc.shape[-1]
    nch = l // S5_SUB
    assert nch & (nch - 1) == 0
    nsteps = nch.bit_length() - 1
    rows = b * nch
    k = S5_SUB * S5_GROUP
    u = h.reshape(b, nch, S5_SUB, g, S5_GROUP).transpose(3, 0, 1, 2, 4).reshape(g, rows, k)
    has_h0 = h0_re is not None
    grp = lambda shape: pl.BlockSpec((1,) + shape, lambda i: (i, 0, 0))
    args = [u, bc, m, cc, a1, a2]
    in_specs = [grp((rows, k)), grp((k, p2)), grp((k, k)), grp((p2, k)),
                grp(a1.shape[1:]), grp(a2.shape[1:])]
    if has_h0:
        h0 = jnp.concatenate([h0_re, h0_im], axis=-1).astype(F32).transpose(1, 0, 2)
        h0 = jnp.pad(h0[:, :, None, :], ((0, 0), (0, 0), (0, nch - 1), (0, 0))).reshape(g, rows, p2)
        args.append(h0)
        in_specs.append(grp((rows, p2)))
    y, hl = pl.pallas_call(
        functools.partial(_s5_kernel, nch=nch, batch=b, nsteps=nsteps, has_h0=has_h0),
        grid=(g,),
        in_specs=in_specs,
        out_specs=[grp((rows, k)), grp((b, p2))],
        out_shape=[jax.ShapeDtypeStruct((g, rows, k), F32), jax.ShapeDtypeStruct((g, b, p2), F32)],
        compiler_params=_params("parallel"),
        name="s5_scan",
    )(*args)
    y = y.reshape(g, b, nch, S5_SUB, S5_GROUP).transpose(1, 2, 3, 0, 4).reshape(b, l, d)
    hl = hl.transpose(1, 0, 2)
    return y, hl[..., :p2 // 2], hl[..., p2 // 2:]


def _diff_qkv_kernel(x_ref, g_ref, sh_ref, sc_ref, w_ref, cos_ref, sin_ref,
                     k_ref, v_ref, qb_ref, kb_ref, vb_ref):
    nb, lt, d = x_ref.shape
    h = _modulated(x_ref, g_ref, sh_ref, sc_ref).reshape(nb * lt, d).astype(BF16)
    reps = d // cos_ref.shape[1]
    cos = jnp.tile(cos_ref[...], (1, reps))[None]
    sin = jnp.tile(sin_ref[...], (1, reps))[None]

    def roped(cols):
        t = jnp.dot(h, w_ref[:, cols * d:(cols + 1) * d], preferred_element_type=F32)
        r = _rotate_pairs(t, DIFF_DH // 2, DIFF_DH, 0)
        return t.reshape(nb, lt, d) * cos + r.reshape(nb, lt, d) * sin

    q = roped(0)
    qb_ref[...] = (q * (LOG2_E * DIFF_DH ** -0.5)).astype(BF16)
    k = roped(1)
    kb_ref[...] = k.astype(BF16)
    v = jnp.dot(h, w_ref[:, 2 * d:], preferred_element_type=F32).reshape(nb, lt, d)
    vb_ref[...] = v.astype(BF16)
    hw = k_ref.shape[2]
    heads = d // hw
    for hh in range(heads):
        k_ref[:, pl.ds(hh, lt, stride=heads), :] = k[:, :, hh * hw:(hh + 1) * hw]
        v_ref[:, pl.ds(hh, lt, stride=heads), :] = v[:, :, hh * hw:(hh + 1) * hw]


def _diff_qkv(x, g, sh, sc, w_qkv, cos, sin):
    b, l, d = x.shape
    nb, lt = _row_tiles(b, l)
    act, vec, par, pos = _tile_specs(b, l, d, nb, lt)
    hw = d // DIFF_HEADS
    return pl.pallas_call(
        _diff_qkv_kernel,
        grid=(b // nb, l // lt),
        in_specs=[act(d), par((1, d)), vec(d), vec(d), par((d, 3 * d)),
                  pos(cos.shape[1]), pos(sin.shape[1])],
        out_specs=[pl.BlockSpec((nb, lt * DIFF_HEADS, hw), lambda i, j: (i, j, 0))] * 2 + [act(d)] * 3,
        out_shape=[jax.ShapeDtypeStruct((b, l * DIFF_HEADS, hw), F32)] * 2
                  + [jax.ShapeDtypeStruct((b, l, d), BF16)] * 3,
        compiler_params=_params("parallel", "parallel"),
        name="diff_qkv",
    )(x, g, sh, sc, w_qkv, cos, sin)


_NT = (((1,), (1,)), ((), ()))
_TN = (((0,), (0,)), ((), ()))


def _softmax_update(m_s, l_s, acc_s, i, s, vb):
    m_prev = m_s[i]
    m_new = jnp.maximum(m_prev, jnp.max(s, axis=0, keepdims=True))
    alpha = jnp.exp2(m_prev - m_new)
    p = jnp.exp2(s - m_new)
    l_s[i] = alpha * l_s[i] + jnp.sum(p, axis=0, keepdims=True)
    acc_s[i] = alpha * acc_s[i] + lax.dot_general(vb, p.astype(BF16), _TN, preferred_element_type=F32)
    m_s[i] = m_new


def _softmax_init(m_s, l_s, acc_s):
    m_s[...] = jnp.full_like(m_s, NEG_INF)
    l_s[...] = jnp.zeros_like(l_s)
    acc_s[...] = jnp.zeros_like(acc_s)


def _softmax_result(l_s, acc_s, i):
    return acc_s[i] / l_s[i]


def _softmax_scratch(n, vw, queries):
    return [pltpu.VMEM((n, 1, queries), F32), pltpu.VMEM((n, 1, queries), F32),
            pltpu.VMEM((n, vw, queries), F32)]


def _attend_causal(scores, k_ref, v_ref, m_s, l_s, acc_s, s_a, s_b, *, tq, qi):
    assert len(scores) == 2 and tq % CHUNK == 0
    _softmax_init(m_s, l_s, acc_s)
    update = functools.partial(_softmax_update, m_s, l_s, acc_s)
    block = lambda ref, j: ref[0, pl.ds(pl.multiple_of(j * tq, tq), tq), :].astype(BF16)

    def qk(dst, j, streams):
        kb = block(k_ref, j)
        for i in streams:
            dst[i] = scores[i](kb)

    def pv(src, j, streams, masked=None):
        vb = block(v_ref, j)
        for i in streams:
            update(i, jnp.where(vis, src[i], NEG_INF) if i == masked else src[i], vb)

    def pair(jj, carry):
        qk(s_b, 2 * jj + 1, (0, 1))
        pv(s_a, 2 * jj, (0, 1))
        qk(s_a, 2 * jj + 2, (0, 1))
        pv(s_b, 2 * jj + 1, (0, 1))
        return carry

    qk(s_a, 0, (0, 1))
    lax.fori_loop(0, qi, pair, 0)
    kc = lax.broadcasted_iota(jnp.int32, (tq, tq), 0) // CHUNK
    qc = lax.broadcasted_iota(jnp.int32, (tq, tq), 1) // CHUNK
    vis = jnp.concatenate([kc <= qc] * 2, axis=1)
    qk(s_b, 2 * qi + 1, (1,))
    pv(s_a, 2 * qi, (0, 1), masked=0)
    pv(s_b, 2 * qi + 1, (1,), masked=1)
    return [_softmax_result(l_s, acc_s, i) for i in range(2)]


def _diff_attn_kernel(q_ref, k_ref, v_ref, lam_ref, gsub_ref, o_ref, m_s, l_s, acc_s, s_a, s_b, *, tq):
    def make_score(i):
        q2 = _diff_query_groups(q_ref[0, i * tq:(i + 1) * tq, :])
        return lambda kb: lax.dot_general(kb, q2, _NT, preferred_element_type=F32)

    outs = _attend_causal([make_score(0), make_score(1)], k_ref, v_ref, m_s, l_s, acc_s, s_a, s_b,
                          tq=tq, qi=pl.program_id(2))
    lam = _diff_lambda(lam_ref)
    for i, o2 in enumerate(outs):
        o = (o2[:, :tq] - lam * o2[:, tq:]).T
        o = _rms(o, gsub_ref[...]) * (1.0 - LAMBDA_INIT)
        o_ref[0, i * tq:(i + 1) * tq, :] = o.astype(o_ref.dtype)


def _causal_attention_call(kernel, qb, k, v, extra, qw, vw, name):
    b, l, _ = qb.shape
    tq, ns = ATTN_TQ, ATTN_STREAMS
    assert l % (ns * tq) == 0
    tile = lambda width: pl.BlockSpec((1, ns * tq, width), lambda bi, h, qi: (bi, qi, h))
    seq = lambda width: pl.BlockSpec((1, l, width), lambda bi, h, qi: (bi, 0, h))
    units = qb.shape[2] // qw
    return pl.pallas_call(
        functools.partial(kernel, tq=tq),
        grid=(b, units, l // (ns * tq)),
        in_specs=[tile(qw), seq(qw), seq(vw)]
                 + [pl.BlockSpec(a.shape, lambda bi, h, qi: (0, 0)) for a in extra],
        out_specs=tile(vw),
        out_shape=jax.ShapeDtypeStruct((b, l, units * vw), BF16),
        scratch_shapes=_softmax_scratch(ns, vw, 2 * tq) + [pltpu.VMEM((ns, tq, 2 * tq), F32)] * 2,
        compiler_params=_params("parallel", "parallel", "arbitrary"),
        name=name,
    )(qb, k, v, *extra)


def _diff_attention(qb, kb, vb, lam_rows, g_sub):
    hw = 2 * DIFF_DH
    return _causal_attention_call(_diff_attn_kernel, qb, kb, vb, [lam_rows, g_sub], hw, hw,
                                  "diff_attention")


def _diff_lambda(lam_ref):
    lp = lam_ref[...]
    return (jnp.exp(jnp.sum(lp[0:1] * lp[1:2], axis=-1, keepdims=True))
            - jnp.exp(jnp.sum(lp[2:3] * lp[3:4], axis=-1, keepdims=True)) + LAMBDA_INIT)


def _diff_query_groups(q):
    lane = lax.broadcasted_iota(jnp.int32, q.shape, 1)
    zero = jnp.zeros_like(q)
    return jnp.concatenate([jnp.where(lane < DIFF_DH, q, zero), jnp.where(lane >= DIFF_DH, q, zero)], axis=0)


def _diff_cached_kernel(q_ref, kc_ref, vc_ref, kn_ref, vn_ref, lam_ref, gsub_ref, o_ref,
                        m_s, l_s, acc_s, *, tk):
    c = pl.program_id(1)
    lq = q_ref.shape[1]
    hw = kc_ref.shape[2]
    heads = q_ref.shape[2] // hw
    update = functools.partial(_softmax_update, m_s, l_s, acc_s)
    q2 = lambda hh: _diff_query_groups(q_ref[0, :, hh * hw:(hh + 1) * hw])

    @pl.when(c == 0)
    def _():
        _softmax_init(m_s, l_s, acc_s)

    def block(j, carry):
        rows = pl.ds(pl.multiple_of(j * tk * heads, tk * heads), tk * heads)
        k_blk, v_blk = kc_ref.at[0, rows, :], vc_ref.at[0, rows, :]
        for hh in range(heads):
            kb = k_blk[pl.ds(hh, tk, stride=heads), :].astype(BF16)
            vb = v_blk[pl.ds(hh, tk, stride=heads), :].astype(BF16)
            update(hh, lax.dot_general(kb, q2(hh), _NT, preferred_element_type=F32), vb)
        return carry

    lax.fori_loop(0, kc_ref.shape[1] // (tk * heads), block, 0)

    @pl.when(c == pl.num_programs(1) - 1)
    def _():
        lam = _diff_lambda(lam_ref)
        for hh in range(heads):
            kb = kn_ref[0, :, hh * hw:(hh + 1) * hw]
            vb = vn_ref[0, :, hh * hw:(hh + 1) * hw]
            update(hh, lax.dot_general(kb, q2(hh), _NT, preferred_element_type=F32), vb)
            o2 = _softmax_result(l_s, acc_s, hh)
            o = (o2[:, :lq] - lam * o2[:, lq:]).T
            o = _rms(o, gsub_ref[...]) * (1.0 - LAMBDA_INIT)
            o_ref[0, :, hh * hw:(hh + 1) * hw] = o.astype(o_ref.dtype)


def _diff_attention_cached(qb, cache_k, cache_v, kb, vb, lam_rows, g_sub):
    b, l, d = qb.shape
    _, p, heads, hw = cache_k.shape
    pc = p if p <= 1024 else 1024
    tk = ATTN_TK if pc % ATTN_TK == 0 else CHUNK
    assert p % pc == 0 and pc % tk == 0
    new = pl.BlockSpec((1, l, d), lambda bi, c: (bi, 0, 0))
    cache = pl.BlockSpec((1, pc * heads, hw), lambda bi, c: (bi, c, 0))
    return pl.pallas_call(
        functools.partial(_diff_cached_kernel, tk=tk),
        grid=(b, p // pc),
        in_specs=[new, cache, cache, new, new,
                  pl.BlockSpec(lam_rows.shape, lambda bi, c: (0, 0)),
                  pl.BlockSpec((1, hw), lambda bi, c: (0, 0))],
        out_specs=new,
        out_shape=jax.ShapeDtypeStruct((b, l, d), BF16),
        scratch_shapes=_softmax_scratch(heads, hw, 2 * l),
        compiler_params=_params("parallel", "arbitrary"),
        name="diff_attention_cached",
    )(qb, cache_k.reshape(b, p * heads, hw), cache_v.reshape(b, p * heads, hw), kb, vb, lam_rows, g_sub)


def _mla_proj_kernel(*refs, cached):
    (x_ref, g_ref, sh_ref, sc_ref, wq_ref, wc_ref, wr_ref, wrr_ref, gq_ref, gkv_ref, kcos_ref, ksin_ref,
     wuq_ref, wuqr_ref, qcos_ref, qsin_ref, *rest) = refs
    nb, lt, d = x_ref.shape
    rows = nb * lt
    dot = lambda a, w_ref: jnp.dot(a, w_ref[...], preferred_element_type=F32)
    h = _modulated(x_ref, g_ref, sh_ref, sc_ref).reshape(rows, d).astype(BF16)
    cq = _rms(dot(h, wq_ref), gq_ref[...]).astype(BF16)
    ckv = _rms(dot(h, wc_ref), gkv_ref[...])
    rope = wr_ref.shape[1]
    kr = (dot(h, wr_ref).reshape(nb, lt, rope) * kcos_ref[...][None]
          + dot(h, wrr_ref).reshape(nb, lt, rope) * ksin_ref[...][None])
    n = wuq_ref.shape[1]
    reps = n // qcos_ref.shape[1]
    cos = jnp.tile(qcos_ref[...], (1, reps))[None]
    sin = jnp.tile(qsin_ref[...], (1, reps))[None]
    q = (dot(cq, wuq_ref).reshape(nb, lt, n) * cos + dot(cq, wuqr_ref).reshape(nb, lt, n) * sin).astype(BF16)
    if cached:
        wabs_ref, ckv_ref, kr_ref, q_ref = rest
        qf = q.reshape(rows, n)
        hp = MLA_HEAD_PAD
        for hh in range(n // hp):
            o = jnp.dot(qf[:, hh * hp:(hh + 1) * hp], wabs_ref[hh], preferred_element_type=F32)
            q_ref[:, hh] = o.reshape(nb, lt, o.shape[1]).astype(q_ref.dtype)
    else:
        wk_ref, we_ref, wv_ref, ckv_ref, kr_ref, q_ref, k_ref, v_ref = rest
        q_ref[...] = q
        c = ckv.astype(BF16)
        k = dot(c, wk_ref) + dot(kr.reshape(rows, rope).astype(BF16), we_ref)
        k_ref[...] = k.reshape(k_ref.shape).astype(k_ref.dtype)
        v_ref[...] = dot(c, wv_ref).reshape(v_ref.shape).astype(v_ref.dtype)
    ckv_ref[...] = ckv.reshape(ckv_ref.shape)
    kr_ref[...] = kr


def _mla_project(x, g, sh, sc, w, k_cos, k_sin, q_cos, q_sin, cached):
    b, l, d = x.shape
    nb, lt = _row_tiles(b, l)
    act, vec, par, pos = _tile_specs(b, l, d, nb, lt)
    weights = [w['mla_dq'], w['mla_dc'], w['mla_dr'], w['mla_drr'], w['mla_g_q'], w['mla_g_kv']]
    q_weights = [w['mla_uq'], w['mla_uq_rot']]
    kvr, rope, n = w['mla_dc'].shape[1], w['mla_dr'].shape[1], w['mla_uq'].shape[1]
    out_specs = [act(kvr), act(rope)]
    out_shape = [jax.ShapeDtypeStruct((b, l, kvr), F32), jax.ShapeDtypeStruct((b, l, rope), F32)]
    if cached:
        tail = [w['mla_abs']]
        heads, _, qw = w['mla_abs'].shape
        out_specs.append(pl.BlockSpec((nb, heads, lt, qw), lambda i, j: (i, 0, j, 0)))
        out_shape.append(jax.ShapeDtypeStruct((b, heads, l, qw), BF16))
    else:
        tail = [w['mla_uk'], w['mla_place'], w['mla_uv']]
        nv = w['mla_uv'].shape[1]
        out_specs += [act(n), act(n), act(nv)]
        out_shape += [jax.ShapeDtypeStruct((b, l, n), BF16)] * 2 + [jax.ShapeDtypeStruct((b, l, nv), BF16)]
    return pl.pallas_call(
        functools.partial(_mla_proj_kernel, cached=cached),
        grid=(b // nb, l // lt),
        in_specs=[act(d), par((1, d)), vec(d), vec(d)] + [par(a.shape) for a in weights]
                 + [pos(rope), pos(rope)] + [par(a.shape) for a in q_weights]
                 + [pos(q_cos.shape[1]), pos(q_sin.shape[1])] + [par(a.shape) for a in tail],
        out_specs=out_specs,
        out_shape=out_shape,
        compiler_params=_params("parallel", "parallel"),
        name="mla_project",
    )(x, g, sh, sc, *weights, k_cos, k_sin, *q_weights, q_cos, q_sin, *tail)


def _mla_attn_kernel(q_ref, k_ref, v_ref, o_ref, m_s, l_s, acc_s, s_a, s_b, *, tq):
    hp = MLA_HEAD_PAD

    def make_score(i):
        qa = q_ref[0, i * tq:(i + 1) * tq, :hp]
        qb = q_ref[0, i * tq:(i + 1) * tq, hp:]
        return lambda kb: jnp.concatenate(
            [lax.dot_general(kb[:, :hp], qa, _NT, preferred_element_type=F32),
             lax.dot_general(kb[:, hp:], qb, _NT, preferred_element_type=F32)], axis=1)

    outs = _attend_causal([make_score(0), make_score(1)], k_ref, v_ref, m_s, l_s, acc_s, s_a, s_b,
                          tq=tq, qi=pl.program_id(2))
    row = lax.broadcasted_iota(jnp.int32, (acc_s.shape[1], tq), 0)
    for i, o2 in enumerate(outs):
        o = jnp.where(row < MLA_V, o2[:, :tq], o2[:, tq:]).T
        o_ref[0, i * tq:(i + 1) * tq, :] = o.astype(o_ref.dtype)


def _mla_attention(qb, kb, vb):
    return _causal_attention_call(_mla_attn_kernel, qb, kb, vb, [], 2 * MLA_HEAD_PAD, 2 * MLA_V,
                                  "mla_attention")


def _mla_cached_kernel(q_ref, cc_ref, rc_ref, cn_ref, rn_ref, wuvt_ref, o_ref, m_s, l_s, acc_s, *, tk):
    heads, lq, qw = q_ref.shape[1:]
    q2 = q_ref[0].reshape(heads * lq, qw)

    def step(c, r):
        rpad = jnp.concatenate([r, jnp.zeros((r.shape[0], qw - c.shape[1] - r.shape[1]), r.dtype)], axis=1)
        kb = jnp.concatenate([c, rpad], axis=1).astype(BF16)
        s = lax.dot_general(kb, q2, _NT, preferred_element_type=F32)
        _softmax_update(m_s, l_s, acc_s, 0, s, c.astype(BF16))

    def block(j, carry):
        rows = pl.ds(pl.multiple_of(j * tk, tk), tk)
        step(cc_ref[0, rows, :], rc_ref[0, rows, :])
        return carry

    _softmax_init(m_s, l_s, acc_s)
    lax.fori_loop(0, cc_ref.shape[1] // tk, block, 0)
    step(cn_ref[0], rn_ref[0])
    lat = _softmax_result(l_s, acc_s, 0)
    assert 2 * lq == LANES and 2 * MLA_V == LANES
    lane = lax.broadcasted_iota(jnp.int32, (lq, LANES), 1)
    for pr in range(heads // 2):
        lp = lat[:, pr * LANES:(pr + 1) * LANES].astype(BF16)
        tt = jnp.dot(wuvt_ref[pr], lp, preferred_element_type=F32).T
        o_ref[0, :, pr * LANES:(pr + 1) * LANES] = jnp.where(lane < MLA_V, tt[:lq], tt[lq:]).astype(o_ref.dtype)


def _mla_attention_cached(q_abs, cache_ckv, cache_krope, ckv, krope, w_uvt):
    b, heads, l, qw = q_abs.shape
    p, rank = cache_ckv.shape[1:]
    rope = cache_krope.shape[2]
    tk = ATTN_TK if p % ATTN_TK == 0 else CHUNK
    assert p % tk == 0
    full = lambda shape: pl.BlockSpec((1,) + shape, lambda bi: (bi,) + (0,) * len(shape))
    return pl.pallas_call(
        functools.partial(_mla_cached_kernel, tk=tk),
        grid=(b,),
        in_specs=[full((heads, l, qw)), full((p, rank)), full((p, rope)), full((l, rank)), full((l, rope)),
                  pl.BlockSpec(w_uvt.shape, lambda bi: (0, 0, 0))],
        out_specs=full((l, heads * MLA_V)),
        out_shape=jax.ShapeDtypeStruct((b, l, heads * MLA_V), BF16),
        scratch_shapes=_softmax_scratch(1, rank, heads * l),
        compiler_params=_params("parallel"),
        name="mla_attention_cached",
    )(q_abs, cache_ckv, cache_krope, ckv, krope, w_uvt)


def _sgu_kernel(x_ref, g_ref, sh_ref, sc_ref, gt_ref, win_ref, gv_ref, ws_ref, bs_ref, wout_ref,
                *out_refs, t, emit_v):
    o_ref = out_refs[0]
    nb, lt, d = x_ref.shape
    rows = nb * lt
    width = gv_ref.shape[1]
    gd = width // SGU_GROUPS
    h = _modulated(x_ref, g_ref, sh_ref, sc_ref).reshape(rows, d).astype(BF16)
    u = jax.nn.gelu(jnp.dot(h, win_ref[:, :width], preferred_element_type=F32))
    v = jax.nn.gelu(jnp.dot(h, win_ref[:, width:], preferred_element_type=F32))
    v = _rms(v, gv_ref[...])
    if emit_v:
        out_refs[1][...] = v.reshape(nb, lt, width)
    vb = v.astype(BF16)
    bias = bs_ref[...]
    gated = []
    for c in range(rows // t):
        sv = [jnp.dot(ws_ref[gi], vb[c * t:(c + 1) * t, gi * gd:(gi + 1) * gd],
                      preferred_element_type=F32) for gi in range(SGU_GROUPS)]
        sv = jnp.concatenate(sv, axis=1) + bias
        gated.append((u[c * t:(c + 1) * t] * sv).astype(BF16))
    gated = jnp.concatenate(gated, axis=0) if len(gated) > 1 else gated[0]
    r = jnp.dot(gated, wout_ref[...], preferred_element_type=F32).reshape(nb, lt, d)
    o_ref[...] = x_ref[...] + (1.0 + gt_ref[...]) * r


def _sgu(x, g, sh, sc, gt, w_in, g_v, w_s, b_full, w_out, emit_v):
    b, l, d = x.shape
    t = w_s.shape[1]
    width = g_v.shape[1]
    nb, lt = _row_tiles(b, l, 512)
    assert lt % t == 0
    act, vec, par, _ = _tile_specs(b, l, d, nb, lt)
    resident = lambda shape: pl.BlockSpec(shape, lambda i, j: (0, 0), pipeline_mode=pl.Buffered(1))
    out_specs = [act(d)]
    out_shape = [jax.ShapeDtypeStruct((b, l, d), F32)]
    if emit_v:
        out_specs.append(act(width))
        out_shape.append(jax.ShapeDtypeStruct((b, l, width), F32))
    return pl.pallas_call(
        functools.partial(_sgu_kernel, t=t, emit_v=emit_v),
        grid=(b // nb, l // lt),
        in_specs=[act(d), par((1, d)), vec(d), vec(d), vec(d), resident(w_in.shape), par((1, width)),
                  par(w_s.shape), par(b_full.shape), resident(w_out.shape)],
        out_specs=out_specs,
        out_shape=out_shape,
        compiler_params=_params("parallel", "parallel"),
        name="sgu",
    )(x, g, sh, sc, gt, w_in, g_v, w_s, b_full, w_out)


def _rope_angles(pos, half):
    inv_freq = ROPE_THETA ** (-jnp.arange(half, dtype=F32) / half)
    ang = pos.astype(F32)[:, None] * inv_freq[None, :]
    return jnp.cos(ang), jnp.sin(ang)


def _diff_rope_tables(pos):
    cos, sin = _rope_angles(pos, DIFF_DH // 2)
    reps = LANES // DIFF_DH
    return (jnp.tile(jnp.concatenate([cos, cos], axis=1), (1, reps)),
            jnp.tile(jnp.concatenate([-sin, sin], axis=1), (1, reps)))


def _mla_rope_tables(pos):
    cos, sin = _rope_angles(pos, MLA_ROPE // 2)
    n = pos.shape[0]
    ones = jnp.ones((n, MLA_NOPE), F32)
    zq = jnp.zeros((n, MLA_NOPE), F32)
    zp = jnp.zeros((n, MLA_HEAD_PAD - MLA_NOPE - MLA_ROPE), F32)
    scale = LOG2_E * MLA_SCALE
    q_cos = jnp.concatenate([ones, cos, cos, zp], axis=1) * scale
    q_sin = jnp.concatenate([zq, -sin, sin, zp], axis=1) * scale
    k_cos = jnp.concatenate([cos, cos], axis=1)
    k_sin = jnp.concatenate([sin, sin], axis=1)
    return q_cos, q_sin, k_cos, k_sin


def _prepare_weights(p):
    d = p['w_up'].shape[1]
    w = {}
    w['w_up'] = [p['w_up'][i].astype(BF16) for i in range(p['w_up'].shape[0])]
    w['w_down'] = [p['w_down'][i].astype(BF16) for i in range(p['w_down'].shape[0])]
    w['glu_a'] = p['s5_w_glu_a'].astype(BF16)
    w['glu_b'] = p['s5_w_glu_b'].astype(BF16)
    w['diff_qkv'] = p['diff_w_qkv'].astype(BF16)
    w['diff_o'] = p['diff_w_o'].astype(BF16)
    lam = jnp.stack([p['diff_lambda_q1'], p['diff_lambda_k1'], p['diff_lambda_q2'], p['diff_lambda_k2']])
    w['diff_lam'] = jnp.pad(lam.astype(F32), ((0, 4), (0, LANES - lam.shape[1])))
    w['diff_g_sub'] = p['diff_g_sub'].reshape(1, -1)

    w['mla_dq'] = p['mla_w_dq'].astype(BF16)
    kvr = p['mla_g_kv'].shape[0]
    w['mla_dc'] = p['mla_w_dkv'][:, :kvr].astype(BF16)
    wr = p['mla_w_dkv'][:, kvr:]
    hr = MLA_ROPE // 2
    w['mla_dr'] = wr.astype(BF16)
    w['mla_drr'] = jnp.concatenate([-wr[:, hr:], wr[:, :hr]], axis=1).astype(BF16)
    w['mla_g_q'] = p['mla_g_q'].reshape(1, -1)
    w['mla_g_kv'] = p['mla_g_kv'].reshape(1, -1)
    pad = MLA_HEAD_PAD - MLA_NOPE - MLA_ROPE
    qr = p['mla_w_uq'].shape[0]
    uq = p['mla_w_uq'].reshape(qr, MLA_HEADS, MLA_NOPE + MLA_ROPE)
    w['mla_uq'] = jnp.pad(uq, ((0, 0), (0, 0), (0, pad))).reshape(qr, -1).astype(BF16)
    uq_rot = jnp.concatenate([jnp.zeros_like(uq[..., :MLA_NOPE]), uq[..., MLA_NOPE + hr:],
                              uq[..., MLA_NOPE:MLA_NOPE + hr]], axis=2)
    w['mla_uq_rot'] = jnp.pad(uq_rot, ((0, 0), (0, 0), (0, pad))).reshape(qr, -1).astype(BF16)
    uk = jnp.pad(p['mla_w_uk'], ((0, 0), (0, 0), (0, MLA_HEAD_PAD - MLA_NOPE)))
    w['mla_uk'] = uk.reshape(kvr, -1).astype(BF16)
    place = jnp.pad(jnp.eye(MLA_ROPE, dtype=F32), ((0, 0), (MLA_NOPE, pad)))
    w['mla_place'] = jnp.tile(place, (1, MLA_HEADS)).astype(BF16)
    w['mla_uv'] = p['mla_w_uv'].reshape(kvr, -1).astype(BF16)
    absorb = jnp.pad(p['mla_w_uk'].transpose(1, 2, 0), ((0, 0), (0, MLA_HEAD_PAD - MLA_NOPE), (0, 0)))
    select = jnp.pad(jnp.eye(MLA_ROPE, dtype=F32), ((MLA_NOPE, pad), (0, MLA_HEAD_PAD - MLA_ROPE)))
    w['mla_abs'] = jnp.concatenate(
        [absorb, jnp.broadcast_to(select, (MLA_HEADS,) + select.shape)], axis=2).astype(BF16)
    w['mla_uvt'] = p['mla_w_uv'].transpose(1, 2, 0).reshape(MLA_HEADS // 2, 2 * MLA_V, kvr).astype(BF16)
    w['mla_o'] = p['mla_w_o'].astype(BF16)

    w['sgu_in'] = p['sgu_w_in'].astype(BF16)
    w['sgu_g_v'] = p['sgu_g_v'].reshape(1, -1)
    w['sgu_out'] = p['sgu_w_out'].astype(BF16)
    return w


def _sgu_spatial(w_s, b_s, t, width):
    ws = jnp.tril(w_s[:, :t, :t]).astype(BF16)
    gd = width // SGU_GROUPS
    b_full = jnp.repeat(b_s[:, :t].T.astype(F32), gd, axis=1)
    return ws, b_full


def _run_trunk(x, mods, pos, past, p, w, s5_tables):
    b, l, d = x.shape
    new = {}
    depth = p['g_mix'].shape[0]
    for i in range(depth):
        sh1, sc1, gt1, sh2, sc2, gt2 = [mods[i][:, k][:, None, :] for k in range(6)]
        g_mix = p['g_mix'][i].reshape(1, d)
        kind = i % N_MIXERS
        mixer = None
        if kind == 0:
            h = _modulate(x, g_mix, sh1, sc1)
            h0_re = None if past is None else past['s5_re']
            h0_im = None if past is None else past['s5_im']
            y, new['s5_re'], new['s5_im'] = _s5_scan(h, s5_tables, h0_re, h0_im)
            mixer = ('glu', y, h, p['s5_d'].reshape(1, d), w['glu_a'], w['glu_b'], gt1)
        elif kind == 1:
            cos, sin = _diff_rope_tables(pos)
            k, v, qb, kb, vb = _diff_qkv(x, g_mix, sh1, sc1, w['diff_qkv'], cos, sin)
            if past is None:
                o = _diff_attention(qb, kb, vb, w['diff_lam'], w['diff_g_sub'])
            else:
                o = _diff_attention_cached(qb, past['diff_k'], past['diff_v'], kb, vb,
                                           w['diff_lam'], w['diff_g_sub'])
            mixer = ('proj', o, w['diff_o'], gt1)
            new['diff_k'] = k.reshape(b, l, DIFF_HEADS, 2 * DIFF_DH)
            new['diff_v'] = v.reshape(b, l, DIFF_HEADS, 2 * DIFF_DH)
        elif kind == 2:
            q_cos, q_sin, k_cos, k_sin = _mla_rope_tables(pos)
            proj = _mla_project(x, g_mix, sh1, sc1, w, k_cos, k_sin, q_cos, q_sin, cached=past is not None)
            ckv, krope = proj[:2]
            if past is None:
                o = _mla_attention(*proj[2:])
            else:
                o = _mla_attention_cached(proj[2], past['mla_ckv'], past['mla_krope'], ckv, krope,
                                          w['mla_uvt'])
            mixer = ('proj', o, w['mla_o'], gt1)
            new['mla_ckv'], new['mla_krope'] = ckv, krope
        else:
            t = min(l, SGU_CHUNK)
            ws, b_full = _sgu_spatial(p['sgu_w_s'], p['sgu_b_s'], t, w['sgu_g_v'].shape[1])
            outs = _sgu(x, g_mix, sh1, sc1, gt1, w['sgu_in'], w['sgu_g_v'], ws, b_full, w['sgu_out'],
                        emit_v=past is not None)
            x = outs[0]
            if past is not None:
                new['sgu_v'] = outs[1]
        x = _mlp(x, p['g_ffn'][i].reshape(1, d), sh2, sc2, gt2, w['w_up'][i], w['w_down'][i],
                 p['g_final'].reshape(1, d), final_norm=(i == depth - 1), mixer=mixer)
    return x, new


def kernel(x_prompt, x_sample, c_prompt, c_sample, state_s5_re, state_s5_im, cache_diff_k, cache_diff_v, cache_mla_ckv, cache_mla_krope, w_ada, b_ada, g_mix, g_ffn, w_up, w_down, g_final, s5_a_re, s5_a_im, s5_b_re, s5_b_im, s5_c_re, s5_c_im, s5_d, s5_log_dt, s5_w_glu_a, s5_w_glu_b, diff_w_qkv, diff_lambda_q1, diff_lambda_k1, diff_lambda_q2, diff_lambda_k2, diff_g_sub, diff_w_o, mla_w_dq, mla_g_q, mla_w_uq, mla_w_dkv, mla_g_kv, mla_w_uk, mla_w_uv, mla_w_o, sgu_w_in, sgu_g_v, sgu_w_s, sgu_b_s, sgu_w_out):
    p = {
        'w_ada': w_ada, 'b_ada': b_ada, 'g_mix': g_mix, 'g_ffn': g_ffn,
        'w_up': w_up, 'w_down': w_down, 'g_final': g_final,
        's5_d': s5_d, 's5_w_glu_a': s5_w_glu_a, 's5_w_glu_b': s5_w_glu_b,
        'diff_w_qkv': diff_w_qkv, 'diff_lambda_q1': diff_lambda_q1, 'diff_lambda_k1': diff_lambda_k1,
        'diff_lambda_q2': diff_lambda_q2, 'diff_lambda_k2': diff_lambda_k2,
        'diff_g_sub': diff_g_sub, 'diff_w_o': diff_w_o,
        'mla_w_dq': mla_w_dq, 'mla_g_q': mla_g_q, 'mla_w_uq': mla_w_uq, 'mla_w_dkv': mla_w_dkv,
        'mla_g_kv': mla_g_kv, 'mla_w_uk': mla_w_uk, 'mla_w_uv': mla_w_uv, 'mla_w_o': mla_w_o,
        'sgu_w_in': sgu_w_in, 'sgu_g_v': sgu_g_v, 'sgu_w_s': sgu_w_s, 'sgu_b_s': sgu_b_s,
        'sgu_w_out': sgu_w_out,
    }
    past = {
        's5_re': state_s5_re, 's5_im': state_s5_im,
        'diff_k': cache_diff_k, 'diff_v': cache_diff_v,
        'mla_ckv': cache_mla_ckv, 'mla_krope': cache_mla_krope,
    }
    bp, lp, d = x_prompt.shape
    bs, ls, _ = x_sample.shape
    depth = w_ada.shape[0]

    c_all = jnp.concatenate([c_prompt, c_sample], axis=0)
    c_all = jnp.pad(c_all, ((0, (-c_all.shape[0]) % 8), (0, 0)))
    mod = _adaln(c_all, w_ada, b_ada).reshape(depth, c_all.shape[0], 6, d)
    mods_p = [mod[i, :bp] for i in range(depth)]
    mods_s = [mod[i, bp:bp + bs] for i in range(depth)]

    w = _prepare_weights(p)
    max_steps = max((lp // S5_SUB).bit_length() - 1, (ls // S5_SUB).bit_length() - 1, 1)
    s5_tables = _s5_tables(s5_a_re, s5_a_im, s5_b_re, s5_b_im, s5_c_re, s5_c_im, s5_log_dt, max_steps)

    pos_p = jnp.arange(lp, dtype=jnp.int32)
    pos_s = cache_diff_k.shape[1] + jnp.arange(ls, dtype=jnp.int32)
    y_prompt, sp = _run_trunk(x_prompt, mods_p, pos_p, None, p, w, s5_tables)
    y_sample, ss = _run_trunk(x_sample, mods_s, pos_s, past, p, w, s5_tables)
    return (y_prompt, y_sample,
            sp['s5_re'], sp['s5_im'], ss['s5_re'], ss['s5_im'],
            sp['diff_k'], sp['diff_v'], ss['diff_k'], ss['diff_v'],
            sp['mla_ckv'], sp['mla_krope'], ss['mla_ckv'], ss['mla_krope'],
            ss['sgu_v'])
```

```python
import functools
import math

import jax
import jax.numpy as jnp
from jax import lax
from jax.experimental import pallas as pl
from jax.experimental.pallas import tpu as pltpu

F32 = jnp.float32
BF16 = jnp.bfloat16
HIGHEST = lax.Precision.HIGHEST

NORM_EPS = 1e-6
ROPE_THETA = 10000.0
NEG_INF = -1e30
LOG2_E = math.log2(math.e)
CHUNK = 64
N_MIXERS = 4

S5_GROUP = 16
S5_SUB = 16

DIFF_HEADS = 8
DIFF_DH = 64
LAMBDA_INIT = 0.8 - 0.6 * math.exp(-0.3 * 1)

MLA_HEADS = 16
MLA_NOPE = 64
MLA_ROPE = 32
MLA_V = 64
MLA_SCALE = (MLA_NOPE + MLA_ROPE) ** -0.5
MLA_HEAD_PAD = 128

SGU_CHUNK = 128
SGU_GROUPS = 8

LANES = 128
ROW_TILE = 512
ATTN_TQ = 512
ATTN_STREAMS = 2
ATTN_LANE_BLOCK = 256
ATTN_TK = 512
VMEM_LIMIT_BYTES = 48 * 1024 * 1024


def _params(*sem):
    return pltpu.CompilerParams(dimension_semantics=sem, vmem_limit_bytes=VMEM_LIMIT_BYTES)


def _row_tiles(batch, length, rows=ROW_TILE):
    if length >= rows:
        assert length % rows == 0
        return 1, rows
    nb = max(1, min(batch, rows // length))
    while batch % nb:
        nb -= 1
    return nb, length


def _rms(x, g):
    return x * lax.rsqrt(jnp.mean(x * x, axis=-1, keepdims=True) + NORM_EPS) * g


def _modulated(x_ref, g_ref, sh_ref, sc_ref):
    return _rms(x_ref[...], g_ref[...]) * (1.0 + sc_ref[...]) + sh_ref[...]


def _rotate_pairs(x, half, period, lo):
    width = x.shape[-1]
    lane = lax.broadcasted_iota(jnp.int32, x.shape, x.ndim - 1) % period
    fwd = pltpu.roll(x, width - half, axis=x.ndim - 1)
    bwd = pltpu.roll(x, half, axis=x.ndim - 1)
    return jnp.where((lane >= lo) & (lane < lo + half), fwd, bwd)


def _adaln_kernel(c_ref, w_ref, b_ref, o_ref):
    c = c_ref[...]
    s = c * jax.nn.sigmoid(c)
    o_ref[0] = jnp.dot(s, w_ref[0], precision=HIGHEST, preferred_element_type=F32) + b_ref[0]


def _adaln(c_all, w_ada, b_ada):
    depth, d, n = w_ada.shape
    rows = c_all.shape[0]
    tn = n // 4
    return pl.pallas_call(
        _adaln_kernel,
        grid=(depth, n // tn),
        in_specs=[pl.BlockSpec((rows, d), lambda i, j: (0, 0)),
                  pl.BlockSpec((1, d, tn), lambda i, j: (i, 0, j)),
                  pl.BlockSpec((1, 1, tn), lambda i, j: (i, 0, j))],
        out_specs=pl.BlockSpec((1, rows, tn), lambda i, j: (i, 0, j)),
        out_shape=jax.ShapeDtypeStruct((depth, rows, n), F32),
        compiler_params=_params("parallel", "parallel"),
        name="adaln",
    )(c_all, w_ada, b_ada.reshape(depth, 1, n))


def _tile_specs(batch, length, d, nb, lt, extra_axes=0):
    if extra_axes == 0:
        act = lambda width: pl.BlockSpec((nb, lt, width), lambda i, j: (i, j, 0))
        vec = lambda width: pl.BlockSpec((nb, 1, width), lambda i, j: (i, 0, 0))
        par = lambda shape: pl.BlockSpec(shape, lambda i, j: (0,) * len(shape))
        pos = lambda width: pl.BlockSpec((lt, width), lambda i, j: (j, 0))
    else:
        act = lambda width: pl.BlockSpec((nb, lt, width), lambda i, j, f: (i, j, 0))
        vec = lambda width: pl.BlockSpec((nb, 1, width), lambda i, j, f: (i, 0, 0))
        par = lambda shape: pl.BlockSpec(shape, lambda i, j, f: (0,) * len(shape))
        pos = lambda width: pl.BlockSpec((lt, width), lambda i, j, f: (j, 0))
    return act, vec, par, pos


def _modulate_kernel(x_ref, g_ref, sh_ref, sc_ref, o_ref):
    o_ref[...] = _modulated(x_ref, g_ref, sh_ref, sc_ref)


def _modulate(x, g, sh, sc):
    b, l, d = x.shape
    nb, lt = _row_tiles(b, l)
    act, vec, par, _ = _tile_specs(b, l, d, nb, lt)
    return pl.pallas_call(
        _modulate_kernel,
        grid=(b // nb, l // lt),
        in_specs=[act(d), par((1, d)), vec(d), vec(d)],
        out_specs=act(d),
        out_shape=jax.ShapeDtypeStruct((b, l, d), F32),
        compiler_params=_params("parallel", "parallel"),
        name="modulate",
    )(x, g, sh, sc)


def _mlp_kernel(*refs, final_norm, tf, mixer):
    if mixer == 'proj':
        o_in_ref, wo_ref, gt1_ref, *refs = refs
    elif mixer == 'glu':
        y_ref, h_ref, dskip_ref, wa_ref, wb_ref, gt1_ref, *refs = refs
    x_ref, g_ref, sh_ref, sc_ref, gt_ref, wu_ref, wd_ref, gf_ref, o_ref = refs
    nb, lt, d = x_ref.shape
    x = x_ref[...]
    if mixer == 'proj':
        r = jnp.dot(o_in_ref[...].reshape(nb * lt, o_in_ref.shape[2]), wo_ref[...],
                    preferred_element_type=F32)
        x = x + (1.0 + gt1_ref[...]) * r.reshape(nb, lt, d)
    elif mixer == 'glu':
        z = jax.nn.gelu(y_ref[...] + dskip_ref[...] * h_ref[...]).reshape(nb * lt, d).astype(BF16)
        r = (jnp.dot(z, wa_ref[...], preferred_element_type=F32)
             * jax.nn.sigmoid(jnp.dot(z, wb_ref[...], preferred_element_type=F32)))
        x = x + (1.0 + gt1_ref[...]) * r.reshape(nb, lt, d)
    h = (_rms(x, g_ref[...]) * (1.0 + sc_ref[...]) + sh_ref[...]).reshape(nb * lt, d).astype(BF16)
    acc = None
    for f in range(wu_ref.shape[1] // tf):
        a = jnp.dot(h, wu_ref[:, f * tf:(f + 1) * tf], preferred_element_type=F32)
        a = jnp.square(jnp.maximum(a, 0.0)).astype(BF16)
        r = jnp.dot(a, wd_ref[f * tf:(f + 1) * tf, :], preferred_element_type=F32)
        acc = r if acc is None else acc + r
    y = x + (1.0 + gt_ref[...]) * acc.reshape(nb, lt, d)
    if final_norm:
        y = _rms(y, gf_ref[...])
    o_ref[...] = y


def _mlp(x, g, sh, sc, gt, w_up, w_down, g_final, final_norm, mixer=None, tf=2048):
    b, l, d = x.shape
    dff = w_up.shape[1]
    nb, lt = _row_tiles(b, l)
    act, vec, par, _ = _tile_specs(b, l, d, nb, lt)
    resident = lambda shape: pl.BlockSpec(shape, lambda i, j: (0, 0), pipeline_mode=pl.Buffered(1))
    args = [x, g, sh, sc, gt, w_up, w_down, g_final]
    in_specs = [act(d), par((1, d)), vec(d), vec(d), vec(d),
                resident((d, dff)), resident((dff, d)), par((1, d))]
    kind = None
    if mixer is not None:
        kind, *head = mixer
        if kind == 'proj':
            o, w_o, _ = head
            head_specs = [act(o.shape[2]), resident(w_o.shape), vec(d)]
        else:
            head_specs = [act(d), act(d), par((1, d)), resident((d, d)), resident((d, d)), vec(d)]
        args = head + args
        in_specs = head_specs + in_specs
    return pl.pallas_call(
        functools.partial(_mlp_kernel, final_norm=final_norm, tf=tf, mixer=kind),
        grid=(b // nb, l // lt),
        in_specs=in_specs,
        out_specs=act(d),
        out_shape=jax.ShapeDtypeStruct((b, l, d), F32),
        compiler_params=_params("parallel", "parallel"),
        name="mlp",
    )(*args)


def _s5_kernel(*refs, nch, batch, nsteps, has_h0):
    if has_h0:
        u_ref, bc_ref, m_ref, cc_ref, a1_ref, a2_ref, h0_ref, y_ref, hl_ref = refs
    else:
        u_ref, bc_ref, m_ref, cc_ref, a1_ref, a2_ref, y_ref, hl_ref = refs
    u = u_ref[0]
    rows = u.shape[0]
    x = jnp.dot(u, bc_ref[0], precision=HIGHEST, preferred_element_type=F32)
    half = x.shape[1] // 2
    if has_h0:
        h0 = h0_ref[0]
        x = x + a1_ref[0, 0:1, :] * h0 + a2_ref[0, 0:1, :] * pltpu.roll(h0, half, axis=1)
    kidx = lax.broadcasted_iota(jnp.int32, x.shape, 0) & (nch - 1)
    for j in range(nsteps):
        s = 1 << j
        xs = jnp.where(kidx >= s, pltpu.roll(x, s, axis=0), 0.0)
        x = x + a1_ref[0, j:j + 1, :] * xs + a2_ref[0, j:j + 1, :] * pltpu.roll(xs, half, axis=1)
    for b in range(batch):
        r = (b + 1) * nch - 1
        hl_ref[0, b:b + 1, :] = x[r:r + 1, :]
    hstart = jnp.where(kidx >= 1, pltpu.roll(x, 1, axis=0), 0.0)
    if has_h0:
        hstart = hstart + h0
    y = jnp.dot(u.astype(BF16), m_ref[0], preferred_element_type=F32)
    y = y + jnp.dot(hstart.astype(BF16), cc_ref[0], preferred_element_type=F32)
    y_ref[0] = y


def _s5_tables(a_re, a_im, b_re, b_im, c_re, c_im, log_dt, max_steps):
    g, p = a_re.shape
    sub = S5_SUB
    dt = jnp.exp(log_dt.astype(F32))[:, None]
    zr, zi = a_re.astype(F32) * dt, a_im.astype(F32) * dt

    def zpow(n):
        mag = jnp.exp(zr * n)
        return mag * jnp.cos(zi * n), mag * jnp.sin(zi * n)

    er = jnp.expm1(zr) * jnp.cos(zi) - 2.0 * jnp.square(jnp.sin(0.5 * zi))
    ei = jnp.exp(zr) * jnp.sin(zi)
    den = a_re * a_re + a_im * a_im
    fr = (er * a_re + ei * a_im) / den
    fi = (ei * a_re - er * a_im) / den
    bb_re = fr[..., None] * b_re - fi[..., None] * b_im
    bb_im = fr[..., None] * b_im + fi[..., None] * b_re

    lags = jnp.arange(sub + 1, dtype=F32)[:, None, None]
    pr, pi = zpow(lags)

    qr, qi = pr[:sub][::-1], pi[:sub][::-1]
    bcr = qr[..., None] * bb_re[None] - qi[..., None] * bb_im[None]
    bci = qr[..., None] * bb_im[None] + qi[..., None] * bb_re[None]
    bc = jnp.concatenate([bcr, bci], axis=2)
    bc = bc.transpose(1, 0, 3, 2).reshape(g, sub * S5_GROUP, 2 * p)

    ar, ai = pr[1:], pi[1:]
    ccr = c_re[None] * ar[:, :, None, :] - c_im[None] * ai[:, :, None, :]
    cci = c_re[None] * ai[:, :, None, :] + c_im[None] * ar[:, :, None, :]
    cc = jnp.concatenate([ccr, -cci], axis=3)
    cc = cc.transpose(1, 3, 0, 2).reshape(g, 2 * p, sub * S5_GROUP)

    kr = (jnp.einsum('gcp,lgp,gpd->lgcd', c_re, pr[:sub], bb_re, precision=HIGHEST)
          - jnp.einsum('gcp,lgp,gpd->lgcd', c_re, pi[:sub], bb_im, precision=HIGHEST)
          - jnp.einsum('gcp,lgp,gpd->lgcd', c_im, pr[:sub], bb_im, precision=HIGHEST)
          - jnp.einsum('gcp,lgp,gpd->lgcd', c_im, pi[:sub], bb_re, precision=HIGHEST))
    tt = jnp.arange(sub)
    lag = tt[None, :] - tt[:, None]
    toep = jnp.where((lag >= 0)[:, :, None, None, None],
                     kr[jnp.clip(lag, 0, sub - 1)], 0.0)
    m = toep.transpose(2, 0, 4, 1, 3).reshape(g, sub * S5_GROUP, sub * S5_GROUP)

    steps = (sub * (2 ** jnp.arange(max_steps))).astype(F32)[:, None, None]
    sr, si = zpow(steps)
    a1 = jnp.concatenate([sr, sr], axis=-1).transpose(1, 0, 2)
    a2 = jnp.concatenate([-si, si], axis=-1).transpose(1, 0, 2)
    pad = (-max_steps) % 8
    a1 = jnp.pad(a1, ((0, 0), (0, pad), (0, 0)))
    a2 = jnp.pad(a2, ((0, 0), (0, pad), (0, 0)))
    return bc, m.astype(BF16), cc.astype(BF16), a1, a2


def _s5_lane_kernel(h_ref, bct_ref, mt_ref, cct_ref, a1_ref, a2_ref, y_ref, hl_ref, ut_s, yt_s, hl_s,
                    *, nch, nsteps):
    sub, c = S5_SUB, S5_GROUP
    groups = ut_s.shape[0]
    for t in range(sub):
        att = h_ref[0, pl.ds(t, nch, stride=sub), :].T
        for g in range(groups):
            ut_s[g, t * c:(t + 1) * c, :] = att[g * c:(g + 1) * c, :]
    half = bct_ref.shape[1] // 2

    def group(g, carry):
        lane = lax.broadcasted_iota(jnp.int32, (2 * half, nch), 1)
        ut = ut_s[g]
        x = jnp.dot(bct_ref[g], ut, precision=HIGHEST, preferred_element_type=F32)
        a1, a2 = a1_ref[g], a2_ref[g]
        for j in range(nsteps):
            s = 1 << j
            xs = jnp.where(lane >= s, pltpu.roll(x, s, axis=1), 0.0)
            x = x + a1[:, j:j + 1] * xs + a2[:, j:j + 1] * pltpu.roll(xs, half, axis=0)
        hl_s[g] = x[:, nch - LANES:].T[LANES - 8:, :]
        hstart = jnp.where(lane >= 1, pltpu.roll(x, 1, axis=1), 0.0)
        yt = jnp.dot(mt_ref[g], ut.astype(BF16), preferred_element_type=F32)
        yt = yt + jnp.dot(cct_ref[g], hstart.astype(BF16), preferred_element_type=F32)
        rows = pl.ds(pl.multiple_of(g * c, c), c)
        for t in range(sub):
            yt_s[t, rows, :] = yt[t * c:(t + 1) * c, :]
        return carry

    lax.fori_loop(0, groups, group, 0)
    for g in range(groups):
        hl_ref[0, g:g + 1, :] = hl_s[g, 7:8, :]
    for t in range(sub):
        y_ref[0, pl.ds(t, nch, stride=sub), :] = yt_s[t].T


def _s5_scan_lanes(h, tables):
    bc, m, cc, a1, a2 = tables
    b, l, d = h.shape
    g, k, p2 = bc.shape
    nch = l // S5_SUB
    gpb = LANES // S5_GROUP
    nsteps = nch.bit_length() - 1
    wspec = lambda shape: pl.BlockSpec((gpb,) + shape, lambda j, bi: (j, 0, 0))
    act = pl.BlockSpec((1, l, LANES), lambda j, bi: (bi, 0, j))
    return pl.pallas_call(
        functools.partial(_s5_lane_kernel, nch=nch, nsteps=nsteps),
        grid=(g // gpb, b),
        in_specs=[act, wspec((p2, k)), wspec((k, k)), wspec((k, p2)),
                  wspec((p2, a1.shape[1])), wspec((p2, a2.shape[1]))],
        out_specs=[act, pl.BlockSpec((1, gpb, p2), lambda j, bi: (bi, j, 0))],
        out_shape=[jax.ShapeDtypeStruct((b, l, d), F32), jax.ShapeDtypeStruct((b, g, p2), F32)],
        scratch_shapes=[pltpu.VMEM((gpb, k, nch), F32), pltpu.VMEM((S5_SUB, LANES, nch), F32),
                        pltpu.VMEM((gpb, 8, p2), F32)],
        compiler_params=_params("parallel", "parallel"),
        name="s5_scan_lanes",
    )(h, bc.transpose(0, 2, 1), m.transpose(0, 2, 1), cc.transpose(0, 2, 1),
      a1.transpose(0, 2, 1), a2.transpose(0, 2, 1))


def _s5_scan(h, tables, h0_re, h0_im):
    bc, m, cc, a1, a2 = tables
    b, l, d = h.shape
    if h0_re is None and (l // S5_SUB) % LANES == 0:
        y, hl = _s5_scan_lanes(h, tables)
        return y, hl[..., :hl.shape[-1] // 2], hl[..., hl.shape[-1] // 2:]
    g = d // S5_GROUP
    p2 = bc.shape[-1]
    nch = l // S5_SUB
    assert nch & (nch - 1) == 0
    nsteps = nch.bit_length() - 1
    rows = b * nch
    k = S5_SUB * S5_GROUP
    u = h.reshape(b, nch, S5_SUB, g, S5_GROUP).transpose(3, 0, 1, 2, 4).reshape(g, rows, k)
    has_h0 = h0_re is not None
    grp = lambda shape: pl.BlockSpec((1,) + shape, lambda i: (i, 0, 0))
    args = [u, bc, m, cc, a1, a2]
    in_specs = [grp((rows, k)), grp((k, p2)), grp((k, k)), grp((p2, k)),
                grp(a1.shape[1:]), grp(a2.shape[1:])]
    if has_h0:
        h0 = jnp.concatenate([h0_re, h0_im], axis=-1).astype(F32).transpose(1, 0, 2)
        h0 = jnp.pad(h0[:, :, None, :], ((0, 0), (0, 0), (0, nch - 1), (0, 0))).reshape(g, rows, p2)
        args.append(h0)
        in_specs.append(grp((rows, p2)))
    y, hl = pl.pallas_call(
        functools.partial(_s5_kernel, nch=nch, batch=b, nsteps=nsteps, has_h0=has_h0),
        grid=(g,),
        in_specs=in_specs,
        out_specs=[grp((rows, k)), grp((b, p2))],
        out_shape=[jax.ShapeDtypeStruct((g, rows, k), F32), jax.ShapeDtypeStruct((g, b, p2), F32)],
        compiler_params=_params("parallel"),
        name="s5_scan",
    )(*args)
    y = y.reshape(g, b, nch, S5_SUB, S5_GROUP).transpose(1, 2, 3, 0, 4).reshape(b, l, d)
    hl = hl.transpose(1, 0, 2)
    return y, hl[..., :p2 // 2], hl[..., p2 // 2:]


def _diff_qkv_kernel(x_ref, g_ref, sh_ref, sc_ref, w_ref, cos_ref, sin_ref,
                     k_ref, v_ref, qb_ref, kb_ref, vb_ref):
    nb, lt, d = x_ref.shape
    h = _modulated(x_ref, g_ref, sh_ref, sc_ref).reshape(nb * lt, d).astype(BF16)
    reps = d // cos_ref.shape[1]
    cos = jnp.tile(cos_ref[...], (1, reps))[None]
    sin = jnp.tile(sin_ref[...], (1, reps))[None]

    def roped(cols):
        t = jnp.dot(h, w_ref[:, cols * d:(cols + 1) * d], preferred_element_type=F32)
        r = _rotate_pairs(t, DIFF_DH // 2, DIFF_DH, 0)
        return t.reshape(nb, lt, d) * cos + r.reshape(nb, lt, d) * sin

    q = roped(0)
    qb_ref[...] = (q * (LOG2_E * DIFF_DH ** -0.5)).astype(BF16)
    k = roped(1)
    kb_ref[...] = k.astype(BF16)
    v = jnp.dot(h, w_ref[:, 2 * d:], preferred_element_type=F32).reshape(nb, lt, d)
    vb_ref[...] = v.astype(BF16)
    hw = k_ref.shape[2]
    heads = d // hw
    for hh in range(heads):
        k_ref[:, pl.ds(hh, lt, stride=heads), :] = k[:, :, hh * hw:(hh + 1) * hw]
        v_ref[:, pl.ds(hh, lt, stride=heads), :] = v[:, :, hh * hw:(hh + 1) * hw]


def _diff_qkv(x, g, sh, sc, w_qkv, cos, sin):
    b, l, d = x.shape
    nb, lt = _row_tiles(b, l)
    act, vec, par, pos = _tile_specs(b, l, d, nb, lt)
    hw = d // DIFF_HEADS
    return pl.pallas_call(
        _diff_qkv_kernel,
        grid=(b // nb, l // lt),
        in_specs=[act(d), par((1, d)), vec(d), vec(d), par((d, 3 * d)),
                  pos(cos.shape[1]), pos(sin.shape[1])],
        out_specs=[pl.BlockSpec((nb, lt * DIFF_HEADS, hw), lambda i, j: (i, j, 0))] * 2 + [act(d)] * 3,
        out_shape=[jax.ShapeDtypeStruct((b, l * DIFF_HEADS, hw), F32)] * 2
                  + [jax.ShapeDtypeStruct((b, l, d), BF16)] * 3,
        compiler_params=_params("parallel", "parallel"),
        name="diff_qkv",
    )(x, g, sh, sc, w_qkv, cos, sin)


_NT = (((1,), (1,)), ((), ()))
_TN = (((0,), (0,)), ((), ()))


def _softmax_update(m_s, l_s, acc_s, i, s, vb):
    m_prev = m_s[i]
    m_new = jnp.maximum(m_prev, jnp.max(s, axis=0, keepdims=True))
    alpha = jnp.exp2(m_prev - m_new)
    p = jnp.exp2(s - m_new)
    l_s[i] = alpha * l_s[i] + jnp.sum(p, axis=0, keepdims=True)
    acc_s[i] = alpha * acc_s[i] + lax.dot_general(vb, p.astype(BF16), _TN, preferred_element_type=F32)
    m_s[i] = m_new


def _softmax_init(m_s, l_s, acc_s):
    m_s[...] = jnp.full_like(m_s, NEG_INF)
    l_s[...] = jnp.zeros_like(l_s)
    acc_s[...] = jnp.zeros_like(acc_s)


def _softmax_result(l_s, acc_s, i):
    return acc_s[i] / l_s[i]


def _softmax_scratch(n, vw, queries):
    return [pltpu.VMEM((n, 1, queries), F32), pltpu.VMEM((n, 1, queries), F32),
            pltpu.VMEM((n, vw, queries), F32)]


def _attend_causal(scores, k_ref, v_ref, m_s, l_s, acc_s, s_a, s_b, *, tq, qi):
    assert len(scores) == 2 and tq % CHUNK == 0
    _softmax_init(m_s, l_s, acc_s)
    block = lambda ref, j: ref[0, pl.ds(pl.multiple_of(j * tq, tq), tq), :].astype(BF16)

    def qk(dst, j, streams):
        kb = block(k_ref, j)
        for i in streams:
            dst[i] = scores[i](kb)

    def pv(src, j, streams, masked=None):
        vb = block(v_ref, j)
        for i in streams:
            for c in range(0, 2 * tq, ATTN_LANE_BLOCK):
                cols = slice(c, c + ATTN_LANE_BLOCK)
                if i == masked:
                    src[i, :, cols] = jnp.where(vis[:, cols], src[i, :, cols], NEG_INF)
                m_prev = m_s[i, :, cols]
                m_new = jnp.maximum(m_prev, jnp.max(src[i, :, cols], axis=0, keepdims=True))
                alpha = jnp.exp2(m_prev - m_new)
                p = jnp.exp2(src[i, :, cols] - m_new)
                l_s[i, :, cols] = alpha * l_s[i, :, cols] + jnp.sum(p, axis=0, keepdims=True)
                acc_s[i, :, cols] = alpha * acc_s[i, :, cols] + lax.dot_general(
                    vb, p.astype(BF16), _TN, preferred_element_type=F32)
                m_s[i, :, cols] = m_new

    def pair(jj, carry):
        qk(s_b, 2 * jj + 1, (0, 1))
        pv(s_a, 2 * jj, (0, 1))
        qk(s_a, 2 * jj + 2, (0, 1))
        pv(s_b, 2 * jj + 1, (0, 1))
        return carry

    qk(s_a, 0, (0, 1))
    lax.fori_loop(0, qi, pair, 0)
    kc = lax.broadcasted_iota(jnp.int32, (tq, tq), 0) // CHUNK
    qc = lax.broadcasted_iota(jnp.int32, (tq, tq), 1) // CHUNK
    vis = jnp.concatenate([kc <= qc] * 2, axis=1)
    qk(s_b, 2 * qi + 1, (1,))
    pv(s_a, 2 * qi, (0, 1), masked=0)
    pv(s_b, 2 * qi + 1, (1,), masked=1)
    return [_softmax_result(l_s, acc_s, i) for i in range(2)]


def _diff_attn_kernel(q_ref, k_ref, v_ref, lam_ref, gsub_ref, o_ref, m_s, l_s, acc_s, s_a, s_b, *, tq):
    def make_score(i):
        q2 = _diff_query_groups(q_ref[0, i * tq:(i + 1) * tq, :])
        return lambda kb: lax.dot_general(kb, q2, _NT, preferred_element_type=F32)

    outs = _attend_causal([make_score(0), make_score(1)], k_ref, v_ref, m_s, l_s, acc_s, s_a, s_b,
                          tq=tq, qi=pl.program_id(2))
    lam = _diff_lambda(lam_ref)
    for i, o2 in enumerate(outs):
        o = (o2[:, :tq] - lam * o2[:, tq:]).T
        o = _rms(o, gsub_ref[...]) * (1.0 - LAMBDA_INIT)
        o_ref[0, i * tq:(i + 1) * tq, :] = o.astype(o_ref.dtype)


def _causal_attention_call(kernel, qb, k, v, extra, qw, vw, name):
    b, l, _ = qb.shape
    tq, ns = ATTN_TQ, ATTN_STREAMS
    assert l % (ns * tq) == 0
    tile = lambda width: pl.BlockSpec((1, ns * tq, width), lambda bi, h, qi: (bi, qi, h))
    seq = lambda width: pl.BlockSpec((1, l, width), lambda bi, h, qi: (bi, 0, h))
    units = qb.shape[2] // qw
    return pl.pallas_call(
        functools.partial(kernel, tq=tq),
        grid=(b, units, l // (ns * tq)),
        in_specs=[tile(qw), seq(qw), seq(vw)]
                 + [pl.BlockSpec(a.shape, lambda bi, h, qi: (0, 0)) for a in extra],
        out_specs=tile(vw),
        out_shape=jax.ShapeDtypeStruct((b, l, units * vw), BF16),
        scratch_shapes=_softmax_scratch(ns, vw, 2 * tq) + [pltpu.VMEM((ns, tq, 2 * tq), F32)] * 2,
        compiler_params=_params("parallel", "parallel", "arbitrary"),
        name=name,
    )(qb, k, v, *extra)


def _diff_attention(qb, kb, vb, lam_rows, g_sub):
    hw = 2 * DIFF_DH
    return _causal_attention_call(_diff_attn_kernel, qb, kb, vb, [lam_rows, g_sub], hw, hw,
                                  "diff_attention")


def _diff_lambda(lam_ref):
    lp = lam_ref[...]
    return (jnp.exp(jnp.sum(lp[0:1] * lp[1:2], axis=-1, keepdims=True))
            - jnp.exp(jnp.sum(lp[2:3] * lp[3:4], axis=-1, keepdims=True)) + LAMBDA_INIT)


def _diff_query_groups(q):
    lane = lax.broadcasted_iota(jnp.int32, q.shape, 1)
    zero = jnp.zeros_like(q)
    return jnp.concatenate([jnp.where(lane < DIFF_DH, q, zero), jnp.where(lane >= DIFF_DH, q, zero)], axis=0)


def _diff_cached_kernel(q_ref, kc_ref, vc_ref, kn_ref, vn_ref, lam_ref, gsub_ref, o_ref,
                        m_s, l_s, acc_s, *, tk):
    c = pl.program_id(1)
    lq = q_ref.shape[1]
    hw = kc_ref.shape[2]
    heads = q_ref.shape[2] // hw
    update = functools.partial(_softmax_update, m_s, l_s, acc_s)
    q2 = lambda hh: _diff_query_groups(q_ref[0, :, hh * hw:(hh + 1) * hw])

    @pl.when(c == 0)
    def _():
        _softmax_init(m_s, l_s, acc_s)

    def block(j, carry):
        rows = pl.ds(pl.multiple_of(j * tk * heads, tk * heads), tk * heads)
        k_blk, v_blk = kc_ref.at[0, rows, :], vc_ref.at[0, rows, :]
        for hh in range(heads):
            kb = k_blk[pl.ds(hh, tk, stride=heads), :].astype(BF16)
            vb = v_blk[pl.ds(hh, tk, stride=heads), :].astype(BF16)
            update(hh, lax.dot_general(kb, q2(hh), _NT, preferred_element_type=F32), vb)
        return carry

    lax.fori_loop(0, kc_ref.shape[1] // (tk * heads), block, 0)

    @pl.when(c == pl.num_programs(1) - 1)
    def _():
        lam = _diff_lambda(lam_ref)
        for hh in range(heads):
            kb = kn_ref[0, :, hh * hw:(hh + 1) * hw]
            vb = vn_ref[0, :, hh * hw:(hh + 1) * hw]
            update(hh, lax.dot_general(kb, q2(hh), _NT, preferred_element_type=F32), vb)
            o2 = _softmax_result(l_s, acc_s, hh)
            o = (o2[:, :lq] - lam * o2[:, lq:]).T
            o = _rms(o, gsub_ref[...]) * (1.0 - LAMBDA_INIT)
            o_ref[0, :, hh * hw:(hh + 1) * hw] = o.astype(o_ref.dtype)


def _diff_attention_cached(qb, cache_k, cache_v, kb, vb, lam_rows, g_sub):
    b, l, d = qb.shape
    _, p, heads, hw = cache_k.shape
    pc = p if p <= 1024 else 1024
    tk = ATTN_TK if pc % ATTN_TK == 0 else CHUNK
    assert p % pc == 0 and pc % tk == 0
    new = pl.BlockSpec((1, l, d), lambda bi, c: (bi, 0, 0))
    cache = pl.BlockSpec((1, pc * heads, hw), lambda bi, c: (bi, c, 0))
    return pl.pallas_call(
        functools.partial(_diff_cached_kernel, tk=tk),
        grid=(b, p // pc),
        in_specs=[new, cache, cache, new, new,
                  pl.BlockSpec(lam_rows.shape, lambda bi, c: (0, 0)),
                  pl.BlockSpec((1, hw), lambda bi, c: (0, 0))],
        out_specs=new,
        out_shape=jax.ShapeDtypeStruct((b, l, d), BF16),
        scratch_shapes=_softmax_scratch(heads, hw, 2 * l),
        compiler_params=_params("parallel", "arbitrary"),
        name="diff_attention_cached",
    )(qb, cache_k.reshape(b, p * heads, hw), cache_v.reshape(b, p * heads, hw), kb, vb, lam_rows, g_sub)


def _mla_proj_kernel(*refs, cached):
    (x_ref, g_ref, sh_ref, sc_ref, wq_ref, wc_ref, wr_ref, wrr_ref, gq_ref, gkv_ref, kcos_ref, ksin_ref,
     wuq_ref, wuqr_ref, qcos_ref, qsin_ref, *rest) = refs
    nb, lt, d = x_ref.shape
    rows = nb * lt
    dot = lambda a, w_ref: jnp.dot(a, w_ref[...], preferred_element_type=F32)
    h = _modulated(x_ref, g_ref, sh_ref, sc_ref).reshape(rows, d).astype(BF16)
    cq = _rms(dot(h, wq_ref), gq_ref[...]).astype(BF16)
    ckv = _rms(dot(h, wc_ref), gkv_ref[...])
    rope = wr_ref.shape[1]
    kr = (dot(h, wr_ref).reshape(nb, lt, rope) * kcos_ref[...][None]
          + dot(h, wrr_ref).reshape(nb, lt, rope) * ksin_ref[...][None])
    n = wuq_ref.shape[1]
    reps = n // qcos_ref.shape[1]
    cos = jnp.tile(qcos_ref[...], (1, reps))[None]
    sin = jnp.tile(qsin_ref[...], (1, reps))[None]
    q = (dot(cq, wuq_ref).reshape(nb, lt, n) * cos + dot(cq, wuqr_ref).reshape(nb, lt, n) * sin).astype(BF16)
    if cached:
        wabs_ref, ckv_ref, kr_ref, q_ref = rest
        qf = q.reshape(rows, n)
        hp = MLA_HEAD_PAD
        for hh in range(n // hp):
            o = jnp.dot(qf[:, hh * hp:(hh + 1) * hp], wabs_ref[hh], preferred_element_type=F32)
            q_ref[:, hh] = o.reshape(nb, lt, o.shape[1]).astype(q_ref.dtype)
    else:
        wk_ref, we_ref, wv_ref, ckv_ref, kr_ref, q_ref, k_ref, v_ref = rest
        q_ref[...] = q
        c = ckv.astype(BF16)
        k = dot(c, wk_ref) + dot(kr.reshape(rows, rope).astype(BF16), we_ref)
        k_ref[...] = k.reshape(k_ref.shape).astype(k_ref.dtype)
        v_ref[...] = dot(c, wv_ref).reshape(v_ref.shape).astype(v_ref.dtype)
    ckv_ref[...] = ckv.reshape(ckv_ref.shape)
    kr_ref[...] = kr


def _mla_project(x, g, sh, sc, w, k_cos, k_sin, q_cos, q_sin, cached):
    b, l, d = x.shape
    nb, lt = _row_tiles(b, l)
    act, vec, par, pos = _tile_specs(b, l, d, nb, lt)
    weights = [w['mla_dq'], w['mla_dc'], w['mla_dr'], w['mla_drr'], w['mla_g_q'], w['mla_g_kv']]
    q_weights = [w['mla_uq'], w['mla_uq_rot']]
    kvr, rope, n = w['mla_dc'].shape[1], w['mla_dr'].shape[1], w['mla_uq'].shape[1]
    out_specs = [act(kvr), act(rope)]
    out_shape = [jax.ShapeDtypeStruct((b, l, kvr), F32), jax.ShapeDtypeStruct((b, l, rope), F32)]
    if cached:
        tail = [w['mla_abs']]
        heads, _, qw = w['mla_abs'].shape
        out_specs.append(pl.BlockSpec((nb, heads, lt, qw), lambda i, j: (i, 0, j, 0)))
        out_shape.append(jax.ShapeDtypeStruct((b, heads, l, qw), BF16))
    else:
        tail = [w['mla_uk'], w['mla_place'], w['mla_uv']]
        nv = w['mla_uv'].shape[1]
        out_specs += [act(n), act(n), act(nv)]
        out_shape += [jax.ShapeDtypeStruct((b, l, n), BF16)] * 2 + [jax.ShapeDtypeStruct((b, l, nv), BF16)]
    return pl.pallas_call(
        functools.partial(_mla_proj_kernel, cached=cached),
        grid=(b // nb, l // lt),
        in_specs=[act(d), par((1, d)), vec(d), vec(d)] + [par(a.shape) for a in weights]
                 + [pos(rope), pos(rope)] + [par(a.shape) for a in q_weights]
                 + [pos(q_cos.shape[1]), pos(q_sin.shape[1])] + [par(a.shape) for a in tail],
        out_specs=out_specs,
        out_shape=out_shape,
        compiler_params=_params("parallel", "parallel"),
        name="mla_project",
    )(x, g, sh, sc, *weights, k_cos, k_sin, *q_weights, q_cos, q_sin, *tail)


def _mla_attn_kernel(q_ref, k_ref, v_ref, o_ref, m_s, l_s, acc_s, s_a, s_b, *, tq):
    hp = MLA_HEAD_PAD

    def make_score(i):
        qa = q_ref[0, i * tq:(i + 1) * tq, :hp]
        qb = q_ref[0, i * tq:(i + 1) * tq, hp:]
        return lambda kb: jnp.concatenate(
            [lax.dot_general(kb[:, :hp], qa, _NT, preferred_element_type=F32),
             lax.dot_general(kb[:, hp:], qb, _NT, preferred_element_type=F32)], axis=1)

    outs = _attend_causal([make_score(0), make_score(1)], k_ref, v_ref, m_s, l_s, acc_s, s_a, s_b,
                          tq=tq, qi=pl.program_id(2))
    row = lax.broadcasted_iota(jnp.int32, (acc_s.shape[1], tq), 0)
    for i, o2 in enumerate(outs):
        o = jnp.where(row < MLA_V, o2[:, :tq], o2[:, tq:]).T
        o_ref[0, i * tq:(i + 1) * tq, :] = o.astype(o_ref.dtype)


def _mla_attention(qb, kb, vb):
    return _causal_attention_call(_mla_attn_kernel, qb, kb, vb, [], 2 * MLA_HEAD_PAD, 2 * MLA_V,
                                  "mla_attention")


def _mla_cached_kernel(q_ref, cc_ref, rc_ref, cn_ref, rn_ref, wuvt_ref, o_ref, m_s, l_s, acc_s, *, tk):
    heads, lq, qw = q_ref.shape[1:]
    q2 = q_ref[0].reshape(heads * lq, qw)

    def step(c, r):
        rpad = jnp.concatenate([r, jnp.zeros((r.shape[0], qw - c.shape[1] - r.shape[1]), r.dtype)], axis=1)
        kb = jnp.concatenate([c, rpad], axis=1).astype(BF16)
        s = lax.dot_general(kb, q2, _NT, preferred_element_type=F32)
        _softmax_update(m_s, l_s, acc_s, 0, s, c.astype(BF16))

    def block(j, carry):
        rows = pl.ds(pl.multiple_of(j * tk, tk), tk)
        step(cc_ref[0, rows, :], rc_ref[0, rows, :])
        return carry

    _softmax_init(m_s, l_s, acc_s)
    lax.fori_loop(0, cc_ref.shape[1] // tk, block, 0)
    step(cn_ref[0], rn_ref[0])
    lat = _softmax_result(l_s, acc_s, 0)
    assert 2 * lq == LANES and 2 * MLA_V == LANES
    lane = lax.broadcasted_iota(jnp.int32, (lq, LANES), 1)
    for pr in range(heads // 2):
        lp = lat[:, pr * LANES:(pr + 1) * LANES].astype(BF16)
        tt = jnp.dot(wuvt_ref[pr], lp, preferred_element_type=F32).T
        o_ref[0, :, pr * LANES:(pr + 1) * LANES] = jnp.where(lane < MLA_V, tt[:lq], tt[lq:]).astype(o_ref.dtype)


def _mla_attention_cached(q_abs, cache_ckv, cache_krope, ckv, krope, w_uvt):
    b, heads, l, qw = q_abs.shape
    p, rank = cache_ckv.shape[1:]
    rope = cache_krope.shape[2]
    tk = ATTN_TK if p % ATTN_TK == 0 else CHUNK
    assert p % tk == 0
    full = lambda shape: pl.BlockSpec((1,) + shape, lambda bi: (bi,) + (0,) * len(shape))
    return pl.pallas_call(
        functools.partial(_mla_cached_kernel, tk=tk),
        grid=(b,),
        in_specs=[full((heads, l, qw)), full((p, rank)), full((p, rope)), full((l, rank)), full((l, rope)),
                  pl.BlockSpec(w_uvt.shape, lambda bi: (0, 0, 0))],
        out_specs=full((l, heads * MLA_V)),
        out_shape=jax.ShapeDtypeStruct((b, l, heads * MLA_V), BF16),
        scratch_shapes=_softmax_scratch(1, rank, heads * l),
        compiler_params=_params("parallel"),
        name="mla_attention_cached",
    )(q_abs, cache_ckv, cache_krope, ckv, krope, w_uvt)


def _sgu_kernel(x_ref, g_ref, sh_ref, sc_ref, gt_ref, win_ref, gv_ref, ws_ref, bs_ref, wout_ref,
                *out_refs, t, emit_v):
    o_ref = out_refs[0]
    nb, lt, d = x_ref.shape
    rows = nb * lt
    width = gv_ref.shape[1]
    gd = width // SGU_GROUPS
    h = _modulated(x_ref, g_ref, sh_ref, sc_ref).reshape(rows, d).astype(BF16)
    u = jax.nn.gelu(jnp.dot(h, win_ref[:, :width], preferred_element_type=F32))
    v = jax.nn.gelu(jnp.dot(h, win_ref[:, width:], preferred_element_type=F32))
    v = _rms(v, gv_ref[...])
    if emit_v:
        out_refs[1][...] = v.reshape(nb, lt, width)
    vb = v.astype(BF16)
    bias = bs_ref[...]
    gated = []
    for c in range(rows // t):
        sv = [jnp.dot(ws_ref[gi], vb[c * t:(c + 1) * t, gi * gd:(gi + 1) * gd],
                      preferred_element_type=F32) for gi in range(SGU_GROUPS)]
        sv = jnp.concatenate(sv, axis=1) + bias
        gated.append((u[c * t:(c + 1) * t] * sv).astype(BF16))
    gated = jnp.concatenate(gated, axis=0) if len(gated) > 1 else gated[0]
    r = jnp.dot(gated, wout_ref[...], preferred_element_type=F32).reshape(nb, lt, d)
    o_ref[...] = x_ref[...] + (1.0 + gt_ref[...]) * r


def _sgu(x, g, sh, sc, gt, w_in, g_v, w_s, b_full, w_out, emit_v):
    b, l, d = x.shape
    t = w_s.shape[1]
    width = g_v.shape[1]
    nb, lt = _row_tiles(b, l, 512)
    assert lt % t == 0
    act, vec, par, _ = _tile_specs(b, l, d, nb, lt)
    resident = lambda shape: pl.BlockSpec(shape, lambda i, j: (0, 0), pipeline_mode=pl.Buffered(1))
    out_specs = [act(d)]
    out_shape = [jax.ShapeDtypeStruct((b, l, d), F32)]
    if emit_v:
        out_specs.append(act(width))
        out_shape.append(jax.ShapeDtypeStruct((b, l, width), F32))
    return pl.pallas_call(
        functools.partial(_sgu_kernel, t=t, emit_v=emit_v),
        grid=(b // nb, l // lt),
        in_specs=[act(d), par((1, d)), vec(d), vec(d), vec(d), resident(w_in.shape), par((1, width)),
                  par(w_s.shape), par(b_full.shape), resident(w_out.shape)],
        out_specs=out_specs,
        out_shape=out_shape,
        compiler_params=_params("parallel", "parallel"),
        name="sgu",
    )(x, g, sh, sc, gt, w_in, g_v, w_s, b_full, w_out)


def _rope_angles(pos, half):
    inv_freq = ROPE_THETA ** (-jnp.arange(half, dtype=F32) / half)
    ang = pos.astype(F32)[:, None] * inv_freq[None, :]
    return jnp.cos(ang), jnp.sin(ang)


def _diff_rope_tables(pos):
    cos, sin = _rope_angles(pos, DIFF_DH // 2)
    reps = LANES // DIFF_DH
    return (jnp.tile(jnp.concatenate([cos, cos], axis=1), (1, reps)),
            jnp.tile(jnp.concatenate([-sin, sin], axis=1), (1, reps)))


def _mla_rope_tables(pos):
    cos, sin = _rope_angles(pos, MLA_ROPE // 2)
    n = pos.shape[0]
    ones = jnp.ones((n, MLA_NOPE), F32)
    zq = jnp.zeros((n, MLA_NOPE), F32)
    zp = jnp.zeros((n, MLA_HEAD_PAD - MLA_NOPE - MLA_ROPE), F32)
    scale = LOG2_E * MLA_SCALE
    q_cos = jnp.concatenate([ones, cos, cos, zp], axis=1) * scale
    q_sin = jnp.concatenate([zq, -sin, sin, zp], axis=1) * scale
    k_cos = jnp.concatenate([cos, cos], axis=1)
    k_sin = jnp.concatenate([sin, sin], axis=1)
    return q_cos, q_sin, k_cos, k_sin


def _prepare_weights(p):
    d = p['w_up'].shape[1]
    w = {}
    w['w_up'] = [p['w_up'][i].astype(BF16) for i in range(p['w_up'].shape[0])]
    w['w_down'] = [p['w_down'][i].astype(BF16) for i in range(p['w_down'].shape[0])]
    w['glu_a'] = p['s5_w_glu_a'].astype(BF16)
    w['glu_b'] = p['s5_w_glu_b'].astype(BF16)
    w['diff_qkv'] = p['diff_w_qkv'].astype(BF16)
    w['diff_o'] = p['diff_w_o'].astype(BF16)
    lam = jnp.stack([p['diff_lambda_q1'], p['diff_lambda_k1'], p['diff_lambda_q2'], p['diff_lambda_k2']])
    w['diff_lam'] = jnp.pad(lam.astype(F32), ((0, 4), (0, LANES - lam.shape[1])))
    w['diff_g_sub'] = p['diff_g_sub'].reshape(1, -1)

    w['mla_dq'] = p['mla_w_dq'].astype(BF16)
    kvr = p['mla_g_kv'].shape[0]
    w['mla_dc'] = p['mla_w_dkv'][:, :kvr].astype(BF16)
    wr = p['mla_w_dkv'][:, kvr:]
    hr = MLA_ROPE // 2
    w['mla_dr'] = wr.astype(BF16)
    w['mla_drr'] = jnp.concatenate([-wr[:, hr:], wr[:, :hr]], axis=1).astype(BF16)
    w['mla_g_q'] = p['mla_g_q'].reshape(1, -1)
    w['mla_g_kv'] = p['mla_g_kv'].reshape(1, -1)
    pad = MLA_HEAD_PAD - MLA_NOPE - MLA_ROPE
    qr = p['mla_w_uq'].shape[0]
    uq = p['mla_w_uq'].reshape(qr, MLA_HEADS, MLA_NOPE + MLA_ROPE)
    w['mla_uq'] = jnp.pad(uq, ((0, 0), (0, 0), (0, pad))).reshape(qr, -1).astype(BF16)
    uq_rot = jnp.concatenate([jnp.zeros_like(uq[..., :MLA_NOPE]), uq[..., MLA_NOPE + hr:],
                              uq[..., MLA_NOPE:MLA_NOPE + hr]], axis=2)
    w['mla_uq_rot'] = jnp.pad(uq_rot, ((0, 0), (0, 0), (0, pad))).reshape(qr, -1).astype(BF16)
    uk = jnp.pad(p['mla_w_uk'], ((0, 0), (0, 0), (0, MLA_HEAD_PAD - MLA_NOPE)))
    w['mla_uk'] = uk.reshape(kvr, -1).astype(BF16)
    place = jnp.pad(jnp.eye(MLA_ROPE, dtype=F32), ((0, 0), (MLA_NOPE, pad)))
    w['mla_place'] = jnp.tile(place, (1, MLA_HEADS)).astype(BF16)
    w['mla_uv'] = p['mla_w_uv'].reshape(kvr, -1).astype(BF16)
    absorb = jnp.pad(p['mla_w_uk'].transpose(1, 2, 0), ((0, 0), (0, MLA_HEAD_PAD - MLA_NOPE), (0, 0)))
    select = jnp.pad(jnp.eye(MLA_ROPE, dtype=F32), ((MLA_NOPE, pad), (0, MLA_HEAD_PAD - MLA_ROPE)))
    w['mla_abs'] = jnp.concatenate(
        [absorb, jnp.broadcast_to(select, (MLA_HEADS,) + select.shape)], axis=2).astype(BF16)
    w['mla_uvt'] = p['mla_w_uv'].transpose(1, 2, 0).reshape(MLA_HEADS // 2, 2 * MLA_V, kvr).astype(BF16)
    w['mla_o'] = p['mla_w_o'].astype(BF16)

    w['sgu_in'] = p['sgu_w_in'].astype(BF16)
    w['sgu_g_v'] = p['sgu_g_v'].reshape(1, -1)
    w['sgu_out'] = p['sgu_w_out'].astype(BF16)
    return w


def _sgu_spatial(w_s, b_s, t, width):
    ws = jnp.tril(w_s[:, :t, :t]).astype(BF16)
    gd = width // SGU_GROUPS
    b_full = jnp.repeat(b_s[:, :t].T.astype(F32), gd, axis=1)
    return ws, b_full


def _run_trunk(x, mods, pos, past, p, w, s5_tables):
    b, l, d = x.shape
    new = {}
    depth = p['g_mix'].shape[0]
    for i in range(depth):
        sh1, sc1, gt1, sh2, sc2, gt2 = [mods[i][:, k][:, None, :] for k in range(6)]
        g_mix = p['g_mix'][i].reshape(1, d)
        kind = i % N_MIXERS
        mixer = None
        if kind == 0:
            h = _modulate(x, g_mix, sh1, sc1)
            h0_re = None if past is None else past['s5_re']
            h0_im = None if past is None else past['s5_im']
            y, new['s5_re'], new['s5_im'] = _s5_scan(h, s5_tables, h0_re, h0_im)
            mixer = ('glu', y, h, p['s5_d'].reshape(1, d), w['glu_a'], w['glu_b'], gt1)
        elif kind == 1:
            cos, sin = _diff_rope_tables(pos)
            k, v, qb, kb, vb = _diff_qkv(x, g_mix, sh1, sc1, w['diff_qkv'], cos, sin)
            if past is None:
                o = _diff_attention(qb, kb, vb, w['diff_lam'], w['diff_g_sub'])
            else:
                o = _diff_attention_cached(qb, past['diff_k'], past['diff_v'], kb, vb,
                                           w['diff_lam'], w['diff_g_sub'])
            mixer = ('proj', o, w['diff_o'], gt1)
            new['diff_k'] = k.reshape(b, l, DIFF_HEADS, 2 * DIFF_DH)
            new['diff_v'] = v.reshape(b, l, DIFF_HEADS, 2 * DIFF_DH)
        elif kind == 2:
            q_cos, q_sin, k_cos, k_sin = _mla_rope_tables(pos)
            proj = _mla_project(x, g_mix, sh1, sc1, w, k_cos, k_sin, q_cos, q_sin, cached=past is not None)
            ckv, krope = proj[:2]
            if past is None:
                o = _mla_attention(*proj[2:])
            else:
                o = _mla_attention_cached(proj[2], past['mla_ckv'], past['mla_krope'], ckv, krope,
                                          w['mla_uvt'])
            mixer = ('proj', o, w['mla_o'], gt1)
            new['mla_ckv'], new['mla_krope'] = ckv, krope
        else:
            t = min(l, SGU_CHUNK)
            ws, b_full = _sgu_spatial(p['sgu_w_s'], p['sgu_b_s'], t, w['sgu_g_v'].shape[1])
            outs = _sgu(x, g_mix, sh1, sc1, gt1, w['sgu_in'], w['sgu_g_v'], ws, b_full, w['sgu_out'],
                        emit_v=past is not None)
            x = outs[0]
            if past is not None:
                new['sgu_v'] = outs[1]
        x = _mlp(x, p['g_ffn'][i].reshape(1, d), sh2, sc2, gt2, w['w_up'][i], w['w_down'][i],
                 p['g_final'].reshape(1, d), final_norm=(i == depth - 1), mixer=mixer)
    return x, new


def kernel(x_prompt, x_sample, c_prompt, c_sample, state_s5_re, state_s5_im, cache_diff_k, cache_diff_v, cache_mla_ckv, cache_mla_krope, w_ada, b_ada, g_mix, g_ffn, w_up, w_down, g_final, s5_a_re, s5_a_im, s5_b_re, s5_b_im, s5_c_re, s5_c_im, s5_d, s5_log_dt, s5_w_glu_a, s5_w_glu_b, diff_w_qkv, diff_lambda_q1, diff_lambda_k1, diff_lambda_q2, diff_lambda_k2, diff_g_sub, diff_w_o, mla_w_dq, mla_g_q, mla_w_uq, mla_w_dkv, mla_g_kv, mla_w_uk, mla_w_uv, mla_w_o, sgu_w_in, sgu_g_v, sgu_w_s, sgu_b_s, sgu_w_out):
    p = {
        'w_ada': w_ada, 'b_ada': b_ada, 'g_mix': g_mix, 'g_ffn': g_ffn,
        'w_up': w_up, 'w_down': w_down, 'g_final': g_final,
        's5_d': s5_d, 's5_w_glu_a': s5_w_glu_a, 's5_w_glu_b': s5_w_glu_b,
        'diff_w_qkv': diff_w_qkv, 'diff_lambda_q1': diff_lambda_q1, 'diff_lambda_k1': diff_lambda_k1,
        'diff_lambda_q2': diff_lambda_q2, 'diff_lambda_k2': diff_lambda_k2,
        'diff_g_sub': diff_g_sub, 'diff_w_o': diff_w_o,
        'mla_w_dq': mla_w_dq, 'mla_g_q': mla_g_q, 'mla_w_uq': mla_w_uq, 'mla_w_dkv': mla_w_dkv,
        'mla_g_kv': mla_g_kv, 'mla_w_uk': mla_w_uk, 'mla_w_uv': mla_w_uv, 'mla_w_o': mla_w_o,
        'sgu_w_in': sgu_w_in, 'sgu_g_v': sgu_g_v, 'sgu_w_s': sgu_w_s, 'sgu_b_s': sgu_b_s,
        'sgu_w_out': sgu_w_out,
    }
    past = {
        's5_re': state_s5_re, 's5_im': state_s5_im,
        'diff_k': cache_diff_k, 'diff_v': cache_diff_v,
        'mla_ckv': cache_mla_ckv, 'mla_krope': cache_mla_krope,
    }
    bp, lp, d = x_prompt.shape
    bs, ls, _ = x_sample.shape
    depth = w_ada.shape[0]

    c_all = jnp.concatenate([c_prompt, c_sample], axis=0)
    c_all = jnp.pad(c_all, ((0, (-c_all.shape[0]) % 8), (0, 0)))
    mod = _adaln(c_all, w_ada, b_ada).reshape(depth, c_all.shape[0], 6, d)
    mods_p = [mod[i, :bp] for i in range(depth)]
    mods_s = [mod[i, bp:bp + bs] for i in range(depth)]

    w = _prepare_weights(p)
    max_steps = max((lp // S5_SUB).bit_length() - 1, (ls // S5_SUB).bit_length() - 1, 1)
    s5_tables = _s5_tables(s5_a_re, s5_a_im, s5_b_re, s5_b_im, s5_c_re, s5_c_im, s5_log_dt, max_steps)

    pos_p = jnp.arange(lp, dtype=jnp.int32)
    pos_s = cache_diff_k.shape[1] + jnp.arange(ls, dtype=jnp.int32)
    y_prompt, sp = _run_trunk(x_prompt, mods_p, pos_p, None, p, w, s5_tables)
    y_sample, ss = _run_trunk(x_sample, mods_s, pos_s, past, p, w, s5_tables)
    return (y_prompt, y_sample,
            sp['s5_re'], sp['s5_im'], ss['s5_re'], ss['s5_im'],
            sp['diff_k'], sp['diff_v'], ss['diff_k'], ss['diff_v'],
            sp['mla_ckv'], sp['mla_krope'], ss['mla_ckv'], ss['mla_krope'],
            ss['sgu_v'])
```

```python
import functools
import math

import jax
import jax.numpy as jnp
from jax import lax
from jax.experimental import pallas as pl
from jax.experimental.pallas import tpu as pltpu

F32 = jnp.float32
BF16 = jnp.bfloat16
HIGHEST = lax.Precision.HIGHEST

NORM_EPS = 1e-6
ROPE_THETA = 10000.0
NEG_INF = -1e30
LOG2_E = math.log2(math.e)
CHUNK = 64
N_MIXERS = 4

S5_GROUP = 16
S5_SUB = 16

DIFF_HEADS = 8
DIFF_DH = 64
LAMBDA_INIT = 0.8 - 0.6 * math.exp(-0.3 * 1)

MLA_HEADS = 16
MLA_NOPE = 64
MLA_ROPE = 32
MLA_V = 64
MLA_SCALE = (MLA_NOPE + MLA_ROPE) ** -0.5
MLA_HEAD_PAD = 128

SGU_CHUNK = 128
SGU_GROUPS = 8

LANES = 128
SUBLANES = 8
ROW_TILE = 512
ATTN_TQ = 512
ATTN_STREAMS = 2
ATTN_LANE_BLOCK = 256
ATTN_TK = 512
CACHE_CHUNK = 1024
VMEM_LIMIT_BYTES = 48 * 1024 * 1024


def _params(*sem):
    return pltpu.CompilerParams(dimension_semantics=sem, vmem_limit_bytes=VMEM_LIMIT_BYTES)


def _row_tiles(batch, length, rows=ROW_TILE):
    if length >= rows:
        assert length % rows == 0
        return 1, rows
    nb = max(1, min(batch, rows // length))
    while batch % nb:
        nb -= 1
    return nb, length


def _rms(x, g):
    return x * lax.rsqrt(jnp.mean(x * x, axis=-1, keepdims=True) + NORM_EPS) * g


def _modulated(x_ref, g_ref, sh_ref, sc_ref):
    return _rms(x_ref[...], g_ref[...]) * (1.0 + sc_ref[...]) + sh_ref[...]


def _rotate_pairs(x, half, period, lo):
    width = x.shape[-1]
    lane = lax.broadcasted_iota(jnp.int32, x.shape, x.ndim - 1) % period
    fwd = pltpu.roll(x, width - half, axis=x.ndim - 1)
    bwd = pltpu.roll(x, half, axis=x.ndim - 1)
    return jnp.where((lane >= lo) & (lane < lo + half), fwd, bwd)


def _adaln_kernel(c_ref, w_ref, b_ref, o_ref):
    c = c_ref[...]
    s = c * jax.nn.sigmoid(c)
    o_ref[0] = jnp.dot(s, w_ref[0], precision=HIGHEST, preferred_element_type=F32) + b_ref[0]


def _adaln(c_all, w_ada, b_ada):
    depth, d, n = w_ada.shape
    rows = c_all.shape[0]
    tn = n // 4
    return pl.pallas_call(
        _adaln_kernel,
        grid=(depth, n // tn),
        in_specs=[pl.BlockSpec((rows, d), lambda i, j: (0, 0)),
                  pl.BlockSpec((1, d, tn), lambda i, j: (i, 0, j)),
                  pl.BlockSpec((1, 1, tn), lambda i, j: (i, 0, j))],
        out_specs=pl.BlockSpec((1, rows, tn), lambda i, j: (i, 0, j)),
        out_shape=jax.ShapeDtypeStruct((depth, rows, n), F32),
        compiler_params=_params("parallel", "parallel"),
        name="adaln",
    )(c_all, w_ada, b_ada.reshape(depth, 1, n))


def _tile_specs(batch, length, d, nb, lt, extra_axes=0):
    if extra_axes == 0:
        act = lambda width: pl.BlockSpec((nb, lt, width), lambda i, j: (i, j, 0))
        vec = lambda width: pl.BlockSpec((nb, 1, width), lambda i, j: (i, 0, 0))
        par = lambda shape: pl.BlockSpec(shape, lambda i, j: (0,) * len(shape))
        pos = lambda width: pl.BlockSpec((lt, width), lambda i, j: (j, 0))
    else:
        act = lambda width: pl.BlockSpec((nb, lt, width), lambda i, j, f: (i, j, 0))
        vec = lambda width: pl.BlockSpec((nb, 1, width), lambda i, j, f: (i, 0, 0))
        par = lambda shape: pl.BlockSpec(shape, lambda i, j, f: (0,) * len(shape))
        pos = lambda width: pl.BlockSpec((lt, width), lambda i, j, f: (j, 0))
    return act, vec, par, pos


def _modulate_kernel(x_ref, g_ref, sh_ref, sc_ref, o_ref):
    o_ref[...] = _modulated(x_ref, g_ref, sh_ref, sc_ref)


def _modulate(x, g, sh, sc):
    b, l, d = x.shape
    nb, lt = _row_tiles(b, l)
    act, vec, par, _ = _tile_specs(b, l, d, nb, lt)
    return pl.pallas_call(
        _modulate_kernel,
        grid=(b // nb, l // lt),
        in_specs=[act(d), par((1, d)), vec(d), vec(d)],
        out_specs=act(d),
        out_shape=jax.ShapeDtypeStruct((b, l, d), F32),
        compiler_params=_params("parallel", "parallel"),
        name="modulate",
    )(x, g, sh, sc)


def _mlp_kernel(*refs, final_norm, tf, mixer):
    if mixer == 'proj':
        o_in_ref, wo_ref, gt1_ref, *refs = refs
    elif mixer == 'glu':
        y_ref, h_ref, dskip_ref, wa_ref, wb_ref, gt1_ref, *refs = refs
    x_ref, g_ref, sh_ref, sc_ref, gt_ref, wu_ref, wd_ref, gf_ref, o_ref = refs
    nb, lt, d = x_ref.shape
    x = x_ref[...]
    if mixer == 'proj':
        r = jnp.dot(o_in_ref[...].reshape(nb * lt, o_in_ref.shape[2]), wo_ref[...],
                    preferred_element_type=F32)
        x = x + (1.0 + gt1_ref[...]) * r.reshape(nb, lt, d)
    elif mixer == 'glu':
        z = jax.nn.gelu(y_ref[...] + dskip_ref[...] * h_ref[...]).reshape(nb * lt, d).astype(BF16)
        r = (jnp.dot(z, wa_ref[...], preferred_element_type=F32)
             * jax.nn.sigmoid(jnp.dot(z, wb_ref[...], preferred_element_type=F32)))
        x = x + (1.0 + gt1_ref[...]) * r.reshape(nb, lt, d)
    h = (_rms(x, g_ref[...]) * (1.0 + sc_ref[...]) + sh_ref[...]).reshape(nb * lt, d).astype(BF16)
    acc = None
    for f in range(wu_ref.shape[1] // tf):
        a = jnp.dot(h, wu_ref[:, f * tf:(f + 1) * tf], preferred_element_type=F32)
        a = jnp.square(jnp.maximum(a, 0.0)).astype(BF16)
        r = jnp.dot(a, wd_ref[f * tf:(f + 1) * tf, :], preferred_element_type=F32)
        acc = r if acc is None else acc + r
    y = x + (1.0 + gt_ref[...]) * acc.reshape(nb, lt, d)
    if final_norm:
        y = _rms(y, gf_ref[...])
    o_ref[...] = y


def _mlp(x, g, sh, sc, gt, w_up, w_down, g_final, final_norm, mixer=None, tf=2048):
    b, l, d = x.shape
    dff = w_up.shape[1]
    nb, lt = _row_tiles(b, l)
    act, vec, par, _ = _tile_specs(b, l, d, nb, lt)
    resident = lambda shape: pl.BlockSpec(shape, lambda i, j: (0, 0), pipeline_mode=pl.Buffered(1))
    args = [x, g, sh, sc, gt, w_up, w_down, g_final]
    in_specs = [act(d), par((1, d)), vec(d), vec(d), vec(d),
                resident((d, dff)), resident((dff, d)), par((1, d))]
    kind = None
    if mixer is not None:
        kind, *head = mixer
        if kind == 'proj':
            o, w_o, _ = head
            head_specs = [act(o.shape[2]), resident(w_o.shape), vec(d)]
        else:
            head_specs = [act(d), act(d), par((1, d)), resident((d, d)), resident((d, d)), vec(d)]
        args = head + args
        in_specs = head_specs + in_specs
    return pl.pallas_call(
        functools.partial(_mlp_kernel, final_norm=final_norm, tf=tf, mixer=kind),
        grid=(b // nb, l // lt),
        in_specs=in_specs,
        out_specs=act(d),
        out_shape=jax.ShapeDtypeStruct((b, l, d), F32),
        compiler_params=_params("parallel", "parallel"),
        name="mlp",
    )(*args)


def _s5_kernel(*refs, nch, batch, nsteps, has_h0):
    if has_h0:
        u_ref, bc_ref, m_ref, cc_ref, a1_ref, a2_ref, h0_ref, y_ref, hl_ref = refs
    else:
        u_ref, bc_ref, m_ref, cc_ref, a1_ref, a2_ref, y_ref, hl_ref = refs
    u = u_ref[0]
    rows = u.shape[0]
    x = jnp.dot(u, bc_ref[0], precision=HIGHEST, preferred_element_type=F32)
    half = x.shape[1] // 2
    if has_h0:
        h0 = h0_ref[0]
        x = x + a1_ref[0, 0:1, :] * h0 + a2_ref[0, 0:1, :] * pltpu.roll(h0, half, axis=1)
    kidx = lax.broadcasted_iota(jnp.int32, x.shape, 0) & (nch - 1)
    for j in range(nsteps):
        s = 1 << j
        xs = jnp.where(kidx >= s, pltpu.roll(x, s, axis=0), 0.0)
        x = x + a1_ref[0, j:j + 1, :] * xs + a2_ref[0, j:j + 1, :] * pltpu.roll(xs, half, axis=1)
    for b in range(batch):
        r = (b + 1) * nch - 1
        hl_ref[0, b:b + 1, :] = x[r:r + 1, :]
    hstart = jnp.where(kidx >= 1, pltpu.roll(x, 1, axis=0), 0.0)
    if has_h0:
        hstart = hstart + h0
    y = jnp.dot(u.astype(BF16), m_ref[0], preferred_element_type=F32)
    y = y + jnp.dot(hstart.astype(BF16), cc_ref[0], preferred_element_type=F32)
    y_ref[0] = y


def _s5_tables(a_re, a_im, b_re, b_im, c_re, c_im, log_dt, max_steps):
    g, p = a_re.shape
    sub = S5_SUB
    dt = jnp.exp(log_dt.astype(F32))[:, None]
    zr, zi = a_re.astype(F32) * dt, a_im.astype(F32) * dt

    def zpow(n):
        mag = jnp.exp(zr * n)
        return mag * jnp.cos(zi * n), mag * jnp.sin(zi * n)

    er = jnp.expm1(zr) * jnp.cos(zi) - 2.0 * jnp.square(jnp.sin(0.5 * zi))
    ei = jnp.exp(zr) * jnp.sin(zi)
    den = a_re * a_re + a_im * a_im
    fr = (er * a_re + ei * a_im) / den
    fi = (ei * a_re - er * a_im) / den
    bb_re = fr[..., None] * b_re - fi[..., None] * b_im
    bb_im = fr[..., None] * b_im + fi[..., None] * b_re

    lags = jnp.arange(sub + 1, dtype=F32)[:, None, None]
    pr, pi = zpow(lags)

    qr, qi = pr[:sub][::-1], pi[:sub][::-1]
    bcr = qr[..., None] * bb_re[None] - qi[..., None] * bb_im[None]
    bci = qr[..., None] * bb_im[None] + qi[..., None] * bb_re[None]
    bc = jnp.concatenate([bcr, bci], axis=2)
    bc = bc.transpose(1, 0, 3, 2).reshape(g, sub * S5_GROUP, 2 * p)

    ar, ai = pr[1:], pi[1:]
    ccr = c_re[None] * ar[:, :, None, :] - c_im[None] * ai[:, :, None, :]
    cci = c_re[None] * ai[:, :, None, :] + c_im[None] * ar[:, :, None, :]
    cc = jnp.concatenate([ccr, -cci], axis=3)
    cc = cc.transpose(1, 3, 0, 2).reshape(g, 2 * p, sub * S5_GROUP)

    kr = (jnp.einsum('gcp,lgp,gpd->lgcd', c_re, pr[:sub], bb_re, precision=HIGHEST)
          - jnp.einsum('gcp,lgp,gpd->lgcd', c_re, pi[:sub], bb_im, precision=HIGHEST)
          - jnp.einsum('gcp,lgp,gpd->lgcd', c_im, pr[:sub], bb_im, precision=HIGHEST)
          - jnp.einsum('gcp,lgp,gpd->lgcd', c_im, pi[:sub], bb_re, precision=HIGHEST))
    tt = jnp.arange(sub)
    lag = tt[None, :] - tt[:, None]
    toep = jnp.where((lag >= 0)[:, :, None, None, None],
                     kr[jnp.clip(lag, 0, sub - 1)], 0.0)
    m = toep.transpose(2, 0, 4, 1, 3).reshape(g, sub * S5_GROUP, sub * S5_GROUP)

    steps = (sub * (2 ** jnp.arange(max_steps))).astype(F32)[:, None, None]
    sr, si = zpow(steps)
    a1 = jnp.concatenate([sr, sr], axis=-1).transpose(1, 0, 2)
    a2 = jnp.concatenate([-si, si], axis=-1).transpose(1, 0, 2)
    pad = (-max_steps) % SUBLANES
    a1 = jnp.pad(a1, ((0, 0), (0, pad), (0, 0)))
    a2 = jnp.pad(a2, ((0, 0), (0, pad), (0, 0)))
    return bc, m.astype(BF16), cc.astype(BF16), a1, a2


def _s5_lane_kernel(h_ref, bct_ref, mt_ref, cct_ref, a1_ref, a2_ref, y_ref, hl_ref, ut_s, yt_s, hl_s,
                    *, nch, nsteps):
    sub, c = S5_SUB, S5_GROUP
    groups = ut_s.shape[0]
    for t in range(sub):
        att = h_ref[0, pl.ds(t, nch, stride=sub), :].T
        for g in range(groups):
            ut_s[g, t * c:(t + 1) * c, :] = att[g * c:(g + 1) * c, :]
    half = bct_ref.shape[1] // 2

    def group_pair(gp, carry):
        lane = lax.broadcasted_iota(jnp.int32, (2 * half, nch), 1)
        reps = nch // a1_ref.shape[3]
        gs = [2 * gp, 2 * gp + 1]
        uts = [ut_s[g] for g in gs]
        xs_ = [jnp.dot(bct_ref[g], ut, precision=HIGHEST, preferred_element_type=F32)
               for g, ut in zip(gs, uts)]
        for j in range(nsteps):
            s = 1 << j
            for i, g in enumerate(gs):
                sh = jnp.where(lane >= s, pltpu.roll(xs_[i], s, axis=1), 0.0)
                xs_[i] = (xs_[i] + jnp.tile(a1_ref[g, j], (1, reps)) * sh
                          + jnp.tile(a2_ref[g, j], (1, reps)) * pltpu.roll(sh, half, axis=0))
        for g, ut, x in zip(gs, uts, xs_):
            hl_s[g] = x[:, nch - LANES:].T[LANES - SUBLANES:, :]
            hstart = jnp.where(lane >= 1, pltpu.roll(x, 1, axis=1), 0.0)
            yt = jnp.dot(mt_ref[g], ut.astype(BF16), preferred_element_type=F32)
            yt = yt + jnp.dot(cct_ref[g], hstart.astype(BF16), preferred_element_type=F32)
            rows = pl.ds(pl.multiple_of(g * c, c), c)
            for t in range(sub):
                yt_s[t, rows, :] = yt[t * c:(t + 1) * c, :]
        return carry

    lax.fori_loop(0, groups // 2, group_pair, 0)
    for g in range(groups):
        hl_ref[0, g:g + 1, :] = hl_s[g, SUBLANES - 1:SUBLANES, :]
    for t in range(sub):
        y_ref[0, pl.ds(t, nch, stride=sub), :] = yt_s[t].T


def _s5_scan_lanes(h, tables):
    bc, m, cc, a1, a2 = tables
    b, l, d = h.shape
    g, k, p2 = bc.shape
    nch = l // S5_SUB
    gpb = LANES // S5_GROUP
    nsteps = nch.bit_length() - 1
    wspec = lambda shape: pl.BlockSpec((gpb,) + shape, lambda j, bi: (j, 0, 0))
    lane_bc = lambda a: jnp.broadcast_to(a[:, :nsteps, :, None], (g, nsteps, p2, LANES))
    mult = pl.BlockSpec((gpb, nsteps, p2, LANES), lambda j, bi: (j, 0, 0, 0), pipeline_mode=pl.Buffered(1))
    act = pl.BlockSpec((1, l, LANES), lambda j, bi: (bi, 0, j))
    return pl.pallas_call(
        functools.partial(_s5_lane_kernel, nch=nch, nsteps=nsteps),
        grid=(g // gpb, b),
        in_specs=[act, wspec((p2, k)), wspec((k, k)), wspec((k, p2)), mult, mult],
        out_specs=[act, pl.BlockSpec((1, gpb, p2), lambda j, bi: (bi, j, 0))],
        out_shape=[jax.ShapeDtypeStruct((b, l, d), F32), jax.ShapeDtypeStruct((b, g, p2), F32)],
        scratch_shapes=[pltpu.VMEM((gpb, k, nch), F32), pltpu.VMEM((S5_SUB, LANES, nch), F32),
                        pltpu.VMEM((gpb, SUBLANES, p2), F32)],
        compiler_params=_params("parallel", "parallel"),
        name="s5_scan_lanes",
    )(h, bc.transpose(0, 2, 1), m.transpose(0, 2, 1), cc.transpose(0, 2, 1), lane_bc(a1), lane_bc(a2))


def _s5_scan(h, tables, h0_re, h0_im):
    bc, m, cc, a1, a2 = tables
    b, l, d = h.shape
    if h0_re is None and (l // S5_SUB) % LANES == 0:
        y, hl = _s5_scan_lanes(h, tables)
        return y, hl[..., :hl.shape[-1] // 2], hl[..., hl.shape[-1] // 2:]
    g = d // S5_GROUP
    p2 = bc.shape[-1]
    nch = l // S5_SUB
    assert nch & (nch - 1) == 0
    nsteps = nch.bit_length() - 1
    rows = b * nch
    k = S5_SUB * S5_GROUP
    u = h.reshape(b, nch, S5_SUB, g, S5_GROUP).transpose(3, 0, 1, 2, 4).reshape(g, rows, k)
    has_h0 = h0_re is not None
    grp = lambda shape: pl.BlockSpec((1,) + shape, lambda i: (i, 0, 0))
    args = [u, bc, m, cc, a1, a2]
    in_specs = [grp((rows, k)), grp((k, p2)), grp((k, k)), grp((p2, k)),
                grp(a1.shape[1:]), grp(a2.shape[1:])]
    if has_h0:
        h0 = jnp.concatenate([h0_re, h0_im], axis=-1).astype(F32).transpose(1, 0, 2)
        h0 = jnp.pad(h0[:, :, None, :], ((0, 0), (0, 0), (0, nch - 1), (0, 0))).reshape(g, rows, p2)
        args.append(h0)
        in_specs.append(grp((rows, p2)))
    y, hl = pl.pallas_call(
        functools.partial(_s5_kernel, nch=nch, batch=b, nsteps=nsteps, has_h0=has_h0),
        grid=(g,),
        in_specs=in_specs,
        out_specs=[grp((rows, k)), grp((b, p2))],
        out_shape=[jax.ShapeDtypeStruct((g, rows, k), F32), jax.ShapeDtypeStruct((g, b, p2), F32)],
        compiler_params=_params("parallel"),
        name="s5_scan",
    )(*args)
    y = y.reshape(g, b, nch, S5_SUB, S5_GROUP).transpose(1, 2, 3, 0, 4).reshape(b, l, d)
    hl = hl.transpose(1, 0, 2)
    return y, hl[..., :p2 // 2], hl[..., p2 // 2:]


def _diff_qkv_kernel(x_ref, g_ref, sh_ref, sc_ref, w_ref, cos_ref, sin_ref,
                     k_ref, v_ref, qb_ref, kb_ref, vb_ref):
    nb, lt, d = x_ref.shape
    h = _modulated(x_ref, g_ref, sh_ref, sc_ref).reshape(nb * lt, d).astype(BF16)
    reps = d // cos_ref.shape[1]
    cos = jnp.tile(cos_ref[...], (1, reps))[None]
    sin = jnp.tile(sin_ref[...], (1, reps))[None]

    def roped(cols):
        t = jnp.dot(h, w_ref[:, cols * d:(cols + 1) * d], preferred_element_type=F32)
        r = _rotate_pairs(t, DIFF_DH // 2, DIFF_DH, 0)
        return t.reshape(nb, lt, d) * cos + r.reshape(nb, lt, d) * sin

    q = roped(0)
    qb_ref[...] = (q * (LOG2_E * DIFF_DH ** -0.5)).astype(BF16)
    k = roped(1)
    kb_ref[...] = k.astype(BF16)
    v = jnp.dot(h, w_ref[:, 2 * d:], preferred_element_type=F32).reshape(nb, lt, d)
    vb_ref[...] = v.astype(BF16)
    hw = k_ref.shape[2]
    heads = d // hw
    for hh in range(heads):
        k_ref[:, pl.ds(hh, lt, stride=heads), :] = k[:, :, hh * hw:(hh + 1) * hw]
        v_ref[:, pl.ds(hh, lt, stride=heads), :] = v[:, :, hh * hw:(hh + 1) * hw]


def _diff_qkv(x, g, sh, sc, w_qkv, cos, sin):
    b, l, d = x.shape
    nb, lt = _row_tiles(b, l)
    act, vec, par, pos = _tile_specs(b, l, d, nb, lt)
    hw = d // DIFF_HEADS
    return pl.pallas_call(
        _diff_qkv_kernel,
        grid=(b // nb, l // lt),
        in_specs=[act(d), par((1, d)), vec(d), vec(d), par((d, 3 * d)),
                  pos(cos.shape[1]), pos(sin.shape[1])],
        out_specs=[pl.BlockSpec((nb, lt * DIFF_HEADS, hw), lambda i, j: (i, j, 0))] * 2 + [act(d)] * 3,
        out_shape=[jax.ShapeDtypeStruct((b, l * DIFF_HEADS, hw), F32)] * 2
                  + [jax.ShapeDtypeStruct((b, l, d), BF16)] * 3,
        compiler_params=_params("parallel", "parallel"),
        name="diff_qkv",
    )(x, g, sh, sc, w_qkv, cos, sin)


_NT = (((1,), (1,)), ((), ()))
_TN = (((0,), (0,)), ((), ()))


def _softmax_update(m_s, l_s, acc_s, i, s, vb):
    m_prev = m_s[i]
    m_new = jnp.maximum(m_prev, jnp.max(s, axis=0, keepdims=True))
    alpha = jnp.exp2(m_prev - m_new)
    p = jnp.exp2(s - m_new)
    l_s[i] = alpha * l_s[i] + jnp.sum(p, axis=0, keepdims=True)
    acc_s[i] = alpha * acc_s[i] + lax.dot_general(vb, p.astype(BF16), _TN, preferred_element_type=F32)
    m_s[i] = m_new


def _softmax_init(m_s, l_s, acc_s):
    m_s[...] = jnp.full_like(m_s, NEG_INF)
    l_s[...] = jnp.zeros_like(l_s)
    acc_s[...] = jnp.zeros_like(acc_s)


def _softmax_result(l_s, acc_s, i):
    return acc_s[i] / l_s[i]


def _softmax_scratch(n, vw, queries):
    return [pltpu.VMEM((n, 1, queries), F32), pltpu.VMEM((n, 1, queries), F32),
            pltpu.VMEM((n, vw, queries), F32)]


def _attend_causal(scores, k_ref, v_ref, m_s, l_s, acc_s, s_a, s_b, *, tq, qi):
    assert len(scores) == 2 and tq % CHUNK == 0
    _softmax_init(m_s, l_s, acc_s)
    block = lambda ref, j: ref[0, pl.ds(pl.multiple_of(j * tq, tq), tq), :].astype(BF16)

    def qk(dst, j, streams):
        kb = block(k_ref, j)
        for i in streams:
            dst[i] = scores[i](kb)

    def pv(src, j, streams, masked=None):
        vb = block(v_ref, j)
        for i in streams:
            for c in range(0, 2 * tq, ATTN_LANE_BLOCK):
                cols = slice(c, c + ATTN_LANE_BLOCK)
                if i == masked:
                    src[i, :, cols] = jnp.where(vis[:, cols], src[i, :, cols], NEG_INF)
                m_prev = m_s[i, :, cols]
                m_new = jnp.maximum(m_prev, jnp.max(src[i, :, cols], axis=0, keepdims=True))
                alpha = jnp.exp2(m_prev - m_new)
                p = jnp.exp2(src[i, :, cols] - m_new)
                l_s[i, :, cols] = alpha * l_s[i, :, cols] + jnp.sum(p, axis=0, keepdims=True)
                acc_s[i, :, cols] = alpha * acc_s[i, :, cols] + lax.dot_general(
                    vb, p.astype(BF16), _TN, preferred_element_type=F32)
                m_s[i, :, cols] = m_new

    def pair(jj, carry):
        qk(s_b, 2 * jj + 1, (0, 1))
        pv(s_a, 2 * jj, (0, 1))
        qk(s_a, 2 * jj + 2, (0, 1))
        pv(s_b, 2 * jj + 1, (0, 1))
        return carry

    qk(s_a, 0, (0, 1))
    lax.fori_loop(0, qi, pair, 0)
    kc = lax.broadcasted_iota(jnp.int32, (tq, tq), 0) // CHUNK
    qc = lax.broadcasted_iota(jnp.int32, (tq, tq), 1) // CHUNK
    vis = jnp.concatenate([kc <= qc] * 2, axis=1)
    qk(s_b, 2 * qi + 1, (1,))
    pv(s_a, 2 * qi, (0, 1), masked=0)
    pv(s_b, 2 * qi + 1, (1,), masked=1)
    return [_softmax_result(l_s, acc_s, i) for i in range(2)]


def _diff_attn_kernel(q_ref, k_ref, v_ref, lam_ref, gsub_ref, o_ref, m_s, l_s, acc_s, s_a, s_b, *, tq):
    def make_score(i):
        q2 = _diff_query_groups(q_ref[0, i * tq:(i + 1) * tq, :])
        return lambda kb: lax.dot_general(kb, q2, _NT, preferred_element_type=F32)

    outs = _attend_causal([make_score(0), make_score(1)], k_ref, v_ref, m_s, l_s, acc_s, s_a, s_b,
                          tq=tq, qi=pl.program_id(2))
    lam = _diff_lambda(lam_ref)
    for i, o2 in enumerate(outs):
        o = (o2[:, :tq] - lam * o2[:, tq:]).T
        o = _rms(o, gsub_ref[...]) * (1.0 - LAMBDA_INIT)
        o_ref[0, i * tq:(i + 1) * tq, :] = o.astype(o_ref.dtype)


def _causal_attention_call(kernel, qb, k, v, extra, qw, vw, name):
    b, l, _ = qb.shape
    tq, ns = ATTN_TQ, ATTN_STREAMS
    assert l % (ns * tq) == 0
    tile = lambda width: pl.BlockSpec((1, ns * tq, width), lambda bi, h, qi: (bi, qi, h))
    seq = lambda width: pl.BlockSpec((1, l, width), lambda bi, h, qi: (bi, 0, h))
    units = qb.shape[2] // qw
    return pl.pallas_call(
        functools.partial(kernel, tq=tq),
        grid=(b, units, l // (ns * tq)),
        in_specs=[tile(qw), seq(qw), seq(vw)]
                 + [pl.BlockSpec(a.shape, lambda bi, h, qi: (0, 0)) for a in extra],
        out_specs=tile(vw),
        out_shape=jax.ShapeDtypeStruct((b, l, units * vw), BF16),
        scratch_shapes=_softmax_scratch(ns, vw, 2 * tq) + [pltpu.VMEM((ns, tq, 2 * tq), F32)] * 2,
        compiler_params=_params("parallel", "parallel", "arbitrary"),
        name=name,
    )(qb, k, v, *extra)


def _diff_attention(qb, kb, vb, lam_rows, g_sub):
    hw = 2 * DIFF_DH
    return _causal_attention_call(_diff_attn_kernel, qb, kb, vb, [lam_rows, g_sub], hw, hw,
                                  "diff_attention")


def _diff_lambda(lam_ref):
    lp = lam_ref[...]
    return (jnp.exp(jnp.sum(lp[0:1] * lp[1:2], axis=-1, keepdims=True))
            - jnp.exp(jnp.sum(lp[2:3] * lp[3:4], axis=-1, keepdims=True)) + LAMBDA_INIT)


def _diff_query_groups(q):
    lane = lax.broadcasted_iota(jnp.int32, q.shape, 1)
    zero = jnp.zeros_like(q)
    return jnp.concatenate([jnp.where(lane < DIFF_DH, q, zero), jnp.where(lane >= DIFF_DH, q, zero)], axis=0)


def _diff_cached_kernel(q_ref, kc_ref, vc_ref, kn_ref, vn_ref, lam_ref, gsub_ref, o_ref,
                        m_s, l_s, acc_s, *, tk):
    c = pl.program_id(1)
    lq = q_ref.shape[1]
    hw = kc_ref.shape[2]
    heads = q_ref.shape[2] // hw
    update = functools.partial(_softmax_update, m_s, l_s, acc_s)
    q2 = lambda hh: _diff_query_groups(q_ref[0, :, hh * hw:(hh + 1) * hw])

    @pl.when(c == 0)
    def _():
        _softmax_init(m_s, l_s, acc_s)

    def block(j, carry):
        rows = pl.ds(pl.multiple_of(j * tk * heads, tk * heads), tk * heads)
        k_blk, v_blk = kc_ref.at[0, rows, :], vc_ref.at[0, rows, :]
        for hh in range(heads):
            kb = k_blk[pl.ds(hh, tk, stride=heads), :].astype(BF16)
            vb = v_blk[pl.ds(hh, tk, stride=heads), :].astype(BF16)
            update(hh, lax.dot_general(kb, q2(hh), _NT, preferred_element_type=F32), vb)
        return carry

    lax.fori_loop(0, kc_ref.shape[1] // (tk * heads), block, 0)

    @pl.when(c == pl.num_programs(1) - 1)
    def _():
        lam = _diff_lambda(lam_ref)
        for hh in range(heads):
            kb = kn_ref[0, :, hh * hw:(hh + 1) * hw]
            vb = vn_ref[0, :, hh * hw:(hh + 1) * hw]
            update(hh, lax.dot_general(kb, q2(hh), _NT, preferred_element_type=F32), vb)
            o2 = _softmax_result(l_s, acc_s, hh)
            o = (o2[:, :lq] - lam * o2[:, lq:]).T
            o = _rms(o, gsub_ref[...]) * (1.0 - LAMBDA_INIT)
            o_ref[0, :, hh * hw:(hh + 1) * hw] = o.astype(o_ref.dtype)


def _diff_attention_cached(qb, cache_k, cache_v, kb, vb, lam_rows, g_sub):
    b, l, d = qb.shape
    _, p, heads, hw = cache_k.shape
    pc = min(p, CACHE_CHUNK)
    tk = ATTN_TK if pc % ATTN_TK == 0 else CHUNK
    assert p % pc == 0 and pc % tk == 0
    new = pl.BlockSpec((1, l, d), lambda bi, c: (bi, 0, 0))
    cache = pl.BlockSpec((1, pc * heads, hw), lambda bi, c: (bi, c, 0))
    return pl.pallas_call(
        functools.partial(_diff_cached_kernel, tk=tk),
        grid=(b, p // pc),
        in_specs=[new, cache, cache, new, new,
                  pl.BlockSpec(lam_rows.shape, lambda bi, c: (0, 0)),
                  pl.BlockSpec((1, hw), lambda bi, c: (0, 0))],
        out_specs=new,
        out_shape=jax.ShapeDtypeStruct((b, l, d), BF16),
        scratch_shapes=_softmax_scratch(heads, hw, 2 * l),
        compiler_params=_params("parallel", "arbitrary"),
        name="diff_attention_cached",
    )(qb, cache_k.reshape(b, p * heads, hw), cache_v.reshape(b, p * heads, hw), kb, vb, lam_rows, g_sub)


def _mla_proj_kernel(*refs, cached):
    (x_ref, g_ref, sh_ref, sc_ref, wq_ref, wc_ref, wr_ref, wrr_ref, gq_ref, gkv_ref, kcos_ref, ksin_ref,
     wuq_ref, wuqr_ref, qcos_ref, qsin_ref, *rest) = refs
    nb, lt, d = x_ref.shape
    rows = nb * lt
    dot = lambda a, w_ref: jnp.dot(a, w_ref[...], preferred_element_type=F32)
    h = _modulated(x_ref, g_ref, sh_ref, sc_ref).reshape(rows, d).astype(BF16)
    cq = _rms(dot(h, wq_ref), gq_ref[...]).astype(BF16)
    ckv = _rms(dot(h, wc_ref), gkv_ref[...])
    rope = wr_ref.shape[1]
    kr = (dot(h, wr_ref).reshape(nb, lt, rope) * kcos_ref[...][None]
          + dot(h, wrr_ref).reshape(nb, lt, rope) * ksin_ref[...][None])
    n = wuq_ref.shape[1]
    reps = n // qcos_ref.shape[1]
    cos = jnp.tile(qcos_ref[...], (1, reps))[None]
    sin = jnp.tile(qsin_ref[...], (1, reps))[None]
    q = (dot(cq, wuq_ref).reshape(nb, lt, n) * cos + dot(cq, wuqr_ref).reshape(nb, lt, n) * sin).astype(BF16)
    if cached:
        wabs_ref, ckv_ref, kr_ref, q_ref = rest
        qf = q.reshape(rows, n)
        hp = MLA_HEAD_PAD
        for hh in range(n // hp):
            o = jnp.dot(qf[:, hh * hp:(hh + 1) * hp], wabs_ref[hh], preferred_element_type=F32)
            q_ref[:, hh] = o.reshape(nb, lt, o.shape[1]).astype(q_ref.dtype)
    else:
        wk_ref, we_ref, wv_ref, ckv_ref, kr_ref, q_ref, k_ref, v_ref = rest
        q_ref[...] = q
        c = ckv.astype(BF16)
        k = dot(c, wk_ref) + dot(kr.reshape(rows, rope).astype(BF16), we_ref)
        k_ref[...] = k.reshape(k_ref.shape).astype(k_ref.dtype)
        v_ref[...] = dot(c, wv_ref).reshape(v_ref.shape).astype(v_ref.dtype)
    ckv_ref[...] = ckv.reshape(ckv_ref.shape)
    kr_ref[...] = kr


def _mla_project(x, g, sh, sc, w, k_cos, k_sin, q_cos, q_sin, cached):
    b, l, d = x.shape
    nb, lt = _row_tiles(b, l)
    act, vec, par, pos = _tile_specs(b, l, d, nb, lt)
    weights = [w['mla_dq'], w['mla_dc'], w['mla_dr'], w['mla_drr'], w['mla_g_q'], w['mla_g_kv']]
    q_weights = [w['mla_uq'], w['mla_uq_rot']]
    kvr, rope, n = w['mla_dc'].shape[1], w['mla_dr'].shape[1], w['mla_uq'].shape[1]
    out_specs = [act(kvr), act(rope)]
    out_shape = [jax.ShapeDtypeStruct((b, l, kvr), F32), jax.ShapeDtypeStruct((b, l, rope), F32)]
    if cached:
        tail = [w['mla_abs']]
        heads, _, qw = w['mla_abs'].shape
        out_specs.append(pl.BlockSpec((nb, heads, lt, qw), lambda i, j: (i, 0, j, 0)))
        out_shape.append(jax.ShapeDtypeStruct((b, heads, l, qw), BF16))
    else:
        tail = [w['mla_uk'], w['mla_place'], w['mla_uv']]
        nv = w['mla_uv'].shape[1]
        out_specs += [act(n), act(n), act(nv)]
        out_shape += [jax.ShapeDtypeStruct((b, l, n), BF16)] * 2 + [jax.ShapeDtypeStruct((b, l, nv), BF16)]
    return pl.pallas_call(
        functools.partial(_mla_proj_kernel, cached=cached),
        grid=(b // nb, l // lt),
        in_specs=[act(d), par((1, d)), vec(d), vec(d)] + [par(a.shape) for a in weights]
                 + [pos(rope), pos(rope)] + [par(a.shape) for a in q_weights]
                 + [pos(q_cos.shape[1]), pos(q_sin.shape[1])] + [par(a.shape) for a in tail],
        out_specs=out_specs,
        out_shape=out_shape,
        compiler_params=_params("parallel", "parallel"),
        name="mla_project",
    )(x, g, sh, sc, *weights, k_cos, k_sin, *q_weights, q_cos, q_sin, *tail)


def _mla_attn_kernel(q_ref, k_ref, v_ref, o_ref, m_s, l_s, acc_s, s_a, s_b, *, tq):
    hp = MLA_HEAD_PAD

    def make_score(i):
        qa = q_ref[0, i * tq:(i + 1) * tq, :hp]
        qb = q_ref[0, i * tq:(i + 1) * tq, hp:]
        return lambda kb: jnp.concatenate(
            [lax.dot_general(kb[:, :hp], qa, _NT, preferred_element_type=F32),
             lax.dot_general(kb[:, hp:], qb, _NT, preferred_element_type=F32)], axis=1)

    outs = _attend_causal([make_score(0), make_score(1)], k_ref, v_ref, m_s, l_s, acc_s, s_a, s_b,
                          tq=tq, qi=pl.program_id(2))
    row = lax.broadcasted_iota(jnp.int32, (acc_s.shape[1], tq), 0)
    for i, o2 in enumerate(outs):
        o = jnp.where(row < MLA_V, o2[:, :tq], o2[:, tq:]).T
        o_ref[0, i * tq:(i + 1) * tq, :] = o.astype(o_ref.dtype)


def _mla_attention(qb, kb, vb):
    return _causal_attention_call(_mla_attn_kernel, qb, kb, vb, [], 2 * MLA_HEAD_PAD, 2 * MLA_V,
                                  "mla_attention")


def _mla_cached_kernel(q_ref, cc_ref, rc_ref, cn_ref, rn_ref, wuvt_ref, o_ref, m_s, l_s, acc_s, *, tk):
    heads, lq, qw = q_ref.shape[1:]
    q2 = q_ref[0].reshape(heads * lq, qw)

    def step(c, r):
        rpad = jnp.concatenate([r, jnp.zeros((r.shape[0], qw - c.shape[1] - r.shape[1]), r.dtype)], axis=1)
        kb = jnp.concatenate([c, rpad], axis=1).astype(BF16)
        s = lax.dot_general(kb, q2, _NT, preferred_element_type=F32)
        _softmax_update(m_s, l_s, acc_s, 0, s, c.astype(BF16))

    def block(j, carry):
        rows = pl.ds(pl.multiple_of(j * tk, tk), tk)
        step(cc_ref[0, rows, :], rc_ref[0, rows, :])
        return carry

    _softmax_init(m_s, l_s, acc_s)
    lax.fori_loop(0, cc_ref.shape[1] // tk, block, 0)
    step(cn_ref[0], rn_ref[0])
    lat = _softmax_result(l_s, acc_s, 0)
    assert 2 * lq == LANES and 2 * MLA_V == LANES
    lane = lax.broadcasted_iota(jnp.int32, (lq, LANES), 1)
    for pr in range(heads // 2):
        lp = lat[:, pr * LANES:(pr + 1) * LANES].astype(BF16)
        tt = jnp.dot(wuvt_ref[pr], lp, preferred_element_type=F32).T
        o_ref[0, :, pr * LANES:(pr + 1) * LANES] = jnp.where(lane < MLA_V, tt[:lq], tt[lq:]).astype(o_ref.dtype)


def _mla_attention_cached(q_abs, cache_ckv, cache_krope, ckv, krope, w_uvt):
    b, heads, l, qw = q_abs.shape
    p, rank = cache_ckv.shape[1:]
    rope = cache_krope.shape[2]
    tk = ATTN_TK if p % ATTN_TK == 0 else CHUNK
    assert p % tk == 0
    full = lambda shape: pl.BlockSpec((1,) + shape, lambda bi: (bi,) + (0,) * len(shape))
    return pl.pallas_call(
        functools.partial(_mla_cached_kernel, tk=tk),
        grid=(b,),
        in_specs=[full((heads, l, qw)), full((p, rank)), full((p, rope)), full((l, rank)), full((l, rope)),
                  pl.BlockSpec(w_uvt.shape, lambda bi: (0, 0, 0))],
        out_specs=full((l, heads * MLA_V)),
        out_shape=jax.ShapeDtypeStruct((b, l, heads * MLA_V), BF16),
        scratch_shapes=_softmax_scratch(1, rank, heads * l),
        compiler_params=_params("parallel"),
        name="mla_attention_cached",
    )(q_abs, cache_ckv, cache_krope, ckv, krope, w_uvt)


def _sgu_kernel(x_ref, g_ref, sh_ref, sc_ref, gt_ref, win_ref, gv_ref, ws_ref, bs_ref, wout_ref,
                *out_refs, t, emit_v):
    o_ref = out_refs[0]
    nb, lt, d = x_ref.shape
    rows = nb * lt
    width = gv_ref.shape[1]
    gd = width // SGU_GROUPS
    h = _modulated(x_ref, g_ref, sh_ref, sc_ref).reshape(rows, d).astype(BF16)
    u = jax.nn.gelu(jnp.dot(h, win_ref[:, :width], preferred_element_type=F32))
    v = jax.nn.gelu(jnp.dot(h, win_ref[:, width:], preferred_element_type=F32))
    v = _rms(v, gv_ref[...])
    if emit_v:
        out_refs[1][...] = v.reshape(nb, lt, width)
    vb = v.astype(BF16)
    bias = bs_ref[...]
    gated = []
    for c in range(rows // t):
        sv = [jnp.dot(ws_ref[gi], vb[c * t:(c + 1) * t, gi * gd:(gi + 1) * gd],
                      preferred_element_type=F32) for gi in range(SGU_GROUPS)]
        sv = jnp.concatenate(sv, axis=1) + bias
        gated.append((u[c * t:(c + 1) * t] * sv).astype(BF16))
    gated = jnp.concatenate(gated, axis=0) if len(gated) > 1 else gated[0]
    r = jnp.dot(gated, wout_ref[...], preferred_element_type=F32).reshape(nb, lt, d)
    o_ref[...] = x_ref[...] + (1.0 + gt_ref[...]) * r


def _sgu(x, g, sh, sc, gt, w_in, g_v, w_s, b_full, w_out, emit_v):
    b, l, d = x.shape
    t = w_s.shape[1]
    width = g_v.shape[1]
    nb, lt = _row_tiles(b, l, 512)
    assert lt % t == 0
    act, vec, par, _ = _tile_specs(b, l, d, nb, lt)
    resident = lambda shape: pl.BlockSpec(shape, lambda i, j: (0, 0), pipeline_mode=pl.Buffered(1))
    out_specs = [act(d)]
    out_shape = [jax.ShapeDtypeStruct((b, l, d), F32)]
    if emit_v:
        out_specs.append(act(width))
        out_shape.append(jax.ShapeDtypeStruct((b, l, width), F32))
    return pl.pallas_call(
        functools.partial(_sgu_kernel, t=t, emit_v=emit_v),
        grid=(b // nb, l // lt),
        in_specs=[act(d), par((1, d)), vec(d), vec(d), vec(d), resident(w_in.shape), par((1, width)),
                  par(w_s.shape), par(b_full.shape), resident(w_out.shape)],
        out_specs=out_specs,
        out_shape=out_shape,
        compiler_params=_params("parallel", "parallel"),
        name="sgu",
    )(x, g, sh, sc, gt, w_in, g_v, w_s, b_full, w_out)


def _rope_angles(pos, half):
    inv_freq = ROPE_THETA ** (-jnp.arange(half, dtype=F32) / half)
    ang = pos.astype(F32)[:, None] * inv_freq[None, :]
    return jnp.cos(ang), jnp.sin(ang)


def _diff_rope_tables(pos):
    cos, sin = _rope_angles(pos, DIFF_DH // 2)
    reps = LANES // DIFF_DH
    return (jnp.tile(jnp.concatenate([cos, cos], axis=1), (1, reps)),
            jnp.tile(jnp.concatenate([-sin, sin], axis=1), (1, reps)))


def _mla_rope_tables(pos):
    cos, sin = _rope_angles(pos, MLA_ROPE // 2)
    n = pos.shape[0]
    ones = jnp.ones((n, MLA_NOPE), F32)
    zq = jnp.zeros((n, MLA_NOPE), F32)
    zp = jnp.zeros((n, MLA_HEAD_PAD - MLA_NOPE - MLA_ROPE), F32)
    scale = LOG2_E * MLA_SCALE
    q_cos = jnp.concatenate([ones, cos, cos, zp], axis=1) * scale
    q_sin = jnp.concatenate([zq, -sin, sin, zp], axis=1) * scale
    k_cos = jnp.concatenate([cos, cos], axis=1)
    k_sin = jnp.concatenate([sin, sin], axis=1)
    return q_cos, q_sin, k_cos, k_sin


def _prepare_weights(p):
    d = p['w_up'].shape[1]
    w = {}
    w['w_up'] = [p['w_up'][i].astype(BF16) for i in range(p['w_up'].shape[0])]
    w['w_down'] = [p['w_down'][i].astype(BF16) for i in range(p['w_down'].shape[0])]
    w['glu_a'] = p['s5_w_glu_a'].astype(BF16)
    w['glu_b'] = p['s5_w_glu_b'].astype(BF16)
    w['diff_qkv'] = p['diff_w_qkv'].astype(BF16)
    w['diff_o'] = p['diff_w_o'].astype(BF16)
    lam = jnp.stack([p['diff_lambda_q1'], p['diff_lambda_k1'], p['diff_lambda_q2'], p['diff_lambda_k2']])
    w['diff_lam'] = jnp.pad(lam.astype(F32), ((0, 4), (0, LANES - lam.shape[1])))
    w['diff_g_sub'] = p['diff_g_sub'].reshape(1, -1)

    w['mla_dq'] = p['mla_w_dq'].astype(BF16)
    kvr = p['mla_g_kv'].shape[0]
    w['mla_dc'] = p['mla_w_dkv'][:, :kvr].astype(BF16)
    wr = p['mla_w_dkv'][:, kvr:]
    hr = MLA_ROPE // 2
    w['mla_dr'] = wr.astype(BF16)
    w['mla_drr'] = jnp.concatenate([-wr[:, hr:], wr[:, :hr]], axis=1).astype(BF16)
    w['mla_g_q'] = p['mla_g_q'].reshape(1, -1)
    w['mla_g_kv'] = p['mla_g_kv'].reshape(1, -1)
    pad = MLA_HEAD_PAD - MLA_NOPE - MLA_ROPE
    qr = p['mla_w_uq'].shape[0]
    uq = p['mla_w_uq'].reshape(qr, MLA_HEADS, MLA_NOPE + MLA_ROPE)
    w['mla_uq'] = jnp.pad(uq, ((0, 0), (0, 0), (0, pad))).reshape(qr, -1).astype(BF16)
    uq_rot = jnp.concatenate([jnp.zeros_like(uq[..., :MLA_NOPE]), uq[..., MLA_NOPE + hr:],
                              uq[..., MLA_NOPE:MLA_NOPE + hr]], axis=2)
    w['mla_uq_rot'] = jnp.pad(uq_rot, ((0, 0), (0, 0), (0, pad))).reshape(qr, -1).astype(BF16)
    uk = jnp.pad(p['mla_w_uk'], ((0, 0), (0, 0), (0, MLA_HEAD_PAD - MLA_NOPE)))
    w['mla_uk'] = uk.reshape(kvr, -1).astype(BF16)
    place = jnp.pad(jnp.eye(MLA_ROPE, dtype=F32), ((0, 0), (MLA_NOPE, pad)))
    w['mla_place'] = jnp.tile(place, (1, MLA_HEADS)).astype(BF16)
    w['mla_uv'] = p['mla_w_uv'].reshape(kvr, -1).astype(BF16)
    absorb = jnp.pad(p['mla_w_uk'].transpose(1, 2, 0), ((0, 0), (0, MLA_HEAD_PAD - MLA_NOPE), (0, 0)))
    select = jnp.pad(jnp.eye(MLA_ROPE, dtype=F32), ((MLA_NOPE, pad), (0, MLA_HEAD_PAD - MLA_ROPE)))
    w['mla_abs'] = jnp.concatenate(
        [absorb, jnp.broadcast_to(select, (MLA_HEADS,) + select.shape)], axis=2).astype(BF16)
    w['mla_uvt'] = p['mla_w_uv'].transpose(1, 2, 0).reshape(MLA_HEADS // 2, 2 * MLA_V, kvr).astype(BF16)
    w['mla_o'] = p['mla_w_o'].astype(BF16)

    w['sgu_in'] = p['sgu_w_in'].astype(BF16)
    w['sgu_g_v'] = p['sgu_g_v'].reshape(1, -1)
    w['sgu_out'] = p['sgu_w_out'].astype(BF16)
    return w


def _sgu_spatial(w_s, b_s, t, width):
    ws = jnp.tril(w_s[:, :t, :t]).astype(BF16)
    gd = width // SGU_GROUPS
    b_full = jnp.repeat(b_s[:, :t].T.astype(F32), gd, axis=1)
    return ws, b_full


def _run_trunk(x, mods, pos, past, p, w, s5_tables):
    b, l, d = x.shape
    new = {}
    depth = p['g_mix'].shape[0]
    for i in range(depth):
        sh1, sc1, gt1, sh2, sc2, gt2 = [mods[i][:, k][:, None, :] for k in range(6)]
        g_mix = p['g_mix'][i].reshape(1, d)
        kind = i % N_MIXERS
        mixer = None
        if kind == 0:
            h = _modulate(x, g_mix, sh1, sc1)
            h0_re = None if past is None else past['s5_re']
            h0_im = None if past is None else past['s5_im']
            y, new['s5_re'], new['s5_im'] = _s5_scan(h, s5_tables, h0_re, h0_im)
            mixer = ('glu', y, h, p['s5_d'].reshape(1, d), w['glu_a'], w['glu_b'], gt1)
        elif kind == 1:
            cos, sin = _diff_rope_tables(pos)
            k, v, qb, kb, vb = _diff_qkv(x, g_mix, sh1, sc1, w['diff_qkv'], cos, sin)
            if past is None:
                o = _diff_attention(qb, kb, vb, w['diff_lam'], w['diff_g_sub'])
            else:
                o = _diff_attention_cached(qb, past['diff_k'], past['diff_v'], kb, vb,
                                           w['diff_lam'], w['diff_g_sub'])
            mixer = ('proj', o, w['diff_o'], gt1)
            new['diff_k'] = k.reshape(b, l, DIFF_HEADS, 2 * DIFF_DH)
            new['diff_v'] = v.reshape(b, l, DIFF_HEADS, 2 * DIFF_DH)
        elif kind == 2:
            q_cos, q_sin, k_cos, k_sin = _mla_rope_tables(pos)
            proj = _mla_project(x, g_mix, sh1, sc1, w, k_cos, k_sin, q_cos, q_sin, cached=past is not None)
            ckv, krope = proj[:2]
            if past is None:
                o = _mla_attention(*proj[2:])
            else:
                o = _mla_attention_cached(proj[2], past['mla_ckv'], past['mla_krope'], ckv, krope,
                                          w['mla_uvt'])
            mixer = ('proj', o, w['mla_o'], gt1)
            new['mla_ckv'], new['mla_krope'] = ckv, krope
        else:
            t = min(l, SGU_CHUNK)
            ws, b_full = _sgu_spatial(p['sgu_w_s'], p['sgu_b_s'], t, w['sgu_g_v'].shape[1])
            outs = _sgu(x, g_mix, sh1, sc1, gt1, w['sgu_in'], w['sgu_g_v'], ws, b_full, w['sgu_out'],
                        emit_v=past is not None)
            x = outs[0]
            if past is not None:
                new['sgu_v'] = outs[1]
        x = _mlp(x, p['g_ffn'][i].reshape(1, d), sh2, sc2, gt2, w['w_up'][i], w['w_down'][i],
                 p['g_final'].reshape(1, d), final_norm=(i == depth - 1), mixer=mixer)
    return x, new


def kernel(x_prompt, x_sample, c_prompt, c_sample, state_s5_re, state_s5_im, cache_diff_k, cache_diff_v, cache_mla_ckv, cache_mla_krope, w_ada, b_ada, g_mix, g_ffn, w_up, w_down, g_final, s5_a_re, s5_a_im, s5_b_re, s5_b_im, s5_c_re, s5_c_im, s5_d, s5_log_dt, s5_w_glu_a, s5_w_glu_b, diff_w_qkv, diff_lambda_q1, diff_lambda_k1, diff_lambda_q2, diff_lambda_k2, diff_g_sub, diff_w_o, mla_w_dq, mla_g_q, mla_w_uq, mla_w_dkv, mla_g_kv, mla_w_uk, mla_w_uv, mla_w_o, sgu_w_in, sgu_g_v, sgu_w_s, sgu_b_s, sgu_w_out):
    p = {
        'w_ada': w_ada, 'b_ada': b_ada, 'g_mix': g_mix, 'g_ffn': g_ffn,
        'w_up': w_up, 'w_down': w_down, 'g_final': g_final,
        's5_d': s5_d, 's5_w_glu_a': s5_w_glu_a, 's5_w_glu_b': s5_w_glu_b,
        'diff_w_qkv': diff_w_qkv, 'diff_lambda_q1': diff_lambda_q1, 'diff_lambda_k1': diff_lambda_k1,
        'diff_lambda_q2': diff_lambda_q2, 'diff_lambda_k2': diff_lambda_k2,
        'diff_g_sub': diff_g_sub, 'diff_w_o': diff_w_o,
        'mla_w_dq': mla_w_dq, 'mla_g_q': mla_g_q, 'mla_w_uq': mla_w_uq, 'mla_w_dkv': mla_w_dkv,
        'mla_g_kv': mla_g_kv, 'mla_w_uk': mla_w_uk, 'mla_w_uv': mla_w_uv, 'mla_w_o': mla_w_o,
        'sgu_w_in': sgu_w_in, 'sgu_g_v': sgu_g_v, 'sgu_w_s': sgu_w_s, 'sgu_b_s': sgu_b_s,
        'sgu_w_out': sgu_w_out,
    }
    past = {
        's5_re': state_s5_re, 's5_im': state_s5_im,
        'diff_k': cache_diff_k, 'diff_v': cache_diff_v,
        'mla_ckv': cache_mla_ckv, 'mla_krope': cache_mla_krope,
    }
    bp, lp, d = x_prompt.shape
    bs, ls, _ = x_sample.shape
    depth = w_ada.shape[0]

    c_all = jnp.concatenate([c_prompt, c_sample], axis=0)
    c_all = jnp.pad(c_all, ((0, (-c_all.shape[0]) % SUBLANES), (0, 0)))
    mod = _adaln(c_all, w_ada, b_ada).reshape(depth, c_all.shape[0], 6, d)
    mods_p = [mod[i, :bp] for i in range(depth)]
    mods_s = [mod[i, bp:bp + bs] for i in range(depth)]

    w = _prepare_weights(p)
    max_steps = max((lp // S5_SUB).bit_length() - 1, (ls // S5_SUB).bit_length() - 1, 1)
    s5_tables = _s5_tables(s5_a_re, s5_a_im, s5_b_re, s5_b_im, s5_c_re, s5_c_im, s5_log_dt, max_steps)

    pos_p = jnp.arange(lp, dtype=jnp.int32)
    pos_s = cache_diff_k.shape[1] + jnp.arange(ls, dtype=jnp.int32)
    y_prompt, sp = _run_trunk(x_prompt, mods_p, pos_p, None, p, w, s5_tables)
    y_sample, ss = _run_trunk(x_sample, mods_s, pos_s, past, p, w, s5_tables)
    return (y_prompt, y_sample,
            sp['s5_re'], sp['s5_im'], ss['s5_re'], ss['s5_im'],
            sp['diff_k'], sp['diff_v'], ss['diff_k'], ss['diff_v'],
            sp['mla_ckv'], sp['mla_krope'], ss['mla_ckv'], ss['mla_krope'],
            ss['sgu_v'])
```

```python
import functools
import math

import jax
import jax.numpy as jnp
from jax import lax
from jax.experimental import pallas as pl
from jax.experimental.pallas import tpu as pltpu

F32 = jnp.float32
BF16 = jnp.bfloat16
HIGHEST = lax.Precision.HIGHEST

NORM_EPS = 1e-6
ROPE_THETA = 10000.0
NEG_INF = -1e30
LOG2_E = math.log2(math.e)
CHUNK = 64
N_MIXERS = 4

S5_GROUP = 16
S5_SUB = 16

DIFF_HEADS = 8
DIFF_DH = 64
LAMBDA_INIT = 0.8 - 0.6 * math.exp(-0.3 * 1)

MLA_HEADS = 16
MLA_NOPE = 64
MLA_ROPE = 32
MLA_V = 64
MLA_SCALE = (MLA_NOPE + MLA_ROPE) ** -0.5
MLA_HEAD_PAD = 128

SGU_CHUNK = 128
SGU_GROUPS = 8

LANES = 128
SUBLANES = 8
ROW_TILE = 512
ATTN_TQ = 512
ATTN_STREAMS = 2
ATTN_LANE_BLOCK = 256
ATTN_TK = 512
CACHE_CHUNK = 1024
VMEM_LIMIT_BYTES = 48 * 1024 * 1024


def _params(*sem):
    return pltpu.CompilerParams(dimension_semantics=sem, vmem_limit_bytes=VMEM_LIMIT_BYTES)


def _row_tiles(batch, length, rows=ROW_TILE):
    if length >= rows:
        assert length % rows == 0
        return 1, rows
    nb = max(1, min(batch, rows // length))
    while batch % nb:
        nb -= 1
    return nb, length


def _rms(x, g):
    return x * lax.rsqrt(jnp.mean(x * x, axis=-1, keepdims=True) + NORM_EPS) * g


def _modulated(x_ref, g_ref, sh_ref, sc_ref):
    return _rms(x_ref[...], g_ref[...]) * (1.0 + sc_ref[...]) + sh_ref[...]


def _rotate_pairs(x, half, period, lo):
    width = x.shape[-1]
    lane = lax.broadcasted_iota(jnp.int32, x.shape, x.ndim - 1) % period
    fwd = pltpu.roll(x, width - half, axis=x.ndim - 1)
    bwd = pltpu.roll(x, half, axis=x.ndim - 1)
    return jnp.where((lane >= lo) & (lane < lo + half), fwd, bwd)


def _adaln_kernel(c_ref, w_ref, b_ref, o_ref):
    c = c_ref[...]
    s = c * jax.nn.sigmoid(c)
    o_ref[0] = jnp.dot(s, w_ref[0], precision=HIGHEST, preferred_element_type=F32) + b_ref[0]


def _adaln(c_all, w_ada, b_ada):
    depth, d, n = w_ada.shape
    rows = c_all.shape[0]
    tn = n // 4
    return pl.pallas_call(
        _adaln_kernel,
        grid=(depth, n // tn),
        in_specs=[pl.BlockSpec((rows, d), lambda i, j: (0, 0)),
                  pl.BlockSpec((1, d, tn), lambda i, j: (i, 0, j)),
                  pl.BlockSpec((1, 1, tn), lambda i, j: (i, 0, j))],
        out_specs=pl.BlockSpec((1, rows, tn), lambda i, j: (i, 0, j)),
        out_shape=jax.ShapeDtypeStruct((depth, rows, n), F32),
        compiler_params=_params("parallel", "parallel"),
        name="adaln",
    )(c_all, w_ada, b_ada.reshape(depth, 1, n))


def _tile_specs(batch, length, d, nb, lt, extra_axes=0):
    if extra_axes == 0:
        act = lambda width: pl.BlockSpec((nb, lt, width), lambda i, j: (i, j, 0))
        vec = lambda width: pl.BlockSpec((nb, 1, width), lambda i, j: (i, 0, 0))
        par = lambda shape: pl.BlockSpec(shape, lambda i, j: (0,) * len(shape))
        pos = lambda width: pl.BlockSpec((lt, width), lambda i, j: (j, 0))
    else:
        act = lambda width: pl.BlockSpec((nb, lt, width), lambda i, j, f: (i, j, 0))
        vec = lambda width: pl.BlockSpec((nb, 1, width), lambda i, j, f: (i, 0, 0))
        par = lambda shape: pl.BlockSpec(shape, lambda i, j, f: (0,) * len(shape))
        pos = lambda width: pl.BlockSpec((lt, width), lambda i, j, f: (j, 0))
    return act, vec, par, pos


def _modulate_kernel(x_ref, g_ref, sh_ref, sc_ref, o_ref):
    o_ref[...] = _modulated(x_ref, g_ref, sh_ref, sc_ref)


def _modulate(x, g, sh, sc):
    b, l, d = x.shape
    nb, lt = _row_tiles(b, l)
    act, vec, par, _ = _tile_specs(b, l, d, nb, lt)
    return pl.pallas_call(
        _modulate_kernel,
        grid=(b // nb, l // lt),
        in_specs=[act(d), par((1, d)), vec(d), vec(d)],
        out_specs=act(d),
        out_shape=jax.ShapeDtypeStruct((b, l, d), F32),
        compiler_params=_params("parallel", "parallel"),
        name="modulate",
    )(x, g, sh, sc)


def _mlp_kernel(*refs, final_norm, tf, mixer):
    if mixer == 'proj':
        o_in_ref, wo_ref, gt1_ref, *refs = refs
    elif mixer == 'glu':
        y_ref, h_ref, dskip_ref, wa_ref, wb_ref, gt1_ref, *refs = refs
    x_ref, g_ref, sh_ref, sc_ref, gt_ref, wu_ref, wd_ref, gf_ref, o_ref = refs
    nb, lt, d = x_ref.shape
    x = x_ref[...]
    if mixer == 'proj':
        r = jnp.dot(o_in_ref[...].reshape(nb * lt, o_in_ref.shape[2]), wo_ref[...],
                    preferred_element_type=F32)
        x = x + (1.0 + gt1_ref[...]) * r.reshape(nb, lt, d)
    elif mixer == 'glu':
        z = jax.nn.gelu(y_ref[...] + dskip_ref[...] * h_ref[...]).reshape(nb * lt, d).astype(BF16)
        r = (jnp.dot(z, wa_ref[...], preferred_element_type=F32)
             * jax.nn.sigmoid(jnp.dot(z, wb_ref[...], preferred_element_type=F32)))
        x = x + (1.0 + gt1_ref[...]) * r.reshape(nb, lt, d)
    h = (_rms(x, g_ref[...]) * (1.0 + sc_ref[...]) + sh_ref[...]).reshape(nb * lt, d).astype(BF16)
    acc = None
    for f in range(wu_ref.shape[1] // tf):
        a = jnp.dot(h, wu_ref[:, f * tf:(f + 1) * tf], preferred_element_type=F32)
        a = jnp.square(jnp.maximum(a, 0.0)).astype(BF16)
        r = jnp.dot(a, wd_ref[f * tf:(f + 1) * tf, :], preferred_element_type=F32)
        acc = r if acc is None else acc + r
    y = x + (1.0 + gt_ref[...]) * acc.reshape(nb, lt, d)
    if final_norm:
        y = _rms(y, gf_ref[...])
    o_ref[...] = y


def _mlp(x, g, sh, sc, gt, w_up, w_down, g_final, final_norm, mixer=None, tf=2048):
    b, l, d = x.shape
    dff = w_up.shape[1]
    nb, lt = _row_tiles(b, l)
    act, vec, par, _ = _tile_specs(b, l, d, nb, lt)
    resident = lambda shape: pl.BlockSpec(shape, lambda i, j: (0, 0), pipeline_mode=pl.Buffered(1))
    args = [x, g, sh, sc, gt, w_up, w_down, g_final]
    in_specs = [act(d), par((1, d)), vec(d), vec(d), vec(d),
                resident((d, dff)), resident((dff, d)), par((1, d))]
    kind = None
    if mixer is not None:
        kind, *head = mixer
        if kind == 'proj':
            o, w_o, _ = head
            head_specs = [act(o.shape[2]), resident(w_o.shape), vec(d)]
        else:
            head_specs = [act(d), act(d), par((1, d)), resident((d, d)), resident((d, d)), vec(d)]
        args = head + args
        in_specs = head_specs + in_specs
    return pl.pallas_call(
        functools.partial(_mlp_kernel, final_norm=final_norm, tf=tf, mixer=kind),
        grid=(b // nb, l // lt),
        in_specs=in_specs,
        out_specs=act(d),
        out_shape=jax.ShapeDtypeStruct((b, l, d), F32),
        compiler_params=_params("parallel", "parallel"),
        name="mlp",
    )(*args)


def _s5_kernel(*refs, nch, batch, nsteps, has_h0):
    if has_h0:
        u_ref, bc_ref, m_ref, cc_ref, a1_ref, a2_ref, h0_ref, y_ref, hl_ref = refs
    else:
        u_ref, bc_ref, m_ref, cc_ref, a1_ref, a2_ref, y_ref, hl_ref = refs
    u = u_ref[0]
    rows = u.shape[0]
    x = jnp.dot(u, bc_ref[0], precision=HIGHEST, preferred_element_type=F32)
    half = x.shape[1] // 2
    if has_h0:
        h0 = h0_ref[0]
        x = x + a1_ref[0, 0:1, :] * h0 + a2_ref[0, 0:1, :] * pltpu.roll(h0, half, axis=1)
    kidx = lax.broadcasted_iota(jnp.int32, x.shape, 0) & (nch - 1)
    for j in range(nsteps):
        s = 1 << j
        xs = jnp.where(kidx >= s, pltpu.roll(x, s, axis=0), 0.0)
        x = x + a1_ref[0, j:j + 1, :] * xs + a2_ref[0, j:j + 1, :] * pltpu.roll(xs, half, axis=1)
    for b in range(batch):
        r = (b + 1) * nch - 1
        hl_ref[0, b:b + 1, :] = x[r:r + 1, :]
    hstart = jnp.where(kidx >= 1, pltpu.roll(x, 1, axis=0), 0.0)
    if has_h0:
        hstart = hstart + h0
    y = jnp.dot(u.astype(BF16), m_ref[0], preferred_element_type=F32)
    y = y + jnp.dot(hstart.astype(BF16), cc_ref[0], preferred_element_type=F32)
    y_ref[0] = y


def _s5_tables(a_re, a_im, b_re, b_im, c_re, c_im, log_dt, max_steps):
    g, p = a_re.shape
    sub = S5_SUB
    dt = jnp.exp(log_dt.astype(F32))[:, None]
    zr, zi = a_re.astype(F32) * dt, a_im.astype(F32) * dt

    def zpow(n):
        mag = jnp.exp(zr * n)
        return mag * jnp.cos(zi * n), mag * jnp.sin(zi * n)

    er = jnp.expm1(zr) * jnp.cos(zi) - 2.0 * jnp.square(jnp.sin(0.5 * zi))
    ei = jnp.exp(zr) * jnp.sin(zi)
    den = a_re * a_re + a_im * a_im
    fr = (er * a_re + ei * a_im) / den
    fi = (ei * a_re - er * a_im) / den
    bb_re = fr[..., None] * b_re - fi[..., None] * b_im
    bb_im = fr[..., None] * b_im + fi[..., None] * b_re

    lags = jnp.arange(sub + 1, dtype=F32)[:, None, None]
    pr, pi = zpow(lags)

    qr, qi = pr[:sub][::-1], pi[:sub][::-1]
    bcr = qr[..., None] * bb_re[None] - qi[..., None] * bb_im[None]
    bci = qr[..., None] * bb_im[None] + qi[..., None] * bb_re[None]
    bc = jnp.concatenate([bcr, bci], axis=2)
    bc = bc.transpose(1, 0, 3, 2).reshape(g, sub * S5_GROUP, 2 * p)

    ar, ai = pr[1:], pi[1:]
    ccr = c_re[None] * ar[:, :, None, :] - c_im[None] * ai[:, :, None, :]
    cci = c_re[None] * ai[:, :, None, :] + c_im[None] * ar[:, :, None, :]
    cc = jnp.concatenate([ccr, -cci], axis=3)
    cc = cc.transpose(1, 3, 0, 2).reshape(g, 2 * p, sub * S5_GROUP)

    kr = (jnp.einsum('gcp,lgp,gpd->lgcd', c_re, pr[:sub], bb_re, precision=HIGHEST)
          - jnp.einsum('gcp,lgp,gpd->lgcd', c_re, pi[:sub], bb_im, precision=HIGHEST)
          - jnp.einsum('gcp,lgp,gpd->lgcd', c_im, pr[:sub], bb_im, precision=HIGHEST)
          - jnp.einsum('gcp,lgp,gpd->lgcd', c_im, pi[:sub], bb_re, precision=HIGHEST))
    tt = jnp.arange(sub)
    lag = tt[None, :] - tt[:, None]
    toep = jnp.where((lag >= 0)[:, :, None, None, None],
                     kr[jnp.clip(lag, 0, sub - 1)], 0.0)
    m = toep.transpose(2, 0, 4, 1, 3).reshape(g, sub * S5_GROUP, sub * S5_GROUP)

    steps = (sub * (2 ** jnp.arange(max_steps))).astype(F32)[:, None, None]
    sr, si = zpow(steps)
    a1 = jnp.concatenate([sr, sr], axis=-1).transpose(1, 0, 2)
    a2 = jnp.concatenate([-si, si], axis=-1).transpose(1, 0, 2)
    pad = (-max_steps) % SUBLANES
    a1 = jnp.pad(a1, ((0, 0), (0, pad), (0, 0)))
    a2 = jnp.pad(a2, ((0, 0), (0, pad), (0, 0)))
    return bc, m.astype(BF16), cc.astype(BF16), a1, a2


def _s5_lane_kernel(h_ref, bct_ref, mt_ref, cct_ref, a1_ref, a2_ref, y_ref, hl_ref, ut_s, yt_s, hl_s,
                    *, nch, nsteps):
    sub, c = S5_SUB, S5_GROUP
    groups = ut_s.shape[0]
    for t in range(sub):
        att = h_ref[0, pl.ds(t, nch, stride=sub), :].T
        for g in range(groups):
            ut_s[g, t * c:(t + 1) * c, :] = att[g * c:(g + 1) * c, :]
    half = bct_ref.shape[1] // 2

    def group_pair(gp, carry):
        lane = lax.broadcasted_iota(jnp.int32, (2 * half, nch), 1)
        reps = nch // a1_ref.shape[3]
        gs = [2 * gp, 2 * gp + 1]
        uts = [ut_s[g] for g in gs]
        xs_ = [jnp.dot(bct_ref[g], ut, precision=HIGHEST, preferred_element_type=F32)
               for g, ut in zip(gs, uts)]
        for j in range(nsteps):
            s = 1 << j
            for i, g in enumerate(gs):
                sh = jnp.where(lane >= s, pltpu.roll(xs_[i], s, axis=1), 0.0)
                xs_[i] = (xs_[i] + jnp.tile(a1_ref[g, j], (1, reps)) * sh
                          + jnp.tile(a2_ref[g, j], (1, reps)) * pltpu.roll(sh, half, axis=0))
        for g, ut, x in zip(gs, uts, xs_):
            hl_s[g] = x[:, nch - LANES:].T[LANES - SUBLANES:, :]
            hstart = jnp.where(lane >= 1, pltpu.roll(x, 1, axis=1), 0.0)
            yt = jnp.dot(mt_ref[g], ut.astype(BF16), preferred_element_type=F32)
            yt = yt + jnp.dot(cct_ref[g], hstart.astype(BF16), preferred_element_type=F32)
            rows = pl.ds(pl.multiple_of(g * c, c), c)
            for t in range(sub):
                yt_s[t, rows, :] = yt[t * c:(t + 1) * c, :]
        return carry

    lax.fori_loop(0, groups // 2, group_pair, 0)
    for g in range(groups):
        hl_ref[0, g:g + 1, :] = hl_s[g, SUBLANES - 1:SUBLANES, :]
    for t in range(sub):
        y_ref[0, pl.ds(t, nch, stride=sub), :] = yt_s[t].T


def _s5_scan_lanes(h, tables):
    bc, m, cc, a1, a2 = tables
    b, l, d = h.shape
    g, k, p2 = bc.shape
    nch = l // S5_SUB
    gpb = LANES // S5_GROUP
    nsteps = nch.bit_length() - 1
    wspec = lambda shape: pl.BlockSpec((gpb,) + shape, lambda j, bi: (j, 0, 0))
    lane_bc = lambda a: jnp.broadcast_to(a[:, :nsteps, :, None], (g, nsteps, p2, LANES))
    mult = pl.BlockSpec((gpb, nsteps, p2, LANES), lambda j, bi: (j, 0, 0, 0), pipeline_mode=pl.Buffered(1))
    act = pl.BlockSpec((1, l, LANES), lambda j, bi: (bi, 0, j))
    return pl.pallas_call(
        functools.partial(_s5_lane_kernel, nch=nch, nsteps=nsteps),
        grid=(g // gpb, b),
        in_specs=[act, wspec((p2, k)), wspec((k, k)), wspec((k, p2)), mult, mult],
        out_specs=[act, pl.BlockSpec((1, gpb, p2), lambda j, bi: (bi, j, 0))],
        out_shape=[jax.ShapeDtypeStruct((b, l, d), F32), jax.ShapeDtypeStruct((b, g, p2), F32)],
        scratch_shapes=[pltpu.VMEM((gpb, k, nch), F32), pltpu.VMEM((S5_SUB, LANES, nch), F32),
                        pltpu.VMEM((gpb, SUBLANES, p2), F32)],
        compiler_params=_params("parallel", "parallel"),
        name="s5_scan_lanes",
    )(h, bc.transpose(0, 2, 1), m.transpose(0, 2, 1), cc.transpose(0, 2, 1), lane_bc(a1), lane_bc(a2))


def _s5_scan(h, tables, h0_re, h0_im):
    bc, m, cc, a1, a2 = tables
    b, l, d = h.shape
    if h0_re is None and (l // S5_SUB) % LANES == 0:
        y, hl = _s5_scan_lanes(h, tables)
        return y, hl[..., :hl.shape[-1] // 2], hl[..., hl.shape[-1] // 2:]
    g = d // S5_GROUP
    p2 = bc.shape[-1]
    nch = l // S5_SUB
    assert nch & (nch - 1) == 0
    nsteps = nch.bit_length() - 1
    rows = b * nch
    k = S5_SUB * S5_GROUP
    u = h.reshape(b, nch, S5_SUB, g, S5_GROUP).transpose(3, 0, 1, 2, 4).reshape(g, rows, k)
    has_h0 = h0_re is not None
    grp = lambda shape: pl.BlockSpec((1,) + shape, lambda i: (i, 0, 0))
    args = [u, bc, m, cc, a1, a2]
    in_specs = [grp((rows, k)), grp((k, p2)), grp((k, k)), grp((p2, k)),
                grp(a1.shape[1:]), grp(a2.shape[1:])]
    if has_h0:
        h0 = jnp.concatenate([h0_re, h0_im], axis=-1).astype(F32).transpose(1, 0, 2)
        h0 = jnp.pad(h0[:, :, None, :], ((0, 0), (0, 0), (0, nch - 1), (0, 0))).reshape(g, rows, p2)
        args.append(h0)
        in_specs.append(grp((rows, p2)))
    y, hl = pl.pallas_call(
        functools.partial(_s5_kernel, nch=nch, batch=b, nsteps=nsteps, has_h0=has_h0),
        grid=(g,),
        in_specs=in_specs,
        out_specs=[grp((rows, k)), grp((b, p2))],
        out_shape=[jax.ShapeDtypeStruct((g, rows, k), F32), jax.ShapeDtypeStruct((g, b, p2), F32)],
        compiler_params=_params("parallel"),
        name="s5_scan",
    )(*args)
    y = y.reshape(g, b, nch, S5_SUB, S5_GROUP).transpose(1, 2, 3, 0, 4).reshape(b, l, d)
    hl = hl.transpose(1, 0, 2)
    return y, hl[..., :p2 // 2], hl[..., p2 // 2:]


def _diff_qkv_kernel(x_ref, g_ref, sh_ref, sc_ref, w_ref, cos_ref, sin_ref,
                     k_ref, v_ref, qb_ref, kb_ref, vb_ref):
    nb, lt, d = x_ref.shape
    h = _modulated(x_ref, g_ref, sh_ref, sc_ref).reshape(nb * lt, d).astype(BF16)
    reps = d // cos_ref.shape[1]
    cos = jnp.tile(cos_ref[...], (1, reps))[None]
    sin = jnp.tile(sin_ref[...], (1, reps))[None]

    def roped(cols):
        t = jnp.dot(h, w_ref[:, cols * d:(cols + 1) * d], preferred_element_type=F32)
        r = _rotate_pairs(t, DIFF_DH // 2, DIFF_DH, 0)
        return t.reshape(nb, lt, d) * cos + r.reshape(nb, lt, d) * sin

    q = roped(0)
    qb_ref[...] = (q * (LOG2_E * DIFF_DH ** -0.5)).astype(BF16)
    k = roped(1)
    kb_ref[...] = k.astype(BF16)
    v = jnp.dot(h, w_ref[:, 2 * d:], preferred_element_type=F32).reshape(nb, lt, d)
    vb_ref[...] = v.astype(BF16)
    hw = k_ref.shape[2]
    heads = d // hw
    for hh in range(heads):
        k_ref[:, pl.ds(hh, lt, stride=heads), :] = k[:, :, hh * hw:(hh + 1) * hw]
        v_ref[:, pl.ds(hh, lt, stride=heads), :] = v[:, :, hh * hw:(hh + 1) * hw]


def _diff_qkv(x, g, sh, sc, w_qkv, cos, sin):
    b, l, d = x.shape
    nb, lt = _row_tiles(b, l)
    act, vec, par, pos = _tile_specs(b, l, d, nb, lt)
    hw = d // DIFF_HEADS
    return pl.pallas_call(
        _diff_qkv_kernel,
        grid=(b // nb, l // lt),
        in_specs=[act(d), par((1, d)), vec(d), vec(d), par((d, 3 * d)),
                  pos(cos.shape[1]), pos(sin.shape[1])],
        out_specs=[pl.BlockSpec((nb, lt * DIFF_HEADS, hw), lambda i, j: (i, j, 0))] * 2 + [act(d)] * 3,
        out_shape=[jax.ShapeDtypeStruct((b, l * DIFF_HEADS, hw), F32)] * 2
                  + [jax.ShapeDtypeStruct((b, l, d), BF16)] * 3,
        compiler_params=_params("parallel", "parallel"),
        name="diff_qkv",
    )(x, g, sh, sc, w_qkv, cos, sin)


_NT = (((1,), (1,)), ((), ()))
_TN = (((0,), (0,)), ((), ()))


def _softmax_update(m_s, l_s, acc_s, i, s, vb):
    m_prev = m_s[i]
    m_new = jnp.maximum(m_prev, jnp.max(s, axis=0, keepdims=True))
    alpha = jnp.exp2(m_prev - m_new)
    p = jnp.exp2(s - m_new)
    l_s[i] = alpha * l_s[i] + jnp.sum(p, axis=0, keepdims=True)
    acc_s[i] = alpha * acc_s[i] + lax.dot_general(vb, p.astype(BF16), _TN, preferred_element_type=F32)
    m_s[i] = m_new


def _softmax_init(m_s, l_s, acc_s):
    m_s[...] = jnp.full_like(m_s, NEG_INF)
    l_s[...] = jnp.zeros_like(l_s)
    acc_s[...] = jnp.zeros_like(acc_s)


def _softmax_result(l_s, acc_s, i):
    return acc_s[i] / l_s[i]


def _softmax_scratch(n, vw, queries):
    return [pltpu.VMEM((n, 1, queries), F32), pltpu.VMEM((n, 1, queries), F32),
            pltpu.VMEM((n, vw, queries), F32)]


def _attend_causal(scores, k_ref, value_dot, m_s, l_s, acc_s, s_a, s_b, *, tq, qi):
    assert len(scores) == 2 and tq % CHUNK == 0
    _softmax_init(m_s, l_s, acc_s)
    block = lambda ref, j: ref[0, pl.ds(pl.multiple_of(j * tq, tq), tq), :].astype(BF16)

    def qk(dst, j, streams):
        kb = block(k_ref, j)
        for i in streams:
            dst[i] = scores[i](kb)

    def pv(src, j, streams, masked=None):
        for i in streams:
            for c in range(0, 2 * tq, ATTN_LANE_BLOCK):
                cols = slice(c, c + ATTN_LANE_BLOCK)
                if i == masked:
                    src[i, :, cols] = jnp.where(vis[:, cols], src[i, :, cols], NEG_INF)
                m_prev = m_s[i, :, cols]
                m_new = jnp.maximum(m_prev, jnp.max(src[i, :, cols], axis=0, keepdims=True))
                alpha = jnp.exp2(m_prev - m_new)
                p = jnp.exp2(src[i, :, cols] - m_new)
                l_s[i, :, cols] = alpha * l_s[i, :, cols] + jnp.sum(p, axis=0, keepdims=True)
                acc_s[i, :, cols] = alpha * acc_s[i, :, cols] + value_dot(j, c, p.astype(BF16))
                m_s[i, :, cols] = m_new

    def pair(jj, carry):
        qk(s_b, 2 * jj + 1, (0, 1))
        pv(s_a, 2 * jj, (0, 1))
        qk(s_a, 2 * jj + 2, (0, 1))
        pv(s_b, 2 * jj + 1, (0, 1))
        return carry

    qk(s_a, 0, (0, 1))
    lax.fori_loop(0, qi, pair, 0)
    kc = lax.broadcasted_iota(jnp.int32, (tq, tq), 0) // CHUNK
    qc = lax.broadcasted_iota(jnp.int32, (tq, tq), 1) // CHUNK
    vis = jnp.concatenate([kc <= qc] * 2, axis=1)
    qk(s_b, 2 * qi + 1, (1,))
    pv(s_a, 2 * qi, (0, 1), masked=0)
    pv(s_b, 2 * qi + 1, (1,), masked=1)
    return [_softmax_result(l_s, acc_s, i) for i in range(2)]


def _diff_attn_kernel(q_ref, k_ref, v_ref, lam_ref, gsub_ref, o_ref, m_s, l_s, acc_s, s_a, s_b, *, tq):
    def make_score(i):
        q2 = _diff_query_groups(q_ref[0, i * tq:(i + 1) * tq, :])
        return lambda kb: lax.dot_general(kb, q2, _NT, preferred_element_type=F32)

    def value_dot(j, c, p):
        vb = v_ref[0, pl.ds(pl.multiple_of(j * tq, tq), tq), :]
        return lax.dot_general(vb, p, _TN, preferred_element_type=F32)

    outs = _attend_causal([make_score(0), make_score(1)], k_ref, value_dot, m_s, l_s, acc_s, s_a, s_b,
                          tq=tq, qi=pl.program_id(2))
    lam = _diff_lambda(lam_ref)
    for i, o2 in enumerate(outs):
        o = (o2[:, :tq] - lam * o2[:, tq:]).T
        o = _rms(o, gsub_ref[...]) * (1.0 - LAMBDA_INIT)
        o_ref[0, i * tq:(i + 1) * tq, :] = o.astype(o_ref.dtype)


def _causal_attention_call(kernel, qb, k, v, extra, qw, ow, acc_rows, name, v_spec=None):
    b, l, _ = qb.shape
    tq, ns = ATTN_TQ, ATTN_STREAMS
    assert l % (ns * tq) == 0
    tile = lambda width: pl.BlockSpec((1, ns * tq, width), lambda bi, h, qi: (bi, qi, h))
    seq = lambda width: pl.BlockSpec((1, l, width), lambda bi, h, qi: (bi, 0, h))
    units = qb.shape[2] // qw
    return pl.pallas_call(
        functools.partial(kernel, tq=tq),
        grid=(b, units, l // (ns * tq)),
        in_specs=[tile(qw), seq(qw), v_spec or seq(ow)]
                 + [pl.BlockSpec(a.shape, lambda bi, h, qi: (0, 0)) for a in extra],
        out_specs=tile(ow),
        out_shape=jax.ShapeDtypeStruct((b, l, units * ow), BF16),
        scratch_shapes=_softmax_scratch(ns, acc_rows, 2 * tq) + [pltpu.VMEM((ns, tq, 2 * tq), F32)] * 2,
        compiler_params=_params("parallel", "parallel", "arbitrary"),
        name=name,
    )(qb, k, v, *extra)


def _diff_attention(qb, kb, vb, lam_rows, g_sub):
    hw = 2 * DIFF_DH
    return _causal_attention_call(_diff_attn_kernel, qb, kb, vb, [lam_rows, g_sub], hw, hw, hw,
                                  "diff_attention")


def _diff_lambda(lam_ref):
    lp = lam_ref[...]
    return (jnp.exp(jnp.sum(lp[0:1] * lp[1:2], axis=-1, keepdims=True))
            - jnp.exp(jnp.sum(lp[2:3] * lp[3:4], axis=-1, keepdims=True)) + LAMBDA_INIT)


def _diff_query_groups(q):
    lane = lax.broadcasted_iota(jnp.int32, q.shape, 1)
    zero = jnp.zeros_like(q)
    return jnp.concatenate([jnp.where(lane < DIFF_DH, q, zero), jnp.where(lane >= DIFF_DH, q, zero)], axis=0)


def _diff_cached_kernel(q_ref, kc_ref, vc_ref, kn_ref, vn_ref, lam_ref, gsub_ref, o_ref,
                        m_s, l_s, acc_s, *, tk):
    c = pl.program_id(1)
    lq = q_ref.shape[1]
    hw = kc_ref.shape[2]
    heads = q_ref.shape[2] // hw
    update = functools.partial(_softmax_update, m_s, l_s, acc_s)
    q2 = lambda hh: _diff_query_groups(q_ref[0, :, hh * hw:(hh + 1) * hw])

    @pl.when(c == 0)
    def _():
        _softmax_init(m_s, l_s, acc_s)

    def block(j, carry):
        rows = pl.ds(pl.multiple_of(j * tk * heads, tk * heads), tk * heads)
        k_blk, v_blk = kc_ref.at[0, rows, :], vc_ref.at[0, rows, :]
        for hh in range(heads):
            kb = k_blk[pl.ds(hh, tk, stride=heads), :].astype(BF16)
            vb = v_blk[pl.ds(hh, tk, stride=heads), :].astype(BF16)
            update(hh, lax.dot_general(kb, q2(hh), _NT, preferred_element_type=F32), vb)
        return carry

    lax.fori_loop(0, kc_ref.shape[1] // (tk * heads), block, 0)

    @pl.when(c == pl.num_programs(1) - 1)
    def _():
        lam = _diff_lambda(lam_ref)
        for hh in range(heads):
            kb = kn_ref[0, :, hh * hw:(hh + 1) * hw]
            vb = vn_ref[0, :, hh * hw:(hh + 1) * hw]
            update(hh, lax.dot_general(kb, q2(hh), _NT, preferred_element_type=F32), vb)
            o2 = _softmax_result(l_s, acc_s, hh)
            o = (o2[:, :lq] - lam * o2[:, lq:]).T
            o = _rms(o, gsub_ref[...]) * (1.0 - LAMBDA_INIT)
            o_ref[0, :, hh * hw:(hh + 1) * hw] = o.astype(o_ref.dtype)


def _diff_attention_cached(qb, cache_k, cache_v, kb, vb, lam_rows, g_sub):
    b, l, d = qb.shape
    _, p, heads, hw = cache_k.shape
    pc = min(p, CACHE_CHUNK)
    tk = ATTN_TK if pc % ATTN_TK == 0 else CHUNK
    assert p % pc == 0 and pc % tk == 0
    new = pl.BlockSpec((1, l, d), lambda bi, c: (bi, 0, 0))
    cache = pl.BlockSpec((1, pc * heads, hw), lambda bi, c: (bi, c, 0))
    return pl.pallas_call(
        functools.partial(_diff_cached_kernel, tk=tk),
        grid=(b, p // pc),
        in_specs=[new, cache, cache, new, new,
                  pl.BlockSpec(lam_rows.shape, lambda bi, c: (0, 0)),
                  pl.BlockSpec((1, hw), lambda bi, c: (0, 0))],
        out_specs=new,
        out_shape=jax.ShapeDtypeStruct((b, l, d), BF16),
        scratch_shapes=_softmax_scratch(heads, hw, 2 * l),
        compiler_params=_params("parallel", "arbitrary"),
        name="diff_attention_cached",
    )(qb, cache_k.reshape(b, p * heads, hw), cache_v.reshape(b, p * heads, hw), kb, vb, lam_rows, g_sub)


def _mla_proj_kernel(*refs, cached):
    (x_ref, g_ref, sh_ref, sc_ref, wq_ref, wc_ref, wr_ref, wrr_ref, gq_ref, gkv_ref, kcos_ref, ksin_ref,
     wuq_ref, wuqr_ref, qcos_ref, qsin_ref, *rest) = refs
    nb, lt, d = x_ref.shape
    rows = nb * lt
    dot = lambda a, w_ref: jnp.dot(a, w_ref[...], preferred_element_type=F32)
    h = _modulated(x_ref, g_ref, sh_ref, sc_ref).reshape(rows, d).astype(BF16)
    cq = _rms(dot(h, wq_ref), gq_ref[...]).astype(BF16)
    ckv = _rms(dot(h, wc_ref), gkv_ref[...])
    rope = wr_ref.shape[1]
    kr = (dot(h, wr_ref).reshape(nb, lt, rope) * kcos_ref[...][None]
          + dot(h, wrr_ref).reshape(nb, lt, rope) * ksin_ref[...][None])
    n = wuq_ref.shape[1]
    reps = n // qcos_ref.shape[1]
    cos = jnp.tile(qcos_ref[...], (1, reps))[None]
    sin = jnp.tile(qsin_ref[...], (1, reps))[None]
    q = (dot(cq, wuq_ref).reshape(nb, lt, n) * cos + dot(cq, wuqr_ref).reshape(nb, lt, n) * sin).astype(BF16)
    if cached:
        wabs_ref, ckv_ref, kr_ref, q_ref = rest
        qf = q.reshape(rows, n)
        hp = MLA_HEAD_PAD
        for hh in range(n // hp):
            o = jnp.dot(qf[:, hh * hp:(hh + 1) * hp], wabs_ref[hh], preferred_element_type=F32)
            q_ref[:, hh] = o.reshape(nb, lt, o.shape[1]).astype(q_ref.dtype)
    else:
        wk_ref, we_ref, wv_ref, ckv_ref, kr_ref, q_ref, k_ref, v_ref = rest
        q_ref[...] = q
        c = ckv.astype(BF16)
        k = dot(c, wk_ref) + dot(kr.reshape(rows, rope).astype(BF16), we_ref)
        k_ref[...] = k.reshape(k_ref.shape).astype(k_ref.dtype)
        v = dot(c, wv_ref)
        for pr in range(v.shape[1] // LANES):
            vt = v[:, pr * LANES:(pr + 1) * LANES].T.astype(v_ref.dtype)
            v_ref[0, 2 * pr, 0] = vt[:MLA_V]
            v_ref[0, 2 * pr + 1, 0] = vt[MLA_V:]
    ckv_ref[...] = ckv.reshape(ckv_ref.shape)
    kr_ref[...] = kr


def _mla_project(x, g, sh, sc, w, k_cos, k_sin, q_cos, q_sin, cached):
    b, l, d = x.shape
    nb, lt = _row_tiles(b, l)
    act, vec, par, pos = _tile_specs(b, l, d, nb, lt)
    weights = [w['mla_dq'], w['mla_dc'], w['mla_dr'], w['mla_drr'], w['mla_g_q'], w['mla_g_kv']]
    q_weights = [w['mla_uq'], w['mla_uq_rot']]
    kvr, rope, n = w['mla_dc'].shape[1], w['mla_dr'].shape[1], w['mla_uq'].shape[1]
    out_specs = [act(kvr), act(rope)]
    out_shape = [jax.ShapeDtypeStruct((b, l, kvr), F32), jax.ShapeDtypeStruct((b, l, rope), F32)]
    if cached:
        tail = [w['mla_abs']]
        heads, _, qw = w['mla_abs'].shape
        out_specs.append(pl.BlockSpec((nb, heads, lt, qw), lambda i, j: (i, 0, j, 0)))
        out_shape.append(jax.ShapeDtypeStruct((b, heads, l, qw), BF16))
    else:
        tail = [w['mla_uk'], w['mla_place'], w['mla_uv']]
        nv = w['mla_uv'].shape[1]
        assert nb == 1 and lt == ATTN_TQ and 2 * MLA_V == LANES
        heads = nv // MLA_V
        out_specs += [act(n), act(n), pl.BlockSpec((1, heads, 1, MLA_V, lt), lambda i, j: (i, 0, j, 0, 0))]
        out_shape += [jax.ShapeDtypeStruct((b, l, n), BF16)] * 2
        out_shape += [jax.ShapeDtypeStruct((b, heads, l // lt, MLA_V, lt), BF16)]
    return pl.pallas_call(
        functools.partial(_mla_proj_kernel, cached=cached),
        grid=(b // nb, l // lt),
        in_specs=[act(d), par((1, d)), vec(d), vec(d)] + [par(a.shape) for a in weights]
                 + [pos(rope), pos(rope)] + [par(a.shape) for a in q_weights]
                 + [pos(q_cos.shape[1]), pos(q_sin.shape[1])] + [par(a.shape) for a in tail],
        out_specs=out_specs,
        out_shape=out_shape,
        compiler_params=_params("parallel", "parallel"),
        name="mla_project",
    )(x, g, sh, sc, *weights, k_cos, k_sin, *q_weights, q_cos, q_sin, *tail)


def _mla_attn_kernel(q_ref, k_ref, v_ref, o_ref, m_s, l_s, acc_s, s_a, s_b, *, tq):
    hp = MLA_HEAD_PAD

    def make_score(i):
        qa = q_ref[0, i * tq:(i + 1) * tq, :hp]
        qb = q_ref[0, i * tq:(i + 1) * tq, hp:]
        return lambda kb: jnp.concatenate(
            [lax.dot_general(kb[:, :hp], qa, _NT, preferred_element_type=F32),
             lax.dot_general(kb[:, hp:], qb, _NT, preferred_element_type=F32)], axis=1)

    value_dot = lambda j, c, p: jnp.dot(v_ref[0, c // tq, j], p, preferred_element_type=F32)
    outs = _attend_causal([make_score(0), make_score(1)], k_ref, value_dot, m_s, l_s, acc_s, s_a, s_b,
                          tq=tq, qi=pl.program_id(2))
    for i, o2 in enumerate(outs):
        o = jnp.concatenate([o2[:, :tq], o2[:, tq:]], axis=0).T
        o_ref[0, i * tq:(i + 1) * tq, :] = o.astype(o_ref.dtype)


def _mla_attention(qb, kb, vt):
    v_spec = pl.BlockSpec((1, 2) + vt.shape[2:], lambda bi, h, qi: (bi, h, 0, 0, 0))
    return _causal_attention_call(_mla_attn_kernel, qb, kb, vt, [], 2 * MLA_HEAD_PAD, 2 * MLA_V, MLA_V,
                                  "mla_attention", v_spec=v_spec)


def _mla_cached_kernel(q_ref, cc_ref, rc_ref, cn_ref, rn_ref, wuvt_ref, o_ref, m_s, l_s, acc_s, *, tk):
    heads, lq, qw = q_ref.shape[1:]
    q2 = q_ref[0].reshape(heads * lq, qw)

    def step(c, r):
        rpad = jnp.concatenate([r, jnp.zeros((r.shape[0], qw - c.shape[1] - r.shape[1]), r.dtype)], axis=1)
        kb = jnp.concatenate([c, rpad], axis=1).astype(BF16)
        s = lax.dot_general(kb, q2, _NT, preferred_element_type=F32)
        _softmax_update(m_s, l_s, acc_s, 0, s, c.astype(BF16))

    def block(j, carry):
        rows = pl.ds(pl.multiple_of(j * tk, tk), tk)
        step(cc_ref[0, rows, :], rc_ref[0, rows, :])
        return carry

    _softmax_init(m_s, l_s, acc_s)
    lax.fori_loop(0, cc_ref.shape[1] // tk, block, 0)
    step(cn_ref[0], rn_ref[0])
    lat = _softmax_result(l_s, acc_s, 0)
    assert 2 * lq == LANES and 2 * MLA_V == LANES
    lane = lax.broadcasted_iota(jnp.int32, (lq, LANES), 1)
    for pr in range(heads // 2):
        lp = lat[:, pr * LANES:(pr + 1) * LANES].astype(BF16)
        tt = jnp.dot(wuvt_ref[pr], lp, preferred_element_type=F32).T
        o_ref[0, :, pr * LANES:(pr + 1) * LANES] = jnp.where(lane < MLA_V, tt[:lq], tt[lq:]).astype(o_ref.dtype)


def _mla_attention_cached(q_abs, cache_ckv, cache_krope, ckv, krope, w_uvt):
    b, heads, l, qw = q_abs.shape
    p, rank = cache_ckv.shape[1:]
    rope = cache_krope.shape[2]
    tk = ATTN_TK if p % ATTN_TK == 0 else CHUNK
    assert p % tk == 0
    full = lambda shape: pl.BlockSpec((1,) + shape, lambda bi: (bi,) + (0,) * len(shape))
    return pl.pallas_call(
        functools.partial(_mla_cached_kernel, tk=tk),
        grid=(b,),
        in_specs=[full((heads, l, qw)), full((p, rank)), full((p, rope)), full((l, rank)), full((l, rope)),
                  pl.BlockSpec(w_uvt.shape, lambda bi: (0, 0, 0))],
        out_specs=full((l, heads * MLA_V)),
        out_shape=jax.ShapeDtypeStruct((b, l, heads * MLA_V), BF16),
        scratch_shapes=_softmax_scratch(1, rank, heads * l),
        compiler_params=_params("parallel"),
        name="mla_attention_cached",
    )(q_abs, cache_ckv, cache_krope, ckv, krope, w_uvt)


def _sgu_kernel(x_ref, g_ref, sh_ref, sc_ref, gt_ref, win_ref, gv_ref, ws_ref, bs_ref, wout_ref,
                *out_refs, t, emit_v):
    o_ref = out_refs[0]
    nb, lt, d = x_ref.shape
    rows = nb * lt
    width = gv_ref.shape[1]
    gd = width // SGU_GROUPS
    h = _modulated(x_ref, g_ref, sh_ref, sc_ref).reshape(rows, d).astype(BF16)
    u = jax.nn.gelu(jnp.dot(h, win_ref[:, :width], preferred_element_type=F32))
    v = jax.nn.gelu(jnp.dot(h, win_ref[:, width:], preferred_element_type=F32))
    v = _rms(v, gv_ref[...])
    if emit_v:
        out_refs[1][...] = v.reshape(nb, lt, width)
    vb = v.astype(BF16)
    bias = bs_ref[...]
    gated = []
    for c in range(rows // t):
        sv = [jnp.dot(ws_ref[gi], vb[c * t:(c + 1) * t, gi * gd:(gi + 1) * gd],
                      preferred_element_type=F32) for gi in range(SGU_GROUPS)]
        sv = jnp.concatenate(sv, axis=1) + bias
        gated.append((u[c * t:(c + 1) * t] * sv).astype(BF16))
    gated = jnp.concatenate(gated, axis=0) if len(gated) > 1 else gated[0]
    r = jnp.dot(gated, wout_ref[...], preferred_element_type=F32).reshape(nb, lt, d)
    o_ref[...] = x_ref[...] + (1.0 + gt_ref[...]) * r


def _sgu(x, g, sh, sc, gt, w_in, g_v, w_s, b_full, w_out, emit_v):
    b, l, d = x.shape
    t = w_s.shape[1]
    width = g_v.shape[1]
    nb, lt = _row_tiles(b, l, 512)
    assert lt % t == 0
    act, vec, par, _ = _tile_specs(b, l, d, nb, lt)
    resident = lambda shape: pl.BlockSpec(shape, lambda i, j: (0, 0), pipeline_mode=pl.Buffered(1))
    out_specs = [act(d)]
    out_shape = [jax.ShapeDtypeStruct((b, l, d), F32)]
    if emit_v:
        out_specs.append(act(width))
        out_shape.append(jax.ShapeDtypeStruct((b, l, width), F32))
    return pl.pallas_call(
        functools.partial(_sgu_kernel, t=t, emit_v=emit_v),
        grid=(b // nb, l // lt),
        in_specs=[act(d), par((1, d)), vec(d), vec(d), vec(d), resident(w_in.shape), par((1, width)),
                  par(w_s.shape), par(b_full.shape), resident(w_out.shape)],
        out_specs=out_specs,
        out_shape=out_shape,
        compiler_params=_params("parallel", "parallel"),
        name="sgu",
    )(x, g, sh, sc, gt, w_in, g_v, w_s, b_full, w_out)


def _rope_angles(pos, half):
    inv_freq = ROPE_THETA ** (-jnp.arange(half, dtype=F32) / half)
    ang = pos.astype(F32)[:, None] * inv_freq[None, :]
    return jnp.cos(ang), jnp.sin(ang)


def _diff_rope_tables(pos):
    cos, sin = _rope_angles(pos, DIFF_DH // 2)
    reps = LANES // DIFF_DH
    return (jnp.tile(jnp.concatenate([cos, cos], axis=1), (1, reps)),
            jnp.tile(jnp.concatenate([-sin, sin], axis=1), (1, reps)))


def _mla_rope_tables(pos):
    cos, sin = _rope_angles(pos, MLA_ROPE // 2)
    n = pos.shape[0]
    ones = jnp.ones((n, MLA_NOPE), F32)
    zq = jnp.zeros((n, MLA_NOPE), F32)
    zp = jnp.zeros((n, MLA_HEAD_PAD - MLA_NOPE - MLA_ROPE), F32)
    scale = LOG2_E * MLA_SCALE
    q_cos = jnp.concatenate([ones, cos, cos, zp], axis=1) * scale
    q_sin = jnp.concatenate([zq, -sin, sin, zp], axis=1) * scale
    k_cos = jnp.concatenate([cos, cos], axis=1)
    k_sin = jnp.concatenate([sin, sin], axis=1)
    return q_cos, q_sin, k_cos, k_sin


def _prepare_weights(p):
    d = p['w_up'].shape[1]
    w = {}
    w['w_up'] = [p['w_up'][i].astype(BF16) for i in range(p['w_up'].shape[0])]
    w['w_down'] = [p['w_down'][i].astype(BF16) for i in range(p['w_down'].shape[0])]
    w['glu_a'] = p['s5_w_glu_a'].astype(BF16)
    w['glu_b'] = p['s5_w_glu_b'].astype(BF16)
    w['diff_qkv'] = p['diff_w_qkv'].astype(BF16)
    w['diff_o'] = p['diff_w_o'].astype(BF16)
    lam = jnp.stack([p['diff_lambda_q1'], p['diff_lambda_k1'], p['diff_lambda_q2'], p['diff_lambda_k2']])
    w['diff_lam'] = jnp.pad(lam.astype(F32), ((0, 4), (0, LANES - lam.shape[1])))
    w['diff_g_sub'] = p['diff_g_sub'].reshape(1, -1)

    w['mla_dq'] = p['mla_w_dq'].astype(BF16)
    kvr = p['mla_g_kv'].shape[0]
    w['mla_dc'] = p['mla_w_dkv'][:, :kvr].astype(BF16)
    wr = p['mla_w_dkv'][:, kvr:]
    hr = MLA_ROPE // 2
    w['mla_dr'] = wr.astype(BF16)
    w['mla_drr'] = jnp.concatenate([-wr[:, hr:], wr[:, :hr]], axis=1).astype(BF16)
    w['mla_g_q'] = p['mla_g_q'].reshape(1, -1)
    w['mla_g_kv'] = p['mla_g_kv'].reshape(1, -1)
    pad = MLA_HEAD_PAD - MLA_NOPE - MLA_ROPE
    qr = p['mla_w_uq'].shape[0]
    uq = p['mla_w_uq'].reshape(qr, MLA_HEADS, MLA_NOPE + MLA_ROPE)
    w['mla_uq'] = jnp.pad(uq, ((0, 0), (0, 0), (0, pad))).reshape(qr, -1).astype(BF16)
    uq_rot = jnp.concatenate([jnp.zeros_like(uq[..., :MLA_NOPE]), uq[..., MLA_NOPE + hr:],
                              uq[..., MLA_NOPE:MLA_NOPE + hr]], axis=2)
    w['mla_uq_rot'] = jnp.pad(uq_rot, ((0, 0), (0, 0), (0, pad))).reshape(qr, -1).astype(BF16)
    uk = jnp.pad(p['mla_w_uk'], ((0, 0), (0, 0), (0, MLA_HEAD_PAD - MLA_NOPE)))
    w['mla_uk'] = uk.reshape(kvr, -1).astype(BF16)
    place = jnp.pad(jnp.eye(MLA_ROPE, dtype=F32), ((0, 0), (MLA_NOPE, pad)))
    w['mla_place'] = jnp.tile(place, (1, MLA_HEADS)).astype(BF16)
    w['mla_uv'] = p['mla_w_uv'].reshape(kvr, -1).astype(BF16)
    absorb = jnp.pad(p['mla_w_uk'].transpose(1, 2, 0), ((0, 0), (0, MLA_HEAD_PAD - MLA_NOPE), (0, 0)))
    select = jnp.pad(jnp.eye(MLA_ROPE, dtype=F32), ((MLA_NOPE, pad), (0, MLA_HEAD_PAD - MLA_ROPE)))
    w['mla_abs'] = jnp.concatenate(
        [absorb, jnp.broadcast_to(select, (MLA_HEADS,) + select.shape)], axis=2).astype(BF16)
    w['mla_uvt'] = p['mla_w_uv'].transpose(1, 2, 0).reshape(MLA_HEADS // 2, 2 * MLA_V, kvr).astype(BF16)
    w['mla_o'] = p['mla_w_o'].astype(BF16)

    w['sgu_in'] = p['sgu_w_in'].astype(BF16)
    w['sgu_g_v'] = p['sgu_g_v'].reshape(1, -1)
    w['sgu_out'] = p['sgu_w_out'].astype(BF16)
    return w


def _sgu_spatial(w_s, b_s, t, width):
    ws = jnp.tril(w_s[:, :t, :t]).astype(BF16)
    gd = width // SGU_GROUPS
    b_full = jnp.repeat(b_s[:, :t].T.astype(F32), gd, axis=1)
    return ws, b_full


def _run_trunk(x, mods, pos, past, p, w, s5_tables):
    b, l, d = x.shape
    new = {}
    depth = p['g_mix'].shape[0]
    for i in range(depth):
        sh1, sc1, gt1, sh2, sc2, gt2 = [mods[i][:, k][:, None, :] for k in range(6)]
        g_mix = p['g_mix'][i].reshape(1, d)
        kind = i % N_MIXERS
        mixer = None
        if kind == 0:
            h = _modulate(x, g_mix, sh1, sc1)
            h0_re = None if past is None else past['s5_re']
            h0_im = None if past is None else past['s5_im']
            y, new['s5_re'], new['s5_im'] = _s5_scan(h, s5_tables, h0_re, h0_im)
            mixer = ('glu', y, h, p['s5_d'].reshape(1, d), w['glu_a'], w['glu_b'], gt1)
        elif kind == 1:
            cos, sin = _diff_rope_tables(pos)
            k, v, qb, kb, vb = _diff_qkv(x, g_mix, sh1, sc1, w['diff_qkv'], cos, sin)
            if past is None:
                o = _diff_attention(qb, kb, vb, w['diff_lam'], w['diff_g_sub'])
            else:
                o = _diff_attention_cached(qb, past['diff_k'], past['diff_v'], kb, vb,
                                           w['diff_lam'], w['diff_g_sub'])
            mixer = ('proj', o, w['diff_o'], gt1)
            new['diff_k'] = k.reshape(b, l, DIFF_HEADS, 2 * DIFF_DH)
            new['diff_v'] = v.reshape(b, l, DIFF_HEADS, 2 * DIFF_DH)
        elif kind == 2:
            q_cos, q_sin, k_cos, k_sin = _mla_rope_tables(pos)
            proj = _mla_project(x, g_mix, sh1, sc1, w, k_cos, k_sin, q_cos, q_sin, cached=past is not None)
            ckv, krope = proj[:2]
            if past is None:
                o = _mla_attention(*proj[2:])
            else:
                o = _mla_attention_cached(proj[2], past['mla_ckv'], past['mla_krope'], ckv, krope,
                                          w['mla_uvt'])
            mixer = ('proj', o, w['mla_o'], gt1)
            new['mla_ckv'], new['mla_krope'] = ckv, krope
        else:
            t = min(l, SGU_CHUNK)
            ws, b_full = _sgu_spatial(p['sgu_w_s'], p['sgu_b_s'], t, w['sgu_g_v'].shape[1])
            outs = _sgu(x, g_mix, sh1, sc1, gt1, w['sgu_in'], w['sgu_g_v'], ws, b_full, w['sgu_out'],
                        emit_v=past is not None)
            x = outs[0]
            if past is not None:
                new['sgu_v'] = outs[1]
        x = _mlp(x, p['g_ffn'][i].reshape(1, d), sh2, sc2, gt2, w['w_up'][i], w['w_down'][i],
                 p['g_final'].reshape(1, d), final_norm=(i == depth - 1), mixer=mixer)
    return x, new


def kernel(x_prompt, x_sample, c_prompt, c_sample, state_s5_re, state_s5_im, cache_diff_k, cache_diff_v, cache_mla_ckv, cache_mla_krope, w_ada, b_ada, g_mix, g_ffn, w_up, w_down, g_final, s5_a_re, s5_a_im, s5_b_re, s5_b_im, s5_c_re, s5_c_im, s5_d, s5_log_dt, s5_w_glu_a, s5_w_glu_b, diff_w_qkv, diff_lambda_q1, diff_lambda_k1, diff_lambda_q2, diff_lambda_k2, diff_g_sub, diff_w_o, mla_w_dq, mla_g_q, mla_w_uq, mla_w_dkv, mla_g_kv, mla_w_uk, mla_w_uv, mla_w_o, sgu_w_in, sgu_g_v, sgu_w_s, sgu_b_s, sgu_w_out):
    p = {
        'w_ada': w_ada, 'b_ada': b_ada, 'g_mix': g_mix, 'g_ffn': g_ffn,
        'w_up': w_up, 'w_down': w_down, 'g_final': g_final,
        's5_d': s5_d, 's5_w_glu_a': s5_w_glu_a, 's5_w_glu_b': s5_w_glu_b,
        'diff_w_qkv': diff_w_qkv, 'diff_lambda_q1': diff_lambda_q1, 'diff_lambda_k1': diff_lambda_k1,
        'diff_lambda_q2': diff_lambda_q2, 'diff_lambda_k2': diff_lambda_k2,
        'diff_g_sub': diff_g_sub, 'diff_w_o': diff_w_o,
        'mla_w_dq': mla_w_dq, 'mla_g_q': mla_g_q, 'mla_w_uq': mla_w_uq, 'mla_w_dkv': mla_w_dkv,
        'mla_g_kv': mla_g_kv, 'mla_w_uk': mla_w_uk, 'mla_w_uv': mla_w_uv, 'mla_w_o': mla_w_o,
        'sgu_w_in': sgu_w_in, 'sgu_g_v': sgu_g_v, 'sgu_w_s': sgu_w_s, 'sgu_b_s': sgu_b_s,
        'sgu_w_out': sgu_w_out,
    }
    past = {
        's5_re': state_s5_re, 's5_im': state_s5_im,
        'diff_k': cache_diff_k, 'diff_v': cache_diff_v,
        'mla_ckv': cache_mla_ckv, 'mla_krope': cache_mla_krope,
    }
    bp, lp, d = x_prompt.shape
    bs, ls, _ = x_sample.shape
    depth = w_ada.shape[0]

    c_all = jnp.concatenate([c_prompt, c_sample], axis=0)
    c_all = jnp.pad(c_all, ((0, (-c_all.shape[0]) % SUBLANES), (0, 0)))
    mod = _adaln(c_all, w_ada, b_ada).reshape(depth, c_all.shape[0], 6, d)
    mods_p = [mod[i, :bp] for i in range(depth)]
    mods_s = [mod[i, bp:bp + bs] for i in range(depth)]

    w = _prepare_weights(p)
    max_steps = max((lp // S5_SUB).bit_length() - 1, (ls // S5_SUB).bit_length() - 1, 1)
    s5_tables = _s5_tables(s5_a_re, s5_a_im, s5_b_re, s5_b_im, s5_c_re, s5_c_im, s5_log_dt, max_steps)

    pos_p = jnp.arange(lp, dtype=jnp.int32)
    pos_s = cache_diff_k.shape[1] + jnp.arange(ls, dtype=jnp.int32)
    y_prompt, sp = _run_trunk(x_prompt, mods_p, pos_p, None, p, w, s5_tables)
    y_sample, ss = _run_trunk(x_sample, mods_s, pos_s, past, p, w, s5_tables)
    return (y_prompt, y_sample,
            sp['s5_re'], sp['s5_im'], ss['s5_re'], ss['s5_im'],
            sp['diff_k'], sp['diff_v'], ss['diff_k'], ss['diff_v'],
            sp['mla_ckv'], sp['mla_krope'], ss['mla_ckv'], ss['mla_krope'],
            ss['sgu_v'])
```
